```python
import math
import jax, jax.numpy as jnp
from jax import lax
import numpy as np

D_MODEL = 1024
BATCH = 8
SEQ = 2048
DEPTH = 1
DEC_BATCH = 128
DEC_SEQ = 1
PAST_LEN = 16384
PAGE_SIZE = 128

SSD_HEADS = 16
SSD_HEADDIM = 64
SSD_D = SSD_HEADS * SSD_HEADDIM
SSD_GROUPS = 2
D_STATE = 128
CONV_W = 4
CONV_DIM = SSD_D + 2 * SSD_GROUPS * D_STATE
SSD_CHUNK = 64
HG_HEADS = 8
HG_DK = 128
HG_DV = 128
HG_D = HG_HEADS * HG_DV
HG_CHUNK = 32
D_MIX = SSD_D + HG_D
IN_SPLITS = (SSD_D, CONV_DIM, SSD_HEADS, HG_HEADS * HG_DK, HG_HEADS * HG_DK, HG_D, HG_D)
IN_COLS = sum(IN_SPLITS)
N_EGROUPS = 4
EXPERTS_PER_GROUP = 4
N_EXPERTS = N_EGROUPS * EXPERTS_PER_GROUP
EXPERT_FF = 512
TOP_K = 2
PLE_DIM = 256
EPS = 1e-6

kernel_name = 'hybrid_ssd_hgrn2_hmoe_decode_step'


def _rms_unit(x):
    x32 = x.astype(jnp.float32)
    return x32 * lax.rsqrt(jnp.mean(x32 * x32, axis=-1, keepdims=True) + EPS)


def rmsnorm(x, w):
    return (_rms_unit(x) * w.astype(jnp.float32)).astype(x.dtype)


def causal_dwconv(x, buf, w, b):
    xp = jnp.concatenate([buf.astype(x.dtype), x], axis=1)
    out = lax.conv_general_dilated(
        xp, w[:, None, :].astype(x.dtype), window_strides=(1,), padding='VALID',
        dimension_numbers=('NWC', 'WIO', 'NWC'), feature_group_count=x.shape[-1])
    return out + b.astype(x.dtype), xp[:, -(CONV_W - 1):]


def _segsum(a):
    T = a.shape[-1]
    ar = jnp.broadcast_to(a[..., :, None], a.shape + (T,))
    ar = jnp.where(jnp.tril(jnp.ones((T, T), bool), -1), ar, 0.0)
    cs = jnp.cumsum(ar, axis=-2)
    return jnp.where(jnp.tril(jnp.ones((T, T), bool)), cs, -jnp.inf)


def _pad_seq(t, pad):
    return jnp.pad(t, [(0, 0), (0, pad)] + [(0, 0)] * (t.ndim - 2))


def ssd_chunked(x, dt, a_head, b_in, c_in, h0):
    bsz, L, H, P = x.shape
    Q = min(SSD_CHUNK, L)
    nc = -(-L // Q)
    pad = nc * Q - L
    rep = H // b_in.shape[2]
    bh = jnp.repeat(b_in, rep, axis=2)
    ch = jnp.repeat(c_in, rep, axis=2)
    N = bh.shape[-1]
    xdt = x * dt[..., None]
    a = dt * a_head
    xdt, a, bh, ch = (_pad_seq(t, pad) for t in (xdt, a, bh, ch))
    xdt = xdt.reshape(bsz, nc, Q, H, P)
    bh = bh.reshape(bsz, nc, Q, H, N)
    ch = ch.reshape(bsz, nc, Q, H, N)
    a = a.reshape(bsz, nc, Q, H).transpose(0, 3, 1, 2)
    a_cs = jnp.cumsum(a, axis=-1)
    lmat = jnp.exp(_segsum(a))
    y_diag = jnp.einsum('bclhn,bcshn,bhcls,bcshp->bclhp', ch, bh, lmat, xdt)
    decay_states = jnp.exp(a_cs[..., -1:] - a_cs)
    states = jnp.einsum('bclhn,bhcl,bclhp->bchpn', bh, decay_states, xdt)
    states = jnp.concatenate([h0[:, None], states], axis=1)
    chunk_a = jnp.pad(a_cs[..., -1], [(0, 0), (0, 0), (1, 0)])
    decay_chunk = jnp.exp(_segsum(chunk_a))
    new_states = jnp.einsum('bhzc,bchpn->bzhpn', decay_chunk, states)
    states_in, h_last = new_states[:, :-1], new_states[:, -1]
    y_off = jnp.einsum('bclhn,bchpn,bhcl->bclhp', ch, states_in, jnp.exp(a_cs))
    y = (y_diag + y_off).reshape(bsz, nc * Q, H, P)[:, :L]
    return y, h_last


def hgrn2_chunked(q, logf, k, v, s0):
    bsz, L, H, K = q.shape
    Q = min(HG_CHUNK, L)
    nc = -(-L // Q)
    pad = nc * Q - L
    q, logf, k, v = (_pad_seq(t, pad) for t in (q, logf, k, v))
    to_chunks = lambda t: t.reshape(bsz, nc, Q, H, t.shape[-1]).swapaxes(0, 1)
    causal = jnp.tril(jnp.ones((Q, Q), bool))[None, :, :, None, None]

    def step(s, inp):
        qc, gc, kc, vc = inp
        gcs = jnp.cumsum(gc, axis=1)
        o_inter = jnp.einsum('blhk,bhkv->blhv', qc * jnp.exp(gcs), s)
        diff = gcs[:, :, None] - gcs[:, None, :]
        decay = jnp.exp(jnp.where(causal, diff, -jnp.inf))
        attn = jnp.einsum('bihk,bjhk,bijhk->bhij', qc, kc, decay)
        o_intra = jnp.einsum('bhij,bjhv->bihv', attn, vc)
        g_last = gcs[:, -1]
        s_new = jnp.exp(g_last)[..., None] * s + jnp.einsum(
            'blhk,blhv->bhkv', kc * jnp.exp(g_last[:, None] - gcs), vc)
        return s_new, o_inter + o_intra

    s_last, o = lax.scan(step, s0, (to_chunks(q), to_chunks(logf), to_chunks(k), to_chunks(v)))
    o = o.swapaxes(0, 1).reshape(bsz, nc * Q, H, v.shape[-1])[:, :L]
    return o, s_last


def token_mixer(h, conv_buf, ssm0, hg0, lb, w_in, conv_w, conv_b, dt_bias, a_log, d_skip,
                ssd_norm_w, hg_norm_w, w_out):
    f32 = jnp.float32
    bsz, L, _ = h.shape
    offs = [int(o) for o in np.cumsum(IN_SPLITS)[:-1]]
    proj = h @ w_in
    z, xbc, dt_raw, q, f_raw, v, g = jnp.split(proj, offs, axis=-1)
    xbc, conv_new = causal_dwconv(xbc, conv_buf, conv_w, conv_b)
    xbc = jax.nn.silu(xbc)
    xs, b_in, c_in = jnp.split(xbc, [SSD_D, SSD_D + SSD_GROUPS * D_STATE], axis=-1)
    xs = xs.astype(f32).reshape(bsz, L, SSD_HEADS, SSD_HEADDIM)
    b_in = b_in.astype(f32).reshape(bsz, L, SSD_GROUPS, D_STATE)
    c_in = c_in.astype(f32).reshape(bsz, L, SSD_GROUPS, D_STATE)
    dt = jax.nn.softplus(dt_raw.astype(f32) + dt_bias.astype(f32))
    a_head = -jnp.exp(a_log.astype(f32))
    y, ssm_new = ssd_chunked(xs, dt, a_head, b_in, c_in, ssm0.astype(f32))
    y = y + d_skip.astype(f32)[:, None] * xs
    y = y.reshape(bsz, L, SSD_D) * jax.nn.silu(z.astype(f32))
    y = _rms_unit(y.reshape(bsz, L, SSD_GROUPS, SSD_D // SSD_GROUPS)).reshape(bsz, L, SSD_D)
    y = y * ssd_norm_w.astype(f32)
    f_raw = f_raw.astype(f32)
    fgate = lb + (1.0 - lb) * jax.nn.sigmoid(f_raw)
    kk = (1.0 - lb) * jax.nn.sigmoid(-f_raw)
    shp = (bsz, L, HG_HEADS, HG_DK)
    o, hg_new = hgrn2_chunked(q.astype(f32).reshape(shp), jnp.log(fgate).reshape(shp),
                              kk.reshape(shp), v.astype(f32).reshape(bsz, L, HG_HEADS, HG_DV),
                              hg0.astype(f32))
    o = _rms_unit(o).reshape(bsz, L, HG_D) * hg_norm_w.astype(f32) * jax.nn.silu(g.astype(f32))
    mixed = jnp.concatenate([y, o], axis=-1).astype(h.dtype)
    return mixed @ w_out, conv_new, ssm_new.astype(h.dtype), hg_new.astype(h.dtype)


def hier_moe(h, w_rg, b_rg, w_re, b_re, w_gate, w_up, w_down):
    f32 = jnp.float32
    bsz, L, D = h.shape
    t = h.reshape(bsz * L, D)
    glog = (t @ w_rg).astype(f32)
    gprob = jax.nn.softmax(glog, axis=-1)
    gsel = jnp.argmax(glog + b_rg.astype(f32), axis=-1)
    gw = jnp.take_along_axis(gprob, gsel[:, None], axis=-1)
    elog = (t @ w_re).astype(f32).reshape(-1, N_EGROUPS, EXPERTS_PER_GROUP)
    elog = jnp.take_along_axis(elog, gsel[:, None, None], axis=1)[:, 0]
    eprob = jax.nn.softmax(elog, axis=-1)
    _, topi = lax.top_k(elog + b_re.astype(f32)[gsel], TOP_K)
    ew = jnp.take_along_axis(eprob, topi, axis=-1)
    ew = ew / jnp.sum(ew, axis=-1, keepdims=True)
    eid = gsel[:, None] * EXPERTS_PER_GROUP + topi
    gates = jnp.sum(jax.nn.one_hot(eid, N_EXPERTS, dtype=f32) * (gw * ew)[..., None], axis=1)
    hg = jnp.einsum('nd,edf->nef', t, w_gate)
    hu = jnp.einsum('nd,edf->nef', t, w_up)
    act = jax.nn.silu(hg) * hu * gates[..., None].astype(t.dtype)
    return jnp.einsum('nef,efd->nd', act, w_down).reshape(bsz, L, D)


def trunk(x, p, ssm_in, conv_in, hg_in, W):
    f32 = jnp.float32
    lb_all = jnp.cumsum(jax.nn.softmax(W['lb_logits'].astype(f32), axis=0), axis=0)
    ssm_out, conv_out, hg_out = [], [], []
    for i in range(DEPTH):
        h = rmsnorm(x, W['norm_mix_w'][i])
        mix, conv_new, ssm_new, hg_new = token_mixer(
            h, conv_in[i], ssm_in[i], hg_in[i], lb_all[i], W['w_in'][i], W['conv_w'][i],
            W['conv_b'][i], W['dt_bias'][i], W['a_log'][i], W['d_skip'][i],
            W['ssd_norm_w'][i], W['hg_norm_w'][i], W['w_out'][i])
        x = x + mix
        x = x + hier_moe(rmsnorm(x, W['norm_ffn_w'][i]), W['w_router_group'][i],
                         W['b_router_group'][i], W['w_router_expert'][i],
                         W['b_router_expert'][i], W['w_exp_gate'][i], W['w_exp_up'][i],
                         W['w_exp_down'][i])
        gate = jax.nn.sigmoid((rmsnorm(x, W['norm_ple_w'][i]) @ W['w_ple_gate'][i]).astype(f32))
        x = x + (gate * (p[i] @ W['w_ple_proj'][i]).astype(f32)).astype(x.dtype)
        ssm_out.append(ssm_new)
        conv_out.append(conv_new)
        hg_out.append(hg_new)
    y = rmsnorm(x, W['final_norm_w'])
    return y, jnp.stack(ssm_out), jnp.stack(conv_out), jnp.stack(hg_out)


def setup_inputs(seed: int = 0) -> dict:
    key = jax.random.key(seed)
    ks = jax.random.split(key, 32)
    f32 = jnp.float32

    def nrm(k, shape, scale):
        return jax.random.normal(k, shape, f32) * scale

    dt0 = jnp.exp(jax.random.uniform(ks[10], (DEPTH, SSD_HEADS), f32,
                                     math.log(1e-3), math.log(1e-1)))
    return {
        'x_prompt': nrm(ks[0], (BATCH, SEQ, D_MODEL), 1.0),
        'x_sample': nrm(ks[1], (DEC_BATCH, DEC_SEQ, D_MODEL), 1.0),
        'state_ssm': nrm(ks[2], (DEPTH, DEC_BATCH, SSD_HEADS, SSD_HEADDIM, D_STATE), 0.5),
        'state_conv': nrm(ks[3], (DEPTH, DEC_BATCH, CONV_W - 1, CONV_DIM), 1.0),
        'state_hgrn': nrm(ks[4], (DEPTH, DEC_BATCH, HG_HEADS, HG_DK, HG_DV), 0.5),
        'p_prompt': nrm(ks[5], (DEPTH, BATCH, SEQ, PLE_DIM), 1.0),
        'p_sample': nrm(ks[6], (DEPTH, DEC_BATCH, DEC_SEQ, PLE_DIM), 1.0),
        'norm_mix_w': 1.0 + nrm(ks[7], (DEPTH, D_MODEL), 0.02),
        'w_in': nrm(ks[8], (DEPTH, D_MODEL, IN_COLS), D_MODEL ** -0.5),
        'conv_w': nrm(ks[9], (DEPTH, CONV_W, CONV_DIM), CONV_W ** -0.5),
        'conv_b': nrm(ks[11], (DEPTH, CONV_DIM), 0.02),
        'dt_bias': dt0 + jnp.log(-jnp.expm1(-dt0)),
        'a_log': jnp.log(jax.random.uniform(ks[12], (DEPTH, SSD_HEADS), f32, 1.0, 16.0)),
        'd_skip': 1.0 + nrm(ks[13], (DEPTH, SSD_HEADS), 0.1),
        'ssd_norm_w': 1.0 + nrm(ks[14], (DEPTH, SSD_D), 0.02),
        'lb_logits': nrm(ks[15], (DEPTH + 1, HG_HEADS * HG_DK), 0.1),
        'hg_norm_w': 1.0 + nrm(ks[16], (DEPTH, HG_D), 0.02),
        'w_out': nrm(ks[17], (DEPTH, D_MIX, D_MODEL), D_MIX ** -0.5),
        'norm_ffn_w': 1.0 + nrm(ks[18], (DEPTH, D_MODEL), 0.02),
        'w_router_group': nrm(ks[19], (DEPTH, D_MODEL, N_EGROUPS), D_MODEL ** -0.5),
        'b_router_group': nrm(ks[20], (DEPTH, N_EGROUPS), 0.01),
        'w_router_expert': nrm(ks[21], (DEPTH, D_MODEL, N_EXPERTS), D_MODEL ** -0.5),
        'b_router_expert': nrm(ks[22], (DEPTH, N_EGROUPS, EXPERTS_PER_GROUP), 0.01),
        'w_exp_gate': nrm(ks[23], (DEPTH, N_EXPERTS, D_MODEL, EXPERT_FF), D_MODEL ** -0.5),
        'w_exp_up': nrm(ks[24], (DEPTH, N_EXPERTS, D_MODEL, EXPERT_FF), D_MODEL ** -0.5),
        'w_exp_down': nrm(ks[25], (DEPTH, N_EXPERTS, EXPERT_FF, D_MODEL), EXPERT_FF ** -0.5),
        'norm_ple_w': 1.0 + nrm(ks[26], (DEPTH, D_MODEL), 0.02),
        'w_ple_gate': nrm(ks[27], (DEPTH, D_MODEL, D_MODEL), D_MODEL ** -0.5),
        'w_ple_proj': nrm(ks[28], (DEPTH, PLE_DIM, D_MODEL), PLE_DIM ** -0.5),
        'final_norm_w': 1.0 + nrm(ks[29], (D_MODEL,), 0.02),
    }


def reference(x_prompt, x_sample, state_ssm, state_conv, state_hgrn, p_prompt, p_sample,
              norm_mix_w, w_in, conv_w, conv_b, dt_bias, a_log, d_skip, ssd_norm_w, lb_logits,
              hg_norm_w, w_out, norm_ffn_w, w_router_group, b_router_group, w_router_expert,
              b_router_expert, w_exp_gate, w_exp_up, w_exp_down, norm_ple_w, w_ple_gate,
              w_ple_proj, final_norm_w):
    W = {
        'norm_mix_w': norm_mix_w, 'w_in': w_in, 'conv_w': conv_w, 'conv_b': conv_b,
        'dt_bias': dt_bias, 'a_log': a_log, 'd_skip': d_skip, 'ssd_norm_w': ssd_norm_w,
        'lb_logits': lb_logits, 'hg_norm_w': hg_norm_w, 'w_out': w_out,
        'norm_ffn_w': norm_ffn_w, 'w_router_group': w_router_group,
        'b_router_group': b_router_group, 'w_router_expert': w_router_expert,
        'b_router_expert': b_router_expert, 'w_exp_gate': w_exp_gate, 'w_exp_up': w_exp_up,
        'w_exp_down': w_exp_down, 'norm_ple_w': norm_ple_w, 'w_ple_gate': w_ple_gate,
        'w_ple_proj': w_ple_proj, 'final_norm_w': final_norm_w,
    }
    bp = x_prompt.shape[0]
    dtp = x_prompt.dtype
    ssm0 = jnp.zeros((DEPTH, bp, SSD_HEADS, SSD_HEADDIM, D_STATE), dtp)
    conv0 = jnp.zeros((DEPTH, bp, CONV_W - 1, CONV_DIM), dtp)
    hg0 = jnp.zeros((DEPTH, bp, HG_HEADS, HG_DK, HG_DV), dtp)
    y_prompt, ssm_p, conv_p, hg_p = trunk(x_prompt, p_prompt, ssm0, conv0, hg0, W)
    y_sample, ssm_s, conv_s, hg_s = trunk(x_sample, p_sample, state_ssm, state_conv,
                                          state_hgrn, W)
    return (y_prompt, y_sample, ssm_p, conv_p, hg_p, ssm_s, conv_s, hg_s)
```

```python
import functools

import jax
import jax.numpy as jnp
from jax import lax
from jax.experimental import pallas as pl
from jax.experimental.pallas import tpu as pltpu

F32 = jnp.float32
BF16 = jnp.bfloat16

EPS = 1e-6
SSD_HEADS = 16
SSD_HEADDIM = 64
SSD_D = SSD_HEADS * SSD_HEADDIM
SSD_GROUPS = 2
D_STATE = 128
CONV_W = 4
CONV_DIM = SSD_D + 2 * SSD_GROUPS * D_STATE
HG_HEADS = 8
HG_DK = 128
HG_DV = 128
HG_D = HG_HEADS * HG_DV
N_EGROUPS = 4
EXPERTS_PER_GROUP = 4
N_EXPERTS = N_EGROUPS * EXPERTS_PER_GROUP

LANES = 128
SUBLANES = 8
VMEM_LIMIT = 48 * 1024 * 1024
CHUNK = 128
HG_SAFE_LOG_DECAY = 60.0
SEG_WIDTHS = (SSD_D, CONV_DIM, HG_D, HG_D, HG_D, HG_D, LANES)


def _dot(a, b):
    return jnp.dot(a, b, preferred_element_type=F32)


def _dot_nt(a, b):
    return lax.dot_general(a, b, (((1,), (1,)), ((), ())), preferred_element_type=F32)


def _dot_tn(a, b):
    return lax.dot_general(a, b, (((0,), (0,)), ((), ())), preferred_element_type=F32)


def _rms_unit(x):
    return x * lax.rsqrt(jnp.mean(x * x, axis=-1, keepdims=True) + EPS)


def _sigmoid(x):
    return 1.0 / (1.0 + jnp.exp(-x))


def _silu(x):
    return x * _sigmoid(x)


def _split3(x):
    hi = x.astype(BF16)
    r = x - hi.astype(F32)
    mid = r.astype(BF16)
    lo = (r - mid.astype(F32)).astype(BF16)
    return hi, mid, lo


def _cumsum_rows(x, tri):
    hi, mid, lo = _split3(x)
    return _dot(tri, hi) + _dot(tri, mid) + _dot(tri, lo)


def _tri(n):
    r = lax.broadcasted_iota(jnp.int32, (n, n), 0)
    c = lax.broadcasted_iota(jnp.int32, (n, n), 1)
    return r >= c


def _load_rows(ref, qb, rows):
    if qb == rows:
        return ref[0]
    assert qb == 1
    x = jnp.broadcast_to(ref[0], (rows, ref.shape[-1]))
    r = lax.broadcasted_iota(jnp.int32, x.shape, 0)
    return jnp.where(r < qb, x, 0.0)


def _cparams(*sem):
    return pltpu.CompilerParams(dimension_semantics=sem, vmem_limit_bytes=VMEM_LIMIT)


def _const_spec(shape):
    nd = len(shape)
    return pl.BlockSpec(shape, lambda *_: (0,) * nd)


def _inproj_kernel(x_ref, nw_ref, w_ref, *out_refs):
    hb = (_rms_unit(x_ref[...]) * nw_ref[...]).astype(BF16)
    off = 0
    for ref in out_refs:
        n = ref.shape[-1]
        for c0 in range(0, n, 512):
            cw = min(512, n - c0)
            ref[:, c0:c0 + cw] = _dot(hb, w_ref[:, off + c0:off + c0 + cw])
        off += n


def _in_proj(x, nw, w, tm):
    n, d = x.shape
    return pl.pallas_call(
        _inproj_kernel,
        grid=(n // tm,),
        in_specs=[pl.BlockSpec((tm, d), lambda i: (i, 0)), _const_spec(nw.shape),
                  _const_spec(w.shape)],
        out_specs=[pl.BlockSpec((tm, s), lambda i: (i, 0)) for s in SEG_WIDTHS],
        out_shape=[jax.ShapeDtypeStruct((n, s), F32) for s in SEG_WIDTHS],
        compiler_params=_cparams("arbitrary"),
    )(x, nw, w)


def _ssd_kernel(*refs, qb, rows, has_init):
    if has_init:
        (z_ref, xbc_ref, dt_ref, cw_ref, cb_ref, dtb_ref, alog_ref, dsk_ref, nw_ref,
         cinit_ref, sinit_ref, y_ref, cout_ref, sout_ref, xc_ref, ys_ref) = refs
    else:
        (z_ref, xbc_ref, dt_ref, cw_ref, cb_ref, dtb_ref, alog_ref, dsk_ref, nw_ref,
         y_ref, cout_ref, sout_ref, xc_ref, ys_ref) = refs
    hist = SUBLANES
    c = pl.program_id(1)

    @pl.when(c == 0)
    def _():
        xc_ref[0:hist, :] = jnp.zeros((hist, CONV_DIM), F32)
        if has_init:
            xc_ref[hist - (CONV_W - 1):hist, :] = cinit_ref[0]
            sout_ref[...] = sinit_ref[...]
        else:
            sout_ref[...] = jnp.zeros(sout_ref.shape, F32)

    xc_ref[hist:hist + rows, :] = _load_rows(xbc_ref, qb, rows)
    base = hist - (CONV_W - 1)
    acc = cb_ref[...] + cw_ref[0:1, :] * xc_ref[base:base + rows, :]
    for k in range(1, CONV_W):
        acc = acc + cw_ref[k:k + 1, :] * xc_ref[base + k:base + k + rows, :]
    cout_ref[0] = xc_ref[base + qb:base + qb + CONV_W - 1, :]
    xc_ref[0:hist, :] = xc_ref[rows:rows + hist, :]
    xbc = _silu(acc)
    xs = xbc[:, :SSD_D]
    gn = D_STATE
    b_bf = [xbc[:, SSD_D + g * gn:SSD_D + (g + 1) * gn].astype(BF16) for g in range(SSD_GROUPS)]
    c_f32 = [xbc[:, SSD_D + (SSD_GROUPS + g) * gn:SSD_D + (SSD_GROUPS + g + 1) * gn]
             for g in range(SSD_GROUPS)]

    dtr = _load_rows(dt_ref, qb, rows) + dtb_ref[...]
    dt = jnp.maximum(dtr, 0.0) + jnp.log(1.0 + jnp.exp(-jnp.abs(dtr)))
    if qb < rows:
        dt = jnp.where(lax.broadcasted_iota(jnp.int32, dt.shape, 0) < qb, dt, 0.0)
    a = dt * (-jnp.exp(alog_ref[...]))
    tri = _tri(rows)
    a_cs = _cumsum_rows(a, tri.astype(BF16))
    pad = max(rows, LANES)
    a_sq = a_cs if rows == pad else jnp.concatenate(
        [a_cs, jnp.zeros((pad - rows, LANES), F32)], axis=0)
    a_cs_t = a_sq.T

    scores = [_dot_nt(c_f32[g].astype(BF16), b_bf[g]) for g in range(SSD_GROUPS)]
    heads_per_group = SSD_HEADS // SSD_GROUPS
    for h in range(SSD_HEADS):
        g = h // heads_per_group
        lo, hi = h * SSD_HEADDIM, (h + 1) * SSD_HEADDIM
        acol = a_cs[:, h:h + 1]
        arow = a_cs_t[h:h + 1, :rows]
        decay = jnp.exp(jnp.where(tri, acol - arow, -jnp.inf))
        m = (scores[g] * decay).astype(BF16)
        xs_h = xs[:, lo:hi]
        xdt = xs_h * dt[:, h:h + 1]
        st = sout_ref[0, h]
        y = _dot(m, xdt.astype(BF16))
        y = y + _dot_nt((c_f32[g] * jnp.exp(acol)).astype(BF16), st.astype(BF16))
        alast = a_cs[rows - 1:rows, h:h + 1]
        upd = _dot_tn((xdt * jnp.exp(alast - acol)).astype(BF16), b_bf[g])
        sout_ref[0, h] = jnp.exp(alast) * st + upd
        ys_ref[:, lo:hi] = y + dsk_ref[:, lo:hi] * xs_h

    y = ys_ref[...] * _silu(_load_rows(z_ref, qb, rows))
    gw = SSD_D // SSD_GROUPS
    for g in range(SSD_GROUPS):
        seg = _rms_unit(y[:, g * gw:(g + 1) * gw]) * nw_ref[:, g * gw:(g + 1) * gw]
        y_ref[0, :, g * gw:(g + 1) * gw] = seg[:qb].astype(y_ref.dtype)


def _ssd(z, xbc, dt, conv_w, conv_b, dt_bias, a_log, d_skip, norm_w, conv_init, ssm_init):
    b, l, _ = z.shape
    qb = min(CHUNK, l)
    assert l % qb == 0
    rows = max(qb, SUBLANES)
    has_init = ssm_init is not None
    seq = lambda w: pl.BlockSpec((1, qb, w), lambda i, c: (i, c, 0))
    per_b3 = lambda s: pl.BlockSpec((1,) + s, lambda i, c: (i, 0, 0))
    per_b4 = lambda s: pl.BlockSpec((1,) + s, lambda i, c: (i, 0, 0, 0))
    params = (conv_w, conv_b, dt_bias, a_log, d_skip, norm_w)
    ins = [z, xbc, dt, *params]
    in_specs = [seq(SSD_D), seq(CONV_DIM), seq(LANES)] + [_const_spec(p.shape) for p in params]
    state_shape = (SSD_HEADS, SSD_HEADDIM, D_STATE)
    if has_init:
        ins += [conv_init, ssm_init]
        in_specs += [per_b3((CONV_W - 1, CONV_DIM)), per_b4(state_shape)]
    return pl.pallas_call(
        functools.partial(_ssd_kernel, qb=qb, rows=rows, has_init=has_init),
        grid=(b, l // qb),
        in_specs=in_specs,
        out_specs=[seq(SSD_D), per_b3((CONV_W - 1, CONV_DIM)), per_b4(state_shape)],
        out_shape=[jax.ShapeDtypeStruct((b, l, SSD_D), BF16),
                   jax.ShapeDtypeStruct((b, CONV_W - 1, CONV_DIM), F32),
                   jax.ShapeDtypeStruct((b,) + state_shape, F32)],
        scratch_shapes=[pltpu.VMEM((rows + SUBLANES, CONV_DIM), F32),
                        pltpu.VMEM((rows, SSD_D), F32)],
        compiler_params=_cparams("arbitrary", "arbitrary"),
    )(*ins)


def _hgrn_kernel(*refs, qb, rows, has_init):
    if has_init:
        (q_ref, f_ref, v_ref, g_ref, lb_ref, nw_ref, sinit_ref,
         o_ref, sout_ref, st_ref, a_ref, kk_ref, gcs_ref) = refs
    else:
        (q_ref, f_ref, v_ref, g_ref, lb_ref, nw_ref,
         o_ref, sout_ref, st_ref, a_ref, kk_ref, gcs_ref) = refs
    c = pl.program_id(1)
    dk = HG_DK

    @pl.when(c == 0)
    def _():
        for h in range(HG_HEADS):
            if has_init:
                st_ref[h] = sinit_ref[0, h].T
            else:
                st_ref[h] = jnp.zeros((HG_DV, HG_DK), F32)

    fr = _load_rows(f_ref, qb, rows)
    lb = lb_ref[...]
    logf = jnp.log(lb + (1.0 - lb) * _sigmoid(fr))
    kk = (1.0 - lb) * _sigmoid(-fr)
    if qb < rows:
        valid = lax.broadcasted_iota(jnp.int32, fr.shape, 0) < qb
        logf = jnp.where(valid, logf, 0.0)
        kk = jnp.where(valid, kk, 0.0)
    tri = _tri(rows)
    gcs = _cumsum_rows(logf, tri.astype(BF16))
    glast = gcs[rows - 1:rows, :]
    q = _load_rows(q_ref, qb, rows)
    qt = q * jnp.exp(gcs)
    kk_ref[...] = kk
    gcs_ref[...] = gcs

    safe = jnp.min(glast) >= -HG_SAFE_LOG_DECAY

    @pl.when(safe)
    def _():
        kt = kk * jnp.exp(-gcs)
        for h in range(HG_HEADS):
            sl = slice(h * dk, (h + 1) * dk)
            a_ref[h] = _dot_nt(qt[:, sl].astype(BF16), kt[:, sl].astype(BF16))

    @pl.when(jnp.logical_not(safe))
    def _():
        ri = lax.broadcasted_iota(jnp.int32, (rows, dk), 0)
        ci = lax.broadcasted_iota(jnp.int32, (rows, rows), 1)
        for h in range(HG_HEADS):
            sl = slice(h * dk, (h + 1) * dk)
            q_h = q[:, sl]
            g_h = gcs[:, sl]

            def cols(j8, a_h):
                base = pl.multiple_of(j8 * SUBLANES, SUBLANES)
                k_blk = kk_ref[pl.ds(base, SUBLANES), sl]
                g_blk = gcs_ref[pl.ds(base, SUBLANES), sl]
                for r in range(SUBLANES):
                    j = base + r
                    t = q_h * k_blk[r:r + 1] * jnp.exp(
                        jnp.where(ri >= j, g_h - g_blk[r:r + 1], -jnp.inf))
                    a_h = jnp.where(ci == j, jnp.sum(t, axis=-1, keepdims=True), a_h)
                return a_h

            a_ref[h] = lax.fori_loop(0, rows // SUBLANES, cols, jnp.zeros((rows, rows), F32))

    khat = kk * jnp.exp(glast - gcs)
    v = _load_rows(v_ref, qb, rows)
    gate = _silu(_load_rows(g_ref, qb, rows))
    for h in range(HG_HEADS):
        sl = slice(h * dk, (h + 1) * dk)
        st = st_ref[h]
        v_bf = v[:, sl].astype(BF16)
        a_h = jnp.where(tri, a_ref[h], 0.0).astype(BF16)
        o = _dot(a_h, v_bf) + _dot_nt(qt[:, sl].astype(BF16), st.astype(BF16))
        st_ref[h] = st * jnp.exp(glast[:, sl]) + _dot_tn(v_bf, khat[:, sl].astype(BF16))
        o = _rms_unit(o) * nw_ref[:, sl] * gate[:, sl]
        o_ref[0, :, sl] = o[:qb].astype(o_ref.dtype)

    @pl.when(c == pl.num_programs(1) - 1)
    def _():
        for h in range(HG_HEADS):
            sout_ref[0, h] = st_ref[h].T


def _hgrn(q, f, v, g, lb, norm_w, init):
    b, l, _ = q.shape
    qb = min(CHUNK, l)
    assert l % qb == 0
    rows = max(qb, SUBLANES)
    has_init = init is not None
    seq = pl.BlockSpec((1, qb, HG_D), lambda i, c: (i, c, 0))
    state_shape = (HG_HEADS, HG_DK, HG_DV)
    per_b = pl.BlockSpec((1,) + state_shape, lambda i, c: (i, 0, 0, 0))
    ins = [q, f, v, g, lb, norm_w]
    in_specs = [seq, seq, seq, seq, _const_spec(lb.shape), _const_spec(norm_w.shape)]
    if has_init:
        ins.append(init)
        in_specs.append(per_b)
    return pl.pallas_call(
        functools.partial(_hgrn_kernel, qb=qb, rows=rows, has_init=has_init),
        grid=(b, l // qb),
        in_specs=in_specs,
        out_specs=[seq, per_b],
        out_shape=[jax.ShapeDtypeStruct((b, l, HG_D), BF16),
                   jax.ShapeDtypeStruct((b,) + state_shape, F32)],
        scratch_shapes=[pltpu.VMEM((HG_HEADS, HG_DV, HG_DK), F32),
                        pltpu.VMEM((HG_HEADS, rows, rows), F32),
                        pltpu.VMEM((rows, HG_D), F32),
                        pltpu.VMEM((rows, HG_D), F32)],
        compiler_params=_cparams("arbitrary", "arbitrary"),
    )(*ins)


def _route(logits, bias):
    lane = lax.broadcasted_iota(jnp.int32, logits.shape, 1)
    ninf = -jnp.inf
    big = jnp.int32(LANES)
    is_g = (lane >= N_EXPERTS) & (lane < N_EXPERTS + N_EGROUPS)
    rmax = lambda t: jnp.max(t, axis=-1, keepdims=True)
    rsum = lambda t: jnp.sum(t, axis=-1, keepdims=True)
    first = lambda m: jnp.min(jnp.where(m, lane, big), axis=-1, keepdims=True)

    gl = jnp.where(is_g, logits, ninf)
    gp = jnp.exp(gl - rmax(gl))
    gprob = gp / rsum(gp)
    gb = jnp.where(is_g, logits + bias, ninf)
    gsel = first(gb == rmax(gb))
    gw = rsum(jnp.where(lane == gsel, gprob, 0.0))
    e0 = (gsel - N_EXPERTS) * EXPERTS_PER_GROUP
    in_grp = (lane >= e0) & (lane < e0 + EXPERTS_PER_GROUP)
    el = jnp.where(in_grp, logits, ninf)
    ep = jnp.exp(el - rmax(el))
    eprob = ep / rsum(ep)
    eb = jnp.where(in_grp, logits + bias, ninf)
    i1 = first(eb == rmax(eb))
    eb2 = jnp.where(lane == i1, ninf, eb)
    i2 = first(eb2 == rmax(eb2))
    p1 = rsum(jnp.where(lane == i1, eprob, 0.0))
    p2 = rsum(jnp.where(lane == i2, eprob, 0.0))
    den = p1 + p2
    return (jnp.where(lane == i1, gw * (p1 / den), 0.0)
            + jnp.where(lane == i2, gw * (p2 / den), 0.0))


def _outproj_kernel(x_ref, y_ref, o_ref, w_ref, nw_ref, rhi_ref, rlo_ref, rb_ref,
                    x1_ref, h_ref, gates_ref):
    dy = y_ref.shape[-1]
    mix = _dot(y_ref[...], w_ref[:dy, :]) + _dot(o_ref[...], w_ref[dy:, :])
    x1 = x_ref[...] + mix
    x1_ref[...] = x1
    h = _rms_unit(x1) * nw_ref[...]
    h_hi = h.astype(BF16)
    h_ref[...] = h_hi
    h_lo = (h - h_hi.astype(F32)).astype(BF16)
    logits = _dot(h_hi, rhi_ref[...]) + _dot(h_hi, rlo_ref[...]) + _dot(h_lo, rhi_ref[...])
    gates_ref[...] = _route(logits, rb_ref[...])


def _out_proj(x, y, o, w, nw, r_hi, r_lo, r_bias, tm):
    n, d = x.shape
    row = lambda width: pl.BlockSpec((tm, width), lambda i: (i, 0))
    consts = (w, nw, r_hi, r_lo, r_bias)
    return pl.pallas_call(
        _outproj_kernel,
        grid=(n // tm,),
        in_specs=[row(d), row(y.shape[-1]), row(o.shape[-1])] + [_const_spec(c.shape) for c in consts],
        out_specs=[row(d), row(d), row(LANES)],
        out_shape=[jax.ShapeDtypeStruct((n, d), F32), jax.ShapeDtypeStruct((n, d), BF16),
                   jax.ShapeDtypeStruct((n, LANES), F32)],
        compiler_params=_cparams("arbitrary"),
    )(x, y, o, *consts)


def _moe_kernel(x1_ref, h_ref, gates_ref, wg_ref, wu_ref, wd_ref, out_ref):
    e = pl.program_id(1)

    @pl.when(e == 0)
    def _():
        out_ref[...] = x1_ref[...]

    gates = gates_ref[...]
    lane = lax.broadcasted_iota(jnp.int32, gates.shape, 1)
    gcol = jnp.sum(jnp.where(lane == e, gates, 0.0), axis=-1, keepdims=True)
    h = h_ref[...]
    act = _silu(_dot(h, wg_ref[0])) * _dot(h, wu_ref[0]) * gcol
    out_ref[...] += _dot(act.astype(BF16), wd_ref[0])


def _moe(x1, h, gates, wg, wu, wd, tm):
    n, d = x1.shape
    ne, _, ff = wg.shape
    row = lambda width: pl.BlockSpec((tm, width), lambda i, e: (i, 0))
    return pl.pallas_call(
        _moe_kernel,
        grid=(n // tm, ne),
        in_specs=[row(d), row(d), row(LANES),
                  pl.BlockSpec((1, d, ff), lambda i, e: (e, 0, 0)),
                  pl.BlockSpec((1, d, ff), lambda i, e: (e, 0, 0)),
                  pl.BlockSpec((1, ff, d), lambda i, e: (e, 0, 0))],
        out_specs=row(d),
        out_shape=jax.ShapeDtypeStruct((n, d), F32),
        compiler_params=_cparams("arbitrary", "arbitrary"),
    )(x1, h, gates, wg, wu, wd)


def _ple_kernel(x_ref, p_ref, npw_ref, wg_ref, wp_ref, fw_ref, y_ref):
    x = x_ref[...]
    hn = (_rms_unit(x) * npw_ref[...]).astype(BF16)
    gate = _sigmoid(_dot(hn, wg_ref[...]))
    x = x + gate * _dot(p_ref[...].astype(BF16), wp_ref[...])
    y_ref[...] = _rms_unit(x) * fw_ref[...]


def _ple(x, p, npw, wg, wp, fw, tm):
    n, d = x.shape
    consts = (npw, wg, wp, fw)
    return pl.pallas_call(
        _ple_kernel,
        grid=(n // tm,),
        in_specs=[pl.BlockSpec((tm, d), lambda i: (i, 0)),
                  pl.BlockSpec((tm, p.shape[-1]), lambda i: (i, 0))]
                 + [_const_spec(c.shape) for c in consts],
        out_specs=pl.BlockSpec((tm, d), lambda i: (i, 0)),
        out_shape=jax.ShapeDtypeStruct((n, d), F32),
        compiler_params=_cparams("arbitrary"),
    )(x, p, *consts)


def _prepare_weights(norm_mix_w, w_in, conv_w, conv_b, dt_bias, a_log, d_skip, ssd_norm_w,
                     lb_logits, hg_norm_w, w_out, norm_ffn_w, w_router_group, b_router_group,
                     w_router_expert, b_router_expert, w_exp_gate, w_exp_up, w_exp_down,
                     norm_ple_w, w_ple_gate, w_ple_proj, final_norm_w, layer):
    i = layer
    row = lambda t: t.reshape(1, -1).astype(F32)
    lane_pad = lambda t: jnp.pad(t, [(0, 0)] * (t.ndim - 1) + [(0, LANES - t.shape[-1])])
    splits = (SSD_D, CONV_DIM, SSD_HEADS, HG_D, HG_D, HG_D, HG_D)
    offs = [0]
    for s in splits:
        offs.append(offs[-1] + s)
    seg = lambda k: w_in[i][:, offs[k]:offs[k + 1]]
    w_in_r = jnp.concatenate([seg(0), seg(1), seg(3), seg(4), seg(5), seg(6), lane_pad(seg(2))],
                             axis=1).astype(BF16)
    w_router = lane_pad(jnp.concatenate([w_router_expert[i], w_router_group[i]], axis=1))
    r_hi = w_router.astype(BF16)
    r_lo = (w_router - r_hi.astype(F32)).astype(BF16)
    r_bias = lane_pad(jnp.concatenate([b_router_expert[i].reshape(-1), b_router_group[i]]).reshape(1, -1))
    lb = jnp.cumsum(jax.nn.softmax(lb_logits.astype(F32), axis=0), axis=0)[i]
    return dict(
        norm_mix_w=row(norm_mix_w[i]), w_in=w_in_r, conv_w=conv_w[i].astype(F32),
        conv_b=row(conv_b[i]), dt_bias=lane_pad(row(dt_bias[i])), a_log=lane_pad(row(a_log[i])),
        d_skip=row(jnp.repeat(d_skip[i], SSD_HEADDIM)), ssd_norm_w=row(ssd_norm_w[i]),
        lb=row(lb), hg_norm_w=row(hg_norm_w[i]), w_out=w_out[i].astype(BF16),
        norm_ffn_w=row(norm_ffn_w[i]), r_hi=r_hi, r_lo=r_lo, r_bias=r_bias,
        w_exp_gate=w_exp_gate[i].astype(BF16), w_exp_up=w_exp_up[i].astype(BF16),
        w_exp_down=w_exp_down[i].astype(BF16), norm_ple_w=row(norm_ple_w[i]),
        w_ple_gate=w_ple_gate[i].astype(BF16), w_ple_proj=w_ple_proj[i].astype(BF16),
        final_norm_w=row(final_norm_w))


def _token_tile(n):
    for tm in (512, 256, 128, 64, 32, 16, 8):
        if n % tm == 0:
            return tm
    raise ValueError(f"token count {n} must be a multiple of 8")


def _trunk(x, p, ssm_in, conv_in, hg_in, w):
    b, l, d = x.shape
    n = b * l
    tm = _token_tile(n)
    tm_small = min(tm, 256)
    x2d = x.reshape(n, d)
    z, xbc, q, f, v, g, dt = _in_proj(x2d, w["norm_mix_w"], w["w_in"], tm_small)
    seq = lambda t: t.reshape(b, l, t.shape[-1])
    y, conv_new, ssm_new = _ssd(seq(z), seq(xbc), seq(dt), w["conv_w"], w["conv_b"], w["dt_bias"],
                                w["a_log"], w["d_skip"], w["ssd_norm_w"], conv_in, ssm_in)
    o, hg_new = _hgrn(seq(q), seq(f), seq(v), seq(g), w["lb"], w["hg_norm_w"], hg_in)
    x1, h, gates = _out_proj(x2d, y.reshape(n, -1), o.reshape(n, -1), w["w_out"], w["norm_ffn_w"],
                             w["r_hi"], w["r_lo"], w["r_bias"], tm_small)
    x2 = _moe(x1, h, gates, w["w_exp_gate"], w["w_exp_up"], w["w_exp_down"], tm)
    y_out = _ple(x2, p.reshape(n, -1), w["norm_ple_w"], w["w_ple_gate"], w["w_ple_proj"],
                 w["final_norm_w"], tm_small)
    return y_out.reshape(b, l, d), ssm_new[None], conv_new[None], hg_new[None]


def kernel(x_prompt, x_sample, state_ssm, state_conv, state_hgrn, p_prompt, p_sample, norm_mix_w, w_in, conv_w, conv_b, dt_bias, a_log, d_skip, ssd_norm_w, lb_logits, hg_norm_w, w_out, norm_ffn_w, w_router_group, b_router_group, w_router_expert, b_router_expert, w_exp_gate, w_exp_up, w_exp_down, norm_ple_w, w_ple_gate, w_ple_proj, final_norm_w):
    assert p_prompt.shape[0] == 1, "the per-layer-embedding kernel also applies the final norm: depth 1 only"
    w = _prepare_weights(norm_mix_w, w_in, conv_w, conv_b, dt_bias, a_log, d_skip, ssd_norm_w,
                         lb_logits, hg_norm_w, w_out, norm_ffn_w, w_router_group, b_router_group,
                         w_router_expert, b_router_expert, w_exp_gate, w_exp_up, w_exp_down,
                         norm_ple_w, w_ple_gate, w_ple_proj, final_norm_w, layer=0)
    y_p, ssm_p, conv_p, hg_p = _trunk(x_prompt, p_prompt[0], None, None, None, w)
    y_s, ssm_s, conv_s, hg_s = _trunk(x_sample, p_sample[0], state_ssm[0], state_conv[0],
                                      state_hgrn[0], w)
    return (y_p, y_s, ssm_p, conv_p, hg_p, ssm_s, conv_s, hg_s)
```

```python
import functools

import jax
import jax.numpy as jnp
from jax import lax
from jax.experimental import pallas as pl
from jax.experimental.pallas import tpu as pltpu

F32 = jnp.float32
BF16 = jnp.bfloat16

EPS = 1e-6
SSD_HEADS = 16
SSD_HEADDIM = 64
SSD_D = SSD_HEADS * SSD_HEADDIM
SSD_GROUPS = 2
D_STATE = 128
CONV_W = 4
CONV_DIM = SSD_D + 2 * SSD_GROUPS * D_STATE
HG_HEADS = 8
HG_DK = 128
HG_DV = 128
HG_D = HG_HEADS * HG_DV
N_EGROUPS = 4
EXPERTS_PER_GROUP = 4
N_EXPERTS = N_EGROUPS * EXPERTS_PER_GROUP

LANES = 128
SUBLANES = 8
VMEM_LIMIT = 48 * 1024 * 1024
CHUNK = 128
HG_SAFE_LOG_DECAY = 60.0
GID_LANE = N_EXPERTS + N_EGROUPS
MOE_TILE = 256
SORTED_MOE_MIN_TOKENS = 8 * SUBLANES * LANES
SEG_WIDTHS = (SSD_D, CONV_DIM, HG_D, HG_D, HG_D, HG_D, LANES)


def _dot(a, b):
    return jnp.dot(a, b, preferred_element_type=F32)


def _dot_nt(a, b):
    return lax.dot_general(a, b, (((1,), (1,)), ((), ())), preferred_element_type=F32)


def _dot_tn(a, b):
    return lax.dot_general(a, b, (((0,), (0,)), ((), ())), preferred_element_type=F32)


def _rms_unit(x):
    return x * lax.rsqrt(jnp.mean(x * x, axis=-1, keepdims=True) + EPS)


def _sigmoid(x):
    return 1.0 / (1.0 + jnp.exp(-x))


def _silu(x):
    return x * _sigmoid(x)


def _split3(x):
    hi = x.astype(BF16)
    r = x - hi.astype(F32)
    mid = r.astype(BF16)
    lo = (r - mid.astype(F32)).astype(BF16)
    return hi, mid, lo


def _cumsum_rows(x, tri):
    hi, mid, lo = _split3(x)
    return _dot(tri, hi) + _dot(tri, mid) + _dot(tri, lo)


def _tri(n):
    r = lax.broadcasted_iota(jnp.int32, (n, n), 0)
    c = lax.broadcasted_iota(jnp.int32, (n, n), 1)
    return r >= c


def _load_rows(ref, qb, rows):
    if qb == rows:
        return ref[0]
    assert qb == 1
    x = jnp.broadcast_to(ref[0], (rows, ref.shape[-1]))
    r = lax.broadcasted_iota(jnp.int32, x.shape, 0)
    return jnp.where(r < qb, x, 0.0)


def _cparams(*sem):
    return pltpu.CompilerParams(dimension_semantics=sem, vmem_limit_bytes=VMEM_LIMIT)


def _const_spec(shape):
    nd = len(shape)
    return pl.BlockSpec(shape, lambda *_: (0,) * nd)


def _inproj_kernel(x_ref, nw_ref, w_ref, *out_refs):
    hb = (_rms_unit(x_ref[...]) * nw_ref[...]).astype(BF16)
    off = 0
    for ref in out_refs:
        n = ref.shape[-1]
        for c0 in range(0, n, 512):
            cw = min(512, n - c0)
            ref[:, c0:c0 + cw] = _dot(hb, w_ref[:, off + c0:off + c0 + cw])
        off += n


def _in_proj(x, nw, w, tm):
    n, d = x.shape
    return pl.pallas_call(
        _inproj_kernel,
        grid=(n // tm,),
        in_specs=[pl.BlockSpec((tm, d), lambda i: (i, 0)), _const_spec(nw.shape),
                  _const_spec(w.shape)],
        out_specs=[pl.BlockSpec((tm, s), lambda i: (i, 0)) for s in SEG_WIDTHS],
        out_shape=[jax.ShapeDtypeStruct((n, s), F32) for s in SEG_WIDTHS],
        compiler_params=_cparams("arbitrary"),
    )(x, nw, w)


def _ssd_kernel(*refs, qb, rows, has_init):
    if has_init:
        (z_ref, xbc_ref, dt_ref, cw_ref, cb_ref, dtb_ref, alog_ref, dsk_ref, nw_ref,
         cinit_ref, sinit_ref, y_ref, cout_ref, sout_ref, xc_ref, ys_ref) = refs
    else:
        (z_ref, xbc_ref, dt_ref, cw_ref, cb_ref, dtb_ref, alog_ref, dsk_ref, nw_ref,
         y_ref, cout_ref, sout_ref, xc_ref, ys_ref) = refs
    hist = SUBLANES
    c = pl.program_id(1)

    @pl.when(c == 0)
    def _():
        xc_ref[0:hist, :] = jnp.zeros((hist, CONV_DIM), F32)
        if has_init:
            xc_ref[hist - (CONV_W - 1):hist, :] = cinit_ref[0]
            sout_ref[...] = sinit_ref[...]
        else:
            sout_ref[...] = jnp.zeros(sout_ref.shape, F32)

    xc_ref[hist:hist + rows, :] = _load_rows(xbc_ref, qb, rows)
    base = hist - (CONV_W - 1)
    acc = cb_ref[...] + cw_ref[0:1, :] * xc_ref[base:base + rows, :]
    for k in range(1, CONV_W):
        acc = acc + cw_ref[k:k + 1, :] * xc_ref[base + k:base + k + rows, :]
    cout_ref[0] = xc_ref[base + qb:base + qb + CONV_W - 1, :]
    xc_ref[0:hist, :] = xc_ref[rows:rows + hist, :]
    xbc = _silu(acc)
    xs = xbc[:, :SSD_D]
    gn = D_STATE
    b_bf = [xbc[:, SSD_D + g * gn:SSD_D + (g + 1) * gn].astype(BF16) for g in range(SSD_GROUPS)]
    c_f32 = [xbc[:, SSD_D + (SSD_GROUPS + g) * gn:SSD_D + (SSD_GROUPS + g + 1) * gn]
             for g in range(SSD_GROUPS)]

    dtr = _load_rows(dt_ref, qb, rows) + dtb_ref[...]
    dt = jnp.maximum(dtr, 0.0) + jnp.log(1.0 + jnp.exp(-jnp.abs(dtr)))
    if qb < rows:
        dt = jnp.where(lax.broadcasted_iota(jnp.int32, dt.shape, 0) < qb, dt, 0.0)
    a = dt * (-jnp.exp(alog_ref[...]))
    tri = _tri(rows)
    a_cs = _cumsum_rows(a, tri.astype(BF16))
    pad = max(rows, LANES)
    a_sq = a_cs if rows == pad else jnp.concatenate(
        [a_cs, jnp.zeros((pad - rows, LANES), F32)], axis=0)
    a_cs_t = a_sq.T

    scores = [_dot_nt(c_f32[g].astype(BF16), b_bf[g]) for g in range(SSD_GROUPS)]
    heads_per_group = SSD_HEADS // SSD_GROUPS
    for h in range(SSD_HEADS):
        g = h // heads_per_group
        lo, hi = h * SSD_HEADDIM, (h + 1) * SSD_HEADDIM
        acol = a_cs[:, h:h + 1]
        arow = a_cs_t[h:h + 1, :rows]
        decay = jnp.exp(jnp.where(tri, acol - arow, -jnp.inf))
        m = (scores[g] * decay).astype(BF16)
        xs_h = xs[:, lo:hi]
        xdt = xs_h * dt[:, h:h + 1]
        st = sout_ref[0, h]
        y = _dot(m, xdt.astype(BF16))
        y = y + _dot_nt((c_f32[g] * jnp.exp(acol)).astype(BF16), st.astype(BF16))
        alast = a_cs[rows - 1:rows, h:h + 1]
        upd = _dot_tn((xdt * jnp.exp(alast - acol)).astype(BF16), b_bf[g])
        sout_ref[0, h] = jnp.exp(alast) * st + upd
        ys_ref[:, lo:hi] = y + dsk_ref[:, lo:hi] * xs_h

    y = ys_ref[...] * _silu(_load_rows(z_ref, qb, rows))
    gw = SSD_D // SSD_GROUPS
    for g in range(SSD_GROUPS):
        seg = _rms_unit(y[:, g * gw:(g + 1) * gw]) * nw_ref[:, g * gw:(g + 1) * gw]
        y_ref[0, :, g * gw:(g + 1) * gw] = seg[:qb].astype(y_ref.dtype)


def _ssd(z, xbc, dt, conv_w, conv_b, dt_bias, a_log, d_skip, norm_w, conv_init, ssm_init):
    b, l, _ = z.shape
    qb = min(CHUNK, l)
    assert l % qb == 0
    rows = max(qb, SUBLANES)
    has_init = ssm_init is not None
    seq = lambda w: pl.BlockSpec((1, qb, w), lambda i, c: (i, c, 0))
    per_b3 = lambda s: pl.BlockSpec((1,) + s, lambda i, c: (i, 0, 0))
    per_b4 = lambda s: pl.BlockSpec((1,) + s, lambda i, c: (i, 0, 0, 0))
    params = (conv_w, conv_b, dt_bias, a_log, d_skip, norm_w)
    ins = [z, xbc, dt, *params]
    in_specs = [seq(SSD_D), seq(CONV_DIM), seq(LANES)] + [_const_spec(p.shape) for p in params]
    state_shape = (SSD_HEADS, SSD_HEADDIM, D_STATE)
    if has_init:
        ins += [conv_init, ssm_init]
        in_specs += [per_b3((CONV_W - 1, CONV_DIM)), per_b4(state_shape)]
    return pl.pallas_call(
        functools.partial(_ssd_kernel, qb=qb, rows=rows, has_init=has_init),
        grid=(b, l // qb),
        in_specs=in_specs,
        out_specs=[seq(SSD_D), per_b3((CONV_W - 1, CONV_DIM)), per_b4(state_shape)],
        out_shape=[jax.ShapeDtypeStruct((b, l, SSD_D), BF16),
                   jax.ShapeDtypeStruct((b, CONV_W - 1, CONV_DIM), F32),
                   jax.ShapeDtypeStruct((b,) + state_shape, F32)],
        scratch_shapes=[pltpu.VMEM((rows + SUBLANES, CONV_DIM), F32),
                        pltpu.VMEM((rows, SSD_D), F32)],
        compiler_params=_cparams("arbitrary", "arbitrary"),
    )(*ins)


def _hgrn_kernel(*refs, qb, rows, has_init):
    if has_init:
        (q_ref, f_ref, v_ref, g_ref, lb_ref, nw_ref, sinit_ref,
         o_ref, sout_ref, st_ref, a_ref, kk_ref, gcs_ref) = refs
    else:
        (q_ref, f_ref, v_ref, g_ref, lb_ref, nw_ref,
         o_ref, sout_ref, st_ref, a_ref, kk_ref, gcs_ref) = refs
    c = pl.program_id(1)
    dk = HG_DK

    @pl.when(c == 0)
    def _():
        for h in range(HG_HEADS):
            if has_init:
                st_ref[h] = sinit_ref[0, h].T
            else:
                st_ref[h] = jnp.zeros((HG_DV, HG_DK), F32)

    fr = _load_rows(f_ref, qb, rows)
    lb = lb_ref[...]
    logf = jnp.log(lb + (1.0 - lb) * _sigmoid(fr))
    kk = (1.0 - lb) * _sigmoid(-fr)
    if qb < rows:
        valid = lax.broadcasted_iota(jnp.int32, fr.shape, 0) < qb
        logf = jnp.where(valid, logf, 0.0)
        kk = jnp.where(valid, kk, 0.0)
    tri = _tri(rows)
    gcs = _cumsum_rows(logf, tri.astype(BF16))
    glast = gcs[rows - 1:rows, :]
    q = _load_rows(q_ref, qb, rows)
    qt = q * jnp.exp(gcs)
    kk_ref[...] = kk
    gcs_ref[...] = gcs

    safe = jnp.min(glast) >= -HG_SAFE_LOG_DECAY

    @pl.when(safe)
    def _():
        kt = kk * jnp.exp(-gcs)
        for h in range(HG_HEADS):
            sl = slice(h * dk, (h + 1) * dk)
            a_ref[h] = _dot_nt(qt[:, sl].astype(BF16), kt[:, sl].astype(BF16))

    @pl.when(jnp.logical_not(safe))
    def _():
        ri = lax.broadcasted_iota(jnp.int32, (rows, dk), 0)
        ci = lax.broadcasted_iota(jnp.int32, (rows, rows), 1)
        for h in range(HG_HEADS):
            sl = slice(h * dk, (h + 1) * dk)
            q_h = q[:, sl]
            g_h = gcs[:, sl]

            def cols(j8, a_h):
                base = pl.multiple_of(j8 * SUBLANES, SUBLANES)
                k_blk = kk_ref[pl.ds(base, SUBLANES), sl]
                g_blk = gcs_ref[pl.ds(base, SUBLANES), sl]
                for r in range(SUBLANES):
                    j = base + r
                    t = q_h * k_blk[r:r + 1] * jnp.exp(
                        jnp.where(ri >= j, g_h - g_blk[r:r + 1], -jnp.inf))
                    a_h = jnp.where(ci == j, jnp.sum(t, axis=-1, keepdims=True), a_h)
                return a_h

            a_ref[h] = lax.fori_loop(0, rows // SUBLANES, cols, jnp.zeros((rows, rows), F32))

    khat = kk * jnp.exp(glast - gcs)
    v = _load_rows(v_ref, qb, rows)
    gate = _silu(_load_rows(g_ref, qb, rows))
    for h in range(HG_HEADS):
        sl = slice(h * dk, (h + 1) * dk)
        st = st_ref[h]
        v_bf = v[:, sl].astype(BF16)
        a_h = jnp.where(tri, a_ref[h], 0.0).astype(BF16)
        o = _dot(a_h, v_bf) + _dot_nt(qt[:, sl].astype(BF16), st.astype(BF16))
        st_ref[h] = st * jnp.exp(glast[:, sl]) + _dot_tn(v_bf, khat[:, sl].astype(BF16))
        o = _rms_unit(o) * nw_ref[:, sl] * gate[:, sl]
        o_ref[0, :, sl] = o[:qb].astype(o_ref.dtype)

    @pl.when(c == pl.num_programs(1) - 1)
    def _():
        for h in range(HG_HEADS):
            sout_ref[0, h] = st_ref[h].T


def _hgrn(q, f, v, g, lb, norm_w, init):
    b, l, _ = q.shape
    qb = min(CHUNK, l)
    assert l % qb == 0
    rows = max(qb, SUBLANES)
    has_init = init is not None
    seq = pl.BlockSpec((1, qb, HG_D), lambda i, c: (i, c, 0))
    state_shape = (HG_HEADS, HG_DK, HG_DV)
    per_b = pl.BlockSpec((1,) + state_shape, lambda i, c: (i, 0, 0, 0))
    ins = [q, f, v, g, lb, norm_w]
    in_specs = [seq, seq, seq, seq, _const_spec(lb.shape), _const_spec(norm_w.shape)]
    if has_init:
        ins.append(init)
        in_specs.append(per_b)
    return pl.pallas_call(
        functools.partial(_hgrn_kernel, qb=qb, rows=rows, has_init=has_init),
        grid=(b, l // qb),
        in_specs=in_specs,
        out_specs=[seq, per_b],
        out_shape=[jax.ShapeDtypeStruct((b, l, HG_D), BF16),
                   jax.ShapeDtypeStruct((b,) + state_shape, F32)],
        scratch_shapes=[pltpu.VMEM((HG_HEADS, HG_DV, HG_DK), F32),
                        pltpu.VMEM((HG_HEADS, rows, rows), F32),
                        pltpu.VMEM((rows, HG_D), F32),
                        pltpu.VMEM((rows, HG_D), F32)],
        compiler_params=_cparams("arbitrary", "arbitrary"),
    )(*ins)


def _route(logits, bias):
    lane = lax.broadcasted_iota(jnp.int32, logits.shape, 1)
    ninf = -jnp.inf
    big = jnp.int32(LANES)
    is_g = (lane >= N_EXPERTS) & (lane < N_EXPERTS + N_EGROUPS)
    rmax = lambda t: jnp.max(t, axis=-1, keepdims=True)
    rsum = lambda t: jnp.sum(t, axis=-1, keepdims=True)
    first = lambda m: jnp.min(jnp.where(m, lane, big), axis=-1, keepdims=True)

    gl = jnp.where(is_g, logits, ninf)
    gp = jnp.exp(gl - rmax(gl))
    gprob = gp / rsum(gp)
    gb = jnp.where(is_g, logits + bias, ninf)
    gsel = first(gb == rmax(gb))
    gw = rsum(jnp.where(lane == gsel, gprob, 0.0))
    e0 = (gsel - N_EXPERTS) * EXPERTS_PER_GROUP
    in_grp = (lane >= e0) & (lane < e0 + EXPERTS_PER_GROUP)
    el = jnp.where(in_grp, logits, ninf)
    ep = jnp.exp(el - rmax(el))
    eprob = ep / rsum(ep)
    eb = jnp.where(in_grp, logits + bias, ninf)
    i1 = first(eb == rmax(eb))
    eb2 = jnp.where(lane == i1, ninf, eb)
    i2 = first(eb2 == rmax(eb2))
    p1 = rsum(jnp.where(lane == i1, eprob, 0.0))
    p2 = rsum(jnp.where(lane == i2, eprob, 0.0))
    den = p1 + p2
    gid = (gsel - N_EXPERTS).astype(F32)
    return (jnp.where(lane == i1, gw * (p1 / den), 0.0)
            + jnp.where(lane == i2, gw * (p2 / den), 0.0)
            + jnp.where(lane == GID_LANE, gid, 0.0))


def _outproj_kernel(x_ref, y_ref, o_ref, w_ref, nw_ref, rhi_ref, rlo_ref, rb_ref,
                    x1_ref, hx_ref):
    dy = y_ref.shape[-1]
    d = x_ref.shape[-1]
    mix = _dot(y_ref[...], w_ref[:dy, :]) + _dot(o_ref[...], w_ref[dy:, :])
    x1 = x_ref[...] + mix
    x1_ref[...] = x1
    h = _rms_unit(x1) * nw_ref[...]
    h_hi = h.astype(BF16)
    h_lo = (h - h_hi.astype(F32)).astype(BF16)
    logits = _dot(h_hi, rhi_ref[...]) + _dot(h_hi, rlo_ref[...]) + _dot(h_lo, rhi_ref[...])
    hx_ref[:, :d] = h
    hx_ref[:, d:] = _route(logits, rb_ref[...])


def _out_proj(x, y, o, w, nw, r_hi, r_lo, r_bias, tm):
    n, d = x.shape
    row = lambda width: pl.BlockSpec((tm, width), lambda i: (i, 0))
    consts = (w, nw, r_hi, r_lo, r_bias)
    return pl.pallas_call(
        _outproj_kernel,
        grid=(n // tm,),
        in_specs=[row(d), row(y.shape[-1]), row(o.shape[-1])] + [_const_spec(c.shape) for c in consts],
        out_specs=[row(d), row(d + LANES)],
        out_shape=[jax.ShapeDtypeStruct((n, d), F32), jax.ShapeDtypeStruct((n, d + LANES), F32)],
        compiler_params=_cparams("arbitrary"),
    )(x, y, o, *consts)


def _pos_kernel(g_ref, pos_ref, tg_ref, *, tm):
    n = g_ref.shape[0]
    nr = n // LANES
    sel = (lax.broadcasted_iota(jnp.int32, (SUBLANES, LANES), 1) == GID_LANE).astype(BF16)
    sub = lax.broadcasted_iota(jnp.int32, (SUBLANES, LANES), 0)
    blocks = []
    span = SUBLANES * LANES
    for r8 in range(nr // SUBLANES):
        gt = _dot_nt(sel, g_ref[r8 * span:(r8 + 1) * span, :].astype(BF16))
        blk = jnp.zeros((SUBLANES, LANES), F32)
        for s in range(SUBLANES):
            blk = jnp.where(sub == s, gt[:, s * LANES:(s + 1) * LANES], blk)
        blocks.append(blk)
    gid = jnp.concatenate(blocks, axis=0)
    ri = lax.broadcasted_iota(jnp.int32, (LANES, LANES), 0)
    ci = lax.broadcasted_iota(jnp.int32, (LANES, LANES), 1)
    upper = (ri <= ci).astype(BF16)
    rr = lax.broadcasted_iota(jnp.int32, (nr, nr), 0)
    rc = lax.broadcasted_iota(jnp.int32, (nr, nr), 1)
    lower_strict = (rc < rr).astype(BF16)
    tile_lane = lax.broadcasted_iota(jnp.int32, (1, LANES), 1).astype(F32)
    pos = jnp.zeros((nr, LANES), F32)
    tiles_before = jnp.zeros((1, 1), F32)
    tile_group = jnp.zeros((1, LANES), F32)
    for g in range(N_EGROUPS):
        onehot = jnp.where(gid == float(g), 1.0, 0.0)
        inc = _dot(onehot.astype(BF16), upper)
        rowtot = jnp.broadcast_to(inc[:, LANES - 1:LANES], (nr, LANES))
        rowpre = _dot(lower_strict, rowtot.astype(BF16))
        cnt = rowpre[nr - 1:nr, 0:1] + rowtot[nr - 1:nr, 0:1]
        pos = pos + onehot * (tiles_before * tm + rowpre + inc - 1.0)
        tiles_before = tiles_before + jnp.floor((cnt + (tm - 1.0)) * (1.0 / tm))
        tile_group = tile_group + jnp.where(tile_lane >= tiles_before, 1.0, 0.0)
    pos_ref[...] = pos.astype(jnp.int32)
    tg_ref[...] = tile_group.astype(jnp.int32)


def _positions(hx, d, tm):
    n = hx.shape[0]
    assert n % (SUBLANES * LANES) == 0 and n // tm + N_EGROUPS <= LANES
    pos, tg = pl.pallas_call(
        functools.partial(_pos_kernel, tm=tm),
        grid=(1,),
        in_specs=[pl.BlockSpec((n, LANES), lambda i: (0, d // LANES))],
        out_specs=[_const_spec((n // LANES, LANES)), _const_spec((1, LANES))],
        out_shape=[jax.ShapeDtypeStruct((n // LANES, LANES), jnp.int32),
                   jax.ShapeDtypeStruct((1, LANES), jnp.int32)],
        compiler_params=_cparams("arbitrary"),
    )(hx)
    return pos.reshape(n), tg.reshape(LANES)


def _dispatch_kernel(pos_ref, hx_ref, hs_in_ref, hs_ref, buf_ref, sem_ref, *, td):
    del hs_in_ref
    i = pl.program_id(0)
    nsteps = pl.num_programs(0)
    slot = i % 2

    def row_copy(step, s, r):
        return pltpu.make_async_copy(buf_ref.at[s, pl.ds(r, 1)],
                                     hs_ref.at[pl.ds(pos_ref[step * td + r], 1)], sem_ref.at[s])

    def start_all(step, s):
        def body(r, carry):
            row_copy(step, s, r).start()
            return carry
        lax.fori_loop(0, td, body, 0, unroll=8)

    def wait_all(step, s):
        def body(r, carry):
            row_copy(step, s, r).wait()
            return carry
        lax.fori_loop(0, td, body, 0, unroll=8)

    @pl.when(i >= 2)
    def _():
        wait_all(i - 2, slot)

    buf_ref[slot] = hx_ref[...]
    start_all(i, slot)

    @pl.when(i == nsteps - 1)
    def _():
        @pl.when(i >= 1)
        def _():
            wait_all(i - 1, 1 - slot)
        wait_all(i, slot)


def _dispatch(hx, pos, n_sorted, td):
    n, w = hx.shape
    return pl.pallas_call(
        functools.partial(_dispatch_kernel, td=td),
        grid_spec=pltpu.PrefetchScalarGridSpec(
            num_scalar_prefetch=1,
            grid=(n // td,),
            in_specs=[pl.BlockSpec((td, w), lambda i, pos: (i, 0)),
                      pl.BlockSpec(memory_space=pl.ANY)],
            out_specs=pl.BlockSpec(memory_space=pl.ANY),
            scratch_shapes=[pltpu.VMEM((2, td, w), F32), pltpu.SemaphoreType.DMA((2,))]),
        out_shape=jax.ShapeDtypeStruct((n_sorted, w), F32),
        input_output_aliases={2: 0},
        compiler_params=_cparams("arbitrary"),
    )(pos, hx, jnp.zeros((n_sorted, w), F32))


def _expert(h, gates, e, wg, wu, wd):
    lane = lax.broadcasted_iota(jnp.int32, gates.shape, 1)
    gcol = jnp.sum(jnp.where(lane == e, gates, 0.0), axis=-1, keepdims=True)
    act = _silu(_dot(h, wg)) * _dot(h, wu) * gcol
    return _dot(act.astype(BF16), wd)


def _moe_dense_kernel(hx_ref, wg_ref, wu_ref, wd_ref, out_ref):
    e = pl.program_id(1)
    d = out_ref.shape[-1]
    part = _expert(hx_ref[:, :d].astype(BF16), hx_ref[:, d:], e, wg_ref[0], wu_ref[0], wd_ref[0])

    @pl.when(e == 0)
    def _():
        out_ref[...] = part

    @pl.when(e > 0)
    def _():
        out_ref[...] += part


def _moe_dense(hx, wg, wu, wd, tm):
    n, w = hx.shape
    ne, d, ff = wg.shape
    return pl.pallas_call(
        _moe_dense_kernel,
        grid=(n // tm, ne),
        in_specs=[pl.BlockSpec((tm, w), lambda i, e: (i, 0)),
                  pl.BlockSpec((1, d, ff), lambda i, e: (e, 0, 0)),
                  pl.BlockSpec((1, d, ff), lambda i, e: (e, 0, 0)),
                  pl.BlockSpec((1, ff, d), lambda i, e: (e, 0, 0))],
        out_specs=pl.BlockSpec((tm, d), lambda i, e: (i, 0)),
        out_shape=jax.ShapeDtypeStruct((n, d), F32),
        compiler_params=_cparams("arbitrary", "arbitrary"),
    )(hx, wg, wu, wd)


def _moe_sorted_kernel(tg_ref, hs_ref, wg_ref, wu_ref, wd_ref, ys_ref):
    g = tg_ref[pl.program_id(0)]
    d = ys_ref.shape[-1]

    @pl.when(g < N_EGROUPS)
    def _():
        h = hs_ref[:, :d].astype(BF16)
        gates = hs_ref[:, d:]
        acc = None
        for j in range(EXPERTS_PER_GROUP):
            part = _expert(h, gates, g * EXPERTS_PER_GROUP + j, wg_ref[j], wu_ref[j], wd_ref[j])
            acc = part if acc is None else acc + part
        ys_ref[...] = acc

    @pl.when(g >= N_EGROUPS)
    def _():
        ys_ref[...] = jnp.zeros(ys_ref.shape, F32)


def _moe_sorted(hs, tile_group, wg, wu, wd, tm):
    n_sorted, w = hs.shape
    _, d, ff = wg.shape
    epg = EXPERTS_PER_GROUP
    grp = lambda i, tg: (jnp.minimum(tg[i], N_EGROUPS - 1), 0, 0)
    return pl.pallas_call(
        _moe_sorted_kernel,
        grid_spec=pltpu.PrefetchScalarGridSpec(
            num_scalar_prefetch=1,
            grid=(n_sorted // tm,),
            in_specs=[pl.BlockSpec((tm, w), lambda i, tg: (i, 0)),
                      pl.BlockSpec((epg, d, ff), grp), pl.BlockSpec((epg, d, ff), grp),
                      pl.BlockSpec((epg, ff, d), grp)],
            out_specs=pl.BlockSpec((tm, d), lambda i, tg: (i, 0))),
        out_shape=jax.ShapeDtypeStruct((n_sorted, d), F32),
        compiler_params=_cparams("arbitrary"),
    )(tile_group, hs, wg, wu, wd)


def _ple_math(x, p, npw_ref, wg_ref, wp_ref, fw_ref):
    hn = (_rms_unit(x) * npw_ref[...]).astype(BF16)
    gate = _sigmoid(_dot(hn, wg_ref[...]))
    x = x + gate * _dot(p.astype(BF16), wp_ref[...])
    return _rms_unit(x) * fw_ref[...]


def _ple_kernel(x1_ref, moe_ref, p_ref, npw_ref, wg_ref, wp_ref, fw_ref, y_ref):
    y_ref[...] = _ple_math(x1_ref[...] + moe_ref[...], p_ref[...], npw_ref, wg_ref, wp_ref, fw_ref)


def _ple_gather_kernel(pos_ref, x1_ref, ys_ref, p_ref, npw_ref, wg_ref, wp_ref, fw_ref, y_ref,
                       buf_ref, sem_ref, *, tm):
    i = pl.program_id(0)
    nsteps = pl.num_programs(0)
    slot = i % 2

    def row_copy(step, s, r):
        return pltpu.make_async_copy(ys_ref.at[pl.ds(pos_ref[step * tm + r], 1)],
                                     buf_ref.at[s, pl.ds(r, 1)], sem_ref.at[s])

    def start_all(step, s):
        def body(r, carry):
            row_copy(step, s, r).start()
            return carry
        lax.fori_loop(0, tm, body, 0, unroll=8)

    @pl.when(i == 0)
    def _():
        start_all(0, 0)

    @pl.when(i + 1 < nsteps)
    def _():
        start_all(i + 1, 1 - slot)

    def wait_row(r, carry):
        row_copy(i, slot, r).wait()
        return carry
    lax.fori_loop(0, tm, wait_row, 0, unroll=8)

    y_ref[...] = _ple_math(x1_ref[...] + buf_ref[slot], p_ref[...], npw_ref, wg_ref, wp_ref, fw_ref)


def _ple(x1, moe, pos, p, npw, wg, wp, fw, tm):
    n, d = x1.shape
    consts = (npw, wg, wp, fw)
    out_shape = jax.ShapeDtypeStruct((n, d), F32)
    if pos is None:
        row = lambda width: pl.BlockSpec((tm, width), lambda i: (i, 0))
        return pl.pallas_call(
            _ple_kernel,
            grid=(n // tm,),
            in_specs=[row(d), row(d), row(p.shape[-1])] + [_const_spec(c.shape) for c in consts],
            out_specs=row(d),
            out_shape=out_shape,
            compiler_params=_cparams("arbitrary"),
        )(x1, moe, p, *consts)
    row = lambda width: pl.BlockSpec((tm, width), lambda i, pos: (i, 0))
    const = lambda shape: pl.BlockSpec(shape, lambda i, pos: (0,) * len(shape))
    return pl.pallas_call(
        functools.partial(_ple_gather_kernel, tm=tm),
        grid_spec=pltpu.PrefetchScalarGridSpec(
            num_scalar_prefetch=1,
            grid=(n // tm,),
            in_specs=[row(d), pl.BlockSpec(memory_space=pl.ANY), row(p.shape[-1])]
                     + [const(c.shape) for c in consts],
            out_specs=row(d),
            scratch_shapes=[pltpu.VMEM((2, tm, d), F32), pltpu.SemaphoreType.DMA((2,))]),
        out_shape=out_shape,
        compiler_params=_cparams("arbitrary"),
    )(pos, x1, moe, p, *consts)


def _prepare_weights(norm_mix_w, w_in, conv_w, conv_b, dt_bias, a_log, d_skip, ssd_norm_w,
                     lb_logits, hg_norm_w, w_out, norm_ffn_w, w_router_group, b_router_group,
                     w_router_expert, b_router_expert, w_exp_gate, w_exp_up, w_exp_down,
                     norm_ple_w, w_ple_gate, w_ple_proj, final_norm_w, layer):
    i = layer
    row = lambda t: t.reshape(1, -1).astype(F32)
    lane_pad = lambda t: jnp.pad(t, [(0, 0)] * (t.ndim - 1) + [(0, LANES - t.shape[-1])])
    splits = (SSD_D, CONV_DIM, SSD_HEADS, HG_D, HG_D, HG_D, HG_D)
    offs = [0]
    for s in splits:
        offs.append(offs[-1] + s)
    seg = lambda k: w_in[i][:, offs[k]:offs[k + 1]]
    w_in_r = jnp.concatenate([seg(0), seg(1), seg(3), seg(4), seg(5), seg(6), lane_pad(seg(2))],
                             axis=1).astype(BF16)
    w_router = lane_pad(jnp.concatenate([w_router_expert[i], w_router_group[i]], axis=1))
    r_hi = w_router.astype(BF16)
    r_lo = (w_router - r_hi.astype(F32)).astype(BF16)
    r_bias = lane_pad(jnp.concatenate([b_router_expert[i].reshape(-1), b_router_group[i]]).reshape(1, -1))
    lb = jnp.cumsum(jax.nn.softmax(lb_logits.astype(F32), axis=0), axis=0)[i]
    return dict(
        norm_mix_w=row(norm_mix_w[i]), w_in=w_in_r, conv_w=conv_w[i].astype(F32),
        conv_b=row(conv_b[i]), dt_bias=lane_pad(row(dt_bias[i])), a_log=lane_pad(row(a_log[i])),
        d_skip=row(jnp.repeat(d_skip[i], SSD_HEADDIM)), ssd_norm_w=row(ssd_norm_w[i]),
        lb=row(lb), hg_norm_w=row(hg_norm_w[i]), w_out=w_out[i].astype(BF16),
        norm_ffn_w=row(norm_ffn_w[i]), r_hi=r_hi, r_lo=r_lo, r_bias=r_bias,
        w_exp_gate=w_exp_gate[i].astype(BF16), w_exp_up=w_exp_up[i].astype(BF16),
        w_exp_down=w_exp_down[i].astype(BF16), norm_ple_w=row(norm_ple_w[i]),
        w_ple_gate=w_ple_gate[i].astype(BF16), w_ple_proj=w_ple_proj[i].astype(BF16),
        final_norm_w=row(final_norm_w))


def _token_tile(n):
    for tm in (512, 256, 128, 64, 32, 16, 8):
        if n % tm == 0:
            return tm
    raise ValueError(f"token count {n} must be a multiple of 8")


def _trunk(x, p, ssm_in, conv_in, hg_in, w):
    b, l, d = x.shape
    n = b * l
    tm = _token_tile(n)
    tm_small = min(tm, 256)
    x2d = x.reshape(n, d)
    z, xbc, q, f, v, g, dt = _in_proj(x2d, w["norm_mix_w"], w["w_in"], tm_small)
    seq = lambda t: t.reshape(b, l, t.shape[-1])
    y, conv_new, ssm_new = _ssd(seq(z), seq(xbc), seq(dt), w["conv_w"], w["conv_b"], w["dt_bias"],
                                w["a_log"], w["d_skip"], w["ssd_norm_w"], conv_in, ssm_in)
    o, hg_new = _hgrn(seq(q), seq(f), seq(v), seq(g), w["lb"], w["hg_norm_w"], hg_in)
    x1, hx = _out_proj(x2d, y.reshape(n, -1), o.reshape(n, -1), w["w_out"], w["norm_ffn_w"],
                       w["r_hi"], w["r_lo"], w["r_bias"], tm_small)
    experts = (w["w_exp_gate"], w["w_exp_up"], w["w_exp_down"])
    if n >= SORTED_MOE_MIN_TOKENS:
        pos, tile_group = _positions(hx, d, MOE_TILE)
        hs = _dispatch(hx, pos, n + N_EGROUPS * MOE_TILE, tm_small)
        moe = _moe_sorted(hs, tile_group, *experts, MOE_TILE)
    else:
        pos = None
        moe = _moe_dense(hx, *experts, tm)
    y_out = _ple(x1, moe, pos, p.reshape(n, -1), w["norm_ple_w"], w["w_ple_gate"], w["w_ple_proj"],
                 w["final_norm_w"], tm_small)
    return y_out.reshape(b, l, d), ssm_new[None], conv_new[None], hg_new[None]


def kernel(x_prompt, x_sample, state_ssm, state_conv, state_hgrn, p_prompt, p_sample, norm_mix_w, w_in, conv_w, conv_b, dt_bias, a_log, d_skip, ssd_norm_w, lb_logits, hg_norm_w, w_out, norm_ffn_w, w_router_group, b_router_group, w_router_expert, b_router_expert, w_exp_gate, w_exp_up, w_exp_down, norm_ple_w, w_ple_gate, w_ple_proj, final_norm_w):
    assert p_prompt.shape[0] == 1, "the per-layer-embedding kernel also applies the final norm: depth 1 only"
    w = _prepare_weights(norm_mix_w, w_in, conv_w, conv_b, dt_bias, a_log, d_skip, ssd_norm_w,
                         lb_logits, hg_norm_w, w_out, norm_ffn_w, w_router_group, b_router_group,
                         w_router_expert, b_router_expert, w_exp_gate, w_exp_up, w_exp_down,
                         norm_ple_w, w_ple_gate, w_ple_proj, final_norm_w, layer=0)
    y_p, ssm_p, conv_p, hg_p = _trunk(x_prompt, p_prompt[0], None, None, None, w)
    y_s, ssm_s, conv_s, hg_s = _trunk(x_sample, p_sample[0], state_ssm[0], state_conv[0],
                                      state_hgrn[0], w)
    return (y_p, y_s, ssm_p, conv_p, hg_p, ssm_s, conv_s, hg_s)
```

```python
import functools

import jax
import jax.numpy as jnp
from jax import lax
from jax.experimental import pallas as pl
from jax.experimental.pallas import tpu as pltpu

F32 = jnp.float32
BF16 = jnp.bfloat16

EPS = 1e-6
SSD_HEADS = 16
SSD_HEADDIM = 64
SSD_D = SSD_HEADS * SSD_HEADDIM
SSD_GROUPS = 2
D_STATE = 128
CONV_W = 4
CONV_DIM = SSD_D + 2 * SSD_GROUPS * D_STATE
HG_HEADS = 8
HG_DK = 128
HG_DV = 128
HG_D = HG_HEADS * HG_DV
N_EGROUPS = 4
EXPERTS_PER_GROUP = 4
N_EXPERTS = N_EGROUPS * EXPERTS_PER_GROUP

LANES = 128
SUBLANES = 8
VMEM_LIMIT = 48 * 1024 * 1024
CHUNK = 128
HG_SAFE_LOG_DECAY = 60.0
GID_LANE = N_EXPERTS + N_EGROUPS
MOE_TILE = 256
SORTED_MOE_MIN_TOKENS = 8 * SUBLANES * LANES
SEG_WIDTHS = (SSD_D, CONV_DIM, HG_D, HG_D, HG_D, HG_D, LANES)


def _dot(a, b):
    return jnp.dot(a, b, preferred_element_type=F32)


def _dot_nt(a, b):
    return lax.dot_general(a, b, (((1,), (1,)), ((), ())), preferred_element_type=F32)


def _dot_tn(a, b):
    return lax.dot_general(a, b, (((0,), (0,)), ((), ())), preferred_element_type=F32)


def _rms_unit(x):
    return x * lax.rsqrt(jnp.mean(x * x, axis=-1, keepdims=True) + EPS)


def _sigmoid(x):
    return 1.0 / (1.0 + jnp.exp(-x))


def _silu(x):
    return x * _sigmoid(x)


def _split3(x):
    hi = x.astype(BF16)
    r = x - hi.astype(F32)
    mid = r.astype(BF16)
    lo = (r - mid.astype(F32)).astype(BF16)
    return hi, mid, lo


def _cumsum_rows(x, tri):
    hi, mid, lo = _split3(x)
    return _dot(tri, hi) + _dot(tri, mid) + _dot(tri, lo)


def _tri(n):
    r = lax.broadcasted_iota(jnp.int32, (n, n), 0)
    c = lax.broadcasted_iota(jnp.int32, (n, n), 1)
    return r >= c


def _load_rows(ref, qb, rows):
    if qb == rows:
        return ref[0]
    assert qb == 1
    x = jnp.broadcast_to(ref[0], (rows, ref.shape[-1]))
    r = lax.broadcasted_iota(jnp.int32, x.shape, 0)
    return jnp.where(r < qb, x, 0.0)


def _cparams(*sem):
    return pltpu.CompilerParams(dimension_semantics=sem, vmem_limit_bytes=VMEM_LIMIT)


def _const_spec(shape):
    nd = len(shape)
    return pl.BlockSpec(shape, lambda *_: (0,) * nd)


def _inproj_kernel(x_ref, nw_ref, w_ref, *out_refs):
    hb = (_rms_unit(x_ref[...]) * nw_ref[...]).astype(BF16)
    off = 0
    for ref in out_refs:
        n = ref.shape[-1]
        for c0 in range(0, n, 512):
            cw = min(512, n - c0)
            ref[:, c0:c0 + cw] = _dot(hb, w_ref[:, off + c0:off + c0 + cw])
        off += n


def _in_proj(x, nw, w, tm):
    n, d = x.shape
    return pl.pallas_call(
        _inproj_kernel,
        grid=(n // tm,),
        in_specs=[pl.BlockSpec((tm, d), lambda i: (i, 0)), _const_spec(nw.shape),
                  _const_spec(w.shape)],
        out_specs=[pl.BlockSpec((tm, s), lambda i: (i, 0)) for s in SEG_WIDTHS],
        out_shape=[jax.ShapeDtypeStruct((n, s), F32) for s in SEG_WIDTHS],
        compiler_params=_cparams("arbitrary"),
    )(x, nw, w)


def _expand(x, e_ref):
    hi, mid, lo = _split3(x)
    e = e_ref[...]
    return _dot(hi, e) + _dot(mid, e) + _dot(lo, e)


def _ssd_kernel(*refs, qb, rows, has_init):
    if has_init:
        (z_ref, xbc_ref, dt_ref, cw_ref, cb_ref, dtb_ref, alog_ref, dsk_ref, nw_ref, e1_ref, e2_ref,
         cinit_ref, sinit_ref, y_ref, cout_ref, sout_ref, xc_ref, st_ref) = refs
    else:
        (z_ref, xbc_ref, dt_ref, cw_ref, cb_ref, dtb_ref, alog_ref, dsk_ref, nw_ref, e1_ref, e2_ref,
         y_ref, cout_ref, sout_ref, xc_ref, st_ref) = refs
    hist = SUBLANES
    c = pl.program_id(1)
    hpg = SSD_HEADS // SSD_GROUPS
    gw = hpg * SSD_HEADDIM
    gn = D_STATE

    @pl.when(c == 0)
    def _():
        xc_ref[0:hist, :] = jnp.zeros((hist, CONV_DIM), F32)
        if has_init:
            xc_ref[hist - (CONV_W - 1):hist, :] = cinit_ref[0]
            for g in range(SSD_GROUPS):
                st_ref[g] = sinit_ref[0, g * hpg:(g + 1) * hpg].reshape(gw, gn).T
        else:
            st_ref[...] = jnp.zeros(st_ref.shape, F32)

    xc_ref[hist:hist + rows, :] = _load_rows(xbc_ref, qb, rows)
    base = hist - (CONV_W - 1)
    acc = cb_ref[...] + cw_ref[0:1, :] * xc_ref[base:base + rows, :]
    for k in range(1, CONV_W):
        acc = acc + cw_ref[k:k + 1, :] * xc_ref[base + k:base + k + rows, :]
    cout_ref[0] = xc_ref[base + qb:base + qb + CONV_W - 1, :]
    xc_ref[0:hist, :] = xc_ref[rows:rows + hist, :]
    xbc = _silu(acc)
    xs = xbc[:, :SSD_D]
    b_bf = [xbc[:, SSD_D + g * gn:SSD_D + (g + 1) * gn].astype(BF16) for g in range(SSD_GROUPS)]
    c_bf = [xbc[:, SSD_D + (SSD_GROUPS + g) * gn:SSD_D + (SSD_GROUPS + g + 1) * gn].astype(BF16)
            for g in range(SSD_GROUPS)]

    dtr = _load_rows(dt_ref, qb, rows) + dtb_ref[...]
    dt = jnp.maximum(dtr, 0.0) + jnp.log(1.0 + jnp.exp(-jnp.abs(dtr)))
    if qb < rows:
        dt = jnp.where(lax.broadcasted_iota(jnp.int32, dt.shape, 0) < qb, dt, 0.0)
    a = dt * (-jnp.exp(alog_ref[...]))
    tri = _tri(rows)
    a_cs = _cumsum_rows(a, tri.astype(BF16))
    pad = max(rows, LANES)
    a_sq = a_cs if rows == pad else jnp.concatenate(
        [a_cs, jnp.zeros((pad - rows, LANES), F32)], axis=0)
    a_cs_t = a_sq.T
    dt_full = _expand(dt, e1_ref)
    a_full = _expand(a_cs, e1_ref)
    a_rep = _expand(a_cs, e2_ref)
    a_last = a_full[rows - 1:rows, :]

    xdt = xs * dt_full
    xdt_bf = xdt.astype(BF16)
    xw_bf = (xdt * jnp.exp(a_last - a_full)).astype(BF16)
    skip = dsk_ref[...] * xs
    lane = lax.broadcasted_iota(jnp.int32, (rows, LANES), 1)
    first_half = lane < SSD_HEADDIM
    zg = _silu(_load_rows(z_ref, qb, rows))

    for g in range(SSD_GROUPS):
        gs = slice(g * gw, (g + 1) * gw)
        scores = jnp.where(tri, _dot_nt(c_bf[g], b_bf[g]), 0.0)
        st_old = st_ref[g]
        y_off = _dot(c_bf[g], st_old.astype(BF16)) * jnp.exp(a_full[:, gs])
        st_ref[g] = st_old * jnp.exp(a_last[:, gs]) + _dot_tn(b_bf[g], xw_bf[:, gs])
        parts = []
        for pair in range(hpg // 2):
            h0 = g * hpg + 2 * pair
            cols = slice(h0 * SSD_HEADDIM, (h0 + 2) * SSD_HEADDIM)
            x_pair = xdt_bf[:, cols]
            acc = None
            for hh, keep in ((h0, first_half), (h0 + 1, jnp.logical_not(first_half))):
                diff = a_rep[:, hh * LANES:hh * LANES + rows] - a_cs_t[hh:hh + 1, :rows]
                m = (scores * jnp.exp(jnp.minimum(diff, 0.0))).astype(BF16)
                part = _dot(m, jnp.where(keep, x_pair, jnp.zeros_like(x_pair)))
                acc = part if acc is None else acc + part
            parts.append(acc)
        y = jnp.concatenate(parts, axis=1) + y_off + skip[:, gs]
        y = _rms_unit(y * zg[:, gs]) * nw_ref[:, gs]
        y_ref[0, :, gs] = y[:qb].astype(y_ref.dtype)

    @pl.when(c == pl.num_programs(1) - 1)
    def _():
        for g in range(SSD_GROUPS):
            sout_ref[0, g * hpg:(g + 1) * hpg] = st_ref[g].T.reshape(hpg, SSD_HEADDIM, gn)


def _ssd(z, xbc, dt, conv_w, conv_b, dt_bias, a_log, d_skip, norm_w, conv_init, ssm_init):
    b, l, _ = z.shape
    qb = min(CHUNK, l)
    assert l % qb == 0
    rows = max(qb, SUBLANES)
    has_init = ssm_init is not None
    seq = lambda w: pl.BlockSpec((1, qb, w), lambda i, c: (i, c, 0))
    per_b3 = lambda s: pl.BlockSpec((1,) + s, lambda i, c: (i, 0, 0))
    per_b4 = lambda s: pl.BlockSpec((1,) + s, lambda i, c: (i, 0, 0, 0))
    head = jnp.arange(LANES)[:, None]
    e1 = (head == jnp.arange(SSD_D)[None, :] // SSD_HEADDIM).astype(BF16)
    e2 = (head == jnp.arange(SSD_HEADS * LANES)[None, :] // LANES).astype(BF16)
    params = (conv_w, conv_b, dt_bias, a_log, d_skip, norm_w, e1, e2)
    ins = [z, xbc, dt, *params]
    in_specs = [seq(SSD_D), seq(CONV_DIM), seq(LANES)] + [_const_spec(p.shape) for p in params]
    state_shape = (SSD_HEADS, SSD_HEADDIM, D_STATE)
    if has_init:
        ins += [conv_init, ssm_init]
        in_specs += [per_b3((CONV_W - 1, CONV_DIM)), per_b4(state_shape)]
    return pl.pallas_call(
        functools.partial(_ssd_kernel, qb=qb, rows=rows, has_init=has_init),
        grid=(b, l // qb),
        in_specs=in_specs,
        out_specs=[seq(SSD_D), per_b3((CONV_W - 1, CONV_DIM)), per_b4(state_shape)],
        out_shape=[jax.ShapeDtypeStruct((b, l, SSD_D), BF16),
                   jax.ShapeDtypeStruct((b, CONV_W - 1, CONV_DIM), F32),
                   jax.ShapeDtypeStruct((b,) + state_shape, F32)],
        scratch_shapes=[pltpu.VMEM((rows + SUBLANES, CONV_DIM), F32),
                        pltpu.VMEM((SSD_GROUPS, D_STATE, SSD_D // SSD_GROUPS), F32)],
        compiler_params=_cparams("arbitrary", "arbitrary"),
    )(*ins)


def _tokens_to_lanes(x):
    tb, n = x.shape
    return jnp.concatenate([x, jnp.zeros((LANES - tb, n), F32)], axis=0).T


def _softplus(x):
    return jnp.maximum(x, 0.0) + jnp.log(1.0 + jnp.exp(-jnp.abs(x)))


def _step_decay_kernel(dt_ref, dtb_ref, alog_ref, da_ref):
    da_ref[...] = jnp.exp(_softplus(dt_ref[...] + dtb_ref[...]) * (-jnp.exp(alog_ref[...])))


def _ssd_step_kernel(da_ref, z_ref, xbc_ref, dt_ref, cw_ref, cb_ref, dtb_ref, dsk_ref, nw_ref, e1_ref,
                     cs_ref, st_ref, y_ref, cnew_ref, snew_ref, *, tb):
    cd = CONV_DIM
    gn = D_STATE
    hpg = SSD_HEADS // SSD_GROUPS
    gw = hpg * SSD_HEADDIM
    first = pl.program_id(0) * tb
    x_in = xbc_ref[...]
    acc = cb_ref[...] + cw_ref[CONV_W - 1:CONV_W, :] * x_in
    for k in range(CONV_W - 1):
        acc = acc + cw_ref[k:k + 1, :] * cs_ref[:, k * cd:(k + 1) * cd]
    cnew_ref[:, :(CONV_W - 2) * cd] = cs_ref[:, cd:]
    cnew_ref[:, (CONV_W - 2) * cd:] = x_in
    xbc = _silu(acc)
    xs = xbc[:, :SSD_D]
    dt = _softplus(dt_ref[...] + dtb_ref[...])
    xdt_t = _tokens_to_lanes(xs * _expand(dt, e1_ref)).astype(BF16)
    row_tok = lax.broadcasted_iota(jnp.int32, (LANES, tb * gn), 0)
    col_tok = lax.broadcasted_iota(jnp.int32, (LANES, tb * gn), 1) // gn
    lane = lax.broadcasted_iota(jnp.int32, (SSD_HEADDIM, LANES), 1)
    y_groups = []
    for g in range(SSD_GROUPS):
        b_g = xbc[:, SSD_D + g * gn:SSD_D + (g + 1) * gn]
        c_g = xbc[:, SSD_D + (SSD_GROUPS + g) * gn:SSD_D + (SSD_GROUPS + g + 1) * gn]
        b_wide = jnp.concatenate([jnp.tile(b_g, (1, tb)), jnp.zeros((LANES - tb, tb * gn), F32)], axis=0)
        b_diag = jnp.where(row_tok == col_tok, b_wide, 0.0).astype(BF16)
        upd = _dot(xdt_t[g * gw:(g + 1) * gw, :], b_diag)
        y_heads = []
        for hl in range(hpg):
            h = g * hpg + hl
            y_h = jnp.zeros((SSD_HEADDIM, LANES), F32)
            for j in range(tb):
                new = (st_ref[j, h] * da_ref[first + j, h]
                       + upd[hl * SSD_HEADDIM:(hl + 1) * SSD_HEADDIM, j * gn:(j + 1) * gn])
                snew_ref[j, h] = new
                y_h = jnp.where(lane == j, jnp.sum(new * c_g[j:j + 1, :], axis=-1, keepdims=True), y_h)
            y_heads.append(y_h)
        y_groups.append(jnp.concatenate(y_heads, axis=0))
    y = jnp.concatenate(y_groups, axis=0).T[:tb]
    y = (y + dsk_ref[...] * xs) * _silu(z_ref[...])
    for g in range(SSD_GROUPS):
        gs = slice(g * gw, (g + 1) * gw)
        y_ref[:, gs] = (_rms_unit(y[:, gs]) * nw_ref[:, gs]).astype(y_ref.dtype)


def _ssd_step(z, xbc, dt, conv_w, conv_b, dt_bias, a_log, d_skip, norm_w, conv_init, ssm_init):
    b = z.shape[0]
    tb = SUBLANES
    assert b % tb == 0
    da = pl.pallas_call(
        _step_decay_kernel,
        grid=(1,),
        in_specs=[_const_spec(dt.shape), _const_spec(dt_bias.shape), _const_spec(a_log.shape)],
        out_specs=_const_spec(dt.shape),
        out_shape=jax.ShapeDtypeStruct(dt.shape, F32),
    )(dt, dt_bias, a_log)
    head = jnp.arange(LANES)[:, None]
    e1 = (head == jnp.arange(SSD_D)[None, :] // SSD_HEADDIM).astype(BF16)
    params = (conv_w, conv_b, dt_bias, d_skip, norm_w, e1)
    hist = (CONV_W - 1) * CONV_DIM
    row = lambda w: pl.BlockSpec((tb, w), lambda i, da: (i, 0))
    const = lambda shape: pl.BlockSpec(shape, lambda i, da: (0,) * len(shape))
    state_shape = (SSD_HEADS, SSD_HEADDIM, D_STATE)
    st_spec = pl.BlockSpec((tb,) + state_shape, lambda i, da: (i, 0, 0, 0))
    y, conv_new, ssm_new = pl.pallas_call(
        functools.partial(_ssd_step_kernel, tb=tb),
        grid_spec=pltpu.PrefetchScalarGridSpec(
            num_scalar_prefetch=1,
            grid=(b // tb,),
            in_specs=[row(SSD_D), row(CONV_DIM), row(LANES)] + [const(p.shape) for p in params]
                     + [row(hist), st_spec],
            out_specs=[row(SSD_D), row(hist), st_spec]),
        out_shape=[jax.ShapeDtypeStruct((b, SSD_D), BF16), jax.ShapeDtypeStruct((b, hist), F32),
                   jax.ShapeDtypeStruct((b,) + state_shape, F32)],
        compiler_params=_cparams("arbitrary"),
    )(da, z, xbc, dt, *params, conv_init.reshape(b, hist), ssm_init)
    return y, conv_new.reshape(b, CONV_W - 1, CONV_DIM), ssm_new


def _hgrn_kernel(*refs, qb, rows, has_init):
    if has_init:
        (q_ref, f_ref, v_ref, g_ref, lb_ref, nw_ref, sinit_ref,
         o_ref, sout_ref, st_ref, a_ref, kk_ref, gcs_ref) = refs
    else:
        (q_ref, f_ref, v_ref, g_ref, lb_ref, nw_ref,
         o_ref, sout_ref, st_ref, a_ref, kk_ref, gcs_ref) = refs
    c = pl.program_id(1)
    dk = HG_DK

    @pl.when(c == 0)
    def _():
        for h in range(HG_HEADS):
            if has_init:
                st_ref[h] = sinit_ref[0, h].T
            else:
                st_ref[h] = jnp.zeros((HG_DV, HG_DK), F32)

    fr = _load_rows(f_ref, qb, rows)
    lb = lb_ref[...]
    logf = jnp.log(lb + (1.0 - lb) * _sigmoid(fr))
    kk = (1.0 - lb) * _sigmoid(-fr)
    if qb < rows:
        valid = lax.broadcasted_iota(jnp.int32, fr.shape, 0) < qb
        logf = jnp.where(valid, logf, 0.0)
        kk = jnp.where(valid, kk, 0.0)
    tri = _tri(rows)
    gcs = _cumsum_rows(logf, tri.astype(BF16))
    glast = gcs[rows - 1:rows, :]
    q = _load_rows(q_ref, qb, rows)
    qt = q * jnp.exp(gcs)
    kk_ref[...] = kk
    gcs_ref[...] = gcs

    safe = jnp.min(glast) >= -HG_SAFE_LOG_DECAY

    @pl.when(safe)
    def _():
        kt = kk * jnp.exp(-gcs)
        for h in range(HG_HEADS):
            sl = slice(h * dk, (h + 1) * dk)
            a_ref[h] = _dot_nt(qt[:, sl].astype(BF16), kt[:, sl].astype(BF16))

    @pl.when(jnp.logical_not(safe))
    def _():
        ri = lax.broadcasted_iota(jnp.int32, (rows, dk), 0)
        ci = lax.broadcasted_iota(jnp.int32, (rows, rows), 1)
        for h in range(HG_HEADS):
            sl = slice(h * dk, (h + 1) * dk)
            q_h = q[:, sl]
            g_h = gcs[:, sl]

            def cols(j8, a_h):
                base = pl.multiple_of(j8 * SUBLANES, SUBLANES)
                k_blk = kk_ref[pl.ds(base, SUBLANES), sl]
                g_blk = gcs_ref[pl.ds(base, SUBLANES), sl]
                for r in range(SUBLANES):
                    j = base + r
                    t = q_h * k_blk[r:r + 1] * jnp.exp(
                        jnp.where(ri >= j, g_h - g_blk[r:r + 1], -jnp.inf))
                    a_h = jnp.where(ci == j, jnp.sum(t, axis=-1, keepdims=True), a_h)
                return a_h

            a_ref[h] = lax.fori_loop(0, rows // SUBLANES, cols, jnp.zeros((rows, rows), F32))

    khat = kk * jnp.exp(glast - gcs)
    v = _load_rows(v_ref, qb, rows)
    gate = _silu(_load_rows(g_ref, qb, rows))
    for h in range(HG_HEADS):
        sl = slice(h * dk, (h + 1) * dk)
        st = st_ref[h]
        v_bf = v[:, sl].astype(BF16)
        a_h = jnp.where(tri, a_ref[h], 0.0).astype(BF16)
        o = _dot(a_h, v_bf) + _dot_nt(qt[:, sl].astype(BF16), st.astype(BF16))
        st_ref[h] = st * jnp.exp(glast[:, sl]) + _dot_tn(v_bf, khat[:, sl].astype(BF16))
        o = _rms_unit(o) * nw_ref[:, sl] * gate[:, sl]
        o_ref[0, :, sl] = o[:qb].astype(o_ref.dtype)

    @pl.when(c == pl.num_programs(1) - 1)
    def _():
        for h in range(HG_HEADS):
            sout_ref[0, h] = st_ref[h].T


def _hgrn(q, f, v, g, lb, norm_w, init):
    b, l, _ = q.shape
    qb = min(CHUNK, l)
    assert l % qb == 0
    rows = max(qb, SUBLANES)
    has_init = init is not None
    seq = pl.BlockSpec((1, qb, HG_D), lambda i, c: (i, c, 0))
    state_shape = (HG_HEADS, HG_DK, HG_DV)
    per_b = pl.BlockSpec((1,) + state_shape, lambda i, c: (i, 0, 0, 0))
    ins = [q, f, v, g, lb, norm_w]
    in_specs = [seq, seq, seq, seq, _const_spec(lb.shape), _const_spec(norm_w.shape)]
    if has_init:
        ins.append(init)
        in_specs.append(per_b)
    return pl.pallas_call(
        functools.partial(_hgrn_kernel, qb=qb, rows=rows, has_init=has_init),
        grid=(b, l // qb),
        in_specs=in_specs,
        out_specs=[seq, per_b],
        out_shape=[jax.ShapeDtypeStruct((b, l, HG_D), BF16),
                   jax.ShapeDtypeStruct((b,) + state_shape, F32)],
        scratch_shapes=[pltpu.VMEM((HG_HEADS, HG_DV, HG_DK), F32),
                        pltpu.VMEM((HG_HEADS, rows, rows), F32),
                        pltpu.VMEM((rows, HG_D), F32),
                        pltpu.VMEM((rows, HG_D), F32)],
        compiler_params=_cparams("arbitrary", "arbitrary"),
    )(*ins)


def _hgrn_step_kernel(q_ref, f_ref, v_ref, g_ref, lb_ref, nw_ref, st_ref, o_ref, snew_ref, *, tb):
    dk, dv = HG_DK, HG_DV
    fr = f_ref[...]
    lb = lb_ref[...]
    f_t = _tokens_to_lanes(lb + (1.0 - lb) * _sigmoid(fr))
    k_t = _tokens_to_lanes((1.0 - lb) * _sigmoid(-fr))
    q_t = _tokens_to_lanes(q_ref[...])
    v = v_ref[...]
    gate = _silu(g_ref[...])
    sub = lax.broadcasted_iota(jnp.int32, (tb, dv), 0)
    for h in range(HG_HEADS):
        rs = slice(h * dk, (h + 1) * dk)
        vs = slice(h * dv, (h + 1) * dv)
        o_h = jnp.zeros((tb, dv), F32)
        for j in range(tb):
            fcol = jnp.broadcast_to(f_t[rs, j:j + 1], (dk, dv))
            kcol = jnp.broadcast_to(k_t[rs, j:j + 1], (dk, dv))
            qcol = jnp.broadcast_to(q_t[rs, j:j + 1], (dk, dv))
            new = st_ref[j, h] * fcol + kcol * v[j:j + 1, vs]
            snew_ref[j, h] = new
            o_h = jnp.where(sub == j, jnp.sum(new * qcol, axis=0, keepdims=True), o_h)
        o_ref[:, vs] = (_rms_unit(o_h) * nw_ref[:, vs] * gate[:, vs]).astype(o_ref.dtype)


def _hgrn_step(q, f, v, g, lb, norm_w, init):
    b = q.shape[0]
    tb = SUBLANES
    assert b % tb == 0
    row = pl.BlockSpec((tb, HG_D), lambda i: (i, 0))
    state_shape = (HG_HEADS, HG_DK, HG_DV)
    st_spec = pl.BlockSpec((tb,) + state_shape, lambda i: (i, 0, 0, 0))
    return pl.pallas_call(
        functools.partial(_hgrn_step_kernel, tb=tb),
        grid=(b // tb,),
        in_specs=[row, row, row, row, _const_spec(lb.shape), _const_spec(norm_w.shape), st_spec],
        out_specs=[row, st_spec],
        out_shape=[jax.ShapeDtypeStruct((b, HG_D), BF16),
                   jax.ShapeDtypeStruct((b,) + state_shape, F32)],
        compiler_params=_cparams("arbitrary"),
    )(q, f, v, g, lb, norm_w, init)


def _route(logits, bias):
    lane = lax.broadcasted_iota(jnp.int32, logits.shape, 1)
    ninf = -jnp.inf
    big = jnp.int32(LANES)
    is_g = (lane >= N_EXPERTS) & (lane < N_EXPERTS + N_EGROUPS)
    rmax = lambda t: jnp.max(t, axis=-1, keepdims=True)
    rsum = lambda t: jnp.sum(t, axis=-1, keepdims=True)
    first = lambda m: jnp.min(jnp.where(m, lane, big), axis=-1, keepdims=True)

    gl = jnp.where(is_g, logits, ninf)
    gp = jnp.exp(gl - rmax(gl))
    gprob = gp / rsum(gp)
    gb = jnp.where(is_g, logits + bias, ninf)
    gsel = first(gb == rmax(gb))
    gw = rsum(jnp.where(lane == gsel, gprob, 0.0))
    e0 = (gsel - N_EXPERTS) * EXPERTS_PER_GROUP
    in_grp = (lane >= e0) & (lane < e0 + EXPERTS_PER_GROUP)
    el = jnp.where(in_grp, logits, ninf)
    ep = jnp.exp(el - rmax(el))
    eprob = ep / rsum(ep)
    eb = jnp.where(in_grp, logits + bias, ninf)
    i1 = first(eb == rmax(eb))
    eb2 = jnp.where(lane == i1, ninf, eb)
    i2 = first(eb2 == rmax(eb2))
    p1 = rsum(jnp.where(lane == i1, eprob, 0.0))
    p2 = rsum(jnp.where(lane == i2, eprob, 0.0))
    den = p1 + p2
    gid = (gsel - N_EXPERTS).astype(F32)
    return (jnp.where(lane == i1, gw * (p1 / den), 0.0)
            + jnp.where(lane == i2, gw * (p2 / den), 0.0)
            + jnp.where(lane == GID_LANE, gid, 0.0))


def _outproj_kernel(x_ref, y_ref, o_ref, w_ref, nw_ref, rhi_ref, rlo_ref, rb_ref,
                    x1_ref, hx_ref):
    dy = y_ref.shape[-1]
    d = x_ref.shape[-1]
    mix = _dot(y_ref[...], w_ref[:dy, :]) + _dot(o_ref[...], w_ref[dy:, :])
    x1 = x_ref[...] + mix
    x1_ref[...] = x1
    h = _rms_unit(x1) * nw_ref[...]
    h_hi = h.astype(BF16)
    h_lo = (h - h_hi.astype(F32)).astype(BF16)
    logits = _dot(h_hi, rhi_ref[...]) + _dot(h_hi, rlo_ref[...]) + _dot(h_lo, rhi_ref[...])
    hx_ref[:, :d] = h
    hx_ref[:, d:] = _route(logits, rb_ref[...])


def _out_proj(x, y, o, w, nw, r_hi, r_lo, r_bias, tm):
    n, d = x.shape
    row = lambda width: pl.BlockSpec((tm, width), lambda i: (i, 0))
    consts = (w, nw, r_hi, r_lo, r_bias)
    return pl.pallas_call(
        _outproj_kernel,
        grid=(n // tm,),
        in_specs=[row(d), row(y.shape[-1]), row(o.shape[-1])] + [_const_spec(c.shape) for c in consts],
        out_specs=[row(d), row(d + LANES)],
        out_shape=[jax.ShapeDtypeStruct((n, d), F32), jax.ShapeDtypeStruct((n, d + LANES), F32)],
        compiler_params=_cparams("arbitrary"),
    )(x, y, o, *consts)


def _pos_kernel(g_ref, pos_ref, tg_ref, *, tm):
    n = g_ref.shape[0]
    nr = n // LANES
    sel = (lax.broadcasted_iota(jnp.int32, (SUBLANES, LANES), 1) == GID_LANE).astype(BF16)
    sub = lax.broadcasted_iota(jnp.int32, (SUBLANES, LANES), 0)
    blocks = []
    span = SUBLANES * LANES
    for r8 in range(nr // SUBLANES):
        gt = _dot_nt(sel, g_ref[r8 * span:(r8 + 1) * span, :].astype(BF16))
        blk = jnp.zeros((SUBLANES, LANES), F32)
        for s in range(SUBLANES):
            blk = jnp.where(sub == s, gt[:, s * LANES:(s + 1) * LANES], blk)
        blocks.append(blk)
    gid = jnp.concatenate(blocks, axis=0)
    ri = lax.broadcasted_iota(jnp.int32, (LANES, LANES), 0)
    ci = lax.broadcasted_iota(jnp.int32, (LANES, LANES), 1)
    upper = (ri <= ci).astype(BF16)
    rr = lax.broadcasted_iota(jnp.int32, (nr, nr), 0)
    rc = lax.broadcasted_iota(jnp.int32, (nr, nr), 1)
    lower_strict = (rc < rr).astype(BF16)
    tile_lane = lax.broadcasted_iota(jnp.int32, (1, LANES), 1).astype(F32)
    pos = jnp.zeros((nr, LANES), F32)
    tiles_before = jnp.zeros((1, 1), F32)
    tile_group = jnp.zeros((1, LANES), F32)
    for g in range(N_EGROUPS):
        onehot = jnp.where(gid == float(g), 1.0, 0.0)
        inc = _dot(onehot.astype(BF16), upper)
        rowtot = jnp.broadcast_to(inc[:, LANES - 1:LANES], (nr, LANES))
        rowpre = _dot(lower_strict, rowtot.astype(BF16))
        cnt = rowpre[nr - 1:nr, 0:1] + rowtot[nr - 1:nr, 0:1]
        pos = pos + onehot * (tiles_before * tm + rowpre + inc - 1.0)
        tiles_before = tiles_before + jnp.floor((cnt + (tm - 1.0)) * (1.0 / tm))
        tile_group = tile_group + jnp.where(tile_lane >= tiles_before, 1.0, 0.0)
    pos_ref[...] = pos.astype(jnp.int32)
    tg_ref[...] = tile_group.astype(jnp.int32)


def _positions(hx, d, tm):
    n = hx.shape[0]
    assert n % (SUBLANES * LANES) == 0 and n // tm + N_EGROUPS <= LANES
    pos, tg = pl.pallas_call(
        functools.partial(_pos_kernel, tm=tm),
        grid=(1,),
        in_specs=[pl.BlockSpec((n, LANES), lambda i: (0, d // LANES))],
        out_specs=[_const_spec((n // LANES, LANES)), _const_spec((1, LANES))],
        out_shape=[jax.ShapeDtypeStruct((n // LANES, LANES), jnp.int32),
                   jax.ShapeDtypeStruct((1, LANES), jnp.int32)],
        compiler_params=_cparams("arbitrary"),
    )(hx)
    return pos.reshape(n), tg.reshape(LANES)


def _dispatch_kernel(pos_ref, hx_ref, hs_in_ref, hs_ref, buf_ref, sem_ref, *, td):
    del hs_in_ref
    i = pl.program_id(0)
    nsteps = pl.num_programs(0)
    slot = i % 2

    def row_copy(step, s, r):
        return pltpu.make_async_copy(buf_ref.at[s, pl.ds(r, 1)],
                                     hs_ref.at[pl.ds(pos_ref[step * td + r], 1)], sem_ref.at[s])

    def start_all(step, s):
        def body(r, carry):
            row_copy(step, s, r).start()
            return carry
        lax.fori_loop(0, td, body, 0, unroll=8)

    def wait_all(step, s):
        def body(r, carry):
            row_copy(step, s, r).wait()
            return carry
        lax.fori_loop(0, td, body, 0, unroll=8)

    @pl.when(i >= 2)
    def _():
        wait_all(i - 2, slot)

    buf_ref[slot] = hx_ref[...]
    start_all(i, slot)

    @pl.when(i == nsteps - 1)
    def _():
        @pl.when(i >= 1)
        def _():
            wait_all(i - 1, 1 - slot)
        wait_all(i, slot)


def _dispatch(hx, pos, n_sorted, td):
    n, w = hx.shape
    return pl.pallas_call(
        functools.partial(_dispatch_kernel, td=td),
        grid_spec=pltpu.PrefetchScalarGridSpec(
            num_scalar_prefetch=1,
            grid=(n // td,),
            in_specs=[pl.BlockSpec((td, w), lambda i, pos: (i, 0)),
                      pl.BlockSpec(memory_space=pl.ANY)],
            out_specs=pl.BlockSpec(memory_space=pl.ANY),
            scratch_shapes=[pltpu.VMEM((2, td, w), F32), pltpu.SemaphoreType.DMA((2,))]),
        out_shape=jax.ShapeDtypeStruct((n_sorted, w), F32),
        input_output_aliases={2: 0},
        compiler_params=_cparams("arbitrary"),
    )(pos, hx, jnp.zeros((n_sorted, w), F32))


def _expert(h, gates, e, wg, wu, wd):
    lane = lax.broadcasted_iota(jnp.int32, gates.shape, 1)
    gcol = jnp.sum(jnp.where(lane == e, gates, 0.0), axis=-1, keepdims=True)
    act = _silu(_dot(h, wg)) * _dot(h, wu) * gcol
    return _dot(act.astype(BF16), wd)


def _moe_dense_kernel(hx_ref, wg_ref, wu_ref, wd_ref, out_ref):
    e = pl.program_id(1)
    d = out_ref.shape[-1]
    part = _expert(hx_ref[:, :d].astype(BF16), hx_ref[:, d:], e, wg_ref[0], wu_ref[0], wd_ref[0])

    @pl.when(e == 0)
    def _():
        out_ref[...] = part

    @pl.when(e > 0)
    def _():
        out_ref[...] += part


def _moe_dense(hx, wg, wu, wd, tm):
    n, w = hx.shape
    ne, d, ff = wg.shape
    return pl.pallas_call(
        _moe_dense_kernel,
        grid=(n // tm, ne),
        in_specs=[pl.BlockSpec((tm, w), lambda i, e: (i, 0)),
                  pl.BlockSpec((1, d, ff), lambda i, e: (e, 0, 0)),
                  pl.BlockSpec((1, d, ff), lambda i, e: (e, 0, 0)),
                  pl.BlockSpec((1, ff, d), lambda i, e: (e, 0, 0))],
        out_specs=pl.BlockSpec((tm, d), lambda i, e: (i, 0)),
        out_shape=jax.ShapeDtypeStruct((n, d), F32),
        compiler_params=_cparams("arbitrary", "arbitrary"),
    )(hx, wg, wu, wd)


def _moe_sorted_kernel(tg_ref, hs_ref, wg_ref, wu_ref, wd_ref, ys_ref):
    g = tg_ref[pl.program_id(0)]
    d = ys_ref.shape[-1]

    @pl.when(g < N_EGROUPS)
    def _():
        h = hs_ref[:, :d].astype(BF16)
        gates = hs_ref[:, d:]
        acc = None
        for j in range(EXPERTS_PER_GROUP):
            part = _expert(h, gates, g * EXPERTS_PER_GROUP + j, wg_ref[j], wu_ref[j], wd_ref[j])
            acc = part if acc is None else acc + part
        ys_ref[...] = acc

    @pl.when(g >= N_EGROUPS)
    def _():
        ys_ref[...] = jnp.zeros(ys_ref.shape, F32)


def _moe_sorted(hs, tile_group, wg, wu, wd, tm):
    n_sorted, w = hs.shape
    _, d, ff = wg.shape
    epg = EXPERTS_PER_GROUP
    grp = lambda i, tg: (jnp.minimum(tg[i], N_EGROUPS - 1), 0, 0)
    return pl.pallas_call(
        _moe_sorted_kernel,
        grid_spec=pltpu.PrefetchScalarGridSpec(
            num_scalar_prefetch=1,
            grid=(n_sorted // tm,),
            in_specs=[pl.BlockSpec((tm, w), lambda i, tg: (i, 0)),
                      pl.BlockSpec((epg, d, ff), grp), pl.BlockSpec((epg, d, ff), grp),
                      pl.BlockSpec((epg, ff, d), grp)],
            out_specs=pl.BlockSpec((tm, d), lambda i, tg: (i, 0))),
        out_shape=jax.ShapeDtypeStruct((n_sorted, d), F32),
        compiler_params=_cparams("arbitrary"),
    )(tile_group, hs, wg, wu, wd)


def _ple_math(x, p, npw_ref, wg_ref, wp_ref, fw_ref):
    hn = (_rms_unit(x) * npw_ref[...]).astype(BF16)
    gate = _sigmoid(_dot(hn, wg_ref[...]))
    x = x + gate * _dot(p.astype(BF16), wp_ref[...])
    return _rms_unit(x) * fw_ref[...]


def _ple_kernel(x1_ref, moe_ref, p_ref, npw_ref, wg_ref, wp_ref, fw_ref, y_ref):
    y_ref[...] = _ple_math(x1_ref[...] + moe_ref[...], p_ref[...], npw_ref, wg_ref, wp_ref, fw_ref)


def _ple_gather_kernel(pos_ref, x1_ref, ys_ref, p_ref, npw_ref, wg_ref, wp_ref, fw_ref, y_ref,
                       buf_ref, sem_ref, *, tm):
    i = pl.program_id(0)
    nsteps = pl.num_programs(0)
    slot = i % 2

    def row_copy(step, s, r):
        return pltpu.make_async_copy(ys_ref.at[pl.ds(pos_ref[step * tm + r], 1)],
                                     buf_ref.at[s, pl.ds(r, 1)], sem_ref.at[s])

    def start_all(step, s):
        def body(r, carry):
            row_copy(step, s, r).start()
            return carry
        lax.fori_loop(0, tm, body, 0, unroll=8)

    @pl.when(i == 0)
    def _():
        start_all(0, 0)

    @pl.when(i + 1 < nsteps)
    def _():
        start_all(i + 1, 1 - slot)

    def wait_row(r, carry):
        row_copy(i, slot, r).wait()
        return carry
    lax.fori_loop(0, tm, wait_row, 0, unroll=8)

    y_ref[...] = _ple_math(x1_ref[...] + buf_ref[slot], p_ref[...], npw_ref, wg_ref, wp_ref, fw_ref)


def _ple(x1, moe, pos, p, npw, wg, wp, fw, tm):
    n, d = x1.shape
    consts = (npw, wg, wp, fw)
    out_shape = jax.ShapeDtypeStruct((n, d), F32)
    if pos is None:
        row = lambda width: pl.BlockSpec((tm, width), lambda i: (i, 0))
        return pl.pallas_call(
            _ple_kernel,
            grid=(n // tm,),
            in_specs=[row(d), row(d), row(p.shape[-1])] + [_const_spec(c.shape) for c in consts],
            out_specs=row(d),
            out_shape=out_shape,
            compiler_params=_cparams("arbitrary"),
        )(x1, moe, p, *consts)
    row = lambda width: pl.BlockSpec((tm, width), lambda i, pos: (i, 0))
    const = lambda shape: pl.BlockSpec(shape, lambda i, pos: (0,) * len(shape))
    return pl.pallas_call(
        functools.partial(_ple_gather_kernel, tm=tm),
        grid_spec=pltpu.PrefetchScalarGridSpec(
            num_scalar_prefetch=1,
            grid=(n // tm,),
            in_specs=[row(d), pl.BlockSpec(memory_space=pl.ANY), row(p.shape[-1])]
                     + [const(c.shape) for c in consts],
            out_specs=row(d),
            scratch_shapes=[pltpu.VMEM((2, tm, d), F32), pltpu.SemaphoreType.DMA((2,))]),
        out_shape=out_shape,
        compiler_params=_cparams("arbitrary"),
    )(pos, x1, moe, p, *consts)


def _prepare_weights(norm_mix_w, w_in, conv_w, conv_b, dt_bias, a_log, d_skip, ssd_norm_w,
                     lb_logits, hg_norm_w, w_out, norm_ffn_w, w_router_group, b_router_group,
                     w_router_expert, b_router_expert, w_exp_gate, w_exp_up, w_exp_down,
                     norm_ple_w, w_ple_gate, w_ple_proj, final_norm_w, layer):
    i = layer
    row = lambda t: t.reshape(1, -1).astype(F32)
    lane_pad = lambda t: jnp.pad(t, [(0, 0)] * (t.ndim - 1) + [(0, LANES - t.shape[-1])])
    splits = (SSD_D, CONV_DIM, SSD_HEADS, HG_D, HG_D, HG_D, HG_D)
    offs = [0]
    for s in splits:
        offs.append(offs[-1] + s)
    seg = lambda k: w_in[i][:, offs[k]:offs[k + 1]]
    w_in_r = jnp.concatenate([seg(0), seg(1), seg(3), seg(4), seg(5), seg(6), lane_pad(seg(2))],
                             axis=1).astype(BF16)
    w_router = lane_pad(jnp.concatenate([w_router_expert[i], w_router_group[i]], axis=1))
    r_hi = w_router.astype(BF16)
    r_lo = (w_router - r_hi.astype(F32)).astype(BF16)
    r_bias = lane_pad(jnp.concatenate([b_router_expert[i].reshape(-1), b_router_group[i]]).reshape(1, -1))
    lb = jnp.cumsum(jax.nn.softmax(lb_logits.astype(F32), axis=0), axis=0)[i]
    return dict(
        norm_mix_w=row(norm_mix_w[i]), w_in=w_in_r, conv_w=conv_w[i].astype(F32),
        conv_b=row(conv_b[i]), dt_bias=lane_pad(row(dt_bias[i])), a_log=lane_pad(row(a_log[i])),
        d_skip=row(jnp.repeat(d_skip[i], SSD_HEADDIM)), ssd_norm_w=row(ssd_norm_w[i]),
        lb=row(lb), hg_norm_w=row(hg_norm_w[i]), w_out=w_out[i].astype(BF16),
        norm_ffn_w=row(norm_ffn_w[i]), r_hi=r_hi, r_lo=r_lo, r_bias=r_bias,
        w_exp_gate=w_exp_gate[i].astype(BF16), w_exp_up=w_exp_up[i].astype(BF16),
        w_exp_down=w_exp_down[i].astype(BF16), norm_ple_w=row(norm_ple_w[i]),
        w_ple_gate=w_ple_gate[i].astype(BF16), w_ple_proj=w_ple_proj[i].astype(BF16),
        final_norm_w=row(final_norm_w))


def _token_tile(n):
    for tm in (512, 256, 128, 64, 32, 16, 8):
        if n % tm == 0:
            return tm
    raise ValueError(f"token count {n} must be a multiple of 8")


def _trunk(x, p, ssm_in, conv_in, hg_in, w):
    b, l, d = x.shape
    n = b * l
    tm = _token_tile(n)
    tm_small = min(tm, 256)
    x2d = x.reshape(n, d)
    z, xbc, q, f, v, g, dt = _in_proj(x2d, w["norm_mix_w"], w["w_in"], tm_small)
    ssd_params = (w["conv_w"], w["conv_b"], w["dt_bias"], w["a_log"], w["d_skip"], w["ssd_norm_w"])
    if l == 1 and ssm_in is not None:
        y, conv_new, ssm_new = _ssd_step(z, xbc, dt, *ssd_params, conv_in, ssm_in)
        o, hg_new = _hgrn_step(q, f, v, g, w["lb"], w["hg_norm_w"], hg_in)
    else:
        seq = lambda t: t.reshape(b, l, t.shape[-1])
        y, conv_new, ssm_new = _ssd(seq(z), seq(xbc), seq(dt), *ssd_params, conv_in, ssm_in)
        o, hg_new = _hgrn(seq(q), seq(f), seq(v), seq(g), w["lb"], w["hg_norm_w"], hg_in)
    x1, hx = _out_proj(x2d, y.reshape(n, -1), o.reshape(n, -1), w["w_out"], w["norm_ffn_w"],
                       w["r_hi"], w["r_lo"], w["r_bias"], tm_small)
    experts = (w["w_exp_gate"], w["w_exp_up"], w["w_exp_down"])
    if n >= SORTED_MOE_MIN_TOKENS:
        pos, tile_group = _positions(hx, d, MOE_TILE)
        hs = _dispatch(hx, pos, n + N_EGROUPS * MOE_TILE, tm_small)
        moe = _moe_sorted(hs, tile_group, *experts, MOE_TILE)
    else:
        pos = None
        moe = _moe_dense(hx, *experts, tm)
    y_out = _ple(x1, moe, pos, p.reshape(n, -1), w["norm_ple_w"], w["w_ple_gate"], w["w_ple_proj"],
                 w["final_norm_w"], tm_small)
    return y_out.reshape(b, l, d), ssm_new[None], conv_new[None], hg_new[None]


def kernel(x_prompt, x_sample, state_ssm, state_conv, state_hgrn, p_prompt, p_sample, norm_mix_w, w_in, conv_w, conv_b, dt_bias, a_log, d_skip, ssd_norm_w, lb_logits, hg_norm_w, w_out, norm_ffn_w, w_router_group, b_router_group, w_router_expert, b_router_expert, w_exp_gate, w_exp_up, w_exp_down, norm_ple_w, w_ple_gate, w_ple_proj, final_norm_w):
    assert p_prompt.shape[0] == 1, "the per-layer-embedding kernel also applies the final norm: depth 1 only"
    w = _prepare_weights(norm_mix_w, w_in, conv_w, conv_b, dt_bias, a_log, d_skip, ssd_norm_w,
                         lb_logits, hg_norm_w, w_out, norm_ffn_w, w_router_group, b_router_group,
                         w_router_expert, b_router_expert, w_exp_gate, w_exp_up, w_exp_down,
                         norm_ple_w, w_ple_gate, w_ple_proj, final_norm_w, layer=0)
    y_p, ssm_p, conv_p, hg_p = _trunk(x_prompt, p_prompt[0], None, None, None, w)
    y_s, ssm_s, conv_s, hg_s = _trunk(x_sample, p_sample[0], state_ssm[0], state_conv[0],
                                      state_hgrn[0], w)
    return (y_p, y_s, ssm_p, conv_p, hg_p, ssm_s, conv_s, hg_s)
```

```python
import functools

import jax
import jax.numpy as jnp
from jax import lax
from jax.experimental import pallas as pl
from jax.experimental.pallas import tpu as pltpu

F32 = jnp.float32
BF16 = jnp.bfloat16

EPS = 1e-6
SSD_HEADS = 16
SSD_HEADDIM = 64
SSD_D = SSD_HEADS * SSD_HEADDIM
SSD_GROUPS = 2
D_STATE = 128
CONV_W = 4
CONV_DIM = SSD_D + 2 * SSD_GROUPS * D_STATE
HG_HEADS = 8
HG_DK = 128
HG_DV = 128
HG_D = HG_HEADS * HG_DV
N_EGROUPS = 4
EXPERTS_PER_GROUP = 4
N_EXPERTS = N_EGROUPS * EXPERTS_PER_GROUP

LANES = 128
SUBLANES = 8
VMEM_LIMIT = 48 * 1024 * 1024
CHUNK = 128
HG_SAFE_LOG_DECAY = 60.0
ROUTER_ROWS = 32
GID_ROW = N_EXPERTS + N_EGROUPS
MOE_TILE = 512
IN_PROJ_TILE = 256
ROW_TILE = 512
SORTED_MOE_MIN_TOKENS = 8 * SUBLANES * LANES
SEG_WIDTHS = (SSD_D, CONV_DIM, HG_D, HG_D, HG_D, HG_D, LANES)


def _dot(a, b):
    return jnp.dot(a, b, preferred_element_type=F32)


def _dot_nt(a, b):
    return lax.dot_general(a, b, (((1,), (1,)), ((), ())), preferred_element_type=F32)


def _dot_tn(a, b):
    return lax.dot_general(a, b, (((0,), (0,)), ((), ())), preferred_element_type=F32)


def _rms_unit(x):
    return x * lax.rsqrt(jnp.mean(x * x, axis=-1, keepdims=True) + EPS)


def _sigmoid(x):
    return 1.0 / (1.0 + jnp.exp(-x))


def _silu(x):
    return x * _sigmoid(x)


def _split3(x):
    hi = x.astype(BF16)
    r = x - hi.astype(F32)
    mid = r.astype(BF16)
    lo = (r - mid.astype(F32)).astype(BF16)
    return hi, mid, lo


def _cumsum_rows(x, tri):
    hi, mid, lo = _split3(x)
    return _dot(tri, hi) + _dot(tri, mid) + _dot(tri, lo)


def _tri(n):
    r = lax.broadcasted_iota(jnp.int32, (n, n), 0)
    c = lax.broadcasted_iota(jnp.int32, (n, n), 1)
    return r >= c


def _load_rows(ref, qb, rows):
    if qb == rows:
        return ref[0]
    assert qb == 1
    x = jnp.broadcast_to(ref[0], (rows, ref.shape[-1]))
    r = lax.broadcasted_iota(jnp.int32, x.shape, 0)
    return jnp.where(r < qb, x, 0.0)


def _cparams(*sem):
    return pltpu.CompilerParams(dimension_semantics=sem, vmem_limit_bytes=VMEM_LIMIT)


def _const_spec(shape):
    nd = len(shape)
    return pl.BlockSpec(shape, lambda *_: (0,) * nd)


def _inproj_kernel(x_ref, nw_ref, w_ref, *out_refs):
    hb = (_rms_unit(x_ref[...]) * nw_ref[...]).astype(BF16)
    off = 0
    for ref in out_refs:
        n = ref.shape[-1]
        for c0 in range(0, n, 512):
            cw = min(512, n - c0)
            ref[:, c0:c0 + cw] = _dot(hb, w_ref[:, off + c0:off + c0 + cw])
        off += n


def _in_proj(x, nw, w, tm):
    n, d = x.shape
    return pl.pallas_call(
        _inproj_kernel,
        grid=(n // tm,),
        in_specs=[pl.BlockSpec((tm, d), lambda i: (i, 0)), _const_spec(nw.shape),
                  _const_spec(w.shape)],
        out_specs=[pl.BlockSpec((tm, s), lambda i: (i, 0)) for s in SEG_WIDTHS],
        out_shape=[jax.ShapeDtypeStruct((n, s), F32) for s in SEG_WIDTHS],
        compiler_params=_cparams("arbitrary"),
    )(x, nw, w)


def _expand(x, e_ref):
    hi, mid, lo = _split3(x)
    e = e_ref[...]
    return _dot(hi, e) + _dot(mid, e) + _dot(lo, e)


def _ssd_kernel(*refs, qb, rows, has_init):
    if has_init:
        (z_ref, xbc_ref, dt_ref, cw_ref, cb_ref, dtb_ref, alog_ref, dsk_ref, nw_ref, e1_ref, e2_ref,
         cinit_ref, sinit_ref, y_ref, cout_ref, sout_ref, xc_ref, st_ref) = refs
    else:
        (z_ref, xbc_ref, dt_ref, cw_ref, cb_ref, dtb_ref, alog_ref, dsk_ref, nw_ref, e1_ref, e2_ref,
         y_ref, cout_ref, sout_ref, xc_ref, st_ref) = refs
    hist = SUBLANES
    c = pl.program_id(1)
    hpg = SSD_HEADS // SSD_GROUPS
    gw = hpg * SSD_HEADDIM
    gn = D_STATE

    @pl.when(c == 0)
    def _():
        xc_ref[0:hist, :] = jnp.zeros((hist, CONV_DIM), F32)
        if has_init:
            xc_ref[hist - (CONV_W - 1):hist, :] = cinit_ref[0]
            for g in range(SSD_GROUPS):
                st_ref[g] = sinit_ref[0, g * hpg:(g + 1) * hpg].reshape(gw, gn).T
        else:
            st_ref[...] = jnp.zeros(st_ref.shape, F32)

    xc_ref[hist:hist + rows, :] = _load_rows(xbc_ref, qb, rows)
    base = hist - (CONV_W - 1)
    acc = cb_ref[...] + cw_ref[0:1, :] * xc_ref[base:base + rows, :]
    for k in range(1, CONV_W):
        acc = acc + cw_ref[k:k + 1, :] * xc_ref[base + k:base + k + rows, :]
    cout_ref[0] = xc_ref[base + qb:base + qb + CONV_W - 1, :]
    xc_ref[0:hist, :] = xc_ref[rows:rows + hist, :]
    xbc = _silu(acc)
    xs = xbc[:, :SSD_D]
    b_bf = [xbc[:, SSD_D + g * gn:SSD_D + (g + 1) * gn].astype(BF16) for g in range(SSD_GROUPS)]
    c_bf = [xbc[:, SSD_D + (SSD_GROUPS + g) * gn:SSD_D + (SSD_GROUPS + g + 1) * gn].astype(BF16)
            for g in range(SSD_GROUPS)]

    dtr = _load_rows(dt_ref, qb, rows) + dtb_ref[...]
    dt = jnp.maximum(dtr, 0.0) + jnp.log(1.0 + jnp.exp(-jnp.abs(dtr)))
    if qb < rows:
        dt = jnp.where(lax.broadcasted_iota(jnp.int32, dt.shape, 0) < qb, dt, 0.0)
    a = dt * (-jnp.exp(alog_ref[...]))
    tri = _tri(rows)
    a_cs = _cumsum_rows(a, tri.astype(BF16))
    pad = max(rows, LANES)
    a_sq = a_cs if rows == pad else jnp.concatenate(
        [a_cs, jnp.zeros((pad - rows, LANES), F32)], axis=0)
    a_cs_t = a_sq.T
    dt_full = _expand(dt, e1_ref)
    a_full = _expand(a_cs, e1_ref)
    a_rep = _expand(a_cs, e2_ref)
    a_last = a_full[rows - 1:rows, :]

    xdt = xs * dt_full
    xdt_bf = xdt.astype(BF16)
    xw_bf = (xdt * jnp.exp(a_last - a_full)).astype(BF16)
    skip = dsk_ref[...] * xs
    lane = lax.broadcasted_iota(jnp.int32, (rows, LANES), 1)
    first_half = lane < SSD_HEADDIM
    zg = _silu(_load_rows(z_ref, qb, rows))

    for g in range(SSD_GROUPS):
        gs = slice(g * gw, (g + 1) * gw)
        scores = jnp.where(tri, _dot_nt(c_bf[g], b_bf[g]), 0.0)
        st_old = st_ref[g]
        y_off = _dot(c_bf[g], st_old.astype(BF16)) * jnp.exp(a_full[:, gs])
        st_ref[g] = st_old * jnp.exp(a_last[:, gs]) + _dot_tn(b_bf[g], xw_bf[:, gs])
        parts = []
        for pair in range(hpg // 2):
            h0 = g * hpg + 2 * pair
            cols = slice(h0 * SSD_HEADDIM, (h0 + 2) * SSD_HEADDIM)
            x_pair = xdt_bf[:, cols]
            acc = None
            for hh, keep in ((h0, first_half), (h0 + 1, jnp.logical_not(first_half))):
                diff = a_rep[:, hh * LANES:hh * LANES + rows] - a_cs_t[hh:hh + 1, :rows]
                m = (scores * jnp.exp(jnp.minimum(diff, 0.0))).astype(BF16)
                part = _dot(m, jnp.where(keep, x_pair, jnp.zeros_like(x_pair)))
                acc = part if acc is None else acc + part
            parts.append(acc)
        y = jnp.concatenate(parts, axis=1) + y_off + skip[:, gs]
        y = _rms_unit(y * zg[:, gs]) * nw_ref[:, gs]
        y_ref[0, :, gs] = y[:qb].astype(y_ref.dtype)

    @pl.when(c == pl.num_programs(1) - 1)
    def _():
        for g in range(SSD_GROUPS):
            sout_ref[0, g * hpg:(g + 1) * hpg] = st_ref[g].T.reshape(hpg, SSD_HEADDIM, gn)


def _ssd(z, xbc, dt, conv_w, conv_b, dt_bias, a_log, d_skip, norm_w, conv_init, ssm_init):
    b, l, _ = z.shape
    qb = min(CHUNK, l)
    assert l % qb == 0
    rows = max(qb, SUBLANES)
    has_init = ssm_init is not None
    seq = lambda w: pl.BlockSpec((1, qb, w), lambda i, c: (i, c, 0))
    per_b3 = lambda s: pl.BlockSpec((1,) + s, lambda i, c: (i, 0, 0))
    per_b4 = lambda s: pl.BlockSpec((1,) + s, lambda i, c: (i, 0, 0, 0))
    head = jnp.arange(LANES)[:, None]
    e1 = (head == jnp.arange(SSD_D)[None, :] // SSD_HEADDIM).astype(BF16)
    e2 = (head == jnp.arange(SSD_HEADS * LANES)[None, :] // LANES).astype(BF16)
    params = (conv_w, conv_b, dt_bias, a_log, d_skip, norm_w, e1, e2)
    ins = [z, xbc, dt, *params]
    in_specs = [seq(SSD_D), seq(CONV_DIM), seq(LANES)] + [_const_spec(p.shape) for p in params]
    state_shape = (SSD_HEADS, SSD_HEADDIM, D_STATE)
    if has_init:
        ins += [conv_init, ssm_init]
        in_specs += [per_b3((CONV_W - 1, CONV_DIM)), per_b4(state_shape)]
    return pl.pallas_call(
        functools.partial(_ssd_kernel, qb=qb, rows=rows, has_init=has_init),
        grid=(b, l // qb),
        in_specs=in_specs,
        out_specs=[seq(SSD_D), per_b3((CONV_W - 1, CONV_DIM)), per_b4(state_shape)],
        out_shape=[jax.ShapeDtypeStruct((b, l, SSD_D), BF16),
                   jax.ShapeDtypeStruct((b, CONV_W - 1, CONV_DIM), F32),
                   jax.ShapeDtypeStruct((b,) + state_shape, F32)],
        scratch_shapes=[pltpu.VMEM((rows + SUBLANES, CONV_DIM), F32),
                        pltpu.VMEM((SSD_GROUPS, D_STATE, SSD_D // SSD_GROUPS), F32)],
        compiler_params=_cparams("arbitrary", "arbitrary"),
    )(*ins)


def _tokens_to_lanes(x):
    tb, n = x.shape
    return jnp.concatenate([x, jnp.zeros((LANES - tb, n), F32)], axis=0).T


def _softplus(x):
    return jnp.maximum(x, 0.0) + jnp.log(1.0 + jnp.exp(-jnp.abs(x)))


def _step_decay_kernel(dt_ref, dtb_ref, alog_ref, da_ref):
    da_ref[...] = jnp.exp(_softplus(dt_ref[...] + dtb_ref[...]) * (-jnp.exp(alog_ref[...])))


def _ssd_step_kernel(da_ref, z_ref, xbc_ref, dt_ref, cw_ref, cb_ref, dtb_ref, dsk_ref, nw_ref, e1_ref,
                     cs_ref, st_ref, y_ref, cnew_ref, snew_ref, *, tb):
    cd = CONV_DIM
    gn = D_STATE
    hpg = SSD_HEADS // SSD_GROUPS
    gw = hpg * SSD_HEADDIM
    first = pl.program_id(0) * tb
    x_in = xbc_ref[...]
    acc = cb_ref[...] + cw_ref[CONV_W - 1:CONV_W, :] * x_in
    for k in range(CONV_W - 1):
        acc = acc + cw_ref[k:k + 1, :] * cs_ref[:, k * cd:(k + 1) * cd]
    cnew_ref[:, :(CONV_W - 2) * cd] = cs_ref[:, cd:]
    cnew_ref[:, (CONV_W - 2) * cd:] = x_in
    xbc = _silu(acc)
    xs = xbc[:, :SSD_D]
    dt = _softplus(dt_ref[...] + dtb_ref[...])
    xdt_t = _tokens_to_lanes(xs * _expand(dt, e1_ref)).astype(BF16)
    row_tok = lax.broadcasted_iota(jnp.int32, (LANES, tb * gn), 0)
    col_tok = lax.broadcasted_iota(jnp.int32, (LANES, tb * gn), 1) // gn
    lane = lax.broadcasted_iota(jnp.int32, (SSD_HEADDIM, LANES), 1)
    y_groups = []
    for g in range(SSD_GROUPS):
        b_g = xbc[:, SSD_D + g * gn:SSD_D + (g + 1) * gn]
        c_g = xbc[:, SSD_D + (SSD_GROUPS + g) * gn:SSD_D + (SSD_GROUPS + g + 1) * gn]
        b_wide = jnp.concatenate([jnp.tile(b_g, (1, tb)), jnp.zeros((LANES - tb, tb * gn), F32)], axis=0)
        b_diag = jnp.where(row_tok == col_tok, b_wide, 0.0).astype(BF16)
        upd = _dot(xdt_t[g * gw:(g + 1) * gw, :], b_diag)
        y_heads = []
        for hl in range(hpg):
            h = g * hpg + hl
            y_h = jnp.zeros((SSD_HEADDIM, LANES), F32)
            for j in range(tb):
                new = (st_ref[j, h] * da_ref[first + j, h]
                       + upd[hl * SSD_HEADDIM:(hl + 1) * SSD_HEADDIM, j * gn:(j + 1) * gn])
                snew_ref[j, h] = new
                y_h = jnp.where(lane == j, jnp.sum(new * c_g[j:j + 1, :], axis=-1, keepdims=True), y_h)
            y_heads.append(y_h)
        y_groups.append(jnp.concatenate(y_heads, axis=0))
    y = jnp.concatenate(y_groups, axis=0).T[:tb]
    y = (y + dsk_ref[...] * xs) * _silu(z_ref[...])
    for g in range(SSD_GROUPS):
        gs = slice(g * gw, (g + 1) * gw)
        y_ref[:, gs] = (_rms_unit(y[:, gs]) * nw_ref[:, gs]).astype(y_ref.dtype)


def _ssd_step(z, xbc, dt, conv_w, conv_b, dt_bias, a_log, d_skip, norm_w, conv_init, ssm_init):
    b = z.shape[0]
    tb = SUBLANES
    assert b % tb == 0
    da = pl.pallas_call(
        _step_decay_kernel,
        grid=(1,),
        in_specs=[_const_spec(dt.shape), _const_spec(dt_bias.shape), _const_spec(a_log.shape)],
        out_specs=_const_spec(dt.shape),
        out_shape=jax.ShapeDtypeStruct(dt.shape, F32),
    )(dt, dt_bias, a_log)
    head = jnp.arange(LANES)[:, None]
    e1 = (head == jnp.arange(SSD_D)[None, :] // SSD_HEADDIM).astype(BF16)
    params = (conv_w, conv_b, dt_bias, d_skip, norm_w, e1)
    hist = (CONV_W - 1) * CONV_DIM
    row = lambda w: pl.BlockSpec((tb, w), lambda i, da: (i, 0))
    const = lambda shape: pl.BlockSpec(shape, lambda i, da: (0,) * len(shape))
    state_shape = (SSD_HEADS, SSD_HEADDIM, D_STATE)
    st_spec = pl.BlockSpec((tb,) + state_shape, lambda i, da: (i, 0, 0, 0))
    y, conv_new, ssm_new = pl.pallas_call(
        functools.partial(_ssd_step_kernel, tb=tb),
        grid_spec=pltpu.PrefetchScalarGridSpec(
            num_scalar_prefetch=1,
            grid=(b // tb,),
            in_specs=[row(SSD_D), row(CONV_DIM), row(LANES)] + [const(p.shape) for p in params]
                     + [row(hist), st_spec],
            out_specs=[row(SSD_D), row(hist), st_spec]),
        out_shape=[jax.ShapeDtypeStruct((b, SSD_D), BF16), jax.ShapeDtypeStruct((b, hist), F32),
                   jax.ShapeDtypeStruct((b,) + state_shape, F32)],
        compiler_params=_cparams("arbitrary"),
    )(da, z, xbc, dt, *params, conv_init.reshape(b, hist), ssm_init)
    return y, conv_new.reshape(b, CONV_W - 1, CONV_DIM), ssm_new


def _hgrn_kernel(*refs, qb, rows, has_init):
    if has_init:
        (q_ref, f_ref, v_ref, g_ref, lb_ref, nw_ref, sinit_ref,
         o_ref, sout_ref, st_ref, a_ref, kk_ref, gcs_ref) = refs
    else:
        (q_ref, f_ref, v_ref, g_ref, lb_ref, nw_ref,
         o_ref, sout_ref, st_ref, a_ref, kk_ref, gcs_ref) = refs
    c = pl.program_id(1)
    dk = HG_DK

    @pl.when(c == 0)
    def _():
        for h in range(HG_HEADS):
            if has_init:
                st_ref[h] = sinit_ref[0, h].T
            else:
                st_ref[h] = jnp.zeros((HG_DV, HG_DK), F32)

    fr = _load_rows(f_ref, qb, rows)
    lb = lb_ref[...]
    logf = jnp.log(lb + (1.0 - lb) * _sigmoid(fr))
    kk = (1.0 - lb) * _sigmoid(-fr)
    if qb < rows:
        valid = lax.broadcasted_iota(jnp.int32, fr.shape, 0) < qb
        logf = jnp.where(valid, logf, 0.0)
        kk = jnp.where(valid, kk, 0.0)
    tri = _tri(rows)
    gcs = _cumsum_rows(logf, tri.astype(BF16))
    glast = gcs[rows - 1:rows, :]
    q = _load_rows(q_ref, qb, rows)
    qt = q * jnp.exp(gcs)
    kk_ref[...] = kk
    gcs_ref[...] = gcs

    safe = jnp.min(glast) >= -HG_SAFE_LOG_DECAY

    @pl.when(safe)
    def _():
        kt = kk * jnp.exp(-gcs)
        for h in range(HG_HEADS):
            sl = slice(h * dk, (h + 1) * dk)
            a_ref[h] = _dot_nt(qt[:, sl].astype(BF16), kt[:, sl].astype(BF16))

    @pl.when(jnp.logical_not(safe))
    def _():
        ri = lax.broadcasted_iota(jnp.int32, (rows, dk), 0)
        ci = lax.broadcasted_iota(jnp.int32, (rows, rows), 1)
        for h in range(HG_HEADS):
            sl = slice(h * dk, (h + 1) * dk)
            q_h = q[:, sl]
            g_h = gcs[:, sl]

            def cols(j8, a_h):
                base = pl.multiple_of(j8 * SUBLANES, SUBLANES)
                k_blk = kk_ref[pl.ds(base, SUBLANES), sl]
                g_blk = gcs_ref[pl.ds(base, SUBLANES), sl]
                for r in range(SUBLANES):
                    j = base + r
                    t = q_h * k_blk[r:r + 1] * jnp.exp(
                        jnp.where(ri >= j, g_h - g_blk[r:r + 1], -jnp.inf))
                    a_h = jnp.where(ci == j, jnp.sum(t, axis=-1, keepdims=True), a_h)
                return a_h

            a_ref[h] = lax.fori_loop(0, rows // SUBLANES, cols, jnp.zeros((rows, rows), F32))

    khat = kk * jnp.exp(glast - gcs)
    v = _load_rows(v_ref, qb, rows)
    gate = _silu(_load_rows(g_ref, qb, rows))
    for h in range(HG_HEADS):
        sl = slice(h * dk, (h + 1) * dk)
        st = st_ref[h]
        v_bf = v[:, sl].astype(BF16)
        a_h = jnp.where(tri, a_ref[h], 0.0).astype(BF16)
        o = _dot(a_h, v_bf) + _dot_nt(qt[:, sl].astype(BF16), st.astype(BF16))
        st_ref[h] = st * jnp.exp(glast[:, sl]) + _dot_tn(v_bf, khat[:, sl].astype(BF16))
        o = _rms_unit(o) * nw_ref[:, sl] * gate[:, sl]
        o_ref[0, :, sl] = o[:qb].astype(o_ref.dtype)

    @pl.when(c == pl.num_programs(1) - 1)
    def _():
        for h in range(HG_HEADS):
            sout_ref[0, h] = st_ref[h].T


def _hgrn(q, f, v, g, lb, norm_w, init):
    b, l, _ = q.shape
    qb = min(CHUNK, l)
    assert l % qb == 0
    rows = max(qb, SUBLANES)
    has_init = init is not None
    seq = pl.BlockSpec((1, qb, HG_D), lambda i, c: (i, c, 0))
    state_shape = (HG_HEADS, HG_DK, HG_DV)
    per_b = pl.BlockSpec((1,) + state_shape, lambda i, c: (i, 0, 0, 0))
    ins = [q, f, v, g, lb, norm_w]
    in_specs = [seq, seq, seq, seq, _const_spec(lb.shape), _const_spec(norm_w.shape)]
    if has_init:
        ins.append(init)
        in_specs.append(per_b)
    return pl.pallas_call(
        functools.partial(_hgrn_kernel, qb=qb, rows=rows, has_init=has_init),
        grid=(b, l // qb),
        in_specs=in_specs,
        out_specs=[seq, per_b],
        out_shape=[jax.ShapeDtypeStruct((b, l, HG_D), BF16),
                   jax.ShapeDtypeStruct((b,) + state_shape, F32)],
        scratch_shapes=[pltpu.VMEM((HG_HEADS, HG_DV, HG_DK), F32),
                        pltpu.VMEM((HG_HEADS, rows, rows), F32),
                        pltpu.VMEM((rows, HG_D), F32),
                        pltpu.VMEM((rows, HG_D), F32)],
        compiler_params=_cparams("arbitrary", "arbitrary"),
    )(*ins)


def _hgrn_step_kernel(q_ref, f_ref, v_ref, g_ref, lb_ref, nw_ref, st_ref, o_ref, snew_ref, *, tb):
    dk, dv = HG_DK, HG_DV
    fr = f_ref[...]
    lb = lb_ref[...]
    f_t = _tokens_to_lanes(lb + (1.0 - lb) * _sigmoid(fr))
    k_t = _tokens_to_lanes((1.0 - lb) * _sigmoid(-fr))
    q_t = _tokens_to_lanes(q_ref[...])
    v = v_ref[...]
    gate = _silu(g_ref[...])
    sub = lax.broadcasted_iota(jnp.int32, (tb, dv), 0)
    for h in range(HG_HEADS):
        rs = slice(h * dk, (h + 1) * dk)
        vs = slice(h * dv, (h + 1) * dv)
        o_h = jnp.zeros((tb, dv), F32)
        for j in range(tb):
            fcol = jnp.broadcast_to(f_t[rs, j:j + 1], (dk, dv))
            kcol = jnp.broadcast_to(k_t[rs, j:j + 1], (dk, dv))
            qcol = jnp.broadcast_to(q_t[rs, j:j + 1], (dk, dv))
            new = st_ref[j, h] * fcol + kcol * v[j:j + 1, vs]
            snew_ref[j, h] = new
            o_h = jnp.where(sub == j, jnp.sum(new * qcol, axis=0, keepdims=True), o_h)
        o_ref[:, vs] = (_rms_unit(o_h) * nw_ref[:, vs] * gate[:, vs]).astype(o_ref.dtype)


def _hgrn_step(q, f, v, g, lb, norm_w, init):
    b = q.shape[0]
    tb = SUBLANES
    assert b % tb == 0
    row = pl.BlockSpec((tb, HG_D), lambda i: (i, 0))
    state_shape = (HG_HEADS, HG_DK, HG_DV)
    st_spec = pl.BlockSpec((tb,) + state_shape, lambda i: (i, 0, 0, 0))
    return pl.pallas_call(
        functools.partial(_hgrn_step_kernel, tb=tb),
        grid=(b // tb,),
        in_specs=[row, row, row, row, _const_spec(lb.shape), _const_spec(norm_w.shape), st_spec],
        out_specs=[row, st_spec],
        out_shape=[jax.ShapeDtypeStruct((b, HG_D), BF16),
                   jax.ShapeDtypeStruct((b,) + state_shape, F32)],
        compiler_params=_cparams("arbitrary"),
    )(q, f, v, g, lb, norm_w, init)


def _route_t(logits, bias, group=None):
    row = lax.broadcasted_iota(jnp.int32, logits.shape, 0)
    ninf = -jnp.inf
    big = jnp.int32(ROUTER_ROWS)
    is_g = (row >= N_EXPERTS) & (row < N_EXPERTS + N_EGROUPS)
    cmax = lambda t: jnp.max(t, axis=0, keepdims=True)
    csum = lambda t: jnp.sum(t, axis=0, keepdims=True)
    first = lambda m: jnp.min(jnp.where(m, row, big), axis=0, keepdims=True)

    gl = jnp.where(is_g, logits, ninf)
    gp = jnp.exp(gl - cmax(gl))
    gprob = gp / csum(gp)
    biased = logits + bias
    if group is None:
        gb = jnp.where(is_g, biased, ninf)
        gsel = first(gb == cmax(gb))
    else:
        gsel = group + N_EXPERTS
    gw = csum(jnp.where(row == gsel, gprob, 0.0))
    e0 = (gsel - N_EXPERTS) * EXPERTS_PER_GROUP
    in_grp = (row >= e0) & (row < e0 + EXPERTS_PER_GROUP)
    el = jnp.where(in_grp, logits, ninf)
    ep = jnp.exp(el - cmax(el))
    eprob = ep / csum(ep)
    eb = jnp.where(in_grp, biased, ninf)
    i1 = first(eb == cmax(eb))
    eb2 = jnp.where(row == i1, ninf, eb)
    i2 = first(eb2 == cmax(eb2))
    p1 = csum(jnp.where(row == i1, eprob, 0.0))
    p2 = csum(jnp.where(row == i2, eprob, 0.0))
    den = p1 + p2
    gid = jnp.broadcast_to(gsel - N_EXPERTS, (1, logits.shape[1])).astype(F32)
    return (jnp.where(row == i1, gw * (p1 / den), 0.0)
            + jnp.where(row == i2, gw * (p2 / den), 0.0)
            + jnp.where(row == GID_ROW, gid, 0.0))


def _router_logits_t(h, rhi_ref, rlo_ref):
    h_hi = h.astype(BF16)
    h_lo = (h - h_hi.astype(F32)).astype(BF16)
    logits = _dot(h_hi, rhi_ref[...]) + _dot(h_hi, rlo_ref[...]) + _dot(h_lo, rhi_ref[...])
    return logits.T[:ROUTER_ROWS]


def _rows_to_lanes(gates_t):
    r, tm = gates_t.shape
    return jnp.concatenate([gates_t, jnp.zeros((LANES - r, tm), F32)], axis=0).T


def _outproj_kernel(x_ref, y_ref, o_ref, w_ref, nw_ref, rhi_ref, rlo_ref, rb_ref,
                    x1_ref, h_ref, r_ref, *, tiled_rows):
    dy = y_ref.shape[-1]
    tm, d = x_ref.shape
    mix = _dot(y_ref[...], w_ref[:dy, :]) + _dot(o_ref[...], w_ref[dy:, :])
    x1 = x_ref[...] + mix
    x1_ref[...] = x1
    h = _rms_unit(x1) * nw_ref[...]
    gates_t = _route_t(_router_logits_t(h, rhi_ref, rlo_ref), rb_ref[...])
    if tiled_rows:
        h_ref[...] = h.reshape(tm, d // LANES, LANES)
        r_ref[...] = jnp.broadcast_to(gates_t[GID_ROW:GID_ROW + 1, :], (SUBLANES, tm))
    else:
        h_ref[...] = h
        r_ref[...] = _rows_to_lanes(gates_t)


def _out_proj(x, y, o, w, nw, r_hi, r_lo, r_bias, tm, tiled_rows):
    n, d = x.shape
    row = lambda width: pl.BlockSpec((tm, width), lambda i: (i, 0))
    consts = (w, nw, r_hi, r_lo, r_bias)
    if tiled_rows:
        h_spec = pl.BlockSpec((tm, d // LANES, LANES), lambda i: (i, 0, 0))
        h_shape = jax.ShapeDtypeStruct((n, d // LANES, LANES), F32)
        r_spec = pl.BlockSpec((SUBLANES, tm), lambda i: (i, 0))
        r_shape = jax.ShapeDtypeStruct((n // tm * SUBLANES, tm), F32)
    else:
        h_spec, h_shape = row(d), jax.ShapeDtypeStruct((n, d), F32)
        r_spec, r_shape = row(LANES), jax.ShapeDtypeStruct((n, LANES), F32)
    return pl.pallas_call(
        functools.partial(_outproj_kernel, tiled_rows=tiled_rows),
        grid=(n // tm,),
        in_specs=[row(d), row(y.shape[-1]), row(o.shape[-1])] + [_const_spec(c.shape) for c in consts],
        out_specs=[row(d), h_spec, r_spec],
        out_shape=[jax.ShapeDtypeStruct((n, d), F32), h_shape, r_shape],
        compiler_params=_cparams("arbitrary"),
    )(x, y, o, *consts)


def _pos_kernel(gid_ref, pos_ref, tg_ref, *, tile):
    rows, tm = gid_ref.shape
    gid = gid_ref[...]
    ri = lax.broadcasted_iota(jnp.int32, (tm, tm), 0)
    ci = lax.broadcasted_iota(jnp.int32, (tm, tm), 1)
    upper = (ri <= ci).astype(BF16)
    rr = lax.broadcasted_iota(jnp.int32, (rows, rows), 0)
    rc = lax.broadcasted_iota(jnp.int32, (rows, rows), 1)
    earlier = ((rc // SUBLANES < rr // SUBLANES) & (rc % SUBLANES == 0)).astype(BF16)
    tile_lane = lax.broadcasted_iota(jnp.int32, (1, LANES), 1).astype(F32)
    pos = jnp.zeros((rows, tm), F32)
    tiles_before = jnp.zeros((1, 1), F32)
    tile_group = jnp.zeros((1, LANES), F32)
    for g in range(N_EGROUPS):
        onehot = jnp.where(gid == float(g), 1.0, 0.0)
        inc = _dot(onehot.astype(BF16), upper)
        rowtot = jnp.broadcast_to(inc[:, tm - 1:tm], (rows, tm))
        rowpre = _cumsum_rows(rowtot, earlier)
        cnt = rowpre[rows - 1:rows, 0:1] + rowtot[rows - 1:rows, 0:1]
        pos = pos + onehot * (tiles_before * tile + rowpre + inc - 1.0)
        tiles_before = tiles_before + jnp.floor((cnt + (tile - 1.0)) * (1.0 / tile))
        tile_group = tile_group + jnp.where(tile_lane >= tiles_before, 1.0, 0.0)
    pos_ref[...] = pos.astype(jnp.int32)
    tg_ref[...] = tile_group.astype(jnp.int32)


def _positions(gid, n, tile):
    rows, tm = gid.shape
    assert n // tile + N_EGROUPS <= LANES
    pos, tg = pl.pallas_call(
        functools.partial(_pos_kernel, tile=tile),
        grid=(1,),
        in_specs=[_const_spec(gid.shape)],
        out_specs=[_const_spec(gid.shape), _const_spec((1, LANES))],
        out_shape=[jax.ShapeDtypeStruct(gid.shape, jnp.int32),
                   jax.ShapeDtypeStruct((1, LANES), jnp.int32)],
        compiler_params=_cparams("arbitrary"),
    )(gid)
    return pos.reshape(rows // SUBLANES, SUBLANES, tm)[:, 0, :].reshape(n), tg.reshape(LANES)


def _dispatch_kernel(pos_ref, h_ref, hs_in_ref, hs_ref, sem_ref, *, td):
    del hs_in_ref
    i = pl.program_id(0)

    def row_copy(step, r):
        t = step * td + r
        return pltpu.make_async_copy(h_ref.at[t], hs_ref.at[pos_ref[t]], sem_ref.at[step % 2])

    def start_all(step):
        def body(r, carry):
            row_copy(step, r).start()
            return carry
        lax.fori_loop(0, td, body, 0, unroll=8)

    def wait_all(step):
        def body(r, carry):
            row_copy(step, r).wait()
            return carry
        lax.fori_loop(0, td, body, 0, unroll=8)

    start_all(i)

    @pl.when(i >= 1)
    def _():
        wait_all(i - 1)

    @pl.when(i == pl.num_programs(0) - 1)
    def _():
        wait_all(i)


def _dispatch(h3, pos, n_sorted, td):
    n = h3.shape[0]
    return pl.pallas_call(
        functools.partial(_dispatch_kernel, td=td),
        grid_spec=pltpu.PrefetchScalarGridSpec(
            num_scalar_prefetch=1,
            grid=(n // td,),
            in_specs=[pl.BlockSpec(memory_space=pl.ANY), pl.BlockSpec(memory_space=pl.ANY)],
            out_specs=pl.BlockSpec(memory_space=pl.ANY),
            scratch_shapes=[pltpu.SemaphoreType.DMA((2,))]),
        out_shape=jax.ShapeDtypeStruct((n_sorted,) + h3.shape[1:], F32),
        input_output_aliases={2: 0},
        compiler_params=_cparams("arbitrary"),
    )(pos, h3, jnp.zeros((n_sorted,) + h3.shape[1:], F32))


def _expert(h, gates, e, wg, wu, wd):
    lane = lax.broadcasted_iota(jnp.int32, gates.shape, 1)
    gcol = jnp.sum(jnp.where(lane == e, gates, 0.0), axis=-1, keepdims=True)
    act = _silu(_dot(h, wg)) * _dot(h, wu) * gcol
    return _dot(act.astype(BF16), wd)


def _moe_dense_kernel(h_ref, gates_ref, wg_ref, wu_ref, wd_ref, out_ref):
    e = pl.program_id(1)
    part = _expert(h_ref[...].astype(BF16), gates_ref[...], e, wg_ref[0], wu_ref[0], wd_ref[0])

    @pl.when(e == 0)
    def _():
        out_ref[...] = part

    @pl.when(e > 0)
    def _():
        out_ref[...] += part


def _moe_dense(h, gates, wg, wu, wd, tm):
    n, d = h.shape
    ne, _, ff = wg.shape
    row = lambda width: pl.BlockSpec((tm, width), lambda i, e: (i, 0))
    return pl.pallas_call(
        _moe_dense_kernel,
        grid=(n // tm, ne),
        in_specs=[row(d), row(LANES),
                  pl.BlockSpec((1, d, ff), lambda i, e: (e, 0, 0)),
                  pl.BlockSpec((1, d, ff), lambda i, e: (e, 0, 0)),
                  pl.BlockSpec((1, ff, d), lambda i, e: (e, 0, 0))],
        out_specs=row(d),
        out_shape=jax.ShapeDtypeStruct((n, d), F32),
        compiler_params=_cparams("arbitrary", "arbitrary"),
    )(h, gates, wg, wu, wd)


def _moe_sorted_kernel(tg_ref, hs_ref, rhi_ref, rlo_ref, rb_ref, wg_ref, wu_ref, wd_ref, ys_ref):
    g = tg_ref[pl.program_id(0)]
    tm = hs_ref.shape[0]
    d = wg_ref.shape[1]

    @pl.when(g < N_EGROUPS)
    def _():
        h = hs_ref[...].reshape(tm, d)
        gates = _rows_to_lanes(_route_t(_router_logits_t(h, rhi_ref, rlo_ref), rb_ref[...], group=g))
        h_bf = h.astype(BF16)
        acc = None
        for j in range(EXPERTS_PER_GROUP):
            part = _expert(h_bf, gates, g * EXPERTS_PER_GROUP + j, wg_ref[j], wu_ref[j], wd_ref[j])
            acc = part if acc is None else acc + part
        ys_ref[...] = acc.reshape(ys_ref.shape)

    @pl.when(g >= N_EGROUPS)
    def _():
        ys_ref[...] = jnp.zeros(ys_ref.shape, F32)


def _moe_sorted(hs, tile_group, r_hi, r_lo, r_bias, wg, wu, wd, tm):
    n_sorted = hs.shape[0]
    _, d, ff = wg.shape
    epg = EXPERTS_PER_GROUP
    grp = lambda i, tg: (jnp.minimum(tg[i], N_EGROUPS - 1), 0, 0)
    tok = pl.BlockSpec((tm,) + hs.shape[1:], lambda i, tg: (i, 0, 0))
    const = lambda shape: pl.BlockSpec(shape, lambda i, tg: (0,) * len(shape))
    return pl.pallas_call(
        _moe_sorted_kernel,
        grid_spec=pltpu.PrefetchScalarGridSpec(
            num_scalar_prefetch=1,
            grid=(n_sorted // tm,),
            in_specs=[tok, const(r_hi.shape), const(r_lo.shape), const(r_bias.shape),
                      pl.BlockSpec((epg, d, ff), grp), pl.BlockSpec((epg, d, ff), grp),
                      pl.BlockSpec((epg, ff, d), grp)],
            out_specs=tok),
        out_shape=jax.ShapeDtypeStruct(hs.shape, F32),
        compiler_params=_cparams("arbitrary"),
    )(tile_group, hs, r_hi, r_lo, r_bias, wg, wu, wd)


def _ple_math(x, p, npw_ref, wg_ref, wp_ref, fw_ref):
    hn = (_rms_unit(x) * npw_ref[...]).astype(BF16)
    gate = _sigmoid(_dot(hn, wg_ref[...]))
    x = x + gate * _dot(p.astype(BF16), wp_ref[...])
    return _rms_unit(x) * fw_ref[...]


def _ple_kernel(x1_ref, moe_ref, p_ref, npw_ref, wg_ref, wp_ref, fw_ref, y_ref):
    y_ref[...] = _ple_math(x1_ref[...] + moe_ref[...], p_ref[...], npw_ref, wg_ref, wp_ref, fw_ref)


def _ple_gather_kernel(pos_ref, x1_ref, ys_ref, p_ref, npw_ref, wg_ref, wp_ref, fw_ref, y_ref,
                       buf_ref, sem_ref, *, tm):
    i = pl.program_id(0)
    nsteps = pl.num_programs(0)
    slot = i % 2

    def row_copy(step, s, r):
        return pltpu.make_async_copy(ys_ref.at[pos_ref[step * tm + r]], buf_ref.at[s, r], sem_ref.at[s])

    def start_all(step, s):
        def body(r, carry):
            row_copy(step, s, r).start()
            return carry
        lax.fori_loop(0, tm, body, 0, unroll=8)

    @pl.when(i == 0)
    def _():
        start_all(0, 0)

    @pl.when(i + 1 < nsteps)
    def _():
        start_all(i + 1, 1 - slot)

    def wait_row(r, carry):
        row_copy(i, slot, r).wait()
        return carry
    lax.fori_loop(0, tm, wait_row, 0, unroll=8)

    moe = buf_ref[slot].reshape(x1_ref.shape)
    y_ref[...] = _ple_math(x1_ref[...] + moe, p_ref[...], npw_ref, wg_ref, wp_ref, fw_ref)


def _ple(x1, moe, pos, p, npw, wg, wp, fw, tm):
    n, d = x1.shape
    consts = (npw, wg, wp, fw)
    out_shape = jax.ShapeDtypeStruct((n, d), F32)
    if pos is None:
        row = lambda width: pl.BlockSpec((tm, width), lambda i: (i, 0))
        return pl.pallas_call(
            _ple_kernel,
            grid=(n // tm,),
            in_specs=[row(d), row(d), row(p.shape[-1])] + [_const_spec(c.shape) for c in consts],
            out_specs=row(d),
            out_shape=out_shape,
            compiler_params=_cparams("arbitrary"),
        )(x1, moe, p, *consts)
    row = lambda width: pl.BlockSpec((tm, width), lambda i, pos: (i, 0))
    const = lambda shape: pl.BlockSpec(shape, lambda i, pos: (0,) * len(shape))
    return pl.pallas_call(
        functools.partial(_ple_gather_kernel, tm=tm),
        grid_spec=pltpu.PrefetchScalarGridSpec(
            num_scalar_prefetch=1,
            grid=(n // tm,),
            in_specs=[row(d), pl.BlockSpec(memory_space=pl.ANY), row(p.shape[-1])]
                     + [const(c.shape) for c in consts],
            out_specs=row(d),
            scratch_shapes=[pltpu.VMEM((2, tm) + moe.shape[1:], F32), pltpu.SemaphoreType.DMA((2,))]),
        out_shape=out_shape,
        compiler_params=_cparams("arbitrary"),
    )(pos, x1, moe, p, *consts)


def _prepare_weights(norm_mix_w, w_in, conv_w, conv_b, dt_bias, a_log, d_skip, ssd_norm_w,
                     lb_logits, hg_norm_w, w_out, norm_ffn_w, w_router_group, b_router_group,
                     w_router_expert, b_router_expert, w_exp_gate, w_exp_up, w_exp_down,
                     norm_ple_w, w_ple_gate, w_ple_proj, final_norm_w, layer):
    i = layer
    row = lambda t: t.reshape(1, -1).astype(F32)
    lane_pad = lambda t: jnp.pad(t, [(0, 0)] * (t.ndim - 1) + [(0, LANES - t.shape[-1])])
    splits = (SSD_D, CONV_DIM, SSD_HEADS, HG_D, HG_D, HG_D, HG_D)
    offs = [0]
    for s in splits:
        offs.append(offs[-1] + s)
    seg = lambda k: w_in[i][:, offs[k]:offs[k + 1]]
    w_in_r = jnp.concatenate([seg(0), seg(1), seg(3), seg(4), seg(5), seg(6), lane_pad(seg(2))],
                             axis=1).astype(BF16)
    w_router = lane_pad(jnp.concatenate([w_router_expert[i], w_router_group[i]], axis=1).astype(F32))
    r_hi = w_router.astype(BF16)
    r_lo = (w_router - r_hi.astype(F32)).astype(BF16)
    r_bias = jnp.pad(jnp.concatenate([b_router_expert[i].reshape(-1), b_router_group[i]]).astype(F32),
                     [(0, ROUTER_ROWS - N_EXPERTS - N_EGROUPS)]).reshape(ROUTER_ROWS, 1)
    lb = jnp.cumsum(jax.nn.softmax(lb_logits.astype(F32), axis=0), axis=0)[i]
    return dict(
        norm_mix_w=row(norm_mix_w[i]), w_in=w_in_r, conv_w=conv_w[i].astype(F32),
        conv_b=row(conv_b[i]), dt_bias=lane_pad(row(dt_bias[i])), a_log=lane_pad(row(a_log[i])),
        d_skip=row(jnp.repeat(d_skip[i], SSD_HEADDIM)), ssd_norm_w=row(ssd_norm_w[i]),
        lb=row(lb), hg_norm_w=row(hg_norm_w[i]), w_out=w_out[i].astype(BF16),
        norm_ffn_w=row(norm_ffn_w[i]), r_hi=r_hi, r_lo=r_lo, r_bias=r_bias,
        w_exp_gate=w_exp_gate[i].astype(BF16), w_exp_up=w_exp_up[i].astype(BF16),
        w_exp_down=w_exp_down[i].astype(BF16), norm_ple_w=row(norm_ple_w[i]),
        w_ple_gate=w_ple_gate[i].astype(BF16), w_ple_proj=w_ple_proj[i].astype(BF16),
        final_norm_w=row(final_norm_w))


def _token_tile(n, cap):
    tm = cap
    while tm >= SUBLANES:
        if n % tm == 0:
            return tm
        tm //= 2
    raise ValueError(f"token count {n} must be a multiple of {SUBLANES}")


def _trunk(x, p, ssm_in, conv_in, hg_in, w):
    b, l, d = x.shape
    n = b * l
    tm = _token_tile(n, ROW_TILE)
    x2d = x.reshape(n, d)
    z, xbc, q, f, v, g, dt = _in_proj(x2d, w["norm_mix_w"], w["w_in"], _token_tile(n, IN_PROJ_TILE))
    ssd_params = (w["conv_w"], w["conv_b"], w["dt_bias"], w["a_log"], w["d_skip"], w["ssd_norm_w"])
    if l == 1 and ssm_in is not None:
        y, conv_new, ssm_new = _ssd_step(z, xbc, dt, *ssd_params, conv_in, ssm_in)
        o, hg_new = _hgrn_step(q, f, v, g, w["lb"], w["hg_norm_w"], hg_in)
    else:
        seq = lambda t: t.reshape(b, l, t.shape[-1])
        y, conv_new, ssm_new = _ssd(seq(z), seq(xbc), seq(dt), *ssd_params, conv_in, ssm_in)
        o, hg_new = _hgrn(seq(q), seq(f), seq(v), seq(g), w["lb"], w["hg_norm_w"], hg_in)
    router = (w["r_hi"], w["r_lo"], w["r_bias"])
    experts = (w["w_exp_gate"], w["w_exp_up"], w["w_exp_down"])
    sorted_moe = n >= SORTED_MOE_MIN_TOKENS
    x1, h, routed = _out_proj(x2d, y.reshape(n, -1), o.reshape(n, -1), w["w_out"], w["norm_ffn_w"],
                              *router, tm, tiled_rows=sorted_moe)
    if sorted_moe:
        pos, tile_group = _positions(routed, n, MOE_TILE)
        hs = _dispatch(h, pos, n + N_EGROUPS * MOE_TILE, tm)
        moe = _moe_sorted(hs, tile_group, *router, *experts, MOE_TILE)
    else:
        pos = None
        moe = _moe_dense(h, routed, *experts, tm)
    y_out = _ple(x1, moe, pos, p.reshape(n, -1), w["norm_ple_w"], w["w_ple_gate"], w["w_ple_proj"],
                 w["final_norm_w"], tm)
    return y_out.reshape(b, l, d), ssm_new[None], conv_new[None], hg_new[None]


def kernel(x_prompt, x_sample, state_ssm, state_conv, state_hgrn, p_prompt, p_sample, norm_mix_w, w_in, conv_w, conv_b, dt_bias, a_log, d_skip, ssd_norm_w, lb_logits, hg_norm_w, w_out, norm_ffn_w, w_router_group, b_router_group, w_router_expert, b_router_expert, w_exp_gate, w_exp_up, w_exp_down, norm_ple_w, w_ple_gate, w_ple_proj, final_norm_w):
    assert p_prompt.shape[0] == 1, "the per-layer-embedding kernel also applies the final norm: depth 1 only"
    w = _prepare_weights(norm_mix_w, w_in, conv_w, conv_b, dt_bias, a_log, d_skip, ssd_norm_w,
                         lb_logits, hg_norm_w, w_out, norm_ffn_w, w_router_group, b_router_group,
                         w_router_expert, b_router_expert, w_exp_gate, w_exp_up, w_exp_down,
                         norm_ple_w, w_ple_gate, w_ple_proj, final_norm_w, layer=0)
    y_p, ssm_p, conv_p, hg_p = _trunk(x_prompt, p_prompt[0], None, None, None, w)
    y_s, ssm_s, conv_s, hg_s = _trunk(x_sample, p_sample[0], state_ssm[0], state_conv[0],
                                      state_hgrn[0], w)
    return (y_p, y_s, ssm_p, conv_p, hg_p, ssm_s, conv_s, hg_s)
```

```python
import functools

import jax
import jax.numpy as jnp
from jax import lax
from jax.experimental import pallas as pl
from jax.experimental.pallas import tpu as pltpu

F32 = jnp.float32
BF16 = jnp.bfloat16

EPS = 1e-6
SSD_HEADS = 16
SSD_HEADDIM = 64
SSD_D = SSD_HEADS * SSD_HEADDIM
SSD_GROUPS = 2
D_STATE = 128
CONV_W = 4
CONV_DIM = SSD_D + 2 * SSD_GROUPS * D_STATE
HG_HEADS = 8
HG_DK = 128
HG_DV = 128
HG_D = HG_HEADS * HG_DV
N_EGROUPS = 4
EXPERTS_PER_GROUP = 4
N_EXPERTS = N_EGROUPS * EXPERTS_PER_GROUP

LANES = 128
SUBLANES = 8
VMEM_LIMIT = 48 * 1024 * 1024
CHUNK = 128
HG_SAFE_LOG_DECAY = 60.0
ROUTER_ROWS = 32
GID_ROW = N_EXPERTS + N_EGROUPS
MOE_TILE = 512
IN_PROJ_TILE = 256
ROW_TILE = 512
SORTED_MOE_MIN_TOKENS = 8 * SUBLANES * LANES
SEG_WIDTHS = (SSD_D, CONV_DIM, HG_D, HG_D, HG_D, HG_D, LANES)


def _dot(a, b):
    return jnp.dot(a, b, preferred_element_type=F32)


def _dot_nt(a, b):
    return lax.dot_general(a, b, (((1,), (1,)), ((), ())), preferred_element_type=F32)


def _dot_tn(a, b):
    return lax.dot_general(a, b, (((0,), (0,)), ((), ())), preferred_element_type=F32)


def _rms_unit(x):
    return x * lax.rsqrt(jnp.mean(x * x, axis=-1, keepdims=True) + EPS)


def _sigmoid(x):
    return 1.0 / (1.0 + jnp.exp(-x))


def _silu(x):
    return x * _sigmoid(x)


def _split3(x):
    hi = x.astype(BF16)
    r = x - hi.astype(F32)
    mid = r.astype(BF16)
    lo = (r - mid.astype(F32)).astype(BF16)
    return hi, mid, lo


def _cumsum_rows(x, tri):
    hi, mid, lo = _split3(x)
    return _dot(tri, hi) + _dot(tri, mid) + _dot(tri, lo)


def _tri(n):
    r = lax.broadcasted_iota(jnp.int32, (n, n), 0)
    c = lax.broadcasted_iota(jnp.int32, (n, n), 1)
    return r >= c


def _load_rows(ref, qb, rows):
    if qb == rows:
        return ref[0]
    assert qb == 1
    x = jnp.broadcast_to(ref[0], (rows, ref.shape[-1]))
    r = lax.broadcasted_iota(jnp.int32, x.shape, 0)
    return jnp.where(r < qb, x, 0.0)


def _cparams(*sem):
    return pltpu.CompilerParams(dimension_semantics=sem, vmem_limit_bytes=VMEM_LIMIT)


def _const_spec(shape):
    nd = len(shape)
    return pl.BlockSpec(shape, lambda *_: (0,) * nd)


def _inproj_kernel(x_ref, nw_ref, w_ref, *out_refs):
    hb = (_rms_unit(x_ref[...]) * nw_ref[...]).astype(BF16)
    off = 0
    for ref in out_refs:
        n = ref.shape[-1]
        for c0 in range(0, n, 512):
            cw = min(512, n - c0)
            ref[:, c0:c0 + cw] = _dot(hb, w_ref[:, off + c0:off + c0 + cw])
        off += n


def _in_proj(x, nw, w, tm):
    n, d = x.shape
    return pl.pallas_call(
        _inproj_kernel,
        grid=(n // tm,),
        in_specs=[pl.BlockSpec((tm, d), lambda i: (i, 0)), _const_spec(nw.shape),
                  _const_spec(w.shape)],
        out_specs=[pl.BlockSpec((tm, s), lambda i: (i, 0)) for s in SEG_WIDTHS],
        out_shape=[jax.ShapeDtypeStruct((n, s), F32) for s in SEG_WIDTHS],
        compiler_params=_cparams("arbitrary"),
    )(x, nw, w)


def _expand(x, e_ref):
    hi, mid, lo = _split3(x)
    e = e_ref[...]
    return _dot(hi, e) + _dot(mid, e) + _dot(lo, e)


def _ssd_kernel(*refs, qb, rows, has_init):
    if has_init:
        (z_ref, xbc_ref, dt_ref, cw_ref, cb_ref, dtb_ref, alog_ref, dsk_ref, nw_ref, e1_ref, e2_ref,
         cinit_ref, sinit_ref, y_ref, cout_ref, sout_ref, xc_ref, st_ref) = refs
    else:
        (z_ref, xbc_ref, dt_ref, cw_ref, cb_ref, dtb_ref, alog_ref, dsk_ref, nw_ref, e1_ref, e2_ref,
         y_ref, cout_ref, sout_ref, xc_ref, st_ref) = refs
    hist = SUBLANES
    c = pl.program_id(1)
    hpg = SSD_HEADS // SSD_GROUPS
    gw = hpg * SSD_HEADDIM
    gn = D_STATE

    @pl.when(c == 0)
    def _():
        xc_ref[0:hist, :] = jnp.zeros((hist, CONV_DIM), F32)
        if has_init:
            xc_ref[hist - (CONV_W - 1):hist, :] = cinit_ref[0]
            for g in range(SSD_GROUPS):
                st_ref[g] = sinit_ref[0, g * hpg:(g + 1) * hpg].reshape(gw, gn).T
        else:
            st_ref[...] = jnp.zeros(st_ref.shape, F32)

    xc_ref[hist:hist + rows, :] = _load_rows(xbc_ref, qb, rows)
    base = hist - (CONV_W - 1)
    acc = cb_ref[...] + cw_ref[0:1, :] * xc_ref[base:base + rows, :]
    for k in range(1, CONV_W):
        acc = acc + cw_ref[k:k + 1, :] * xc_ref[base + k:base + k + rows, :]
    cout_ref[0] = xc_ref[base + qb:base + qb + CONV_W - 1, :]
    xc_ref[0:hist, :] = xc_ref[rows:rows + hist, :]
    xbc = _silu(acc)
    xs = xbc[:, :SSD_D]
    b_bf = [xbc[:, SSD_D + g * gn:SSD_D + (g + 1) * gn].astype(BF16) for g in range(SSD_GROUPS)]
    c_bf = [xbc[:, SSD_D + (SSD_GROUPS + g) * gn:SSD_D + (SSD_GROUPS + g + 1) * gn].astype(BF16)
            for g in range(SSD_GROUPS)]

    dtr = _load_rows(dt_ref, qb, rows) + dtb_ref[...]
    dt = jnp.maximum(dtr, 0.0) + jnp.log(1.0 + jnp.exp(-jnp.abs(dtr)))
    if qb < rows:
        dt = jnp.where(lax.broadcasted_iota(jnp.int32, dt.shape, 0) < qb, dt, 0.0)
    a = dt * (-jnp.exp(alog_ref[...]))
    tri = _tri(rows)
    a_cs = _cumsum_rows(a, tri.astype(BF16))
    pad = max(rows, LANES)
    a_sq = a_cs if rows == pad else jnp.concatenate(
        [a_cs, jnp.zeros((pad - rows, LANES), F32)], axis=0)
    a_cs_t = a_sq.T
    dt_full = _expand(dt, e1_ref)
    a_full = _expand(a_cs, e1_ref)
    a_rep = _expand(a_cs, e2_ref)
    a_last = a_full[rows - 1:rows, :]

    xdt = xs * dt_full
    xdt_bf = xdt.astype(BF16)
    xw_bf = (xdt * jnp.exp(a_last - a_full)).astype(BF16)
    skip = dsk_ref[...] * xs
    lane = lax.broadcasted_iota(jnp.int32, (rows, LANES), 1)
    first_half = lane < SSD_HEADDIM
    zg = _silu(_load_rows(z_ref, qb, rows))

    for g in range(SSD_GROUPS):
        gs = slice(g * gw, (g + 1) * gw)
        scores = jnp.where(tri, _dot_nt(c_bf[g], b_bf[g]), 0.0)
        st_old = st_ref[g]
        y_off = _dot(c_bf[g], st_old.astype(BF16)) * jnp.exp(a_full[:, gs])
        st_ref[g] = st_old * jnp.exp(a_last[:, gs]) + _dot_tn(b_bf[g], xw_bf[:, gs])
        parts = []
        for pair in range(hpg // 2):
            h0 = g * hpg + 2 * pair
            cols = slice(h0 * SSD_HEADDIM, (h0 + 2) * SSD_HEADDIM)
            x_pair = xdt_bf[:, cols]
            acc = None
            for hh, keep in ((h0, first_half), (h0 + 1, jnp.logical_not(first_half))):
                diff = a_rep[:, hh * LANES:hh * LANES + rows] - a_cs_t[hh:hh + 1, :rows]
                m = (scores * jnp.exp(jnp.minimum(diff, 0.0))).astype(BF16)
                part = _dot(m, jnp.where(keep, x_pair, jnp.zeros_like(x_pair)))
                acc = part if acc is None else acc + part
            parts.append(acc)
        y = jnp.concatenate(parts, axis=1) + y_off + skip[:, gs]
        y = _rms_unit(y * zg[:, gs]) * nw_ref[:, gs]
        y_ref[0, :, gs] = y[:qb].astype(y_ref.dtype)

    @pl.when(c == pl.num_programs(1) - 1)
    def _():
        for g in range(SSD_GROUPS):
            sout_ref[0, g * hpg:(g + 1) * hpg] = st_ref[g].T.reshape(hpg, SSD_HEADDIM, gn)


def _ssd(z, xbc, dt, conv_w, conv_b, dt_bias, a_log, d_skip, norm_w, conv_init, ssm_init):
    b, l, _ = z.shape
    qb = min(CHUNK, l)
    assert l % qb == 0
    rows = max(qb, SUBLANES)
    has_init = ssm_init is not None
    seq = lambda w: pl.BlockSpec((1, qb, w), lambda i, c: (i, c, 0))
    per_b3 = lambda s: pl.BlockSpec((1,) + s, lambda i, c: (i, 0, 0))
    per_b4 = lambda s: pl.BlockSpec((1,) + s, lambda i, c: (i, 0, 0, 0))
    head = jnp.arange(LANES)[:, None]
    e1 = (head == jnp.arange(SSD_D)[None, :] // SSD_HEADDIM).astype(BF16)
    e2 = (head == jnp.arange(SSD_HEADS * LANES)[None, :] // LANES).astype(BF16)
    params = (conv_w, conv_b, dt_bias, a_log, d_skip, norm_w, e1, e2)
    ins = [z, xbc, dt, *params]
    in_specs = [seq(SSD_D), seq(CONV_DIM), seq(LANES)] + [_const_spec(p.shape) for p in params]
    state_shape = (SSD_HEADS, SSD_HEADDIM, D_STATE)
    if has_init:
        ins += [conv_init, ssm_init]
        in_specs += [per_b3((CONV_W - 1, CONV_DIM)), per_b4(state_shape)]
    return pl.pallas_call(
        functools.partial(_ssd_kernel, qb=qb, rows=rows, has_init=has_init),
        grid=(b, l // qb),
        in_specs=in_specs,
        out_specs=[seq(SSD_D), per_b3((CONV_W - 1, CONV_DIM)), per_b4(state_shape)],
        out_shape=[jax.ShapeDtypeStruct((b, l, SSD_D), BF16),
                   jax.ShapeDtypeStruct((b, CONV_W - 1, CONV_DIM), F32),
                   jax.ShapeDtypeStruct((b,) + state_shape, F32)],
        scratch_shapes=[pltpu.VMEM((rows + SUBLANES, CONV_DIM), F32),
                        pltpu.VMEM((SSD_GROUPS, D_STATE, SSD_D // SSD_GROUPS), F32)],
        compiler_params=_cparams("arbitrary", "arbitrary"),
    )(*ins)


def _tokens_to_lanes(x):
    tb, n = x.shape
    return jnp.concatenate([x, jnp.zeros((LANES - tb, n), F32)], axis=0).T


def _softplus(x):
    return jnp.maximum(x, 0.0) + jnp.log(1.0 + jnp.exp(-jnp.abs(x)))


def _step_decay_kernel(dt_ref, dtb_ref, alog_ref, da_ref):
    da_ref[...] = jnp.exp(_softplus(dt_ref[...] + dtb_ref[...]) * (-jnp.exp(alog_ref[...])))


def _ssd_step_kernel(da_ref, z_ref, xbc_ref, dt_ref, cw_ref, cb_ref, dtb_ref, dsk_ref, nw_ref, e1_ref,
                     cs_ref, st_ref, y_ref, cnew_ref, snew_ref, *, tb):
    cd = CONV_DIM
    gn = D_STATE
    hpg = SSD_HEADS // SSD_GROUPS
    gw = hpg * SSD_HEADDIM
    first = pl.program_id(0) * tb
    x_in = xbc_ref[...]
    acc = cb_ref[...] + cw_ref[CONV_W - 1:CONV_W, :] * x_in
    for k in range(CONV_W - 1):
        acc = acc + cw_ref[k:k + 1, :] * cs_ref[:, k * cd:(k + 1) * cd]
    cnew_ref[:, :(CONV_W - 2) * cd] = cs_ref[:, cd:]
    cnew_ref[:, (CONV_W - 2) * cd:] = x_in
    xbc = _silu(acc)
    xs = xbc[:, :SSD_D]
    dt = _softplus(dt_ref[...] + dtb_ref[...])
    xdt_t = _tokens_to_lanes(xs * _expand(dt, e1_ref)).astype(BF16)
    row_tok = lax.broadcasted_iota(jnp.int32, (LANES, tb * gn), 0)
    col_tok = lax.broadcasted_iota(jnp.int32, (LANES, tb * gn), 1) // gn
    lane = lax.broadcasted_iota(jnp.int32, (SSD_HEADDIM, LANES), 1)
    y_groups = []
    for g in range(SSD_GROUPS):
        b_g = xbc[:, SSD_D + g * gn:SSD_D + (g + 1) * gn]
        c_g = xbc[:, SSD_D + (SSD_GROUPS + g) * gn:SSD_D + (SSD_GROUPS + g + 1) * gn]
        b_wide = jnp.concatenate([jnp.tile(b_g, (1, tb)), jnp.zeros((LANES - tb, tb * gn), F32)], axis=0)
        b_diag = jnp.where(row_tok == col_tok, b_wide, 0.0).astype(BF16)
        upd = _dot(xdt_t[g * gw:(g + 1) * gw, :], b_diag)
        y_heads = []
        for hl in range(hpg):
            h = g * hpg + hl
            y_h = jnp.zeros((SSD_HEADDIM, LANES), F32)
            for j in range(tb):
                new = (st_ref[j, h] * da_ref[first + j, h]
                       + upd[hl * SSD_HEADDIM:(hl + 1) * SSD_HEADDIM, j * gn:(j + 1) * gn])
                snew_ref[j, h] = new
                y_h = jnp.where(lane == j, jnp.sum(new * c_g[j:j + 1, :], axis=-1, keepdims=True), y_h)
            y_heads.append(y_h)
        y_groups.append(jnp.concatenate(y_heads, axis=0))
    y = jnp.concatenate(y_groups, axis=0).T[:tb]
    y = (y + dsk_ref[...] * xs) * _silu(z_ref[...])
    for g in range(SSD_GROUPS):
        gs = slice(g * gw, (g + 1) * gw)
        y_ref[:, gs] = (_rms_unit(y[:, gs]) * nw_ref[:, gs]).astype(y_ref.dtype)


def _ssd_step(z, xbc, dt, conv_w, conv_b, dt_bias, a_log, d_skip, norm_w, conv_init, ssm_init):
    b = z.shape[0]
    tb = SUBLANES
    assert b % tb == 0
    da = pl.pallas_call(
        _step_decay_kernel,
        grid=(1,),
        in_specs=[_const_spec(dt.shape), _const_spec(dt_bias.shape), _const_spec(a_log.shape)],
        out_specs=_const_spec(dt.shape),
        out_shape=jax.ShapeDtypeStruct(dt.shape, F32),
    )(dt, dt_bias, a_log)
    head = jnp.arange(LANES)[:, None]
    e1 = (head == jnp.arange(SSD_D)[None, :] // SSD_HEADDIM).astype(BF16)
    params = (conv_w, conv_b, dt_bias, d_skip, norm_w, e1)
    hist = (CONV_W - 1) * CONV_DIM
    row = lambda w: pl.BlockSpec((tb, w), lambda i, da: (i, 0))
    const = lambda shape: pl.BlockSpec(shape, lambda i, da: (0,) * len(shape))
    state_shape = (SSD_HEADS, SSD_HEADDIM, D_STATE)
    st_spec = pl.BlockSpec((tb,) + state_shape, lambda i, da: (i, 0, 0, 0))
    y, conv_new, ssm_new = pl.pallas_call(
        functools.partial(_ssd_step_kernel, tb=tb),
        grid_spec=pltpu.PrefetchScalarGridSpec(
            num_scalar_prefetch=1,
            grid=(b // tb,),
            in_specs=[row(SSD_D), row(CONV_DIM), row(LANES)] + [const(p.shape) for p in params]
                     + [row(hist), st_spec],
            out_specs=[row(SSD_D), row(hist), st_spec]),
        out_shape=[jax.ShapeDtypeStruct((b, SSD_D), BF16), jax.ShapeDtypeStruct((b, hist), F32),
                   jax.ShapeDtypeStruct((b,) + state_shape, F32)],
        compiler_params=_cparams("arbitrary"),
    )(da, z, xbc, dt, *params, conv_init.reshape(b, hist), ssm_init)
    return y, conv_new.reshape(b, CONV_W - 1, CONV_DIM), ssm_new


def _hgrn_kernel(*refs, qb, rows, has_init):
    if has_init:
        (q_ref, f_ref, v_ref, g_ref, lb_ref, nw_ref, sinit_ref,
         o_ref, sout_ref, st_ref, a_ref, kk_ref, gcs_ref) = refs
    else:
        (q_ref, f_ref, v_ref, g_ref, lb_ref, nw_ref,
         o_ref, sout_ref, st_ref, a_ref, kk_ref, gcs_ref) = refs
    c = pl.program_id(1)
    dk = HG_DK

    @pl.when(c == 0)
    def _():
        for h in range(HG_HEADS):
            if has_init:
                st_ref[h] = sinit_ref[0, h].T
            else:
                st_ref[h] = jnp.zeros((HG_DV, HG_DK), F32)

    fr = _load_rows(f_ref, qb, rows)
    lb = lb_ref[...]
    sig = _sigmoid(fr)
    logf = jnp.log(lb + (1.0 - lb) * sig)
    kk = (1.0 - lb) * (1.0 - sig)
    if qb < rows:
        valid = lax.broadcasted_iota(jnp.int32, fr.shape, 0) < qb
        logf = jnp.where(valid, logf, 0.0)
        kk = jnp.where(valid, kk, 0.0)
    tri = _tri(rows)
    gcs = _cumsum_rows(logf, tri.astype(BF16))
    glast = gcs[rows - 1:rows, :]
    q = _load_rows(q_ref, qb, rows)
    qt = q * jnp.exp(gcs)
    kk_ref[...] = kk
    gcs_ref[...] = gcs

    safe = jnp.min(glast) >= -HG_SAFE_LOG_DECAY

    @pl.when(safe)
    def _():
        kt = kk * jnp.exp(-gcs)
        for h in range(HG_HEADS):
            sl = slice(h * dk, (h + 1) * dk)
            a_ref[h] = _dot_nt(qt[:, sl].astype(BF16), kt[:, sl].astype(BF16))

    @pl.when(jnp.logical_not(safe))
    def _():
        ri = lax.broadcasted_iota(jnp.int32, (rows, dk), 0)
        ci = lax.broadcasted_iota(jnp.int32, (rows, rows), 1)
        for h in range(HG_HEADS):
            sl = slice(h * dk, (h + 1) * dk)
            q_h = q[:, sl]
            g_h = gcs[:, sl]

            def cols(j8, a_h):
                base = pl.multiple_of(j8 * SUBLANES, SUBLANES)
                k_blk = kk_ref[pl.ds(base, SUBLANES), sl]
                g_blk = gcs_ref[pl.ds(base, SUBLANES), sl]
                for r in range(SUBLANES):
                    j = base + r
                    t = q_h * k_blk[r:r + 1] * jnp.exp(
                        jnp.where(ri >= j, g_h - g_blk[r:r + 1], -jnp.inf))
                    a_h = jnp.where(ci == j, jnp.sum(t, axis=-1, keepdims=True), a_h)
                return a_h

            a_ref[h] = lax.fori_loop(0, rows // SUBLANES, cols, jnp.zeros((rows, rows), F32))

    khat = kk * jnp.exp(glast - gcs)
    v = _load_rows(v_ref, qb, rows)
    gate = _silu(_load_rows(g_ref, qb, rows))
    for h in range(HG_HEADS):
        sl = slice(h * dk, (h + 1) * dk)
        st = st_ref[h]
        v_bf = v[:, sl].astype(BF16)
        a_h = jnp.where(tri, a_ref[h], 0.0).astype(BF16)
        o = _dot(a_h, v_bf) + _dot_nt(qt[:, sl].astype(BF16), st.astype(BF16))
        st_ref[h] = st * jnp.exp(glast[:, sl]) + _dot_tn(v_bf, khat[:, sl].astype(BF16))
        o = _rms_unit(o) * nw_ref[:, sl] * gate[:, sl]
        o_ref[0, :, sl] = o[:qb].astype(o_ref.dtype)

    @pl.when(c == pl.num_programs(1) - 1)
    def _():
        for h in range(HG_HEADS):
            sout_ref[0, h] = st_ref[h].T


def _hgrn(q, f, v, g, lb, norm_w, init):
    b, l, _ = q.shape
    qb = min(CHUNK, l)
    assert l % qb == 0
    rows = max(qb, SUBLANES)
    has_init = init is not None
    seq = pl.BlockSpec((1, qb, HG_D), lambda i, c: (i, c, 0))
    state_shape = (HG_HEADS, HG_DK, HG_DV)
    per_b = pl.BlockSpec((1,) + state_shape, lambda i, c: (i, 0, 0, 0))
    ins = [q, f, v, g, lb, norm_w]
    in_specs = [seq, seq, seq, seq, _const_spec(lb.shape), _const_spec(norm_w.shape)]
    if has_init:
        ins.append(init)
        in_specs.append(per_b)
    return pl.pallas_call(
        functools.partial(_hgrn_kernel, qb=qb, rows=rows, has_init=has_init),
        grid=(b, l // qb),
        in_specs=in_specs,
        out_specs=[seq, per_b],
        out_shape=[jax.ShapeDtypeStruct((b, l, HG_D), BF16),
                   jax.ShapeDtypeStruct((b,) + state_shape, F32)],
        scratch_shapes=[pltpu.VMEM((HG_HEADS, HG_DV, HG_DK), F32),
                        pltpu.VMEM((HG_HEADS, rows, rows), F32),
                        pltpu.VMEM((rows, HG_D), F32),
                        pltpu.VMEM((rows, HG_D), F32)],
        compiler_params=_cparams("arbitrary", "arbitrary"),
    )(*ins)


def _hgrn_step_kernel(q_ref, f_ref, v_ref, g_ref, lb_ref, nw_ref, st_ref, o_ref, snew_ref, *, tb):
    dk, dv = HG_DK, HG_DV
    fr = f_ref[...]
    lb = lb_ref[...]
    sig = _sigmoid(fr)
    f_t = _tokens_to_lanes(lb + (1.0 - lb) * sig)
    k_t = _tokens_to_lanes((1.0 - lb) * (1.0 - sig))
    q_t = _tokens_to_lanes(q_ref[...])
    v = v_ref[...]
    gate = _silu(g_ref[...])
    sub = lax.broadcasted_iota(jnp.int32, (tb, dv), 0)
    for h in range(HG_HEADS):
        rs = slice(h * dk, (h + 1) * dk)
        vs = slice(h * dv, (h + 1) * dv)
        o_h = jnp.zeros((tb, dv), F32)
        for j in range(tb):
            fcol = jnp.broadcast_to(f_t[rs, j:j + 1], (dk, dv))
            kcol = jnp.broadcast_to(k_t[rs, j:j + 1], (dk, dv))
            qcol = jnp.broadcast_to(q_t[rs, j:j + 1], (dk, dv))
            new = st_ref[j, h] * fcol + kcol * v[j:j + 1, vs]
            snew_ref[j, h] = new
            o_h = jnp.where(sub == j, jnp.sum(new * qcol, axis=0, keepdims=True), o_h)
        o_ref[:, vs] = (_rms_unit(o_h) * nw_ref[:, vs] * gate[:, vs]).astype(o_ref.dtype)


def _hgrn_step(q, f, v, g, lb, norm_w, init):
    b = q.shape[0]
    tb = SUBLANES
    assert b % tb == 0
    row = pl.BlockSpec((tb, HG_D), lambda i: (i, 0))
    state_shape = (HG_HEADS, HG_DK, HG_DV)
    st_spec = pl.BlockSpec((tb,) + state_shape, lambda i: (i, 0, 0, 0))
    return pl.pallas_call(
        functools.partial(_hgrn_step_kernel, tb=tb),
        grid=(b // tb,),
        in_specs=[row, row, row, row, _const_spec(lb.shape), _const_spec(norm_w.shape), st_spec],
        out_specs=[row, st_spec],
        out_shape=[jax.ShapeDtypeStruct((b, HG_D), BF16),
                   jax.ShapeDtypeStruct((b,) + state_shape, F32)],
        compiler_params=_cparams("arbitrary"),
    )(q, f, v, g, lb, norm_w, init)


def _route_t(logits, bias, group=None):
    row = lax.broadcasted_iota(jnp.int32, logits.shape, 0)
    ninf = -jnp.inf
    big = jnp.int32(ROUTER_ROWS)
    is_g = (row >= N_EXPERTS) & (row < N_EXPERTS + N_EGROUPS)
    cmax = lambda t: jnp.max(t, axis=0, keepdims=True)
    csum = lambda t: jnp.sum(t, axis=0, keepdims=True)
    first = lambda m: jnp.min(jnp.where(m, row, big), axis=0, keepdims=True)

    gl = jnp.where(is_g, logits, ninf)
    gp = jnp.exp(gl - cmax(gl))
    gprob = gp / csum(gp)
    biased = logits + bias
    if group is None:
        gb = jnp.where(is_g, biased, ninf)
        gsel = first(gb == cmax(gb))
    else:
        gsel = group + N_EXPERTS
    gw = csum(jnp.where(row == gsel, gprob, 0.0))
    e0 = (gsel - N_EXPERTS) * EXPERTS_PER_GROUP
    in_grp = (row >= e0) & (row < e0 + EXPERTS_PER_GROUP)
    el = jnp.where(in_grp, logits, ninf)
    ep = jnp.exp(el - cmax(el))
    eprob = ep / csum(ep)
    eb = jnp.where(in_grp, biased, ninf)
    i1 = first(eb == cmax(eb))
    eb2 = jnp.where(row == i1, ninf, eb)
    i2 = first(eb2 == cmax(eb2))
    p1 = csum(jnp.where(row == i1, eprob, 0.0))
    p2 = csum(jnp.where(row == i2, eprob, 0.0))
    den = p1 + p2
    gid = jnp.broadcast_to(gsel - N_EXPERTS, (1, logits.shape[1])).astype(F32)
    return (jnp.where(row == i1, gw * (p1 / den), 0.0)
            + jnp.where(row == i2, gw * (p2 / den), 0.0)
            + jnp.where(row == GID_ROW, gid, 0.0))


def _router_logits_t(h, rhi_ref, rlo_ref):
    h_hi = h.astype(BF16)
    h_lo = (h - h_hi.astype(F32)).astype(BF16)
    logits = _dot(h_hi, rhi_ref[...]) + _dot(h_hi, rlo_ref[...]) + _dot(h_lo, rhi_ref[...])
    return logits.T[:ROUTER_ROWS]


def _rows_to_lanes(gates_t):
    r, tm = gates_t.shape
    return jnp.concatenate([gates_t, jnp.zeros((LANES - r, tm), F32)], axis=0).T


def _outproj_kernel(x_ref, y_ref, o_ref, w_ref, nw_ref, rhi_ref, rlo_ref, rb_ref,
                    x1_ref, h_ref, r_ref, *, tiled_rows):
    dy = y_ref.shape[-1]
    tm, d = x_ref.shape
    mix = _dot(y_ref[...], w_ref[:dy, :]) + _dot(o_ref[...], w_ref[dy:, :])
    x1 = x_ref[...] + mix
    x1_ref[...] = x1
    h = _rms_unit(x1) * nw_ref[...]
    gates_t = _route_t(_router_logits_t(h, rhi_ref, rlo_ref), rb_ref[...])
    if tiled_rows:
        h_ref[...] = h.reshape(tm, d // LANES, LANES)
        r_ref[...] = jnp.broadcast_to(gates_t[GID_ROW:GID_ROW + 1, :], (SUBLANES, tm))
    else:
        h_ref[...] = h
        r_ref[...] = _rows_to_lanes(gates_t)


def _out_proj(x, y, o, w, nw, r_hi, r_lo, r_bias, tm, tiled_rows):
    n, d = x.shape
    row = lambda width: pl.BlockSpec((tm, width), lambda i: (i, 0))
    consts = (w, nw, r_hi, r_lo, r_bias)
    if tiled_rows:
        h_spec = pl.BlockSpec((tm, d // LANES, LANES), lambda i: (i, 0, 0))
        h_shape = jax.ShapeDtypeStruct((n, d // LANES, LANES), F32)
        r_spec = pl.BlockSpec((SUBLANES, tm), lambda i: (i, 0))
        r_shape = jax.ShapeDtypeStruct((n // tm * SUBLANES, tm), F32)
    else:
        h_spec, h_shape = row(d), jax.ShapeDtypeStruct((n, d), F32)
        r_spec, r_shape = row(LANES), jax.ShapeDtypeStruct((n, LANES), F32)
    return pl.pallas_call(
        functools.partial(_outproj_kernel, tiled_rows=tiled_rows),
        grid=(n // tm,),
        in_specs=[row(d), row(y.shape[-1]), row(o.shape[-1])] + [_const_spec(c.shape) for c in consts],
        out_specs=[row(d), h_spec, r_spec],
        out_shape=[jax.ShapeDtypeStruct((n, d), F32), h_shape, r_shape],
        compiler_params=_cparams("arbitrary"),
    )(x, y, o, *consts)


def _pos_kernel(gid_ref, pos_ref, tg_ref, *, tile):
    rows, tm = gid_ref.shape
    gid = gid_ref[...]
    ri = lax.broadcasted_iota(jnp.int32, (tm, tm), 0)
    ci = lax.broadcasted_iota(jnp.int32, (tm, tm), 1)
    upper = (ri <= ci).astype(BF16)
    rr = lax.broadcasted_iota(jnp.int32, (rows, rows), 0)
    rc = lax.broadcasted_iota(jnp.int32, (rows, rows), 1)
    earlier = ((rc // SUBLANES < rr // SUBLANES) & (rc % SUBLANES == 0)).astype(BF16)
    tile_lane = lax.broadcasted_iota(jnp.int32, (1, LANES), 1).astype(F32)
    pos = jnp.zeros((rows, tm), F32)
    tiles_before = jnp.zeros((1, 1), F32)
    tile_group = jnp.zeros((1, LANES), F32)
    for g in range(N_EGROUPS):
        onehot = jnp.where(gid == float(g), 1.0, 0.0)
        inc = _dot(onehot.astype(BF16), upper)
        rowtot = jnp.broadcast_to(inc[:, tm - 1:tm], (rows, tm))
        rowpre = _cumsum_rows(rowtot, earlier)
        cnt = rowpre[rows - 1:rows, 0:1] + rowtot[rows - 1:rows, 0:1]
        pos = pos + onehot * (tiles_before * tile + rowpre + inc - 1.0)
        tiles_before = tiles_before + jnp.floor((cnt + (tile - 1.0)) * (1.0 / tile))
        tile_group = tile_group + jnp.where(tile_lane >= tiles_before, 1.0, 0.0)
    pos_ref[...] = pos.astype(jnp.int32)
    tg_ref[...] = tile_group.astype(jnp.int32)


def _positions(gid, n, tile):
    rows, tm = gid.shape
    assert n // tile + N_EGROUPS <= LANES
    pos, tg = pl.pallas_call(
        functools.partial(_pos_kernel, tile=tile),
        grid=(1,),
        in_specs=[_const_spec(gid.shape)],
        out_specs=[_const_spec(gid.shape), _const_spec((1, LANES))],
        out_shape=[jax.ShapeDtypeStruct(gid.shape, jnp.int32),
                   jax.ShapeDtypeStruct((1, LANES), jnp.int32)],
        compiler_params=_cparams("arbitrary"),
    )(gid)
    return pos.reshape(rows // SUBLANES, SUBLANES, tm)[:, 0, :].reshape(n), tg.reshape(LANES)


def _dispatch_kernel(pos_ref, h_ref, hs_in_ref, hs_ref, buf_ref, sem_ref, *, td):
    del hs_in_ref
    i = pl.program_id(0)
    nsteps = pl.num_programs(0)
    slot = i % 2

    def row_copy(step, s, r):
        return pltpu.make_async_copy(buf_ref.at[s, r], hs_ref.at[pos_ref[step * td + r]], sem_ref.at[s])

    def start_all(step, s):
        def body(r, carry):
            row_copy(step, s, r).start()
            return carry
        lax.fori_loop(0, td, body, 0, unroll=8)

    def wait_all(step, s):
        def body(r, carry):
            row_copy(step, s, r).wait()
            return carry
        lax.fori_loop(0, td, body, 0, unroll=8)

    @pl.when(i >= 2)
    def _():
        wait_all(i - 2, slot)

    buf_ref[slot] = h_ref[...]
    start_all(i, slot)

    @pl.when(i == nsteps - 1)
    def _():
        @pl.when(i >= 1)
        def _():
            wait_all(i - 1, 1 - slot)
        wait_all(i, slot)


def _dispatch(h3, pos, n_sorted, td):
    n = h3.shape[0]
    tok = h3.shape[1:]
    return pl.pallas_call(
        functools.partial(_dispatch_kernel, td=td),
        grid_spec=pltpu.PrefetchScalarGridSpec(
            num_scalar_prefetch=1,
            grid=(n // td,),
            in_specs=[pl.BlockSpec((td,) + tok, lambda i, pos: (i, 0, 0)),
                      pl.BlockSpec(memory_space=pl.ANY)],
            out_specs=pl.BlockSpec(memory_space=pl.ANY),
            scratch_shapes=[pltpu.VMEM((2, td) + tok, F32), pltpu.SemaphoreType.DMA((2,))]),
        out_shape=jax.ShapeDtypeStruct((n_sorted,) + tok, F32),
        input_output_aliases={2: 0},
        compiler_params=_cparams("arbitrary"),
    )(pos, h3, jnp.zeros((n_sorted,) + tok, F32))


def _expert(h, gates, e, wg, wu, wd):
    lane = lax.broadcasted_iota(jnp.int32, gates.shape, 1)
    gcol = jnp.sum(jnp.where(lane == e, gates, 0.0), axis=-1, keepdims=True)
    act = _silu(_dot(h, wg)) * _dot(h, wu) * gcol
    return _dot(act.astype(BF16), wd)


def _moe_dense_kernel(h_ref, gates_ref, wg_ref, wu_ref, wd_ref, out_ref):
    e = pl.program_id(1)
    part = _expert(h_ref[...].astype(BF16), gates_ref[...], e, wg_ref[0], wu_ref[0], wd_ref[0])

    @pl.when(e == 0)
    def _():
        out_ref[...] = part

    @pl.when(e > 0)
    def _():
        out_ref[...] += part


def _moe_dense(h, gates, wg, wu, wd, tm):
    n, d = h.shape
    ne, _, ff = wg.shape
    row = lambda width: pl.BlockSpec((tm, width), lambda i, e: (i, 0))
    return pl.pallas_call(
        _moe_dense_kernel,
        grid=(n // tm, ne),
        in_specs=[row(d), row(LANES),
                  pl.BlockSpec((1, d, ff), lambda i, e: (e, 0, 0)),
                  pl.BlockSpec((1, d, ff), lambda i, e: (e, 0, 0)),
                  pl.BlockSpec((1, ff, d), lambda i, e: (e, 0, 0))],
        out_specs=row(d),
        out_shape=jax.ShapeDtypeStruct((n, d), F32),
        compiler_params=_cparams("arbitrary", "arbitrary"),
    )(h, gates, wg, wu, wd)


def _moe_sorted_kernel(tg_ref, hs_ref, rhi_ref, rlo_ref, rb_ref, wg_ref, wu_ref, wd_ref, ys_ref):
    g = tg_ref[pl.program_id(0)]
    tm = hs_ref.shape[0]
    d = wg_ref.shape[1]

    @pl.when(g < N_EGROUPS)
    def _():
        h = hs_ref[...].reshape(tm, d)
        gates = _rows_to_lanes(_route_t(_router_logits_t(h, rhi_ref, rlo_ref), rb_ref[...], group=g))
        h_bf = h.astype(BF16)
        acc = None
        for j in range(EXPERTS_PER_GROUP):
            part = _expert(h_bf, gates, g * EXPERTS_PER_GROUP + j, wg_ref[j], wu_ref[j], wd_ref[j])
            acc = part if acc is None else acc + part
        ys_ref[...] = acc.reshape(ys_ref.shape)

    @pl.when(g >= N_EGROUPS)
    def _():
        ys_ref[...] = jnp.zeros(ys_ref.shape, F32)


def _moe_sorted(hs, tile_group, r_hi, r_lo, r_bias, wg, wu, wd, tm):
    n_sorted = hs.shape[0]
    _, d, ff = wg.shape
    epg = EXPERTS_PER_GROUP
    grp = lambda i, tg: (jnp.minimum(tg[i], N_EGROUPS - 1), 0, 0)
    tok = pl.BlockSpec((tm,) + hs.shape[1:], lambda i, tg: (i, 0, 0))
    const = lambda shape: pl.BlockSpec(shape, lambda i, tg: (0,) * len(shape))
    return pl.pallas_call(
        _moe_sorted_kernel,
        grid_spec=pltpu.PrefetchScalarGridSpec(
            num_scalar_prefetch=1,
            grid=(n_sorted // tm,),
            in_specs=[tok, const(r_hi.shape), const(r_lo.shape), const(r_bias.shape),
                      pl.BlockSpec((epg, d, ff), grp), pl.BlockSpec((epg, d, ff), grp),
                      pl.BlockSpec((epg, ff, d), grp)],
            out_specs=tok),
        out_shape=jax.ShapeDtypeStruct(hs.shape, F32),
        compiler_params=_cparams("arbitrary"),
    )(tile_group, hs, r_hi, r_lo, r_bias, wg, wu, wd)


def _ple_math(x, p, npw_ref, wg_ref, wp_ref, fw_ref):
    hn = (_rms_unit(x) * npw_ref[...]).astype(BF16)
    gate = _sigmoid(_dot(hn, wg_ref[...]))
    x = x + gate * _dot(p.astype(BF16), wp_ref[...])
    return _rms_unit(x) * fw_ref[...]


def _ple_kernel(x1_ref, moe_ref, p_ref, npw_ref, wg_ref, wp_ref, fw_ref, y_ref):
    y_ref[...] = _ple_math(x1_ref[...] + moe_ref[...], p_ref[...], npw_ref, wg_ref, wp_ref, fw_ref)


def _ple_gather_kernel(pos_ref, x1_ref, ys_ref, p_ref, npw_ref, wg_ref, wp_ref, fw_ref, y_ref,
                       buf_ref, sem_ref, *, tm):
    i = pl.program_id(0)
    nsteps = pl.num_programs(0)
    slot = i % 2

    def row_copy(step, s, r):
        return pltpu.make_async_copy(ys_ref.at[pos_ref[step * tm + r]], buf_ref.at[s, r], sem_ref.at[s])

    def start_all(step, s):
        def body(r, carry):
            row_copy(step, s, r).start()
            return carry
        lax.fori_loop(0, tm, body, 0, unroll=8)

    @pl.when(i == 0)
    def _():
        start_all(0, 0)

    @pl.when(i + 1 < nsteps)
    def _():
        start_all(i + 1, 1 - slot)

    def wait_row(r, carry):
        row_copy(i, slot, r).wait()
        return carry
    lax.fori_loop(0, tm, wait_row, 0, unroll=8)

    moe = buf_ref[slot].reshape(x1_ref.shape)
    y_ref[...] = _ple_math(x1_ref[...] + moe, p_ref[...], npw_ref, wg_ref, wp_ref, fw_ref)


def _ple(x1, moe, pos, p, npw, wg, wp, fw, tm):
    n, d = x1.shape
    consts = (npw, wg, wp, fw)
    out_shape = jax.ShapeDtypeStruct((n, d), F32)
    if pos is None:
        row = lambda width: pl.BlockSpec((tm, width), lambda i: (i, 0))
        return pl.pallas_call(
            _ple_kernel,
            grid=(n // tm,),
            in_specs=[row(d), row(d), row(p.shape[-1])] + [_const_spec(c.shape) for c in consts],
            out_specs=row(d),
            out_shape=out_shape,
            compiler_params=_cparams("arbitrary"),
        )(x1, moe, p, *consts)
    row = lambda width: pl.BlockSpec((tm, width), lambda i, pos: (i, 0))
    const = lambda shape: pl.BlockSpec(shape, lambda i, pos: (0,) * len(shape))
    return pl.pallas_call(
        functools.partial(_ple_gather_kernel, tm=tm),
        grid_spec=pltpu.PrefetchScalarGridSpec(
            num_scalar_prefetch=1,
            grid=(n // tm,),
            in_specs=[row(d), pl.BlockSpec(memory_space=pl.ANY), row(p.shape[-1])]
                     + [const(c.shape) for c in consts],
            out_specs=row(d),
            scratch_shapes=[pltpu.VMEM((2, tm) + moe.shape[1:], F32), pltpu.SemaphoreType.DMA((2,))]),
        out_shape=out_shape,
        compiler_params=_cparams("arbitrary"),
    )(pos, x1, moe, p, *consts)


def _prepare_weights(norm_mix_w, w_in, conv_w, conv_b, dt_bias, a_log, d_skip, ssd_norm_w,
                     lb_logits, hg_norm_w, w_out, norm_ffn_w, w_router_group, b_router_group,
                     w_router_expert, b_router_expert, w_exp_gate, w_exp_up, w_exp_down,
                     norm_ple_w, w_ple_gate, w_ple_proj, final_norm_w, layer):
    i = layer
    row = lambda t: t.reshape(1, -1).astype(F32)
    lane_pad = lambda t: jnp.pad(t, [(0, 0)] * (t.ndim - 1) + [(0, LANES - t.shape[-1])])
    splits = (SSD_D, CONV_DIM, SSD_HEADS, HG_D, HG_D, HG_D, HG_D)
    offs = [0]
    for s in splits:
        offs.append(offs[-1] + s)
    seg = lambda k: w_in[i][:, offs[k]:offs[k + 1]]
    w_in_r = jnp.concatenate([seg(0), seg(1), seg(3), seg(4), seg(5), seg(6), lane_pad(seg(2))],
                             axis=1).astype(BF16)
    w_router = lane_pad(jnp.concatenate([w_router_expert[i], w_router_group[i]], axis=1).astype(F32))
    r_hi = w_router.astype(BF16)
    r_lo = (w_router - r_hi.astype(F32)).astype(BF16)
    r_bias = jnp.pad(jnp.concatenate([b_router_expert[i].reshape(-1), b_router_group[i]]).astype(F32),
                     [(0, ROUTER_ROWS - N_EXPERTS - N_EGROUPS)]).reshape(ROUTER_ROWS, 1)
    lb = jnp.cumsum(jax.nn.softmax(lb_logits.astype(F32), axis=0), axis=0)[i]
    return dict(
        norm_mix_w=row(norm_mix_w[i]), w_in=w_in_r, conv_w=conv_w[i].astype(F32),
        conv_b=row(conv_b[i]), dt_bias=lane_pad(row(dt_bias[i])), a_log=lane_pad(row(a_log[i])),
        d_skip=row(jnp.repeat(d_skip[i], SSD_HEADDIM)), ssd_norm_w=row(ssd_norm_w[i]),
        lb=row(lb), hg_norm_w=row(hg_norm_w[i]), w_out=w_out[i].astype(BF16),
        norm_ffn_w=row(norm_ffn_w[i]), r_hi=r_hi, r_lo=r_lo, r_bias=r_bias,
        w_exp_gate=w_exp_gate[i].astype(BF16), w_exp_up=w_exp_up[i].astype(BF16),
        w_exp_down=w_exp_down[i].astype(BF16), norm_ple_w=row(norm_ple_w[i]),
        w_ple_gate=w_ple_gate[i].astype(BF16), w_ple_proj=w_ple_proj[i].astype(BF16),
        final_norm_w=row(final_norm_w))


def _token_tile(n, cap):
    tm = cap
    while tm >= SUBLANES:
        if n % tm == 0:
            return tm
        tm //= 2
    raise ValueError(f"token count {n} must be a multiple of {SUBLANES}")


def _trunk(x, p, ssm_in, conv_in, hg_in, w):
    b, l, d = x.shape
    n = b * l
    tm = _token_tile(n, ROW_TILE)
    x2d = x.reshape(n, d)
    z, xbc, q, f, v, g, dt = _in_proj(x2d, w["norm_mix_w"], w["w_in"], _token_tile(n, IN_PROJ_TILE))
    ssd_params = (w["conv_w"], w["conv_b"], w["dt_bias"], w["a_log"], w["d_skip"], w["ssd_norm_w"])
    if l == 1 and ssm_in is not None:
        y, conv_new, ssm_new = _ssd_step(z, xbc, dt, *ssd_params, conv_in, ssm_in)
        o, hg_new = _hgrn_step(q, f, v, g, w["lb"], w["hg_norm_w"], hg_in)
    else:
        seq = lambda t: t.reshape(b, l, t.shape[-1])
        y, conv_new, ssm_new = _ssd(seq(z), seq(xbc), seq(dt), *ssd_params, conv_in, ssm_in)
        o, hg_new = _hgrn(seq(q), seq(f), seq(v), seq(g), w["lb"], w["hg_norm_w"], hg_in)
    router = (w["r_hi"], w["r_lo"], w["r_bias"])
    experts = (w["w_exp_gate"], w["w_exp_up"], w["w_exp_down"])
    sorted_moe = n >= SORTED_MOE_MIN_TOKENS
    x1, h, routed = _out_proj(x2d, y.reshape(n, -1), o.reshape(n, -1), w["w_out"], w["norm_ffn_w"],
                              *router, tm, tiled_rows=sorted_moe)
    if sorted_moe:
        pos, tile_group = _positions(routed, n, MOE_TILE)
        hs = _dispatch(h, pos, n + N_EGROUPS * MOE_TILE, tm)
        moe = _moe_sorted(hs, tile_group, *router, *experts, MOE_TILE)
    else:
        pos = None
        moe = _moe_dense(h, routed, *experts, tm)
    y_out = _ple(x1, moe, pos, p.reshape(n, -1), w["norm_ple_w"], w["w_ple_gate"], w["w_ple_proj"],
                 w["final_norm_w"], tm)
    return y_out.reshape(b, l, d), ssm_new[None], conv_new[None], hg_new[None]


def kernel(x_prompt, x_sample, state_ssm, state_conv, state_hgrn, p_prompt, p_sample, norm_mix_w, w_in, conv_w, conv_b, dt_bias, a_log, d_skip, ssd_norm_w, lb_logits, hg_norm_w, w_out, norm_ffn_w, w_router_group, b_router_group, w_router_expert, b_router_expert, w_exp_gate, w_exp_up, w_exp_down, norm_ple_w, w_ple_gate, w_ple_proj, final_norm_w):
    assert p_prompt.shape[0] == 1, "the per-layer-embedding kernel also applies the final norm: depth 1 only"
    w = _prepare_weights(norm_mix_w, w_in, conv_w, conv_b, dt_bias, a_log, d_skip, ssd_norm_w,
                         lb_logits, hg_norm_w, w_out, norm_ffn_w, w_router_group, b_router_group,
                         w_router_expert, b_router_expert, w_exp_gate, w_exp_up, w_exp_down,
                         norm_ple_w, w_ple_gate, w_ple_proj, final_norm_w, layer=0)
    y_p, ssm_p, conv_p, hg_p = _trunk(x_prompt, p_prompt[0], None, None, None, w)
    y_s, ssm_s, conv_s, hg_s = _trunk(x_sample, p_sample[0], state_ssm[0], state_conv[0],
                                      state_hgrn[0], w)
    return (y_p, y_s, ssm_p, conv_p, hg_p, ssm_s, conv_s, hg_s)
```

```python
import functools

import jax
import jax.numpy as jnp
from jax import lax
from jax.experimental import pallas as pl
from jax.experimental.pallas import tpu as pltpu

F32 = jnp.float32
BF16 = jnp.bfloat16

EPS = 1e-6
SSD_HEADS = 16
SSD_HEADDIM = 64
SSD_D = SSD_HEADS * SSD_HEADDIM
SSD_GROUPS = 2
D_STATE = 128
CONV_W = 4
CONV_DIM = SSD_D + 2 * SSD_GROUPS * D_STATE
HG_HEADS = 8
HG_DK = 128
HG_DV = 128
HG_D = HG_HEADS * HG_DV
N_EGROUPS = 4
EXPERTS_PER_GROUP = 4
N_EXPERTS = N_EGROUPS * EXPERTS_PER_GROUP

LANES = 128
SUBLANES = 8
VMEM_LIMIT = 48 * 1024 * 1024
CHUNK = 128
HG_SAFE_LOG_DECAY = 60.0
ROUTER_ROWS = 32
CLASS_ROW = N_EXPERTS + N_EGROUPS
PAIRS_PER_GROUP = EXPERTS_PER_GROUP * (EXPERTS_PER_GROUP - 1) // 2
N_CLASSES = N_EGROUPS * PAIRS_PER_GROUP
MOE_TILE = 256
IN_PROJ_TILE = 256
ROW_TILE = 512
SORTED_MOE_MIN_TOKENS = 8 * SUBLANES * LANES


def _dot(a, b):
    return jnp.dot(a, b, preferred_element_type=F32)


def _dot_nt(a, b):
    return lax.dot_general(a, b, (((1,), (1,)), ((), ())), preferred_element_type=F32)


def _dot_tn(a, b):
    return lax.dot_general(a, b, (((0,), (0,)), ((), ())), preferred_element_type=F32)


def _rms_unit(x):
    return x * lax.rsqrt(jnp.mean(x * x, axis=-1, keepdims=True) + EPS)


def _sigmoid(x):
    return 1.0 / (1.0 + jnp.exp(-x))


def _silu(x):
    return x * _sigmoid(x)


def _split3(x):
    hi = x.astype(BF16)
    r = x - hi.astype(F32)
    mid = r.astype(BF16)
    lo = (r - mid.astype(F32)).astype(BF16)
    return hi, mid, lo


def _cumsum_rows(x, tri):
    hi, mid, lo = _split3(x)
    return _dot(tri, hi) + _dot(tri, mid) + _dot(tri, lo)


def _tri(n):
    r = lax.broadcasted_iota(jnp.int32, (n, n), 0)
    c = lax.broadcasted_iota(jnp.int32, (n, n), 1)
    return r >= c


def _load_rows(ref, qb, rows):
    if qb == rows:
        return ref[0]
    assert qb == 1
    x = jnp.broadcast_to(ref[0], (rows, ref.shape[-1]))
    r = lax.broadcasted_iota(jnp.int32, x.shape, 0)
    return jnp.where(r < qb, x, 0.0)


def _cparams(*sem):
    return pltpu.CompilerParams(dimension_semantics=sem, vmem_limit_bytes=VMEM_LIMIT)


def _const_spec(shape):
    nd = len(shape)
    return pl.BlockSpec(shape, lambda *_: (0,) * nd)


def _inproj_kernel(x_ref, nw_ref, *refs):
    hb = (_rms_unit(x_ref[...]) * nw_ref[...]).astype(BF16)
    nseg = len(refs) // 2
    for w_ref, out_ref in zip(refs[:nseg], refs[nseg:]):
        n = out_ref.shape[-1]
        for c0 in range(0, n, 512):
            cw = min(512, n - c0)
            out_ref[:, c0:c0 + cw] = _dot(hb, w_ref[:, c0:c0 + cw])


def _in_proj(x, nw, ws, tm):
    n, d = x.shape
    widths = [w.shape[-1] for w in ws]
    return pl.pallas_call(
        _inproj_kernel,
        grid=(n // tm,),
        in_specs=[pl.BlockSpec((tm, d), lambda i: (i, 0)), _const_spec(nw.shape)]
                 + [_const_spec(w.shape) for w in ws],
        out_specs=[pl.BlockSpec((tm, s), lambda i: (i, 0)) for s in widths],
        out_shape=[jax.ShapeDtypeStruct((n, s), F32) for s in widths],
        compiler_params=_cparams("arbitrary"),
    )(x, nw, *ws)


def _expand(x, e_ref):
    hi, mid, lo = _split3(x)
    e = e_ref[...]
    return _dot(hi, e) + _dot(mid, e) + _dot(lo, e)


def _ssd_kernel(*refs, qb, rows, has_init):
    if has_init:
        (z_ref, xbc_ref, dt_ref, cw_ref, cb_ref, dtb_ref, alog_ref, dsk_ref, nw_ref, e1_ref, e2_ref,
         cinit_ref, sinit_ref, y_ref, cout_ref, sout_ref, xc_ref, st_ref) = refs
    else:
        (z_ref, xbc_ref, dt_ref, cw_ref, cb_ref, dtb_ref, alog_ref, dsk_ref, nw_ref, e1_ref, e2_ref,
         y_ref, cout_ref, sout_ref, xc_ref, st_ref) = refs
    hist = SUBLANES
    c = pl.program_id(1)
    hpg = SSD_HEADS // SSD_GROUPS
    gw = hpg * SSD_HEADDIM
    gn = D_STATE

    @pl.when(c == 0)
    def _():
        xc_ref[0:hist, :] = jnp.zeros((hist, CONV_DIM), F32)
        if has_init:
            xc_ref[hist - (CONV_W - 1):hist, :] = cinit_ref[0]
            for g in range(SSD_GROUPS):
                st_ref[g] = sinit_ref[0, g * hpg:(g + 1) * hpg].reshape(gw, gn).T
        else:
            st_ref[...] = jnp.zeros(st_ref.shape, F32)

    xc_ref[hist:hist + rows, :] = _load_rows(xbc_ref, qb, rows)
    base = hist - (CONV_W - 1)
    acc = cb_ref[...] + cw_ref[0:1, :] * xc_ref[base:base + rows, :]
    for k in range(1, CONV_W):
        acc = acc + cw_ref[k:k + 1, :] * xc_ref[base + k:base + k + rows, :]
    cout_ref[0] = xc_ref[base + qb:base + qb + CONV_W - 1, :]
    xc_ref[0:hist, :] = xc_ref[rows:rows + hist, :]
    xbc = _silu(acc)
    xs = xbc[:, :SSD_D]
    b_bf = [xbc[:, SSD_D + g * gn:SSD_D + (g + 1) * gn].astype(BF16) for g in range(SSD_GROUPS)]
    c_bf = [xbc[:, SSD_D + (SSD_GROUPS + g) * gn:SSD_D + (SSD_GROUPS + g + 1) * gn].astype(BF16)
            for g in range(SSD_GROUPS)]

    dtr = _load_rows(dt_ref, qb, rows) + dtb_ref[...]
    dt = jnp.maximum(dtr, 0.0) + jnp.log(1.0 + jnp.exp(-jnp.abs(dtr)))
    if qb < rows:
        dt = jnp.where(lax.broadcasted_iota(jnp.int32, dt.shape, 0) < qb, dt, 0.0)
    a = dt * (-jnp.exp(alog_ref[...]))
    tri = _tri(rows)
    a_cs = _cumsum_rows(a, tri.astype(BF16))
    pad = max(rows, LANES)
    a_sq = a_cs if rows == pad else jnp.concatenate(
        [a_cs, jnp.zeros((pad - rows, LANES), F32)], axis=0)
    a_cs_t = a_sq.T
    dt_full = _expand(dt, e1_ref)
    a_full = _expand(a_cs, e1_ref)
    a_rep = _expand(a_cs, e2_ref)
    a_last = a_full[rows - 1:rows, :]

    xdt = xs * dt_full
    xdt_bf = xdt.astype(BF16)
    xw_bf = (xdt * jnp.exp(a_last - a_full)).astype(BF16)
    skip = dsk_ref[...] * xs
    lane = lax.broadcasted_iota(jnp.int32, (rows, LANES), 1)
    first_half = lane < SSD_HEADDIM
    zg = _silu(_load_rows(z_ref, qb, rows))

    for g in range(SSD_GROUPS):
        gs = slice(g * gw, (g + 1) * gw)
        scores = jnp.where(tri, _dot_nt(c_bf[g], b_bf[g]), 0.0)
        st_old = st_ref[g]
        y_off = _dot(c_bf[g], st_old.astype(BF16)) * jnp.exp(a_full[:, gs])
        st_ref[g] = st_old * jnp.exp(a_last[:, gs]) + _dot_tn(b_bf[g], xw_bf[:, gs])
        parts = []
        for pair in range(hpg // 2):
            h0 = g * hpg + 2 * pair
            cols = slice(h0 * SSD_HEADDIM, (h0 + 2) * SSD_HEADDIM)
            x_pair = xdt_bf[:, cols]
            acc = None
            for hh, keep in ((h0, first_half), (h0 + 1, jnp.logical_not(first_half))):
                diff = a_rep[:, hh * LANES:hh * LANES + rows] - a_cs_t[hh:hh + 1, :rows]
                m = (scores * jnp.exp(jnp.minimum(diff, 0.0))).astype(BF16)
                part = _dot(m, jnp.where(keep, x_pair, jnp.zeros_like(x_pair)))
                acc = part if acc is None else acc + part
            parts.append(acc)
        y = jnp.concatenate(parts, axis=1) + y_off + skip[:, gs]
        y = _rms_unit(y * zg[:, gs]) * nw_ref[:, gs]
        y_ref[0, :, gs] = y[:qb].astype(y_ref.dtype)

    @pl.when(c == pl.num_programs(1) - 1)
    def _():
        for g in range(SSD_GROUPS):
            sout_ref[0, g * hpg:(g + 1) * hpg] = st_ref[g].T.reshape(hpg, SSD_HEADDIM, gn)


def _ssd(z, xbc, dt, conv_w, conv_b, dt_bias, a_log, d_skip, norm_w, conv_init, ssm_init):
    b, l, _ = z.shape
    qb = min(CHUNK, l)
    assert l % qb == 0
    rows = max(qb, SUBLANES)
    has_init = ssm_init is not None
    seq = lambda w: pl.BlockSpec((1, qb, w), lambda i, c: (i, c, 0))
    per_b3 = lambda s: pl.BlockSpec((1,) + s, lambda i, c: (i, 0, 0))
    per_b4 = lambda s: pl.BlockSpec((1,) + s, lambda i, c: (i, 0, 0, 0))
    head = jnp.arange(LANES)[:, None]
    e1 = (head == jnp.arange(SSD_D)[None, :] // SSD_HEADDIM).astype(BF16)
    e2 = (head == jnp.arange(SSD_HEADS * LANES)[None, :] // LANES).astype(BF16)
    params = (conv_w, conv_b, dt_bias, a_log, d_skip, norm_w, e1, e2)
    ins = [z, xbc, dt, *params]
    in_specs = [seq(SSD_D), seq(CONV_DIM), seq(LANES)] + [_const_spec(p.shape) for p in params]
    state_shape = (SSD_HEADS, SSD_HEADDIM, D_STATE)
    if has_init:
        ins += [conv_init, ssm_init]
        in_specs += [per_b3((CONV_W - 1, CONV_DIM)), per_b4(state_shape)]
    return pl.pallas_call(
        functools.partial(_ssd_kernel, qb=qb, rows=rows, has_init=has_init),
        grid=(b, l // qb),
        in_specs=in_specs,
        out_specs=[seq(SSD_D), per_b3((CONV_W - 1, CONV_DIM)), per_b4(state_shape)],
        out_shape=[jax.ShapeDtypeStruct((b, l, SSD_D), BF16),
                   jax.ShapeDtypeStruct((b, CONV_W - 1, CONV_DIM), F32),
                   jax.ShapeDtypeStruct((b,) + state_shape, F32)],
        scratch_shapes=[pltpu.VMEM((rows + SUBLANES, CONV_DIM), F32),
                        pltpu.VMEM((SSD_GROUPS, D_STATE, SSD_D // SSD_GROUPS), F32)],
        compiler_params=_cparams("arbitrary", "arbitrary"),
    )(*ins)


def _tokens_to_lanes(x):
    tb, n = x.shape
    return jnp.concatenate([x, jnp.zeros((LANES - tb, n), F32)], axis=0).T


def _softplus(x):
    return jnp.maximum(x, 0.0) + jnp.log(1.0 + jnp.exp(-jnp.abs(x)))


def _step_decay_kernel(dt_ref, dtb_ref, alog_ref, da_ref):
    da_ref[...] = jnp.exp(_softplus(dt_ref[...] + dtb_ref[...]) * (-jnp.exp(alog_ref[...])))


def _ssd_step_kernel(da_ref, z_ref, xbc_ref, dt_ref, cw_ref, cb_ref, dtb_ref, dsk_ref, nw_ref, e1_ref,
                     cs_ref, st_ref, y_ref, cnew_ref, snew_ref, *, tb):
    cd = CONV_DIM
    gn = D_STATE
    hpg = SSD_HEADS // SSD_GROUPS
    gw = hpg * SSD_HEADDIM
    first = pl.program_id(0) * tb
    x_in = xbc_ref[...]
    acc = cb_ref[...] + cw_ref[CONV_W - 1:CONV_W, :] * x_in
    for k in range(CONV_W - 1):
        acc = acc + cw_ref[k:k + 1, :] * cs_ref[:, k * cd:(k + 1) * cd]
    cnew_ref[:, :(CONV_W - 2) * cd] = cs_ref[:, cd:]
    cnew_ref[:, (CONV_W - 2) * cd:] = x_in
    xbc = _silu(acc)
    xs = xbc[:, :SSD_D]
    dt = _softplus(dt_ref[...] + dtb_ref[...])
    xdt_t = _tokens_to_lanes(xs * _expand(dt, e1_ref)).astype(BF16)
    row_tok = lax.broadcasted_iota(jnp.int32, (LANES, tb * gn), 0)
    col_tok = lax.broadcasted_iota(jnp.int32, (LANES, tb * gn), 1) // gn
    lane = lax.broadcasted_iota(jnp.int32, (SSD_HEADDIM, LANES), 1)
    y_groups = []
    for g in range(SSD_GROUPS):
        b_g = xbc[:, SSD_D + g * gn:SSD_D + (g + 1) * gn]
        c_g = xbc[:, SSD_D + (SSD_GROUPS + g) * gn:SSD_D + (SSD_GROUPS + g + 1) * gn]
        b_wide = jnp.concatenate([jnp.tile(b_g, (1, tb)), jnp.zeros((LANES - tb, tb * gn), F32)], axis=0)
        b_diag = jnp.where(row_tok == col_tok, b_wide, 0.0).astype(BF16)
        upd = _dot(xdt_t[g * gw:(g + 1) * gw, :], b_diag)
        y_heads = []
        for hl in range(hpg):
            h = g * hpg + hl
            y_h = jnp.zeros((SSD_HEADDIM, LANES), F32)
            for j in range(tb):
                new = (st_ref[j, h] * da_ref[first + j, h]
                       + upd[hl * SSD_HEADDIM:(hl + 1) * SSD_HEADDIM, j * gn:(j + 1) * gn])
                snew_ref[j, h] = new
                y_h = jnp.where(lane == j, jnp.sum(new * c_g[j:j + 1, :], axis=-1, keepdims=True), y_h)
            y_heads.append(y_h)
        y_groups.append(jnp.concatenate(y_heads, axis=0))
    y = jnp.concatenate(y_groups, axis=0).T[:tb]
    y = (y + dsk_ref[...] * xs) * _silu(z_ref[...])
    for g in range(SSD_GROUPS):
        gs = slice(g * gw, (g + 1) * gw)
        y_ref[:, gs] = (_rms_unit(y[:, gs]) * nw_ref[:, gs]).astype(y_ref.dtype)


def _ssd_step(z, xbc, dt, conv_w, conv_b, dt_bias, a_log, d_skip, norm_w, conv_init, ssm_init):
    b = z.shape[0]
    tb = SUBLANES
    assert b % tb == 0
    da = pl.pallas_call(
        _step_decay_kernel,
        grid=(1,),
        in_specs=[_const_spec(dt.shape), _const_spec(dt_bias.shape), _const_spec(a_log.shape)],
        out_specs=_const_spec(dt.shape),
        out_shape=jax.ShapeDtypeStruct(dt.shape, F32),
    )(dt, dt_bias, a_log)
    head = jnp.arange(LANES)[:, None]
    e1 = (head == jnp.arange(SSD_D)[None, :] // SSD_HEADDIM).astype(BF16)
    params = (conv_w, conv_b, dt_bias, d_skip, norm_w, e1)
    hist = (CONV_W - 1) * CONV_DIM
    row = lambda w: pl.BlockSpec((tb, w), lambda i, da: (i, 0))
    const = lambda shape: pl.BlockSpec(shape, lambda i, da: (0,) * len(shape))
    state_shape = (SSD_HEADS, SSD_HEADDIM, D_STATE)
    st_spec = pl.BlockSpec((tb,) + state_shape, lambda i, da: (i, 0, 0, 0))
    y, conv_new, ssm_new = pl.pallas_call(
        functools.partial(_ssd_step_kernel, tb=tb),
        grid_spec=pltpu.PrefetchScalarGridSpec(
            num_scalar_prefetch=1,
            grid=(b // tb,),
            in_specs=[row(SSD_D), row(CONV_DIM), row(LANES)] + [const(p.shape) for p in params]
                     + [row(hist), st_spec],
            out_specs=[row(SSD_D), row(hist), st_spec]),
        out_shape=[jax.ShapeDtypeStruct((b, SSD_D), BF16), jax.ShapeDtypeStruct((b, hist), F32),
                   jax.ShapeDtypeStruct((b,) + state_shape, F32)],
        compiler_params=_cparams("arbitrary"),
    )(da, z, xbc, dt, *params, conv_init.reshape(b, hist), ssm_init)
    return y, conv_new.reshape(b, CONV_W - 1, CONV_DIM), ssm_new


def _hgrn_kernel(*refs, qb, rows, has_init):
    if has_init:
        (q_ref, f_ref, v_ref, g_ref, lb_ref, nw_ref, sinit_ref,
         o_ref, sout_ref, st_ref, a_ref, kk_ref, gcs_ref) = refs
    else:
        (q_ref, f_ref, v_ref, g_ref, lb_ref, nw_ref,
         o_ref, sout_ref, st_ref, a_ref, kk_ref, gcs_ref) = refs
    c = pl.program_id(1)
    dk = HG_DK

    @pl.when(c == 0)
    def _():
        for h in range(HG_HEADS):
            if has_init:
                st_ref[h] = sinit_ref[0, h].T
            else:
                st_ref[h] = jnp.zeros((HG_DV, HG_DK), F32)

    fr = _load_rows(f_ref, qb, rows)
    lb = lb_ref[...]
    sig = _sigmoid(fr)
    logf = jnp.log(lb + (1.0 - lb) * sig)
    kk = (1.0 - lb) * (1.0 - sig)
    if qb < rows:
        valid = lax.broadcasted_iota(jnp.int32, fr.shape, 0) < qb
        logf = jnp.where(valid, logf, 0.0)
        kk = jnp.where(valid, kk, 0.0)
    tri = _tri(rows)
    gcs = _cumsum_rows(logf, tri.astype(BF16))
    glast = gcs[rows - 1:rows, :]
    q = _load_rows(q_ref, qb, rows)
    qt = q * jnp.exp(gcs)
    kk_ref[...] = kk
    gcs_ref[...] = gcs

    safe = jnp.min(glast) >= -HG_SAFE_LOG_DECAY

    @pl.when(safe)
    def _():
        kt = kk * jnp.exp(-gcs)
        for h in range(HG_HEADS):
            sl = slice(h * dk, (h + 1) * dk)
            a_ref[h] = _dot_nt(qt[:, sl].astype(BF16), kt[:, sl].astype(BF16))

    @pl.when(jnp.logical_not(safe))
    def _():
        ri = lax.broadcasted_iota(jnp.int32, (rows, dk), 0)
        ci = lax.broadcasted_iota(jnp.int32, (rows, rows), 1)
        for h in range(HG_HEADS):
            sl = slice(h * dk, (h + 1) * dk)
            q_h = q[:, sl]
            g_h = gcs[:, sl]

            def cols(j8, a_h):
                base = pl.multiple_of(j8 * SUBLANES, SUBLANES)
                k_blk = kk_ref[pl.ds(base, SUBLANES), sl]
                g_blk = gcs_ref[pl.ds(base, SUBLANES), sl]
                for r in range(SUBLANES):
                    j = base + r
                    t = q_h * k_blk[r:r + 1] * jnp.exp(
                        jnp.where(ri >= j, g_h - g_blk[r:r + 1], -jnp.inf))
                    a_h = jnp.where(ci == j, jnp.sum(t, axis=-1, keepdims=True), a_h)
                return a_h

            a_ref[h] = lax.fori_loop(0, rows // SUBLANES, cols, jnp.zeros((rows, rows), F32))

    khat = kk * jnp.exp(glast - gcs)
    v = _load_rows(v_ref, qb, rows)
    gate = _silu(_load_rows(g_ref, qb, rows))
    for h in range(HG_HEADS):
        sl = slice(h * dk, (h + 1) * dk)
        st = st_ref[h]
        v_bf = v[:, sl].astype(BF16)
        a_h = jnp.where(tri, a_ref[h], 0.0).astype(BF16)
        o = _dot(a_h, v_bf) + _dot_nt(qt[:, sl].astype(BF16), st.astype(BF16))
        st_ref[h] = st * jnp.exp(glast[:, sl]) + _dot_tn(v_bf, khat[:, sl].astype(BF16))
        o = _rms_unit(o) * nw_ref[:, sl] * gate[:, sl]
        o_ref[0, :, sl] = o[:qb].astype(o_ref.dtype)

    @pl.when(c == pl.num_programs(1) - 1)
    def _():
        for h in range(HG_HEADS):
            sout_ref[0, h] = st_ref[h].T


def _hgrn(q, f, v, g, lb, norm_w, init):
    b, l, _ = q.shape
    qb = min(CHUNK, l)
    assert l % qb == 0
    rows = max(qb, SUBLANES)
    has_init = init is not None
    seq = pl.BlockSpec((1, qb, HG_D), lambda i, c: (i, c, 0))
    state_shape = (HG_HEADS, HG_DK, HG_DV)
    per_b = pl.BlockSpec((1,) + state_shape, lambda i, c: (i, 0, 0, 0))
    ins = [q, f, v, g, lb, norm_w]
    in_specs = [seq, seq, seq, seq, _const_spec(lb.shape), _const_spec(norm_w.shape)]
    if has_init:
        ins.append(init)
        in_specs.append(per_b)
    return pl.pallas_call(
        functools.partial(_hgrn_kernel, qb=qb, rows=rows, has_init=has_init),
        grid=(b, l // qb),
        in_specs=in_specs,
        out_specs=[seq, per_b],
        out_shape=[jax.ShapeDtypeStruct((b, l, HG_D), BF16),
                   jax.ShapeDtypeStruct((b,) + state_shape, F32)],
        scratch_shapes=[pltpu.VMEM((HG_HEADS, HG_DV, HG_DK), F32),
                        pltpu.VMEM((HG_HEADS, rows, rows), F32),
                        pltpu.VMEM((rows, HG_D), F32),
                        pltpu.VMEM((rows, HG_D), F32)],
        compiler_params=_cparams("arbitrary", "arbitrary"),
    )(*ins)


def _hgrn_step_kernel(q_ref, f_ref, v_ref, g_ref, lb_ref, nw_ref, st_ref, o_ref, snew_ref, *, tb):
    dk, dv = HG_DK, HG_DV
    fr = f_ref[...]
    lb = lb_ref[...]
    sig = _sigmoid(fr)
    f_t = _tokens_to_lanes(lb + (1.0 - lb) * sig)
    k_t = _tokens_to_lanes((1.0 - lb) * (1.0 - sig))
    q_t = _tokens_to_lanes(q_ref[...])
    v = v_ref[...]
    gate = _silu(g_ref[...])
    sub = lax.broadcasted_iota(jnp.int32, (tb, dv), 0)
    for h in range(HG_HEADS):
        rs = slice(h * dk, (h + 1) * dk)
        vs = slice(h * dv, (h + 1) * dv)
        o_h = jnp.zeros((tb, dv), F32)
        for j in range(tb):
            fcol = jnp.broadcast_to(f_t[rs, j:j + 1], (dk, dv))
            kcol = jnp.broadcast_to(k_t[rs, j:j + 1], (dk, dv))
            qcol = jnp.broadcast_to(q_t[rs, j:j + 1], (dk, dv))
            new = st_ref[j, h] * fcol + kcol * v[j:j + 1, vs]
            snew_ref[j, h] = new
            o_h = jnp.where(sub == j, jnp.sum(new * qcol, axis=0, keepdims=True), o_h)
        o_ref[:, vs] = (_rms_unit(o_h) * nw_ref[:, vs] * gate[:, vs]).astype(o_ref.dtype)


def _hgrn_step(q, f, v, g, lb, norm_w, init):
    b = q.shape[0]
    tb = SUBLANES
    assert b % tb == 0
    row = pl.BlockSpec((tb, HG_D), lambda i: (i, 0))
    state_shape = (HG_HEADS, HG_DK, HG_DV)
    st_spec = pl.BlockSpec((tb,) + state_shape, lambda i: (i, 0, 0, 0))
    return pl.pallas_call(
        functools.partial(_hgrn_step_kernel, tb=tb),
        grid=(b // tb,),
        in_specs=[row, row, row, row, _const_spec(lb.shape), _const_spec(norm_w.shape), st_spec],
        out_specs=[row, st_spec],
        out_shape=[jax.ShapeDtypeStruct((b, HG_D), BF16),
                   jax.ShapeDtypeStruct((b,) + state_shape, F32)],
        compiler_params=_cparams("arbitrary"),
    )(q, f, v, g, lb, norm_w, init)


def _route_t(logits, bias, group=None):
    row = lax.broadcasted_iota(jnp.int32, logits.shape, 0)
    ninf = -jnp.inf
    big = jnp.int32(ROUTER_ROWS)
    is_g = (row >= N_EXPERTS) & (row < N_EXPERTS + N_EGROUPS)
    cmax = lambda t: jnp.max(t, axis=0, keepdims=True)
    csum = lambda t: jnp.sum(t, axis=0, keepdims=True)
    first = lambda m: jnp.min(jnp.where(m, row, big), axis=0, keepdims=True)

    gl = jnp.where(is_g, logits, ninf)
    gp = jnp.exp(gl - cmax(gl))
    gprob = gp / csum(gp)
    biased = logits + bias
    if group is None:
        gb = jnp.where(is_g, biased, ninf)
        gsel = first(gb == cmax(gb))
    else:
        gsel = group + N_EXPERTS
    gw = csum(jnp.where(row == gsel, gprob, 0.0))
    e0 = (gsel - N_EXPERTS) * EXPERTS_PER_GROUP
    in_grp = (row >= e0) & (row < e0 + EXPERTS_PER_GROUP)
    el = jnp.where(in_grp, logits, ninf)
    ep = jnp.exp(el - cmax(el))
    eprob = ep / csum(ep)
    eb = jnp.where(in_grp, biased, ninf)
    i1 = first(eb == cmax(eb))
    eb2 = jnp.where(row == i1, ninf, eb)
    i2 = first(eb2 == cmax(eb2))
    p1 = csum(jnp.where(row == i1, eprob, 0.0))
    p2 = csum(jnp.where(row == i2, eprob, 0.0))
    den = p1 + p2
    a = (jnp.minimum(i1, i2) - e0).astype(F32)
    b = (jnp.maximum(i1, i2) - e0).astype(F32)
    group = jnp.broadcast_to(gsel - N_EXPERTS, (1, logits.shape[1])).astype(F32)
    cls = group * PAIRS_PER_GROUP + a * (7.0 - a) * 0.5 + (b - a - 1.0)
    return (jnp.where(row == i1, gw * (p1 / den), 0.0)
            + jnp.where(row == i2, gw * (p2 / den), 0.0)
            + jnp.where(row == CLASS_ROW, cls, 0.0))


def _router_logits_t(h, rhi_ref, rlo_ref):
    h_hi = h.astype(BF16)
    h_lo = (h - h_hi.astype(F32)).astype(BF16)
    logits = _dot(h_hi, rhi_ref[...]) + _dot(h_hi, rlo_ref[...]) + _dot(h_lo, rhi_ref[...])
    return logits.T[:ROUTER_ROWS]


def _rows_to_lanes(gates_t):
    r, tm = gates_t.shape
    return jnp.concatenate([gates_t, jnp.zeros((LANES - r, tm), F32)], axis=0).T


def _outproj_kernel(x_ref, y_ref, o_ref, w_ref, nw_ref, rhi_ref, rlo_ref, rb_ref,
                    x1_ref, h_ref, r_ref, *, tiled_rows):
    dy = y_ref.shape[-1]
    tm, d = x_ref.shape
    mix = _dot(y_ref[...], w_ref[:dy, :]) + _dot(o_ref[...], w_ref[dy:, :])
    x1 = x_ref[...] + mix
    x1_ref[...] = x1
    h = _rms_unit(x1) * nw_ref[...]
    gates_t = _route_t(_router_logits_t(h, rhi_ref, rlo_ref), rb_ref[...])
    if tiled_rows:
        h_ref[...] = h.reshape(tm, d // LANES, LANES)
        r_ref[...] = jnp.broadcast_to(gates_t[CLASS_ROW:CLASS_ROW + 1, :], (SUBLANES, tm))
    else:
        h_ref[...] = h
        r_ref[...] = _rows_to_lanes(gates_t)


def _out_proj(x, y, o, w, nw, r_hi, r_lo, r_bias, tm, tiled_rows):
    n, d = x.shape
    row = lambda width: pl.BlockSpec((tm, width), lambda i: (i, 0))
    consts = (w, nw, r_hi, r_lo, r_bias)
    if tiled_rows:
        h_spec = pl.BlockSpec((tm, d // LANES, LANES), lambda i: (i, 0, 0))
        h_shape = jax.ShapeDtypeStruct((n, d // LANES, LANES), F32)
        r_spec = pl.BlockSpec((SUBLANES, tm), lambda i: (i, 0))
        r_shape = jax.ShapeDtypeStruct((n // tm * SUBLANES, tm), F32)
    else:
        h_spec, h_shape = row(d), jax.ShapeDtypeStruct((n, d), F32)
        r_spec, r_shape = row(LANES), jax.ShapeDtypeStruct((n, LANES), F32)
    return pl.pallas_call(
        functools.partial(_outproj_kernel, tiled_rows=tiled_rows),
        grid=(n // tm,),
        in_specs=[row(d), row(y.shape[-1]), row(o.shape[-1])] + [_const_spec(c.shape) for c in consts],
        out_specs=[row(d), h_spec, r_spec],
        out_shape=[jax.ShapeDtypeStruct((n, d), F32), h_shape, r_shape],
        compiler_params=_cparams("arbitrary"),
    )(x, y, o, *consts)


def _pos_kernel(cls_ref, pos_ref, tab_ref, *, tile):
    rows, tm = cls_ref.shape
    cls = cls_ref[...]
    ri = lax.broadcasted_iota(jnp.int32, (tm, tm), 0)
    ci = lax.broadcasted_iota(jnp.int32, (tm, tm), 1)
    upper = (ri <= ci).astype(BF16)
    rr = lax.broadcasted_iota(jnp.int32, (rows, rows), 0)
    rc = lax.broadcasted_iota(jnp.int32, (rows, rows), 1)
    earlier = ((rc // SUBLANES < rr // SUBLANES) & (rc % SUBLANES == 0)).astype(BF16)
    tile_lane = lax.broadcasted_iota(jnp.int32, (1, LANES), 1).astype(F32)
    pos = jnp.zeros((rows, tm), F32)
    tiles_before = jnp.zeros((1, 1), F32)
    tile_class = jnp.zeros((1, LANES), F32)
    for c in range(N_CLASSES):
        onehot = jnp.where(cls == float(c), 1.0, 0.0)
        inc = _dot(onehot.astype(BF16), upper)
        rowtot = jnp.broadcast_to(inc[:, tm - 1:tm], (rows, tm))
        rowpre = _cumsum_rows(rowtot, earlier)
        cnt = rowpre[rows - 1:rows, 0:1] + rowtot[rows - 1:rows, 0:1]
        pos = pos + onehot * (tiles_before * tile + rowpre + inc - 1.0)
        tiles_before = tiles_before + jnp.floor((cnt + (tile - 1.0)) * (1.0 / tile))
        tile_class = tile_class + jnp.where(tile_lane >= tiles_before, 1.0, 0.0)
    pos_ref[...] = pos.astype(jnp.int32)
    used = tile_class < N_CLASSES
    group = jnp.floor(tile_class * (1.0 / PAIRS_PER_GROUP))
    pair = tile_class - group * PAIRS_PER_GROUP
    ge3 = jnp.where(pair >= 3.0, 1.0, 0.0)
    ge5 = jnp.where(pair >= 5.0, 1.0, 0.0)
    a = ge3 + ge5
    b = pair + 1.0 - 2.0 * ge3 - ge5
    e1 = jnp.where(used, group * EXPERTS_PER_GROUP + a, float(N_EXPERTS))
    e2 = jnp.where(used, group * EXPERTS_PER_GROUP + b, float(N_EXPERTS))
    sub = lax.broadcasted_iota(jnp.int32, (SUBLANES, LANES), 0)
    tab_ref[...] = jnp.where(sub == 0, e1, jnp.where(sub == 1, e2, 0.0)).astype(jnp.int32)


def _positions(cls, n, tile):
    rows, tm = cls.shape
    assert n // tile + N_CLASSES <= LANES
    pos, tab = pl.pallas_call(
        functools.partial(_pos_kernel, tile=tile),
        grid=(1,),
        in_specs=[_const_spec(cls.shape)],
        out_specs=[_const_spec(cls.shape), _const_spec((SUBLANES, LANES))],
        out_shape=[jax.ShapeDtypeStruct(cls.shape, jnp.int32),
                   jax.ShapeDtypeStruct((SUBLANES, LANES), jnp.int32)],
        compiler_params=_cparams("arbitrary"),
    )(cls)
    return pos.reshape(rows // SUBLANES, SUBLANES, tm)[:, 0, :].reshape(n), tab[0], tab[1]


def _dispatch_kernel(pos_ref, h_ref, hs_in_ref, hs_ref, buf_ref, sem_ref, *, td):
    del hs_in_ref
    i = pl.program_id(0)
    nsteps = pl.num_programs(0)
    slot = i % 2

    def row_copy(step, s, r):
        return pltpu.make_async_copy(buf_ref.at[s, r], hs_ref.at[pos_ref[step * td + r]], sem_ref.at[s])

    def start_all(step, s):
        def body(r, carry):
            row_copy(step, s, r).start()
            return carry
        lax.fori_loop(0, td, body, 0, unroll=8)

    def wait_all(step, s):
        def body(r, carry):
            row_copy(step, s, r).wait()
            return carry
        lax.fori_loop(0, td, body, 0, unroll=8)

    @pl.when(i >= 2)
    def _():
        wait_all(i - 2, slot)

    buf_ref[slot] = h_ref[...]
    start_all(i, slot)

    @pl.when(i == nsteps - 1)
    def _():
        @pl.when(i >= 1)
        def _():
            wait_all(i - 1, 1 - slot)
        wait_all(i, slot)


def _dispatch(h3, pos, n_sorted, td):
    n = h3.shape[0]
    tok = h3.shape[1:]
    return pl.pallas_call(
        functools.partial(_dispatch_kernel, td=td),
        grid_spec=pltpu.PrefetchScalarGridSpec(
            num_scalar_prefetch=1,
            grid=(n // td,),
            in_specs=[pl.BlockSpec((td,) + tok, lambda i, pos: (i, 0, 0)),
                      pl.BlockSpec(memory_space=pl.ANY)],
            out_specs=pl.BlockSpec(memory_space=pl.ANY),
            scratch_shapes=[pltpu.VMEM((2, td) + tok, F32), pltpu.SemaphoreType.DMA((2,))]),
        out_shape=jax.ShapeDtypeStruct((n_sorted,) + tok, F32),
        input_output_aliases={2: 0},
        compiler_params=_cparams("arbitrary"),
    )(pos, h3, jnp.zeros((n_sorted,) + tok, F32))


def _expert(h, gates, e, wg, wu, wd):
    lane = lax.broadcasted_iota(jnp.int32, gates.shape, 1)
    gcol = jnp.sum(jnp.where(lane == e, gates, 0.0), axis=-1, keepdims=True)
    act = _silu(_dot(h, wg)) * _dot(h, wu) * gcol
    return _dot(act.astype(BF16), wd)


def _moe_dense_kernel(h_ref, gates_ref, wg_ref, wu_ref, wd_ref, out_ref):
    e = pl.program_id(1)
    part = _expert(h_ref[...].astype(BF16), gates_ref[...], e, wg_ref[0], wu_ref[0], wd_ref[0])

    @pl.when(e == 0)
    def _():
        out_ref[...] = part

    @pl.when(e > 0)
    def _():
        out_ref[...] += part


def _moe_dense(h, gates, wg, wu, wd, tm):
    n, d = h.shape
    ne, _, ff = wg.shape
    row = lambda width: pl.BlockSpec((tm, width), lambda i, e: (i, 0))
    return pl.pallas_call(
        _moe_dense_kernel,
        grid=(n // tm, ne),
        in_specs=[row(d), row(LANES),
                  pl.BlockSpec((1, d, ff), lambda i, e: (e, 0, 0)),
                  pl.BlockSpec((1, d, ff), lambda i, e: (e, 0, 0)),
                  pl.BlockSpec((1, ff, d), lambda i, e: (e, 0, 0))],
        out_specs=row(d),
        out_shape=jax.ShapeDtypeStruct((n, d), F32),
        compiler_params=_cparams("arbitrary", "arbitrary"),
    )(h, gates, wg, wu, wd)


def _moe_sorted_kernel(e1_ref, e2_ref, hs_ref, rhi_ref, rlo_ref, rb_ref,
                       wg1_ref, wu1_ref, wd1_ref, wg2_ref, wu2_ref, wd2_ref, ys_ref):
    e1 = e1_ref[pl.program_id(0)]
    e2 = e2_ref[pl.program_id(0)]
    tm = hs_ref.shape[0]
    d = wg1_ref.shape[1]

    @pl.when(e1 < N_EXPERTS)
    def _():
        h = hs_ref[...].reshape(tm, d)
        logits_t = _router_logits_t(h, rhi_ref, rlo_ref)
        group = lax.shift_right_logical(e1, jnp.int32(EXPERTS_PER_GROUP.bit_length() - 1))
        gates = _rows_to_lanes(_route_t(logits_t, rb_ref[...], group=group))
        h_bf = h.astype(BF16)
        acc = (_expert(h_bf, gates, e1, wg1_ref[0], wu1_ref[0], wd1_ref[0])
               + _expert(h_bf, gates, e2, wg2_ref[0], wu2_ref[0], wd2_ref[0]))
        ys_ref[...] = acc.reshape(ys_ref.shape)

    @pl.when(e1 >= N_EXPERTS)
    def _():
        ys_ref[...] = jnp.zeros(ys_ref.shape, F32)


def _moe_sorted(hs, e1_tab, e2_tab, r_hi, r_lo, r_bias, wg, wu, wd, tm):
    n_sorted = hs.shape[0]
    ne, d, ff = wg.shape
    first = lambda i, e1, e2: (jnp.minimum(e1[i], ne - 1), 0, 0)
    second = lambda i, e1, e2: (jnp.minimum(e2[i], ne - 1), 0, 0)
    tok = pl.BlockSpec((tm,) + hs.shape[1:], lambda i, e1, e2: (i, 0, 0))
    const = lambda shape: pl.BlockSpec(shape, lambda i, e1, e2: (0,) * len(shape))
    up = lambda sel: pl.BlockSpec((1, d, ff), sel)
    down = lambda sel: pl.BlockSpec((1, ff, d), sel)
    return pl.pallas_call(
        _moe_sorted_kernel,
        grid_spec=pltpu.PrefetchScalarGridSpec(
            num_scalar_prefetch=2,
            grid=(n_sorted // tm,),
            in_specs=[tok, const(r_hi.shape), const(r_lo.shape), const(r_bias.shape),
                      up(first), up(first), down(first), up(second), up(second), down(second)],
            out_specs=tok),
        out_shape=jax.ShapeDtypeStruct(hs.shape, F32),
        compiler_params=_cparams("arbitrary"),
    )(e1_tab, e2_tab, hs, r_hi, r_lo, r_bias, wg, wu, wd, wg, wu, wd)


def _ple_math(x, p, npw_ref, wg_ref, wp_ref, fw_ref):
    hn = (_rms_unit(x) * npw_ref[...]).astype(BF16)
    gate = _sigmoid(_dot(hn, wg_ref[...]))
    x = x + gate * _dot(p.astype(BF16), wp_ref[...])
    return _rms_unit(x) * fw_ref[...]


def _ple_kernel(x1_ref, moe_ref, p_ref, npw_ref, wg_ref, wp_ref, fw_ref, y_ref):
    y_ref[...] = _ple_math(x1_ref[...] + moe_ref[...], p_ref[...], npw_ref, wg_ref, wp_ref, fw_ref)


def _ple_gather_kernel(pos_ref, x1_ref, ys_ref, p_ref, npw_ref, wg_ref, wp_ref, fw_ref, y_ref,
                       buf_ref, sem_ref, *, tm):
    i = pl.program_id(0)
    nsteps = pl.num_programs(0)
    slot = i % 2

    def row_copy(step, s, r):
        return pltpu.make_async_copy(ys_ref.at[pos_ref[step * tm + r]], buf_ref.at[s, r], sem_ref.at[s])

    def start_all(step, s):
        def body(r, carry):
            row_copy(step, s, r).start()
            return carry
        lax.fori_loop(0, tm, body, 0, unroll=8)

    @pl.when(i == 0)
    def _():
        start_all(0, 0)

    @pl.when(i + 1 < nsteps)
    def _():
        start_all(i + 1, 1 - slot)

    def wait_row(r, carry):
        row_copy(i, slot, r).wait()
        return carry
    lax.fori_loop(0, tm, wait_row, 0, unroll=8)

    moe = buf_ref[slot].reshape(x1_ref.shape)
    y_ref[...] = _ple_math(x1_ref[...] + moe, p_ref[...], npw_ref, wg_ref, wp_ref, fw_ref)


def _ple(x1, moe, pos, p, npw, wg, wp, fw, tm):
    n, d = x1.shape
    consts = (npw, wg, wp, fw)
    out_shape = jax.ShapeDtypeStruct((n, d), F32)
    if pos is None:
        row = lambda width: pl.BlockSpec((tm, width), lambda i: (i, 0))
        return pl.pallas_call(
            _ple_kernel,
            grid=(n // tm,),
            in_specs=[row(d), row(d), row(p.shape[-1])] + [_const_spec(c.shape) for c in consts],
            out_specs=row(d),
            out_shape=out_shape,
            compiler_params=_cparams("arbitrary"),
        )(x1, moe, p, *consts)
    row = lambda width: pl.BlockSpec((tm, width), lambda i, pos: (i, 0))
    const = lambda shape: pl.BlockSpec(shape, lambda i, pos: (0,) * len(shape))
    return pl.pallas_call(
        functools.partial(_ple_gather_kernel, tm=tm),
        grid_spec=pltpu.PrefetchScalarGridSpec(
            num_scalar_prefetch=1,
            grid=(n // tm,),
            in_specs=[row(d), pl.BlockSpec(memory_space=pl.ANY), row(p.shape[-1])]
                     + [const(c.shape) for c in consts],
            out_specs=row(d),
            scratch_shapes=[pltpu.VMEM((2, tm) + moe.shape[1:], F32), pltpu.SemaphoreType.DMA((2,))]),
        out_shape=out_shape,
        compiler_params=_cparams("arbitrary"),
    )(pos, x1, moe, p, *consts)


def _prepare_weights(norm_mix_w, w_in, conv_w, conv_b, dt_bias, a_log, d_skip, ssd_norm_w,
                     lb_logits, hg_norm_w, w_out, norm_ffn_w, w_router_group, b_router_group,
                     w_router_expert, b_router_expert, w_exp_gate, w_exp_up, w_exp_down,
                     norm_ple_w, w_ple_gate, w_ple_proj, final_norm_w, layer):
    i = layer
    row = lambda t: t.reshape(1, -1).astype(F32)
    lane_pad = lambda t: jnp.pad(t, [(0, 0)] * (t.ndim - 1) + [(0, LANES - t.shape[-1])])
    splits = (SSD_D, CONV_DIM, SSD_HEADS, HG_D, HG_D, HG_D, HG_D)
    offs = [0]
    for s in splits:
        offs.append(offs[-1] + s)
    seg = lambda k: w_in[i][:, offs[k]:offs[k + 1]].astype(BF16)
    w_in_segs = (seg(0), seg(1), seg(3), seg(4), seg(5), seg(6), lane_pad(seg(2)))
    w_router = lane_pad(jnp.concatenate([w_router_expert[i], w_router_group[i]], axis=1).astype(F32))
    r_hi = w_router.astype(BF16)
    r_lo = (w_router - r_hi.astype(F32)).astype(BF16)
    r_bias = jnp.pad(jnp.concatenate([b_router_expert[i].reshape(-1), b_router_group[i]]).astype(F32),
                     [(0, ROUTER_ROWS - N_EXPERTS - N_EGROUPS)]).reshape(ROUTER_ROWS, 1)
    lb = jnp.cumsum(jax.nn.softmax(lb_logits.astype(F32), axis=0), axis=0)[i]
    return dict(
        norm_mix_w=row(norm_mix_w[i]), w_in=w_in_segs, conv_w=conv_w[i].astype(F32),
        conv_b=row(conv_b[i]), dt_bias=lane_pad(row(dt_bias[i])), a_log=lane_pad(row(a_log[i])),
        d_skip=row(jnp.repeat(d_skip[i], SSD_HEADDIM)), ssd_norm_w=row(ssd_norm_w[i]),
        lb=row(lb), hg_norm_w=row(hg_norm_w[i]), w_out=w_out[i].astype(BF16),
        norm_ffn_w=row(norm_ffn_w[i]), r_hi=r_hi, r_lo=r_lo, r_bias=r_bias,
        w_exp_gate=w_exp_gate[i].astype(BF16), w_exp_up=w_exp_up[i].astype(BF16),
        w_exp_down=w_exp_down[i].astype(BF16), norm_ple_w=row(norm_ple_w[i]),
        w_ple_gate=w_ple_gate[i].astype(BF16), w_ple_proj=w_ple_proj[i].astype(BF16),
        final_norm_w=row(final_norm_w))


def _token_tile(n, cap):
    tm = cap
    while tm >= SUBLANES:
        if n % tm == 0:
            return tm
        tm //= 2
    raise ValueError(f"token count {n} must be a multiple of {SUBLANES}")


def _trunk(x, p, ssm_in, conv_in, hg_in, w):
    b, l, d = x.shape
    n = b * l
    tm = _token_tile(n, ROW_TILE)
    x2d = x.reshape(n, d)
    z, xbc, q, f, v, g, dt = _in_proj(x2d, w["norm_mix_w"], w["w_in"], _token_tile(n, IN_PROJ_TILE))
    ssd_params = (w["conv_w"], w["conv_b"], w["dt_bias"], w["a_log"], w["d_skip"], w["ssd_norm_w"])
    if l == 1 and ssm_in is not None:
        y, conv_new, ssm_new = _ssd_step(z, xbc, dt, *ssd_params, conv_in, ssm_in)
        o, hg_new = _hgrn_step(q, f, v, g, w["lb"], w["hg_norm_w"], hg_in)
    else:
        seq = lambda t: t.reshape(b, l, t.shape[-1])
        y, conv_new, ssm_new = _ssd(seq(z), seq(xbc), seq(dt), *ssd_params, conv_in, ssm_in)
        o, hg_new = _hgrn(seq(q), seq(f), seq(v), seq(g), w["lb"], w["hg_norm_w"], hg_in)
    router = (w["r_hi"], w["r_lo"], w["r_bias"])
    experts = (w["w_exp_gate"], w["w_exp_up"], w["w_exp_down"])
    sorted_moe = n >= SORTED_MOE_MIN_TOKENS
    x1, h, routed = _out_proj(x2d, y.reshape(n, -1), o.reshape(n, -1), w["w_out"], w["norm_ffn_w"],
                              *router, tm, tiled_rows=sorted_moe)
    if sorted_moe:
        pos, e1_tab, e2_tab = _positions(routed, n, MOE_TILE)
        hs = _dispatch(h, pos, n + N_CLASSES * MOE_TILE, tm)
        moe = _moe_sorted(hs, e1_tab, e2_tab, *router, *experts, MOE_TILE)
    else:
        pos = None
        moe = _moe_dense(h, routed, *experts, tm)
    y_out = _ple(x1, moe, pos, p.reshape(n, -1), w["norm_ple_w"], w["w_ple_gate"], w["w_ple_proj"],
                 w["final_norm_w"], tm)
    return y_out.reshape(b, l, d), ssm_new[None], conv_new[None], hg_new[None]


def kernel(x_prompt, x_sample, state_ssm, state_conv, state_hgrn, p_prompt, p_sample, norm_mix_w, w_in, conv_w, conv_b, dt_bias, a_log, d_skip, ssd_norm_w, lb_logits, hg_norm_w, w_out, norm_ffn_w, w_router_group, b_router_group, w_router_expert, b_router_expert, w_exp_gate, w_exp_up, w_exp_down, norm_ple_w, w_ple_gate, w_ple_proj, final_norm_w):
    assert p_prompt.shape[0] == 1, "the per-layer-embedding kernel also applies the final norm: depth 1 only"
    w = _prepare_weights(norm_mix_w, w_in, conv_w, conv_b, dt_bias, a_log, d_skip, ssd_norm_w,
                         lb_logits, hg_norm_w, w_out, norm_ffn_w, w_router_group, b_router_group,
                         w_router_expert, b_router_expert, w_exp_gate, w_exp_up, w_exp_down,
                         norm_ple_w, w_ple_gate, w_ple_proj, final_norm_w, layer=0)
    y_p, ssm_p, conv_p, hg_p = _trunk(x_prompt, p_prompt[0], None, None, None, w)
    y_s, ssm_s, conv_s, hg_s = _trunk(x_sample, p_sample[0], state_ssm[0], state_conv[0],
                                      state_hgrn[0], w)
    return (y_p, y_s, ssm_p, conv_p, hg_p, ssm_s, conv_s, hg_s)
```

```python
import functools

import jax
import jax.numpy as jnp
from jax import lax
from jax.experimental import pallas as pl
from jax.experimental.pallas import tpu as pltpu

F32 = jnp.float32
BF16 = jnp.bfloat16

EPS = 1e-6
SSD_HEADS = 16
SSD_HEADDIM = 64
SSD_D = SSD_HEADS * SSD_HEADDIM
SSD_GROUPS = 2
D_STATE = 128
CONV_W = 4
CONV_DIM = SSD_D + 2 * SSD_GROUPS * D_STATE
HG_HEADS = 8
HG_DK = 128
HG_DV = 128
HG_D = HG_HEADS * HG_DV
N_EGROUPS = 4
EXPERTS_PER_GROUP = 4
N_EXPERTS = N_EGROUPS * EXPERTS_PER_GROUP

LANES = 128
SUBLANES = 8
VMEM_LIMIT = 48 * 1024 * 1024
DMA_QUEUES = 2
CHUNK = 128
HG_SAFE_LOG_DECAY = 60.0
ROUTER_ROWS = 32
CLASS_ROW = N_EXPERTS + N_EGROUPS
PAIRS_PER_GROUP = EXPERTS_PER_GROUP * (EXPERTS_PER_GROUP - 1) // 2
N_CLASSES = N_EGROUPS * PAIRS_PER_GROUP
MOE_TILE = 256
IN_PROJ_TILE = 256
ROW_TILE = 512
SORTED_MOE_MIN_TOKENS = 8 * SUBLANES * LANES


def _dot(a, b):
    return jnp.dot(a, b, preferred_element_type=F32)


def _dot_nt(a, b):
    return lax.dot_general(a, b, (((1,), (1,)), ((), ())), preferred_element_type=F32)


def _dot_tn(a, b):
    return lax.dot_general(a, b, (((0,), (0,)), ((), ())), preferred_element_type=F32)


def _rms_unit(x):
    return x * lax.rsqrt(jnp.mean(x * x, axis=-1, keepdims=True) + EPS)


def _sigmoid(x):
    return 1.0 / (1.0 + jnp.exp(-x))


def _softplus(x):
    return jnp.maximum(x, 0.0) + jnp.log(1.0 + jnp.exp(-jnp.abs(x)))


def _silu(x):
    return x * _sigmoid(x)


def _split3(x):
    hi = x.astype(BF16)
    r = x - hi.astype(F32)
    mid = r.astype(BF16)
    lo = (r - mid.astype(F32)).astype(BF16)
    return hi, mid, lo


def _cumsum_rows(x, tri):
    hi, mid, lo = _split3(x)
    return _dot(tri, hi) + _dot(tri, mid) + _dot(tri, lo)


def _tri(n):
    r = lax.broadcasted_iota(jnp.int32, (n, n), 0)
    c = lax.broadcasted_iota(jnp.int32, (n, n), 1)
    return r >= c


def _cparams(*sem):
    return pltpu.CompilerParams(dimension_semantics=sem, vmem_limit_bytes=VMEM_LIMIT)


def _const_spec(shape):
    nd = len(shape)
    return pl.BlockSpec(shape, lambda *_: (0,) * nd)


def _inproj_kernel(x_ref, nw_ref, *refs):
    hb = (_rms_unit(x_ref[...]) * nw_ref[...]).astype(BF16)
    nseg = len(refs) // 2
    for w_ref, out_ref in zip(refs[:nseg], refs[nseg:]):
        n = out_ref.shape[-1]
        for c0 in range(0, n, 512):
            cw = min(512, n - c0)
            out_ref[:, c0:c0 + cw] = _dot(hb, w_ref[:, c0:c0 + cw])


def _in_proj(x, nw, ws, tm):
    n, d = x.shape
    widths = [w.shape[-1] for w in ws]
    return pl.pallas_call(
        _inproj_kernel,
        grid=(n // tm,),
        in_specs=[pl.BlockSpec((tm, d), lambda i: (i, 0)), _const_spec(nw.shape)]
                 + [_const_spec(w.shape) for w in ws],
        out_specs=[pl.BlockSpec((tm, s), lambda i: (i, 0)) for s in widths],
        out_shape=[jax.ShapeDtypeStruct((n, s), F32) for s in widths],
        compiler_params=_cparams("arbitrary"),
    )(x, nw, *ws)


def _expand(x, e_ref):
    hi, mid, lo = _split3(x)
    e = e_ref[...]
    return _dot(hi, e) + _dot(mid, e) + _dot(lo, e)


PROJ_WIDTHS = (SSD_D, CONV_DIM, HG_D, HG_D, HG_D, HG_D, LANES)
OFF_Z, OFF_XBC, OFF_Q, OFF_F, OFF_V, OFF_G, OFF_DT = (sum(PROJ_WIDTHS[:k]) for k in range(len(PROJ_WIDTHS)))
PROJ_COLS = sum(PROJ_WIDTHS)


def _project(x, nw_ref, w_refs, dst_ref):
    hb = (_rms_unit(x) * nw_ref[...]).astype(BF16)
    off = 0
    for w_ref in w_refs:
        n = w_ref.shape[-1]
        for c0 in range(0, n, 512):
            cw = min(512, n - c0)
            dst_ref[:, off + c0:off + c0 + cw] = _dot(hb, w_ref[:, c0:c0 + cw])
        off += n


def _ssd_chunk(p_ref, cw_ref, cb_ref, dtb_ref, alog_ref, dsk_ref, nw_ref, e1_ref, e2_ref,
               y_ref, cout_ref, xc_ref, st_ref):
    rows = p_ref.shape[0]
    hist = SUBLANES
    hpg = SSD_HEADS // SSD_GROUPS
    gw = hpg * SSD_HEADDIM
    gn = D_STATE

    xc_ref[hist:hist + rows, :] = p_ref[:, OFF_XBC:OFF_XBC + CONV_DIM]
    base = hist - (CONV_W - 1)
    acc = cb_ref[...] + cw_ref[0:1, :] * xc_ref[base:base + rows, :]
    for k in range(1, CONV_W):
        acc = acc + cw_ref[k:k + 1, :] * xc_ref[base + k:base + k + rows, :]
    cout_ref[...] = xc_ref[base + rows:base + rows + CONV_W - 1, :]
    xc_ref[0:hist, :] = xc_ref[rows:rows + hist, :]
    xbc = _silu(acc)
    xs = xbc[:, :SSD_D]
    b_bf = [xbc[:, SSD_D + g * gn:SSD_D + (g + 1) * gn].astype(BF16) for g in range(SSD_GROUPS)]
    c_bf = [xbc[:, SSD_D + (SSD_GROUPS + g) * gn:SSD_D + (SSD_GROUPS + g + 1) * gn].astype(BF16)
            for g in range(SSD_GROUPS)]

    dt = _softplus(p_ref[:, OFF_DT:OFF_DT + LANES] + dtb_ref[...])
    a = dt * (-jnp.exp(alog_ref[...]))
    tri = _tri(rows)
    a_cs = _cumsum_rows(a, tri.astype(BF16))
    a_cs_t = a_cs.T
    dt_full = _expand(dt, e1_ref)
    a_full = _expand(a_cs, e1_ref)
    a_rep = _expand(a_cs, e2_ref)
    a_last = a_full[rows - 1:rows, :]

    xdt = xs * dt_full
    xdt_bf = xdt.astype(BF16)
    xw_bf = (xdt * jnp.exp(a_last - a_full)).astype(BF16)
    skip = dsk_ref[...] * xs
    lane = lax.broadcasted_iota(jnp.int32, (rows, LANES), 1)
    first_half = lane < SSD_HEADDIM
    zg = _silu(p_ref[:, OFF_Z:OFF_Z + SSD_D])

    for g in range(SSD_GROUPS):
        gs = slice(g * gw, (g + 1) * gw)
        scores = jnp.where(tri, _dot_nt(c_bf[g], b_bf[g]), 0.0)
        st_old = st_ref[g]
        y_off = _dot(c_bf[g], st_old.astype(BF16)) * jnp.exp(a_full[:, gs])
        st_ref[g] = st_old * jnp.exp(a_last[:, gs]) + _dot_tn(b_bf[g], xw_bf[:, gs])
        parts = []
        for pair in range(hpg // 2):
            h0 = g * hpg + 2 * pair
            cols = slice(h0 * SSD_HEADDIM, (h0 + 2) * SSD_HEADDIM)
            x_pair = xdt_bf[:, cols]
            acc = None
            for hh, keep in ((h0, first_half), (h0 + 1, jnp.logical_not(first_half))):
                diff = a_rep[:, hh * LANES:hh * LANES + rows] - a_cs_t[hh:hh + 1, :]
                m = (scores * jnp.exp(jnp.minimum(diff, 0.0))).astype(BF16)
                part = _dot(m, jnp.where(keep, x_pair, jnp.zeros_like(x_pair)))
                acc = part if acc is None else acc + part
            parts.append(acc)
        y = jnp.concatenate(parts, axis=1) + y_off + skip[:, gs]
        y_ref[:, gs] = (_rms_unit(y * zg[:, gs]) * nw_ref[:, gs]).astype(y_ref.dtype)


def _hgrn_chunk(p_ref, lb_ref, nw_ref, o_ref, st_ref, a_ref, kk_ref, gcs_ref):
    rows = p_ref.shape[0]
    dk = HG_DK
    lb = lb_ref[...]
    sig = _sigmoid(p_ref[:, OFF_F:OFF_F + HG_D])
    logf = jnp.log(lb + (1.0 - lb) * sig)
    kk = (1.0 - lb) * (1.0 - sig)
    tri = _tri(rows)
    gcs = _cumsum_rows(logf, tri.astype(BF16))
    glast = gcs[rows - 1:rows, :]
    qt = p_ref[:, OFF_Q:OFF_Q + HG_D] * jnp.exp(gcs)
    kk_ref[...] = kk
    gcs_ref[...] = gcs

    kt = kk * jnp.exp(-gcs)
    khat = kk * jnp.exp(glast - gcs)
    for h in range(HG_HEADS):
        sl = slice(h * dk, (h + 1) * dk)
        a_ref[h] = _dot_nt(qt[:, sl].astype(BF16), kt[:, sl].astype(BF16))

    @pl.when(jnp.min(glast) < -HG_SAFE_LOG_DECAY)
    def _():
        ri = lax.broadcasted_iota(jnp.int32, (rows, dk), 0)
        ci = lax.broadcasted_iota(jnp.int32, (rows, rows), 1)
        for h in range(HG_HEADS):
            sl = slice(h * dk, (h + 1) * dk)
            q_h = p_ref[:, OFF_Q + h * dk:OFF_Q + (h + 1) * dk]

            def cols(j8, a_h):
                base = pl.multiple_of(j8 * SUBLANES, SUBLANES)
                k_blk = kk_ref[pl.ds(base, SUBLANES), sl]
                g_blk = gcs_ref[pl.ds(base, SUBLANES), sl]
                g_h = gcs_ref[:, sl]
                for r in range(SUBLANES):
                    j = base + r
                    t = q_h * k_blk[r:r + 1] * jnp.exp(
                        jnp.where(ri >= j, g_h - g_blk[r:r + 1], -jnp.inf))
                    a_h = jnp.where(ci == j, jnp.sum(t, axis=-1, keepdims=True), a_h)
                return a_h

            a_ref[h] = lax.fori_loop(0, rows // SUBLANES, cols, jnp.zeros((rows, rows), F32))

    gate = _silu(p_ref[:, OFF_G:OFF_G + HG_D])
    for h in range(HG_HEADS):
        sl = slice(h * dk, (h + 1) * dk)
        st = st_ref[h]
        v_bf = p_ref[:, OFF_V + h * HG_DV:OFF_V + (h + 1) * HG_DV].astype(BF16)
        a_h = jnp.where(tri, a_ref[h], 0.0).astype(BF16)
        o = _dot(a_h, v_bf) + _dot_nt(qt[:, sl].astype(BF16), st.astype(BF16))
        st_ref[h] = st * jnp.exp(glast[:, sl]) + _dot_tn(v_bf, khat[:, sl].astype(BF16))
        o_ref[:, sl] = (_rms_unit(o) * nw_ref[:, sl] * gate[:, sl]).astype(o_ref.dtype)


def _mixers_kernel(x_ref, xn_ref, nw_ref, wz_ref, wxbc_ref, wq_ref, wf_ref, wv_ref, wg_ref, wdt_ref,
                   cw_ref, cb_ref, dtb_ref, alog_ref, dsk_ref, snw_ref, e1_ref, e2_ref, lb_ref, hnw_ref,
                   y_ref, o_ref, cout_ref, sout_ref, hout_ref,
                   pa_ref, pb_ref, xc_ref, st_ref, hst_ref, a_ref, kk_ref, gcs_ref, *, chunks):
    s = pl.program_id(0)
    rows = pa_ref.shape[0]
    c0 = (s * 2) % chunks
    w_refs = (wz_ref, wxbc_ref, wq_ref, wf_ref, wv_ref, wg_ref, wdt_ref)
    hpg = SSD_HEADS // SSD_GROUPS

    @pl.when(s == 0)
    def _():
        _project(x_ref[0:rows, :], nw_ref, w_refs, pa_ref)

    @pl.when(c0 == 0)
    def _():
        xc_ref[0:SUBLANES, :] = jnp.zeros((SUBLANES, CONV_DIM), F32)
        st_ref[...] = jnp.zeros(st_ref.shape, F32)
        hst_ref[...] = jnp.zeros(hst_ref.shape, F32)

    for half, (cur_ref, nxt_ref) in enumerate(((pa_ref, pb_ref), (pb_ref, pa_ref))):
        rs = slice(half * rows, (half + 1) * rows)
        x_next = x_ref[rows:2 * rows, :] if half == 0 else xn_ref[0:rows, :]
        _project(x_next, nw_ref, w_refs, nxt_ref)
        _ssd_chunk(cur_ref, cw_ref, cb_ref, dtb_ref, alog_ref, dsk_ref, snw_ref, e1_ref, e2_ref,
                   y_ref.at[rs], cout_ref.at[0], xc_ref, st_ref)
        _hgrn_chunk(cur_ref, lb_ref, hnw_ref, o_ref.at[rs], hst_ref, a_ref, kk_ref, gcs_ref)

    @pl.when(c0 + 2 == chunks)
    def _():
        for g in range(SSD_GROUPS):
            sout_ref[0, g * hpg:(g + 1) * hpg] = st_ref[g].T.reshape(hpg, SSD_HEADDIM, D_STATE)
        for h in range(HG_HEADS):
            hout_ref[0, h] = hst_ref[h].T


def _mixers(x, b, l, w):
    n, d = x.shape
    chunks = l // CHUNK
    assert l % (2 * CHUNK) == 0
    step_rows = 2 * CHUNK
    steps = n // step_rows
    head = jnp.arange(LANES)[:, None]
    e1 = (head == jnp.arange(SSD_D)[None, :] // SSD_HEADDIM).astype(BF16)
    e2 = (head == jnp.arange(SSD_HEADS * LANES)[None, :] // LANES).astype(BF16)
    consts = (w["norm_mix_w"], *w["w_in"], w["conv_w"], w["conv_b"], w["dt_bias"], w["a_log"],
              w["d_skip"], w["ssd_norm_w"], e1, e2, w["lb"], w["hg_norm_w"])
    tok = lambda width: pl.BlockSpec((step_rows, width), lambda s: (s, 0))
    per_seq = lambda shape: pl.BlockSpec((1,) + shape, lambda s: (s * 2 // chunks,) + (0,) * len(shape))
    ssm_shape = (SSD_HEADS, SSD_HEADDIM, D_STATE)
    hg_shape = (HG_HEADS, HG_DK, HG_DV)
    return pl.pallas_call(
        functools.partial(_mixers_kernel, chunks=chunks),
        grid=(steps,),
        in_specs=[tok(d), pl.BlockSpec((step_rows, d), lambda s: (jnp.minimum(s + 1, steps - 1), 0))]
                 + [_const_spec(c.shape) for c in consts],
        out_specs=[tok(SSD_D), tok(HG_D), per_seq((CONV_W - 1, CONV_DIM)), per_seq(ssm_shape),
                   per_seq(hg_shape)],
        out_shape=[jax.ShapeDtypeStruct((n, SSD_D), BF16), jax.ShapeDtypeStruct((n, HG_D), BF16),
                   jax.ShapeDtypeStruct((b, CONV_W - 1, CONV_DIM), F32),
                   jax.ShapeDtypeStruct((b,) + ssm_shape, F32),
                   jax.ShapeDtypeStruct((b,) + hg_shape, F32)],
        scratch_shapes=[pltpu.VMEM((CHUNK, PROJ_COLS), F32), pltpu.VMEM((CHUNK, PROJ_COLS), F32),
                        pltpu.VMEM((CHUNK + SUBLANES, CONV_DIM), F32),
                        pltpu.VMEM((SSD_GROUPS, D_STATE, SSD_D // SSD_GROUPS), F32),
                        pltpu.VMEM((HG_HEADS, HG_DV, HG_DK), F32),
                        pltpu.VMEM((HG_HEADS, CHUNK, CHUNK), F32),
                        pltpu.VMEM((CHUNK, HG_D), F32), pltpu.VMEM((CHUNK, HG_D), F32)],
        compiler_params=_cparams("arbitrary"),
    )(x, x, *consts)


def _tokens_to_lanes(x):
    tb, n = x.shape
    return jnp.concatenate([x, jnp.zeros((LANES - tb, n), F32)], axis=0).T


def _step_decay_kernel(dt_ref, dtb_ref, alog_ref, da_ref):
    da_ref[...] = jnp.exp(_softplus(dt_ref[...] + dtb_ref[...]) * (-jnp.exp(alog_ref[...])))


def _ssd_step_kernel(da_ref, z_ref, xbc_ref, dt_ref, cw_ref, cb_ref, dtb_ref, dsk_ref, nw_ref, e1_ref,
                     cs_ref, st_ref, y_ref, cnew_ref, snew_ref, *, tb):
    cd = CONV_DIM
    gn = D_STATE
    hpg = SSD_HEADS // SSD_GROUPS
    gw = hpg * SSD_HEADDIM
    first = pl.program_id(0) * tb
    x_in = xbc_ref[...]
    acc = cb_ref[...] + cw_ref[CONV_W - 1:CONV_W, :] * x_in
    for k in range(CONV_W - 1):
        acc = acc + cw_ref[k:k + 1, :] * cs_ref[:, k * cd:(k + 1) * cd]
    cnew_ref[:, :(CONV_W - 2) * cd] = cs_ref[:, cd:]
    cnew_ref[:, (CONV_W - 2) * cd:] = x_in
    xbc = _silu(acc)
    xs = xbc[:, :SSD_D]
    dt = _softplus(dt_ref[...] + dtb_ref[...])
    xdt_t = _tokens_to_lanes(xs * _expand(dt, e1_ref)).astype(BF16)
    row_tok = lax.broadcasted_iota(jnp.int32, (LANES, tb * gn), 0)
    col_tok = lax.broadcasted_iota(jnp.int32, (LANES, tb * gn), 1) // gn
    lane = lax.broadcasted_iota(jnp.int32, (SSD_HEADDIM, LANES), 1)
    y_groups = []
    for g in range(SSD_GROUPS):
        b_g = xbc[:, SSD_D + g * gn:SSD_D + (g + 1) * gn]
        c_g = xbc[:, SSD_D + (SSD_GROUPS + g) * gn:SSD_D + (SSD_GROUPS + g + 1) * gn]
        b_wide = jnp.concatenate([jnp.tile(b_g, (1, tb)), jnp.zeros((LANES - tb, tb * gn), F32)], axis=0)
        b_diag = jnp.where(row_tok == col_tok, b_wide, 0.0).astype(BF16)
        upd = _dot(xdt_t[g * gw:(g + 1) * gw, :], b_diag)
        y_heads = []
        for hl in range(hpg):
            h = g * hpg + hl
            y_h = jnp.zeros((SSD_HEADDIM, LANES), F32)
            for j in range(tb):
                new = (st_ref[j, h] * da_ref[first + j, h]
                       + upd[hl * SSD_HEADDIM:(hl + 1) * SSD_HEADDIM, j * gn:(j + 1) * gn])
                snew_ref[j, h] = new
                y_h = jnp.where(lane == j, jnp.sum(new * c_g[j:j + 1, :], axis=-1, keepdims=True), y_h)
            y_heads.append(y_h)
        y_groups.append(jnp.concatenate(y_heads, axis=0))
    y = jnp.concatenate(y_groups, axis=0).T[:tb]
    y = (y + dsk_ref[...] * xs) * _silu(z_ref[...])
    for g in range(SSD_GROUPS):
        gs = slice(g * gw, (g + 1) * gw)
        y_ref[:, gs] = (_rms_unit(y[:, gs]) * nw_ref[:, gs]).astype(y_ref.dtype)


def _ssd_step(z, xbc, dt, conv_w, conv_b, dt_bias, a_log, d_skip, norm_w, conv_init, ssm_init):
    b = z.shape[0]
    tb = SUBLANES
    assert b % tb == 0
    da = pl.pallas_call(
        _step_decay_kernel,
        grid=(1,),
        in_specs=[_const_spec(dt.shape), _const_spec(dt_bias.shape), _const_spec(a_log.shape)],
        out_specs=_const_spec(dt.shape),
        out_shape=jax.ShapeDtypeStruct(dt.shape, F32),
    )(dt, dt_bias, a_log)
    head = jnp.arange(LANES)[:, None]
    e1 = (head == jnp.arange(SSD_D)[None, :] // SSD_HEADDIM).astype(BF16)
    params = (conv_w, conv_b, dt_bias, d_skip, norm_w, e1)
    hist = (CONV_W - 1) * CONV_DIM
    row = lambda w: pl.BlockSpec((tb, w), lambda i, da: (i, 0))
    const = lambda shape: pl.BlockSpec(shape, lambda i, da: (0,) * len(shape))
    state_shape = (SSD_HEADS, SSD_HEADDIM, D_STATE)
    st_spec = pl.BlockSpec((tb,) + state_shape, lambda i, da: (i, 0, 0, 0))
    y, conv_new, ssm_new = pl.pallas_call(
        functools.partial(_ssd_step_kernel, tb=tb),
        grid_spec=pltpu.PrefetchScalarGridSpec(
            num_scalar_prefetch=1,
            grid=(b // tb,),
            in_specs=[row(SSD_D), row(CONV_DIM), row(LANES)] + [const(p.shape) for p in params]
                     + [row(hist), st_spec],
            out_specs=[row(SSD_D), row(hist), st_spec]),
        out_shape=[jax.ShapeDtypeStruct((b, SSD_D), BF16), jax.ShapeDtypeStruct((b, hist), F32),
                   jax.ShapeDtypeStruct((b,) + state_shape, F32)],
        compiler_params=_cparams("arbitrary"),
    )(da, z, xbc, dt, *params, conv_init.reshape(b, hist), ssm_init)
    return y, conv_new.reshape(b, CONV_W - 1, CONV_DIM), ssm_new


def _hgrn_step_kernel(q_ref, f_ref, v_ref, g_ref, lb_ref, nw_ref, st_ref, o_ref, snew_ref, *, tb):
    dk, dv = HG_DK, HG_DV
    fr = f_ref[...]
    lb = lb_ref[...]
    sig = _sigmoid(fr)
    f_t = _tokens_to_lanes(lb + (1.0 - lb) * sig)
    k_t = _tokens_to_lanes((1.0 - lb) * (1.0 - sig))
    q_t = _tokens_to_lanes(q_ref[...])
    v = v_ref[...]
    gate = _silu(g_ref[...])
    sub = lax.broadcasted_iota(jnp.int32, (tb, dv), 0)
    for h in range(HG_HEADS):
        rs = slice(h * dk, (h + 1) * dk)
        vs = slice(h * dv, (h + 1) * dv)
        o_h = jnp.zeros((tb, dv), F32)
        for j in range(tb):
            fcol = jnp.broadcast_to(f_t[rs, j:j + 1], (dk, dv))
            kcol = jnp.broadcast_to(k_t[rs, j:j + 1], (dk, dv))
            qcol = jnp.broadcast_to(q_t[rs, j:j + 1], (dk, dv))
            new = st_ref[j, h] * fcol + kcol * v[j:j + 1, vs]
            snew_ref[j, h] = new
            o_h = jnp.where(sub == j, jnp.sum(new * qcol, axis=0, keepdims=True), o_h)
        o_ref[:, vs] = (_rms_unit(o_h) * nw_ref[:, vs] * gate[:, vs]).astype(o_ref.dtype)


def _hgrn_step(q, f, v, g, lb, norm_w, init):
    b = q.shape[0]
    tb = SUBLANES
    assert b % tb == 0
    row = pl.BlockSpec((tb, HG_D), lambda i: (i, 0))
    state_shape = (HG_HEADS, HG_DK, HG_DV)
    st_spec = pl.BlockSpec((tb,) + state_shape, lambda i: (i, 0, 0, 0))
    return pl.pallas_call(
        functools.partial(_hgrn_step_kernel, tb=tb),
        grid=(b // tb,),
        in_specs=[row, row, row, row, _const_spec(lb.shape), _const_spec(norm_w.shape), st_spec],
        out_specs=[row, st_spec],
        out_shape=[jax.ShapeDtypeStruct((b, HG_D), BF16),
                   jax.ShapeDtypeStruct((b,) + state_shape, F32)],
        compiler_params=_cparams("arbitrary"),
    )(q, f, v, g, lb, norm_w, init)


def _route_t(logits, bias, group=None):
    row = lax.broadcasted_iota(jnp.int32, logits.shape, 0)
    ninf = -jnp.inf
    big = jnp.int32(ROUTER_ROWS)
    is_g = (row >= N_EXPERTS) & (row < N_EXPERTS + N_EGROUPS)
    cmax = lambda t: jnp.max(t, axis=0, keepdims=True)
    csum = lambda t: jnp.sum(t, axis=0, keepdims=True)
    first = lambda m: jnp.min(jnp.where(m, row, big), axis=0, keepdims=True)

    gl = jnp.where(is_g, logits, ninf)
    gp = jnp.exp(gl - cmax(gl))
    gprob = gp / csum(gp)
    biased = logits + bias
    if group is None:
        gb = jnp.where(is_g, biased, ninf)
        gsel = first(gb == cmax(gb))
    else:
        gsel = group + N_EXPERTS
    gw = csum(jnp.where(row == gsel, gprob, 0.0))
    e0 = (gsel - N_EXPERTS) * EXPERTS_PER_GROUP
    in_grp = (row >= e0) & (row < e0 + EXPERTS_PER_GROUP)
    el = jnp.where(in_grp, logits, ninf)
    ep = jnp.exp(el - cmax(el))
    eprob = ep / csum(ep)
    eb = jnp.where(in_grp, biased, ninf)
    i1 = first(eb == cmax(eb))
    eb2 = jnp.where(row == i1, ninf, eb)
    i2 = first(eb2 == cmax(eb2))
    p1 = csum(jnp.where(row == i1, eprob, 0.0))
    p2 = csum(jnp.where(row == i2, eprob, 0.0))
    den = p1 + p2
    a = (jnp.minimum(i1, i2) - e0).astype(F32)
    b = (jnp.maximum(i1, i2) - e0).astype(F32)
    group = jnp.broadcast_to(gsel - N_EXPERTS, (1, logits.shape[1])).astype(F32)
    cls = group * PAIRS_PER_GROUP + a * (7.0 - a) * 0.5 + (b - a - 1.0)
    return (jnp.where(row == i1, gw * (p1 / den), 0.0)
            + jnp.where(row == i2, gw * (p2 / den), 0.0)
            + jnp.where(row == CLASS_ROW, cls, 0.0))


def _router_logits_t(h, rhi_ref, rlo_ref):
    h_hi = h.astype(BF16)
    h_lo = (h - h_hi.astype(F32)).astype(BF16)
    logits = _dot(h_hi, rhi_ref[...]) + _dot(h_hi, rlo_ref[...]) + _dot(h_lo, rhi_ref[...])
    return logits.T[:ROUTER_ROWS]


def _rows_to_lanes(gates_t):
    r, tm = gates_t.shape
    return jnp.concatenate([gates_t, jnp.zeros((LANES - r, tm), F32)], axis=0).T


def _outproj_kernel(x_ref, y_ref, o_ref, w_ref, nw_ref, rhi_ref, rlo_ref, rb_ref,
                    x1_ref, h_ref, r_ref, *, tiled_rows):
    dy = y_ref.shape[-1]
    tm, d = x_ref.shape
    mix = _dot(y_ref[...], w_ref[:dy, :]) + _dot(o_ref[...], w_ref[dy:, :])
    x1 = x_ref[...] + mix
    x1_ref[...] = x1
    h = _rms_unit(x1) * nw_ref[...]
    gates_t = _route_t(_router_logits_t(h, rhi_ref, rlo_ref), rb_ref[...])
    if tiled_rows:
        h_ref[...] = h.reshape(tm, d // LANES, LANES)
        r_ref[...] = jnp.broadcast_to(gates_t[CLASS_ROW:CLASS_ROW + 1, :], (SUBLANES, tm))
    else:
        h_ref[...] = h
        r_ref[...] = _rows_to_lanes(gates_t)


def _out_proj(x, y, o, w, nw, r_hi, r_lo, r_bias, tm, tiled_rows):
    n, d = x.shape
    row = lambda width: pl.BlockSpec((tm, width), lambda i: (i, 0))
    consts = (w, nw, r_hi, r_lo, r_bias)
    if tiled_rows:
        h_spec = pl.BlockSpec((tm, d // LANES, LANES), lambda i: (i, 0, 0))
        h_shape = jax.ShapeDtypeStruct((n, d // LANES, LANES), F32)
        r_spec = pl.BlockSpec((SUBLANES, tm), lambda i: (i, 0))
        r_shape = jax.ShapeDtypeStruct((n // tm * SUBLANES, tm), F32)
    else:
        h_spec, h_shape = row(d), jax.ShapeDtypeStruct((n, d), F32)
        r_spec, r_shape = row(LANES), jax.ShapeDtypeStruct((n, LANES), F32)
    return pl.pallas_call(
        functools.partial(_outproj_kernel, tiled_rows=tiled_rows),
        grid=(n // tm,),
        in_specs=[row(d), row(y.shape[-1]), row(o.shape[-1])] + [_const_spec(c.shape) for c in consts],
        out_specs=[row(d), h_spec, r_spec],
        out_shape=[jax.ShapeDtypeStruct((n, d), F32), h_shape, r_shape],
        compiler_params=_cparams("arbitrary"),
    )(x, y, o, *consts)


def _pos_kernel(cls_ref, pos_ref, tab_ref, *, tile):
    rows, tm = cls_ref.shape
    cls = cls_ref[...]
    ri = lax.broadcasted_iota(jnp.int32, (tm, tm), 0)
    ci = lax.broadcasted_iota(jnp.int32, (tm, tm), 1)
    upper = (ri <= ci).astype(BF16)
    rr = lax.broadcasted_iota(jnp.int32, (rows, rows), 0)
    rc = lax.broadcasted_iota(jnp.int32, (rows, rows), 1)
    earlier = ((rc // SUBLANES < rr // SUBLANES) & (rc % SUBLANES == 0)).astype(BF16)
    tile_lane = lax.broadcasted_iota(jnp.int32, (1, LANES), 1).astype(F32)
    pos = jnp.zeros((rows, tm), F32)
    tiles_before = jnp.zeros((1, 1), F32)
    tile_class = jnp.zeros((1, LANES), F32)
    for c in range(N_CLASSES):
        onehot = jnp.where(cls == float(c), 1.0, 0.0)
        inc = _dot(onehot.astype(BF16), upper)
        rowtot = jnp.broadcast_to(inc[:, tm - 1:tm], (rows, tm))
        rowpre = _cumsum_rows(rowtot, earlier)
        cnt = rowpre[rows - 1:rows, 0:1] + rowtot[rows - 1:rows, 0:1]
        pos = pos + onehot * (tiles_before * tile + rowpre + inc - 1.0)
        tiles_before = tiles_before + jnp.floor((cnt + (tile - 1.0)) * (1.0 / tile))
        tile_class = tile_class + jnp.where(tile_lane >= tiles_before, 1.0, 0.0)
    pos_ref[...] = pos.astype(jnp.int32)
    used = tile_class < N_CLASSES
    group = jnp.floor(tile_class * (1.0 / PAIRS_PER_GROUP))
    pair = tile_class - group * PAIRS_PER_GROUP
    ge3 = jnp.where(pair >= 3.0, 1.0, 0.0)
    ge5 = jnp.where(pair >= 5.0, 1.0, 0.0)
    a = ge3 + ge5
    b = pair + 1.0 - 2.0 * ge3 - ge5
    e1 = jnp.where(used, group * EXPERTS_PER_GROUP + a, float(N_EXPERTS))
    e2 = jnp.where(used, group * EXPERTS_PER_GROUP + b, float(N_EXPERTS))
    sub = lax.broadcasted_iota(jnp.int32, (SUBLANES, LANES), 0)
    tab_ref[...] = jnp.where(sub == 0, e1, jnp.where(sub == 1, e2, 0.0)).astype(jnp.int32)


def _positions(cls, n, tile):
    rows, tm = cls.shape
    assert n // tile + N_CLASSES <= LANES
    pos, tab = pl.pallas_call(
        functools.partial(_pos_kernel, tile=tile),
        grid=(1,),
        in_specs=[_const_spec(cls.shape)],
        out_specs=[_const_spec(cls.shape), _const_spec((SUBLANES, LANES))],
        out_shape=[jax.ShapeDtypeStruct(cls.shape, jnp.int32),
                   jax.ShapeDtypeStruct((SUBLANES, LANES), jnp.int32)],
        compiler_params=_cparams("arbitrary"),
    )(cls)
    return pos.reshape(rows // SUBLANES, SUBLANES, tm)[:, 0, :].reshape(n), tab[0], tab[1]


def _dispatch_kernel(pos_ref, h_ref, hs_in_ref, hs_ref, buf_ref, sem_ref, *, td):
    del hs_in_ref
    i = pl.program_id(0)
    nsteps = pl.num_programs(0)
    slot = i % 2

    def row_copy(step, s, r):
        return pltpu.make_async_copy(buf_ref.at[s, r], hs_ref.at[pos_ref[step * td + r]], sem_ref.at[s])

    def start_all(step, s):
        def body(k, carry):
            for u in range(DMA_QUEUES):
                row_copy(step, s, DMA_QUEUES * k + u).start(priority=u)
            return carry
        lax.fori_loop(0, td // DMA_QUEUES, body, 0, unroll=4)

    def wait_all(step, s):
        def body(r, carry):
            row_copy(step, s, r).wait()
            return carry
        lax.fori_loop(0, td, body, 0, unroll=8)

    @pl.when(i >= 2)
    def _():
        wait_all(i - 2, slot)

    buf_ref[slot] = h_ref[...]
    start_all(i, slot)

    @pl.when(i == nsteps - 1)
    def _():
        @pl.when(i >= 1)
        def _():
            wait_all(i - 1, 1 - slot)
        wait_all(i, slot)


def _dispatch(h3, pos, n_sorted, td):
    n = h3.shape[0]
    tok = h3.shape[1:]
    return pl.pallas_call(
        functools.partial(_dispatch_kernel, td=td),
        grid_spec=pltpu.PrefetchScalarGridSpec(
            num_scalar_prefetch=1,
            grid=(n // td,),
            in_specs=[pl.BlockSpec((td,) + tok, lambda i, pos: (i, 0, 0)),
                      pl.BlockSpec(memory_space=pl.ANY)],
            out_specs=pl.BlockSpec(memory_space=pl.ANY),
            scratch_shapes=[pltpu.VMEM((2, td) + tok, F32), pltpu.SemaphoreType.DMA((2,))]),
        out_shape=jax.ShapeDtypeStruct((n_sorted,) + tok, F32),
        input_output_aliases={2: 0},
        compiler_params=_cparams("arbitrary"),
    )(pos, h3, jnp.zeros((n_sorted,) + tok, F32))


def _expert(h, gates, e, wg, wu, wd):
    lane = lax.broadcasted_iota(jnp.int32, gates.shape, 1)
    gcol = jnp.sum(jnp.where(lane == e, gates, 0.0), axis=-1, keepdims=True)
    act = _silu(_dot(h, wg)) * _dot(h, wu) * gcol
    return _dot(act.astype(BF16), wd)


def _moe_dense_kernel(h_ref, gates_ref, wg_ref, wu_ref, wd_ref, out_ref):
    e = pl.program_id(1)
    part = _expert(h_ref[...].astype(BF16), gates_ref[...], e, wg_ref[0], wu_ref[0], wd_ref[0])

    @pl.when(e == 0)
    def _():
        out_ref[...] = part

    @pl.when(e > 0)
    def _():
        out_ref[...] += part


def _moe_dense(h, gates, wg, wu, wd, tm):
    n, d = h.shape
    ne, _, ff = wg.shape
    row = lambda width: pl.BlockSpec((tm, width), lambda i, e: (i, 0))
    return pl.pallas_call(
        _moe_dense_kernel,
        grid=(n // tm, ne),
        in_specs=[row(d), row(LANES),
                  pl.BlockSpec((1, d, ff), lambda i, e: (e, 0, 0)),
                  pl.BlockSpec((1, d, ff), lambda i, e: (e, 0, 0)),
                  pl.BlockSpec((1, ff, d), lambda i, e: (e, 0, 0))],
        out_specs=row(d),
        out_shape=jax.ShapeDtypeStruct((n, d), F32),
        compiler_params=_cparams("arbitrary", "arbitrary"),
    )(h, gates, wg, wu, wd)


def _moe_sorted_kernel(e1_ref, e2_ref, hs_ref, rhi_ref, rlo_ref, rb_ref,
                       wg1_ref, wu1_ref, wd1_ref, wg2_ref, wu2_ref, wd2_ref, ys_ref):
    e1 = e1_ref[pl.program_id(0)]
    e2 = e2_ref[pl.program_id(0)]
    tm = hs_ref.shape[0]
    d = wg1_ref.shape[1]

    @pl.when(e1 < N_EXPERTS)
    def _():
        h = hs_ref[...].reshape(tm, d)
        logits_t = _router_logits_t(h, rhi_ref, rlo_ref)
        group = lax.shift_right_logical(e1, jnp.int32(EXPERTS_PER_GROUP.bit_length() - 1))
        gates = _rows_to_lanes(_route_t(logits_t, rb_ref[...], group=group))
        h_bf = h.astype(BF16)
        acc = (_expert(h_bf, gates, e1, wg1_ref[0], wu1_ref[0], wd1_ref[0])
               + _expert(h_bf, gates, e2, wg2_ref[0], wu2_ref[0], wd2_ref[0]))
        ys_ref[...] = acc.reshape(ys_ref.shape)

    @pl.when(e1 >= N_EXPERTS)
    def _():
        ys_ref[...] = jnp.zeros(ys_ref.shape, F32)


def _moe_sorted(hs, e1_tab, e2_tab, r_hi, r_lo, r_bias, wg, wu, wd, tm):
    n_sorted = hs.shape[0]
    ne, d, ff = wg.shape
    first = lambda i, e1, e2: (jnp.minimum(e1[i], ne - 1), 0, 0)
    second = lambda i, e1, e2: (jnp.minimum(e2[i], ne - 1), 0, 0)
    tok = pl.BlockSpec((tm,) + hs.shape[1:], lambda i, e1, e2: (i, 0, 0))
    const = lambda shape: pl.BlockSpec(shape, lambda i, e1, e2: (0,) * len(shape))
    up = lambda sel: pl.BlockSpec((1, d, ff), sel)
    down = lambda sel: pl.BlockSpec((1, ff, d), sel)
    return pl.pallas_call(
        _moe_sorted_kernel,
        grid_spec=pltpu.PrefetchScalarGridSpec(
            num_scalar_prefetch=2,
            grid=(n_sorted // tm,),
            in_specs=[tok, const(r_hi.shape), const(r_lo.shape), const(r_bias.shape),
                      up(first), up(first), down(first), up(second), up(second), down(second)],
            out_specs=tok),
        out_shape=jax.ShapeDtypeStruct(hs.shape, F32),
        compiler_params=_cparams("arbitrary"),
    )(e1_tab, e2_tab, hs, r_hi, r_lo, r_bias, wg, wu, wd, wg, wu, wd)


def _ple_math(x, p, npw_ref, wg_ref, wp_ref, fw_ref):
    hn = (_rms_unit(x) * npw_ref[...]).astype(BF16)
    gate = _sigmoid(_dot(hn, wg_ref[...]))
    x = x + gate * _dot(p.astype(BF16), wp_ref[...])
    return _rms_unit(x) * fw_ref[...]


def _ple_kernel(x1_ref, moe_ref, p_ref, npw_ref, wg_ref, wp_ref, fw_ref, y_ref):
    y_ref[...] = _ple_math(x1_ref[...] + moe_ref[...], p_ref[...], npw_ref, wg_ref, wp_ref, fw_ref)


def _ple_gather_kernel(pos_ref, x1_ref, ys_ref, p_ref, npw_ref, wg_ref, wp_ref, fw_ref, y_ref,
                       buf_ref, sem_ref, *, tm):
    i = pl.program_id(0)
    nsteps = pl.num_programs(0)
    slot = i % 2

    def row_copy(step, s, r):
        return pltpu.make_async_copy(ys_ref.at[pos_ref[step * tm + r]], buf_ref.at[s, r], sem_ref.at[s])

    def start_all(step, s):
        def body(k, carry):
            for u in range(DMA_QUEUES):
                row_copy(step, s, DMA_QUEUES * k + u).start(priority=u)
            return carry
        lax.fori_loop(0, tm // DMA_QUEUES, body, 0, unroll=4)

    @pl.when(i == 0)
    def _():
        start_all(0, 0)

    @pl.when(i + 1 < nsteps)
    def _():
        start_all(i + 1, 1 - slot)

    def wait_row(r, carry):
        row_copy(i, slot, r).wait()
        return carry
    lax.fori_loop(0, tm, wait_row, 0, unroll=8)

    moe = buf_ref[slot].reshape(x1_ref.shape)
    y_ref[...] = _ple_math(x1_ref[...] + moe, p_ref[...], npw_ref, wg_ref, wp_ref, fw_ref)


def _ple(x1, moe, pos, p, npw, wg, wp, fw, tm):
    n, d = x1.shape
    consts = (npw, wg, wp, fw)
    out_shape = jax.ShapeDtypeStruct((n, d), F32)
    if pos is None:
        row = lambda width: pl.BlockSpec((tm, width), lambda i: (i, 0))
        return pl.pallas_call(
            _ple_kernel,
            grid=(n // tm,),
            in_specs=[row(d), row(d), row(p.shape[-1])] + [_const_spec(c.shape) for c in consts],
            out_specs=row(d),
            out_shape=out_shape,
            compiler_params=_cparams("arbitrary"),
        )(x1, moe, p, *consts)
    row = lambda width: pl.BlockSpec((tm, width), lambda i, pos: (i, 0))
    const = lambda shape: pl.BlockSpec(shape, lambda i, pos: (0,) * len(shape))
    return pl.pallas_call(
        functools.partial(_ple_gather_kernel, tm=tm),
        grid_spec=pltpu.PrefetchScalarGridSpec(
            num_scalar_prefetch=1,
            grid=(n // tm,),
            in_specs=[row(d), pl.BlockSpec(memory_space=pl.ANY), row(p.shape[-1])]
                     + [const(c.shape) for c in consts],
            out_specs=row(d),
            scratch_shapes=[pltpu.VMEM((2, tm) + moe.shape[1:], F32), pltpu.SemaphoreType.DMA((2,))]),
        out_shape=out_shape,
        compiler_params=_cparams("arbitrary"),
    )(pos, x1, moe, p, *consts)


def _prepare_weights(norm_mix_w, w_in, conv_w, conv_b, dt_bias, a_log, d_skip, ssd_norm_w,
                     lb_logits, hg_norm_w, w_out, norm_ffn_w, w_router_group, b_router_group,
                     w_router_expert, b_router_expert, w_exp_gate, w_exp_up, w_exp_down,
                     norm_ple_w, w_ple_gate, w_ple_proj, final_norm_w, layer):
    i = layer
    row = lambda t: t.reshape(1, -1).astype(F32)
    lane_pad = lambda t: jnp.pad(t, [(0, 0)] * (t.ndim - 1) + [(0, LANES - t.shape[-1])])
    splits = (SSD_D, CONV_DIM, SSD_HEADS, HG_D, HG_D, HG_D, HG_D)
    offs = [0]
    for s in splits:
        offs.append(offs[-1] + s)
    seg = lambda k: w_in[i][:, offs[k]:offs[k + 1]].astype(BF16)
    w_in_segs = (seg(0), seg(1), seg(3), seg(4), seg(5), seg(6), lane_pad(seg(2)))
    w_router = lane_pad(jnp.concatenate([w_router_expert[i], w_router_group[i]], axis=1).astype(F32))
    r_hi = w_router.astype(BF16)
    r_lo = (w_router - r_hi.astype(F32)).astype(BF16)
    r_bias = jnp.pad(jnp.concatenate([b_router_expert[i].reshape(-1), b_router_group[i]]).astype(F32),
                     [(0, ROUTER_ROWS - N_EXPERTS - N_EGROUPS)]).reshape(ROUTER_ROWS, 1)
    lb = jnp.cumsum(jax.nn.softmax(lb_logits.astype(F32), axis=0), axis=0)[i]
    return dict(
        norm_mix_w=row(norm_mix_w[i]), w_in=w_in_segs, conv_w=conv_w[i].astype(F32),
        conv_b=row(conv_b[i]), dt_bias=lane_pad(row(dt_bias[i])), a_log=lane_pad(row(a_log[i])),
        d_skip=row(jnp.repeat(d_skip[i], SSD_HEADDIM)), ssd_norm_w=row(ssd_norm_w[i]),
        lb=row(lb), hg_norm_w=row(hg_norm_w[i]), w_out=w_out[i].astype(BF16),
        norm_ffn_w=row(norm_ffn_w[i]), r_hi=r_hi, r_lo=r_lo, r_bias=r_bias,
        w_exp_gate=w_exp_gate[i].astype(BF16), w_exp_up=w_exp_up[i].astype(BF16),
        w_exp_down=w_exp_down[i].astype(BF16), norm_ple_w=row(norm_ple_w[i]),
        w_ple_gate=w_ple_gate[i].astype(BF16), w_ple_proj=w_ple_proj[i].astype(BF16),
        final_norm_w=row(final_norm_w))


def _token_tile(n, cap):
    tm = cap
    while tm >= SUBLANES:
        if n % tm == 0:
            return tm
        tm //= 2
    raise ValueError(f"token count {n} must be a multiple of {SUBLANES}")


def _trunk(x, p, ssm_in, conv_in, hg_in, w):
    b, l, d = x.shape
    n = b * l
    tm = _token_tile(n, ROW_TILE)
    x2d = x.reshape(n, d)
    if l == 1 and ssm_in is not None:
        z, xbc, q, f, v, g, dt = _in_proj(x2d, w["norm_mix_w"], w["w_in"], _token_tile(n, IN_PROJ_TILE))
        y, conv_new, ssm_new = _ssd_step(z, xbc, dt, w["conv_w"], w["conv_b"], w["dt_bias"], w["a_log"],
                                         w["d_skip"], w["ssd_norm_w"], conv_in, ssm_in)
        o, hg_new = _hgrn_step(q, f, v, g, w["lb"], w["hg_norm_w"], hg_in)
    else:
        assert ssm_in is None and conv_in is None and hg_in is None, "multi-token groups start empty"
        y, o, conv_new, ssm_new, hg_new = _mixers(x2d, b, l, w)
    router = (w["r_hi"], w["r_lo"], w["r_bias"])
    experts = (w["w_exp_gate"], w["w_exp_up"], w["w_exp_down"])
    sorted_moe = n >= SORTED_MOE_MIN_TOKENS
    x1, h, routed = _out_proj(x2d, y.reshape(n, -1), o.reshape(n, -1), w["w_out"], w["norm_ffn_w"],
                              *router, tm, tiled_rows=sorted_moe)
    if sorted_moe:
        pos, e1_tab, e2_tab = _positions(routed, n, MOE_TILE)
        hs = _dispatch(h, pos, n + N_CLASSES * MOE_TILE, tm)
        moe = _moe_sorted(hs, e1_tab, e2_tab, *router, *experts, MOE_TILE)
    else:
        pos = None
        moe = _moe_dense(h, routed, *experts, tm)
    y_out = _ple(x1, moe, pos, p.reshape(n, -1), w["norm_ple_w"], w["w_ple_gate"], w["w_ple_proj"],
                 w["final_norm_w"], tm)
    return y_out.reshape(b, l, d), ssm_new[None], conv_new[None], hg_new[None]


def kernel(x_prompt, x_sample, state_ssm, state_conv, state_hgrn, p_prompt, p_sample, norm_mix_w, w_in, conv_w, conv_b, dt_bias, a_log, d_skip, ssd_norm_w, lb_logits, hg_norm_w, w_out, norm_ffn_w, w_router_group, b_router_group, w_router_expert, b_router_expert, w_exp_gate, w_exp_up, w_exp_down, norm_ple_w, w_ple_gate, w_ple_proj, final_norm_w):
    assert p_prompt.shape[0] == 1, "the per-layer-embedding kernel also applies the final norm: depth 1 only"
    w = _prepare_weights(norm_mix_w, w_in, conv_w, conv_b, dt_bias, a_log, d_skip, ssd_norm_w,
                         lb_logits, hg_norm_w, w_out, norm_ffn_w, w_router_group, b_router_group,
                         w_router_expert, b_router_expert, w_exp_gate, w_exp_up, w_exp_down,
                         norm_ple_w, w_ple_gate, w_ple_proj, final_norm_w, layer=0)
    y_p, ssm_p, conv_p, hg_p = _trunk(x_prompt, p_prompt[0], None, None, None, w)
    y_s, ssm_s, conv_s, hg_s = _trunk(x_sample, p_sample[0], state_ssm[0], state_conv[0],
                                      state_hgrn[0], w)
    return (y_p, y_s, ssm_p, conv_p, hg_p, ssm_s, conv_s, hg_s)
```

```python
import functools

import jax
import jax.numpy as jnp
from jax import lax
from jax.experimental import pallas as pl
from jax.experimental.pallas import tpu as pltpu

F32 = jnp.float32
BF16 = jnp.bfloat16

EPS = 1e-6
SSD_HEADS = 16
SSD_HEADDIM = 64
SSD_D = SSD_HEADS * SSD_HEADDIM
SSD_GROUPS = 2
D_STATE = 128
CONV_W = 4
CONV_DIM = SSD_D + 2 * SSD_GROUPS * D_STATE
HG_HEADS = 8
HG_DK = 128
HG_DV = 128
HG_D = HG_HEADS * HG_DV
N_EGROUPS = 4
EXPERTS_PER_GROUP = 4
N_EXPERTS = N_EGROUPS * EXPERTS_PER_GROUP

LANES = 128
SUBLANES = 8
VMEM_LIMIT = 48 * 1024 * 1024
DMA_QUEUES = 2
CHUNK = 128
PROJ_ROWS = 256
HG_SAFE_LOG_DECAY = 60.0
ROUTER_ROWS = 32
CLASS_ROW = N_EXPERTS + N_EGROUPS
PAIRS_PER_GROUP = EXPERTS_PER_GROUP * (EXPERTS_PER_GROUP - 1) // 2
N_CLASSES = N_EGROUPS * PAIRS_PER_GROUP
MOE_TILE = 256
IN_PROJ_TILE = 256
ROW_TILE = 512
OUT_PROJ_ROWS = 256
LOG2E = 1.4426950408889634
SORTED_MOE_MIN_TOKENS = 8 * SUBLANES * LANES


def _dot(a, b):
    return jnp.dot(a, b, preferred_element_type=F32)


def _dot_nt(a, b):
    return lax.dot_general(a, b, (((1,), (1,)), ((), ())), preferred_element_type=F32)


def _dot_tn(a, b):
    return lax.dot_general(a, b, (((0,), (0,)), ((), ())), preferred_element_type=F32)


def _rms_unit(x):
    return x * lax.rsqrt(jnp.mean(x * x, axis=-1, keepdims=True) + EPS)


def _sigmoid(x):
    return 1.0 / (1.0 + jnp.exp(-x))


def _softplus(x):
    return jnp.maximum(x, 0.0) + jnp.log(1.0 + jnp.exp(-jnp.abs(x)))


def _silu(x):
    return x * _sigmoid(x)


def _split3(x):
    hi = x.astype(BF16)
    r = x - hi.astype(F32)
    mid = r.astype(BF16)
    lo = (r - mid.astype(F32)).astype(BF16)
    return hi, mid, lo


def _cumsum_rows(x, tri):
    hi, mid, lo = _split3(x)
    return _dot(tri, hi) + _dot(tri, mid) + _dot(tri, lo)


def _tri(n):
    r = lax.broadcasted_iota(jnp.int32, (n, n), 0)
    c = lax.broadcasted_iota(jnp.int32, (n, n), 1)
    return r >= c


def _cparams(*sem):
    return pltpu.CompilerParams(dimension_semantics=sem, vmem_limit_bytes=VMEM_LIMIT)


def _const_spec(shape):
    nd = len(shape)
    return pl.BlockSpec(shape, lambda *_: (0,) * nd, pipeline_mode=pl.Buffered(1))


def _inproj_kernel(x_ref, nw_ref, *refs):
    hb = (_rms_unit(x_ref[...]) * nw_ref[...]).astype(BF16)
    nseg = len(refs) // 2
    for w_ref, out_ref in zip(refs[:nseg], refs[nseg:]):
        n = out_ref.shape[-1]
        for c0 in range(0, n, 512):
            cw = min(512, n - c0)
            out_ref[:, c0:c0 + cw] = _dot(hb, w_ref[:, c0:c0 + cw])


def _in_proj(x, nw, ws, tm):
    n, d = x.shape
    widths = [w.shape[-1] for w in ws]
    return pl.pallas_call(
        _inproj_kernel,
        grid=(n // tm,),
        in_specs=[pl.BlockSpec((tm, d), lambda i: (i, 0)), _const_spec(nw.shape)]
                 + [_const_spec(w.shape) for w in ws],
        out_specs=[pl.BlockSpec((tm, s), lambda i: (i, 0)) for s in widths],
        out_shape=[jax.ShapeDtypeStruct((n, s), F32) for s in widths],
        compiler_params=_cparams("arbitrary"),
    )(x, nw, *ws)


def _expand(x, e_ref):
    hi, mid, lo = _split3(x)
    e = e_ref[...]
    return _dot(hi, e) + _dot(mid, e) + _dot(lo, e)


PROJ_WIDTHS = (SSD_D, CONV_DIM, HG_D, HG_D, HG_D, HG_D, LANES)
OFF_Z, OFF_XBC, OFF_Q, OFF_F, OFF_V, OFF_G, OFF_DT = (sum(PROJ_WIDTHS[:k]) for k in range(len(PROJ_WIDTHS)))
PROJ_COLS = sum(PROJ_WIDTHS)


def _project(x, nw_ref, w_refs, dst_ref):
    hb = (_rms_unit(x) * nw_ref[...]).astype(BF16)
    off = 0
    for w_ref in w_refs:
        n = w_ref.shape[-1]
        for c0 in range(0, n, 512):
            cw = min(512, n - c0)
            dst_ref[:, off + c0:off + c0 + cw] = _dot(hb, w_ref[:, c0:c0 + cw])
        off += n


def _ssd_chunk(p_ref, cw_ref, cb_ref, dtb_ref, alog_ref, dsk_ref, nw_ref, e1_ref, e2_ref,
               y_ref, cout_ref, xc_ref, st_ref):
    rows = p_ref.shape[0]
    hist = SUBLANES
    hpg = SSD_HEADS // SSD_GROUPS
    gw = hpg * SSD_HEADDIM
    gn = D_STATE

    xc_ref[hist:hist + rows, :] = p_ref[:, OFF_XBC:OFF_XBC + CONV_DIM]
    base = hist - (CONV_W - 1)
    acc = cb_ref[...] + cw_ref[0:1, :] * xc_ref[base:base + rows, :]
    for k in range(1, CONV_W):
        acc = acc + cw_ref[k:k + 1, :] * xc_ref[base + k:base + k + rows, :]
    cout_ref[...] = xc_ref[base + rows:base + rows + CONV_W - 1, :]
    xc_ref[0:hist, :] = xc_ref[rows:rows + hist, :]
    xbc = _silu(acc)
    xs = xbc[:, :SSD_D]
    b_bf = [xbc[:, SSD_D + g * gn:SSD_D + (g + 1) * gn].astype(BF16) for g in range(SSD_GROUPS)]
    c_bf = [xbc[:, SSD_D + (SSD_GROUPS + g) * gn:SSD_D + (SSD_GROUPS + g + 1) * gn].astype(BF16)
            for g in range(SSD_GROUPS)]

    dt = _softplus(p_ref[:, OFF_DT:OFF_DT + LANES] + dtb_ref[...])
    a = dt * (-jnp.exp(alog_ref[...]))
    tri = _tri(rows)
    a_cs = _cumsum_rows(a, tri.astype(BF16)) * LOG2E
    a_cs_t = a_cs.T
    dt_full = _expand(dt, e1_ref)
    a_full = _expand(a_cs, e1_ref)
    a_rep = _expand(a_cs, e2_ref)
    a_last = a_full[rows - 1:rows, :]

    xdt = xs * dt_full
    xdt_bf = xdt.astype(BF16)
    xw_bf = (xdt * jnp.exp2(a_last - a_full)).astype(BF16)
    skip = dsk_ref[...] * xs
    lane = lax.broadcasted_iota(jnp.int32, (rows, LANES), 1)
    first_half = lane < SSD_HEADDIM
    zg = _silu(p_ref[:, OFF_Z:OFF_Z + SSD_D])

    for g in range(SSD_GROUPS):
        gs = slice(g * gw, (g + 1) * gw)
        scores = jnp.where(tri, _dot_nt(c_bf[g], b_bf[g]), 0.0)
        st_old = st_ref[g]
        y_off = _dot(c_bf[g], st_old.astype(BF16)) * jnp.exp2(a_full[:, gs])
        st_ref[g] = st_old * jnp.exp2(a_last[:, gs]) + _dot_tn(b_bf[g], xw_bf[:, gs])
        parts = []
        for pair in range(hpg // 2):
            h0 = g * hpg + 2 * pair
            cols = slice(h0 * SSD_HEADDIM, (h0 + 2) * SSD_HEADDIM)
            x_pair = xdt_bf[:, cols]
            acc = None
            for hh, keep in ((h0, first_half), (h0 + 1, jnp.logical_not(first_half))):
                diff = a_rep[:, hh * LANES:hh * LANES + rows] - a_cs_t[hh:hh + 1, :]
                m = (scores * jnp.exp2(jnp.minimum(diff, 0.0))).astype(BF16)
                part = _dot(m, jnp.where(keep, x_pair, jnp.zeros_like(x_pair)))
                acc = part if acc is None else acc + part
            parts.append(acc)
        y = jnp.concatenate(parts, axis=1) + y_off + skip[:, gs]
        y_ref[:, gs] = (_rms_unit(y * zg[:, gs]) * nw_ref[:, gs]).astype(y_ref.dtype)


def _hgrn_chunk(p_ref, lb_ref, nw_ref, o_ref, st_ref, a_ref, kk_ref, gcs_ref):
    rows = p_ref.shape[0]
    dk = HG_DK
    lb = lb_ref[...]
    sig = _sigmoid(p_ref[:, OFF_F:OFF_F + HG_D])
    logf = jnp.log2(lb + (1.0 - lb) * sig)
    kk = (1.0 - lb) * (1.0 - sig)
    tri = _tri(rows)
    gcs = _cumsum_rows(logf, tri.astype(BF16))
    glast = gcs[rows - 1:rows, :]
    qt = p_ref[:, OFF_Q:OFF_Q + HG_D] * jnp.exp2(gcs)
    kk_ref[...] = kk
    gcs_ref[...] = gcs

    kt = kk * jnp.exp2(-gcs)
    khat = kk * jnp.exp2(glast - gcs)
    for h in range(HG_HEADS):
        sl = slice(h * dk, (h + 1) * dk)
        a_ref[h] = _dot_nt(qt[:, sl].astype(BF16), kt[:, sl].astype(BF16))

    @pl.when(jnp.min(glast) < -HG_SAFE_LOG_DECAY * LOG2E)
    def _():
        ri = lax.broadcasted_iota(jnp.int32, (rows, dk), 0)
        ci = lax.broadcasted_iota(jnp.int32, (rows, rows), 1)
        for h in range(HG_HEADS):
            sl = slice(h * dk, (h + 1) * dk)
            q_h = p_ref[:, OFF_Q + h * dk:OFF_Q + (h + 1) * dk]

            def cols(j8, a_h):
                base = pl.multiple_of(j8 * SUBLANES, SUBLANES)
                k_blk = kk_ref[pl.ds(base, SUBLANES), sl]
                g_blk = gcs_ref[pl.ds(base, SUBLANES), sl]
                g_h = gcs_ref[:, sl]
                for r in range(SUBLANES):
                    j = base + r
                    t = q_h * k_blk[r:r + 1] * jnp.exp2(
                        jnp.where(ri >= j, g_h - g_blk[r:r + 1], -jnp.inf))
                    a_h = jnp.where(ci == j, jnp.sum(t, axis=-1, keepdims=True), a_h)
                return a_h

            a_ref[h] = lax.fori_loop(0, rows // SUBLANES, cols, jnp.zeros((rows, rows), F32))

    gate = _silu(p_ref[:, OFF_G:OFF_G + HG_D])
    for h in range(HG_HEADS):
        sl = slice(h * dk, (h + 1) * dk)
        st = st_ref[h]
        v_bf = p_ref[:, OFF_V + h * HG_DV:OFF_V + (h + 1) * HG_DV].astype(BF16)
        a_h = jnp.where(tri, a_ref[h], 0.0).astype(BF16)
        o = _dot(a_h, v_bf) + _dot_nt(qt[:, sl].astype(BF16), st.astype(BF16))
        st_ref[h] = st * jnp.exp2(glast[:, sl]) + _dot_tn(v_bf, khat[:, sl].astype(BF16))
        o_ref[:, sl] = (_rms_unit(o) * nw_ref[:, sl] * gate[:, sl]).astype(o_ref.dtype)


def _mixers_kernel(x_ref, xn_ref, nw_ref, wz_ref, wxbc_ref, wq_ref, wf_ref, wv_ref, wg_ref, wdt_ref,
                   cw_ref, cb_ref, dtb_ref, alog_ref, dsk_ref, snw_ref, e1_ref, e2_ref, lb_ref, hnw_ref,
                   y_ref, o_ref, cout_ref, sout_ref, hout_ref,
                   pa_ref, pb_ref, xc_ref, st_ref, hst_ref, a_ref, kk_ref, gcs_ref, *, chunks):
    s = pl.program_id(0)
    blk = pa_ref.shape[0]
    per_blk = blk // CHUNK
    c0 = (s * 2 * per_blk) % chunks
    w_refs = (wz_ref, wxbc_ref, wq_ref, wf_ref, wv_ref, wg_ref, wdt_ref)
    hpg = SSD_HEADS // SSD_GROUPS

    @pl.when(s == 0)
    def _():
        _project(x_ref[0:blk, :], nw_ref, w_refs, pa_ref)

    @pl.when(c0 == 0)
    def _():
        xc_ref[0:SUBLANES, :] = jnp.zeros((SUBLANES, CONV_DIM), F32)
        st_ref[...] = jnp.zeros(st_ref.shape, F32)
        hst_ref[...] = jnp.zeros(hst_ref.shape, F32)

    for half, (cur_ref, nxt_ref) in enumerate(((pa_ref, pb_ref), (pb_ref, pa_ref))):
        x_next = x_ref[blk:2 * blk, :] if half == 0 else xn_ref[0:blk, :]
        _project(x_next, nw_ref, w_refs, nxt_ref)
        for k in range(per_blk):
            p_ref = cur_ref.at[k * CHUNK:(k + 1) * CHUNK]
            rs = slice((half * per_blk + k) * CHUNK, (half * per_blk + k + 1) * CHUNK)
            _ssd_chunk(p_ref, cw_ref, cb_ref, dtb_ref, alog_ref, dsk_ref, snw_ref, e1_ref, e2_ref,
                       y_ref.at[rs], cout_ref.at[0], xc_ref, st_ref)
            _hgrn_chunk(p_ref, lb_ref, hnw_ref, o_ref.at[rs], hst_ref, a_ref, kk_ref, gcs_ref)

    @pl.when(c0 + 2 * per_blk == chunks)
    def _():
        for g in range(SSD_GROUPS):
            sout_ref[0, g * hpg:(g + 1) * hpg] = st_ref[g].T.reshape(hpg, SSD_HEADDIM, D_STATE)
        for h in range(HG_HEADS):
            hout_ref[0, h] = hst_ref[h].T


def _mixers(x, b, l, w):
    n, d = x.shape
    chunks = l // CHUNK
    step_rows = 2 * PROJ_ROWS
    assert PROJ_ROWS % CHUNK == 0 and l % step_rows == 0
    steps = n // step_rows
    head = jnp.arange(LANES)[:, None]
    e1 = (head == jnp.arange(SSD_D)[None, :] // SSD_HEADDIM).astype(BF16)
    e2 = (head == jnp.arange(SSD_HEADS * LANES)[None, :] // LANES).astype(BF16)
    consts = (w["norm_mix_w"], *w["w_in"], w["conv_w"], w["conv_b"], w["dt_bias"], w["a_log"],
              w["d_skip"], w["ssd_norm_w"], e1, e2, w["lb"], w["hg_norm_w"])
    tok = lambda width: pl.BlockSpec((step_rows, width), lambda s: (s, 0))
    per_seq = lambda shape: pl.BlockSpec((1,) + shape,
                                         lambda s: (s * step_rows // l,) + (0,) * len(shape))
    ssm_shape = (SSD_HEADS, SSD_HEADDIM, D_STATE)
    hg_shape = (HG_HEADS, HG_DK, HG_DV)
    return pl.pallas_call(
        functools.partial(_mixers_kernel, chunks=chunks),
        grid=(steps,),
        in_specs=[tok(d), pl.BlockSpec((step_rows, d), lambda s: (jnp.minimum(s + 1, steps - 1), 0))]
                 + [_const_spec(c.shape) for c in consts],
        out_specs=[tok(SSD_D), tok(HG_D), per_seq((CONV_W - 1, CONV_DIM)), per_seq(ssm_shape),
                   per_seq(hg_shape)],
        out_shape=[jax.ShapeDtypeStruct((n, SSD_D), BF16), jax.ShapeDtypeStruct((n, HG_D), BF16),
                   jax.ShapeDtypeStruct((b, CONV_W - 1, CONV_DIM), F32),
                   jax.ShapeDtypeStruct((b,) + ssm_shape, F32),
                   jax.ShapeDtypeStruct((b,) + hg_shape, F32)],
        scratch_shapes=[pltpu.VMEM((PROJ_ROWS, PROJ_COLS), F32), pltpu.VMEM((PROJ_ROWS, PROJ_COLS), F32),
                        pltpu.VMEM((CHUNK + SUBLANES, CONV_DIM), F32),
                        pltpu.VMEM((SSD_GROUPS, D_STATE, SSD_D // SSD_GROUPS), F32),
                        pltpu.VMEM((HG_HEADS, HG_DV, HG_DK), F32),
                        pltpu.VMEM((HG_HEADS, CHUNK, CHUNK), F32),
                        pltpu.VMEM((CHUNK, HG_D), F32), pltpu.VMEM((CHUNK, HG_D), F32)],
        compiler_params=_cparams("arbitrary"),
    )(x, x, *consts)


def _tokens_to_lanes(x):
    tb, n = x.shape
    return jnp.concatenate([x, jnp.zeros((LANES - tb, n), F32)], axis=0).T


def _step_decay_kernel(dt_ref, dtb_ref, alog_ref, da_ref):
    da_ref[...] = jnp.exp(_softplus(dt_ref[...] + dtb_ref[...]) * (-jnp.exp(alog_ref[...])))


def _ssd_step_kernel(da_ref, z_ref, xbc_ref, dt_ref, cw_ref, cb_ref, dtb_ref, dsk_ref, nw_ref, e1_ref,
                     cs_ref, st_ref, y_ref, cnew_ref, snew_ref, *, tb):
    cd = CONV_DIM
    gn = D_STATE
    hpg = SSD_HEADS // SSD_GROUPS
    gw = hpg * SSD_HEADDIM
    first = pl.program_id(0) * tb
    x_in = xbc_ref[...]
    acc = cb_ref[...] + cw_ref[CONV_W - 1:CONV_W, :] * x_in
    for k in range(CONV_W - 1):
        acc = acc + cw_ref[k:k + 1, :] * cs_ref[:, k * cd:(k + 1) * cd]
    cnew_ref[:, :(CONV_W - 2) * cd] = cs_ref[:, cd:]
    cnew_ref[:, (CONV_W - 2) * cd:] = x_in
    xbc = _silu(acc)
    xs = xbc[:, :SSD_D]
    dt = _softplus(dt_ref[...] + dtb_ref[...])
    xdt_t = _tokens_to_lanes(xs * _expand(dt, e1_ref)).astype(BF16)
    row_tok = lax.broadcasted_iota(jnp.int32, (LANES, tb * gn), 0)
    col_tok = lax.broadcasted_iota(jnp.int32, (LANES, tb * gn), 1) // gn
    lane = lax.broadcasted_iota(jnp.int32, (SSD_HEADDIM, LANES), 1)
    y_groups = []
    for g in range(SSD_GROUPS):
        b_g = xbc[:, SSD_D + g * gn:SSD_D + (g + 1) * gn]
        c_g = xbc[:, SSD_D + (SSD_GROUPS + g) * gn:SSD_D + (SSD_GROUPS + g + 1) * gn]
        b_wide = jnp.concatenate([jnp.tile(b_g, (1, tb)), jnp.zeros((LANES - tb, tb * gn), F32)], axis=0)
        b_diag = jnp.where(row_tok == col_tok, b_wide, 0.0).astype(BF16)
        upd = _dot(xdt_t[g * gw:(g + 1) * gw, :], b_diag)
        y_heads = []
        for hl in range(hpg):
            h = g * hpg + hl
            y_h = jnp.zeros((SSD_HEADDIM, LANES), F32)
            for j in range(tb):
                new = (st_ref[j, h] * da_ref[first + j, h]
                       + upd[hl * SSD_HEADDIM:(hl + 1) * SSD_HEADDIM, j * gn:(j + 1) * gn])
                snew_ref[j, h] = new
                y_h = jnp.where(lane == j, jnp.sum(new * c_g[j:j + 1, :], axis=-1, keepdims=True), y_h)
            y_heads.append(y_h)
        y_groups.append(jnp.concatenate(y_heads, axis=0))
    y = jnp.concatenate(y_groups, axis=0).T[:tb]
    y = (y + dsk_ref[...] * xs) * _silu(z_ref[...])
    for g in range(SSD_GROUPS):
        gs = slice(g * gw, (g + 1) * gw)
        y_ref[:, gs] = (_rms_unit(y[:, gs]) * nw_ref[:, gs]).astype(y_ref.dtype)


def _ssd_step(z, xbc, dt, conv_w, conv_b, dt_bias, a_log, d_skip, norm_w, conv_init, ssm_init):
    b = z.shape[0]
    tb = SUBLANES
    assert b % tb == 0
    da = pl.pallas_call(
        _step_decay_kernel,
        grid=(1,),
        in_specs=[_const_spec(dt.shape), _const_spec(dt_bias.shape), _const_spec(a_log.shape)],
        out_specs=_const_spec(dt.shape),
        out_shape=jax.ShapeDtypeStruct(dt.shape, F32),
    )(dt, dt_bias, a_log)
    head = jnp.arange(LANES)[:, None]
    e1 = (head == jnp.arange(SSD_D)[None, :] // SSD_HEADDIM).astype(BF16)
    params = (conv_w, conv_b, dt_bias, d_skip, norm_w, e1)
    hist = (CONV_W - 1) * CONV_DIM
    row = lambda w: pl.BlockSpec((tb, w), lambda i, da: (i, 0))
    const = lambda shape: pl.BlockSpec(shape, lambda i, da: (0,) * len(shape))
    state_shape = (SSD_HEADS, SSD_HEADDIM, D_STATE)
    st_spec = pl.BlockSpec((tb,) + state_shape, lambda i, da: (i, 0, 0, 0))
    y, conv_new, ssm_new = pl.pallas_call(
        functools.partial(_ssd_step_kernel, tb=tb),
        grid_spec=pltpu.PrefetchScalarGridSpec(
            num_scalar_prefetch=1,
            grid=(b // tb,),
            in_specs=[row(SSD_D), row(CONV_DIM), row(LANES)] + [const(p.shape) for p in params]
                     + [row(hist), st_spec],
            out_specs=[row(SSD_D), row(hist), st_spec]),
        out_shape=[jax.ShapeDtypeStruct((b, SSD_D), BF16), jax.ShapeDtypeStruct((b, hist), F32),
                   jax.ShapeDtypeStruct((b,) + state_shape, F32)],
        compiler_params=_cparams("arbitrary"),
    )(da, z, xbc, dt, *params, conv_init.reshape(b, hist), ssm_init)
    return y, conv_new.reshape(b, CONV_W - 1, CONV_DIM), ssm_new


def _hgrn_step_kernel(q_ref, f_ref, v_ref, g_ref, lb_ref, nw_ref, st_ref, o_ref, snew_ref, *, tb):
    dk, dv = HG_DK, HG_DV
    fr = f_ref[...]
    lb = lb_ref[...]
    sig = _sigmoid(fr)
    f_t = _tokens_to_lanes(lb + (1.0 - lb) * sig)
    k_t = _tokens_to_lanes((1.0 - lb) * (1.0 - sig))
    q_t = _tokens_to_lanes(q_ref[...])
    v = v_ref[...]
    gate = _silu(g_ref[...])
    sub = lax.broadcasted_iota(jnp.int32, (tb, dv), 0)
    for h in range(HG_HEADS):
        rs = slice(h * dk, (h + 1) * dk)
        vs = slice(h * dv, (h + 1) * dv)
        o_h = jnp.zeros((tb, dv), F32)
        for j in range(tb):
            fcol = jnp.broadcast_to(f_t[rs, j:j + 1], (dk, dv))
            kcol = jnp.broadcast_to(k_t[rs, j:j + 1], (dk, dv))
            qcol = jnp.broadcast_to(q_t[rs, j:j + 1], (dk, dv))
            new = st_ref[j, h] * fcol + kcol * v[j:j + 1, vs]
            snew_ref[j, h] = new
            o_h = jnp.where(sub == j, jnp.sum(new * qcol, axis=0, keepdims=True), o_h)
        o_ref[:, vs] = (_rms_unit(o_h) * nw_ref[:, vs] * gate[:, vs]).astype(o_ref.dtype)


def _hgrn_step(q, f, v, g, lb, norm_w, init):
    b = q.shape[0]
    tb = SUBLANES
    assert b % tb == 0
    row = pl.BlockSpec((tb, HG_D), lambda i: (i, 0))
    state_shape = (HG_HEADS, HG_DK, HG_DV)
    st_spec = pl.BlockSpec((tb,) + state_shape, lambda i: (i, 0, 0, 0))
    return pl.pallas_call(
        functools.partial(_hgrn_step_kernel, tb=tb),
        grid=(b // tb,),
        in_specs=[row, row, row, row, _const_spec(lb.shape), _const_spec(norm_w.shape), st_spec],
        out_specs=[row, st_spec],
        out_shape=[jax.ShapeDtypeStruct((b, HG_D), BF16),
                   jax.ShapeDtypeStruct((b,) + state_shape, F32)],
        compiler_params=_cparams("arbitrary"),
    )(q, f, v, g, lb, norm_w, init)


def _route_t(logits, bias, group=None):
    row = lax.broadcasted_iota(jnp.int32, logits.shape, 0)
    ninf = -jnp.inf
    big = jnp.int32(ROUTER_ROWS)
    is_g = (row >= N_EXPERTS) & (row < N_EXPERTS + N_EGROUPS)
    cmax = lambda t: jnp.max(t, axis=0, keepdims=True)
    csum = lambda t: jnp.sum(t, axis=0, keepdims=True)
    first = lambda m: jnp.min(jnp.where(m, row, big), axis=0, keepdims=True)

    gl = jnp.where(is_g, logits, ninf)
    gp = jnp.exp(gl - cmax(gl))
    gprob = gp / csum(gp)
    biased = logits + bias
    if group is None:
        gb = jnp.where(is_g, biased, ninf)
        gsel = first(gb == cmax(gb))
    else:
        gsel = group + N_EXPERTS
    gw = csum(jnp.where(row == gsel, gprob, 0.0))
    e0 = (gsel - N_EXPERTS) * EXPERTS_PER_GROUP
    in_grp = (row >= e0) & (row < e0 + EXPERTS_PER_GROUP)
    el = jnp.where(in_grp, logits, ninf)
    ep = jnp.exp(el - cmax(el))
    eprob = ep / csum(ep)
    eb = jnp.where(in_grp, biased, ninf)
    i1 = first(eb == cmax(eb))
    eb2 = jnp.where(row == i1, ninf, eb)
    i2 = first(eb2 == cmax(eb2))
    p1 = csum(jnp.where(row == i1, eprob, 0.0))
    p2 = csum(jnp.where(row == i2, eprob, 0.0))
    den = p1 + p2
    a = (jnp.minimum(i1, i2) - e0).astype(F32)
    b = (jnp.maximum(i1, i2) - e0).astype(F32)
    group = jnp.broadcast_to(gsel - N_EXPERTS, (1, logits.shape[1])).astype(F32)
    cls = group * PAIRS_PER_GROUP + a * (7.0 - a) * 0.5 + (b - a - 1.0)
    return (jnp.where(row == i1, gw * (p1 / den), 0.0)
            + jnp.where(row == i2, gw * (p2 / den), 0.0)
            + jnp.where(row == CLASS_ROW, cls, 0.0))


def _router_logits_t(h, rhi_ref, rlo_ref):
    h_hi = h.astype(BF16)
    h_lo = (h - h_hi.astype(F32)).astype(BF16)
    logits = _dot(h_hi, rhi_ref[...]) + _dot(h_hi, rlo_ref[...]) + _dot(h_lo, rhi_ref[...])
    return logits.T[:ROUTER_ROWS]


def _rows_to_lanes(gates_t):
    r, tm = gates_t.shape
    return jnp.concatenate([gates_t, jnp.zeros((LANES - r, tm), F32)], axis=0).T


def _outproj_kernel(x_ref, y_ref, o_ref, w_ref, nw_ref, rhi_ref, rlo_ref, rb_ref,
                    x1_ref, h_ref, r_ref, *, tiled_rows):
    dy = y_ref.shape[-1]
    tm, d = x_ref.shape
    rb = min(tm, OUT_PROJ_ROWS)
    for r0 in range(0, tm, rb):
        rs = slice(r0, r0 + rb)
        mix = _dot(y_ref[rs, :], w_ref[:dy, :]) + _dot(o_ref[rs, :], w_ref[dy:, :])
        x1 = x_ref[rs, :] + mix
        x1_ref[rs, :] = x1
        h = _rms_unit(x1) * nw_ref[...]
        gates_t = _route_t(_router_logits_t(h, rhi_ref, rlo_ref), rb_ref[...])
        if tiled_rows:
            h_ref[rs] = h.reshape(rb, d // LANES, LANES)
            r_ref[:, rs] = jnp.broadcast_to(gates_t[CLASS_ROW:CLASS_ROW + 1, :], (SUBLANES, rb))
        else:
            h_ref[rs, :] = h
            r_ref[rs, :] = _rows_to_lanes(gates_t)


def _out_proj(x, y, o, w, nw, r_hi, r_lo, r_bias, tm, tiled_rows):
    n, d = x.shape
    row = lambda width: pl.BlockSpec((tm, width), lambda i: (i, 0))
    consts = (w, nw, r_hi, r_lo, r_bias)
    if tiled_rows:
        h_spec = pl.BlockSpec((tm, d // LANES, LANES), lambda i: (i, 0, 0))
        h_shape = jax.ShapeDtypeStruct((n, d // LANES, LANES), F32)
        r_spec = pl.BlockSpec((SUBLANES, tm), lambda i: (i, 0))
        r_shape = jax.ShapeDtypeStruct((n // tm * SUBLANES, tm), F32)
    else:
        h_spec, h_shape = row(d), jax.ShapeDtypeStruct((n, d), F32)
        r_spec, r_shape = row(LANES), jax.ShapeDtypeStruct((n, LANES), F32)
    return pl.pallas_call(
        functools.partial(_outproj_kernel, tiled_rows=tiled_rows),
        grid=(n // tm,),
        in_specs=[row(d), row(y.shape[-1]), row(o.shape[-1])] + [_const_spec(c.shape) for c in consts],
        out_specs=[row(d), h_spec, r_spec],
        out_shape=[jax.ShapeDtypeStruct((n, d), F32), h_shape, r_shape],
        compiler_params=_cparams("arbitrary"),
    )(x, y, o, *consts)


def _pos_kernel(cls_ref, pos_ref, tab_ref, *, tile):
    rows, tm = cls_ref.shape
    cls = cls_ref[...]
    ri = lax.broadcasted_iota(jnp.int32, (tm, tm), 0)
    ci = lax.broadcasted_iota(jnp.int32, (tm, tm), 1)
    upper = (ri <= ci).astype(BF16)
    rr = lax.broadcasted_iota(jnp.int32, (rows, rows), 0)
    rc = lax.broadcasted_iota(jnp.int32, (rows, rows), 1)
    earlier = ((rc // SUBLANES < rr // SUBLANES) & (rc % SUBLANES == 0)).astype(BF16)
    tile_lane = lax.broadcasted_iota(jnp.int32, (1, LANES), 1).astype(F32)
    pos = jnp.zeros((rows, tm), F32)
    tiles_before = jnp.zeros((1, 1), F32)
    tile_class = jnp.zeros((1, LANES), F32)
    for c in range(N_CLASSES):
        onehot = jnp.where(cls == float(c), 1.0, 0.0)
        inc = _dot(onehot.astype(BF16), upper)
        rowtot = jnp.broadcast_to(inc[:, tm - 1:tm], (rows, tm))
        rowpre = _cumsum_rows(rowtot, earlier)
        cnt = rowpre[rows - 1:rows, 0:1] + rowtot[rows - 1:rows, 0:1]
        pos = pos + onehot * (tiles_before * tile + rowpre + inc - 1.0)
        tiles_before = tiles_before + jnp.floor((cnt + (tile - 1.0)) * (1.0 / tile))
        tile_class = tile_class + jnp.where(tile_lane >= tiles_before, 1.0, 0.0)
    pos_ref[...] = pos.astype(jnp.int32)
    used = tile_class < N_CLASSES
    group = jnp.floor(tile_class * (1.0 / PAIRS_PER_GROUP))
    pair = tile_class - group * PAIRS_PER_GROUP
    ge3 = jnp.where(pair >= 3.0, 1.0, 0.0)
    ge5 = jnp.where(pair >= 5.0, 1.0, 0.0)
    a = ge3 + ge5
    b = pair + 1.0 - 2.0 * ge3 - ge5
    e1 = jnp.where(used, group * EXPERTS_PER_GROUP + a, float(N_EXPERTS))
    e2 = jnp.where(used, group * EXPERTS_PER_GROUP + b, float(N_EXPERTS))
    sub = lax.broadcasted_iota(jnp.int32, (SUBLANES, LANES), 0)
    tab_ref[...] = jnp.where(sub == 0, e1, jnp.where(sub == 1, e2, 0.0)).astype(jnp.int32)


def _positions(cls, n, tile):
    rows, tm = cls.shape
    assert n // tile + N_CLASSES <= LANES
    pos, tab = pl.pallas_call(
        functools.partial(_pos_kernel, tile=tile),
        grid=(1,),
        in_specs=[_const_spec(cls.shape)],
        out_specs=[_const_spec(cls.shape), _const_spec((SUBLANES, LANES))],
        out_shape=[jax.ShapeDtypeStruct(cls.shape, jnp.int32),
                   jax.ShapeDtypeStruct((SUBLANES, LANES), jnp.int32)],
        compiler_params=_cparams("arbitrary"),
    )(cls)
    return pos.reshape(rows // SUBLANES, SUBLANES, tm)[:, 0, :].reshape(n), tab[0], tab[1]


def _dispatch_kernel(pos_ref, h_ref, hs_in_ref, hs_ref, buf_ref, sem_ref, *, td):
    del hs_in_ref
    i = pl.program_id(0)
    nsteps = pl.num_programs(0)
    slot = i % 2

    def row_copy(step, s, r):
        return pltpu.make_async_copy(buf_ref.at[s, r], hs_ref.at[pos_ref[step * td + r]], sem_ref.at[s])

    def start_all(step, s):
        def body(k, carry):
            for u in range(DMA_QUEUES):
                row_copy(step, s, DMA_QUEUES * k + u).start(priority=u)
            return carry
        lax.fori_loop(0, td // DMA_QUEUES, body, 0, unroll=4)

    def wait_all(step, s):
        def body(r, carry):
            row_copy(step, s, r).wait()
            return carry
        lax.fori_loop(0, td, body, 0, unroll=8)

    @pl.when(i >= 2)
    def _():
        wait_all(i - 2, slot)

    buf_ref[slot] = h_ref[...]
    start_all(i, slot)

    @pl.when(i == nsteps - 1)
    def _():
        @pl.when(i >= 1)
        def _():
            wait_all(i - 1, 1 - slot)
        wait_all(i, slot)


def _dispatch(h3, pos, n_sorted, td):
    n = h3.shape[0]
    tok = h3.shape[1:]
    return pl.pallas_call(
        functools.partial(_dispatch_kernel, td=td),
        grid_spec=pltpu.PrefetchScalarGridSpec(
            num_scalar_prefetch=1,
            grid=(n // td,),
            in_specs=[pl.BlockSpec((td,) + tok, lambda i, pos: (i, 0, 0)),
                      pl.BlockSpec(memory_space=pl.ANY)],
            out_specs=pl.BlockSpec(memory_space=pl.ANY),
            scratch_shapes=[pltpu.VMEM((2, td) + tok, F32), pltpu.SemaphoreType.DMA((2,))]),
        out_shape=jax.ShapeDtypeStruct((n_sorted,) + tok, F32),
        input_output_aliases={2: 0},
        compiler_params=_cparams("arbitrary"),
    )(pos, h3, jnp.zeros((n_sorted,) + tok, F32))


def _expert(h, gates, e, wg, wu, wd):
    lane = lax.broadcasted_iota(jnp.int32, gates.shape, 1)
    gcol = jnp.sum(jnp.where(lane == e, gates, 0.0), axis=-1, keepdims=True)
    act = _silu(_dot(h, wg)) * _dot(h, wu) * gcol
    return _dot(act.astype(BF16), wd)


def _moe_dense_kernel(h_ref, gates_ref, wg_ref, wu_ref, wd_ref, out_ref):
    e = pl.program_id(1)
    part = _expert(h_ref[...].astype(BF16), gates_ref[...], e, wg_ref[0], wu_ref[0], wd_ref[0])

    @pl.when(e == 0)
    def _():
        out_ref[...] = part

    @pl.when(e > 0)
    def _():
        out_ref[...] += part


def _moe_dense(h, gates, wg, wu, wd, tm):
    n, d = h.shape
    ne, _, ff = wg.shape
    row = lambda width: pl.BlockSpec((tm, width), lambda i, e: (i, 0))
    return pl.pallas_call(
        _moe_dense_kernel,
        grid=(n // tm, ne),
        in_specs=[row(d), row(LANES),
                  pl.BlockSpec((1, d, ff), lambda i, e: (e, 0, 0)),
                  pl.BlockSpec((1, d, ff), lambda i, e: (e, 0, 0)),
                  pl.BlockSpec((1, ff, d), lambda i, e: (e, 0, 0))],
        out_specs=row(d),
        out_shape=jax.ShapeDtypeStruct((n, d), F32),
        compiler_params=_cparams("arbitrary", "arbitrary"),
    )(h, gates, wg, wu, wd)


def _moe_sorted_kernel(e1_ref, e2_ref, hs_ref, rhi_ref, rlo_ref, rb_ref,
                       wg1_ref, wu1_ref, wd1_ref, wg2_ref, wu2_ref, wd2_ref, ys_ref):
    e1 = e1_ref[pl.program_id(0)]
    e2 = e2_ref[pl.program_id(0)]
    tm = hs_ref.shape[0]
    d = wg1_ref.shape[1]

    @pl.when(e1 < N_EXPERTS)
    def _():
        h = hs_ref[...].reshape(tm, d)
        logits_t = _router_logits_t(h, rhi_ref, rlo_ref)
        group = lax.shift_right_logical(e1, jnp.int32(EXPERTS_PER_GROUP.bit_length() - 1))
        gates = _rows_to_lanes(_route_t(logits_t, rb_ref[...], group=group))
        h_bf = h.astype(BF16)
        acc = (_expert(h_bf, gates, e1, wg1_ref[0], wu1_ref[0], wd1_ref[0])
               + _expert(h_bf, gates, e2, wg2_ref[0], wu2_ref[0], wd2_ref[0]))
        ys_ref[...] = acc.reshape(ys_ref.shape)

    @pl.when(e1 >= N_EXPERTS)
    def _():
        ys_ref[...] = jnp.zeros(ys_ref.shape, F32)


def _moe_sorted(hs, e1_tab, e2_tab, r_hi, r_lo, r_bias, wg, wu, wd, tm):
    n_sorted = hs.shape[0]
    ne, d, ff = wg.shape
    first = lambda i, e1, e2: (jnp.minimum(e1[i], ne - 1), 0, 0)
    second = lambda i, e1, e2: (jnp.minimum(e2[i], ne - 1), 0, 0)
    tok = pl.BlockSpec((tm,) + hs.shape[1:], lambda i, e1, e2: (i, 0, 0))
    const = lambda shape: pl.BlockSpec(shape, lambda i, e1, e2: (0,) * len(shape))
    up = lambda sel: pl.BlockSpec((1, d, ff), sel)
    down = lambda sel: pl.BlockSpec((1, ff, d), sel)
    return pl.pallas_call(
        _moe_sorted_kernel,
        grid_spec=pltpu.PrefetchScalarGridSpec(
            num_scalar_prefetch=2,
            grid=(n_sorted // tm,),
            in_specs=[tok, const(r_hi.shape), const(r_lo.shape), const(r_bias.shape),
                      up(first), up(first), down(first), up(second), up(second), down(second)],
            out_specs=tok),
        out_shape=jax.ShapeDtypeStruct(hs.shape, F32),
        compiler_params=_cparams("arbitrary"),
    )(e1_tab, e2_tab, hs, r_hi, r_lo, r_bias, wg, wu, wd, wg, wu, wd)


def _ple_math(x, p, npw_ref, wg_ref, wp_ref, fw_ref):
    hn = (_rms_unit(x) * npw_ref[...]).astype(BF16)
    gate = _sigmoid(_dot(hn, wg_ref[...]))
    x = x + gate * _dot(p.astype(BF16), wp_ref[...])
    return _rms_unit(x) * fw_ref[...]


def _ple_kernel(x1_ref, moe_ref, p_ref, npw_ref, wg_ref, wp_ref, fw_ref, y_ref):
    y_ref[...] = _ple_math(x1_ref[...] + moe_ref[...], p_ref[...], npw_ref, wg_ref, wp_ref, fw_ref)


def _ple_gather_kernel(pos_ref, x1_ref, ys_ref, p_ref, npw_ref, wg_ref, wp_ref, fw_ref, y_ref,
                       buf_ref, sem_ref, *, tm):
    i = pl.program_id(0)
    nsteps = pl.num_programs(0)
    slot = i % 2

    def row_copy(step, s, r):
        return pltpu.make_async_copy(ys_ref.at[pos_ref[step * tm + r]], buf_ref.at[s, r], sem_ref.at[s])

    def start_all(step, s):
        def body(k, carry):
            for u in range(DMA_QUEUES):
                row_copy(step, s, DMA_QUEUES * k + u).start(priority=u)
            return carry
        lax.fori_loop(0, tm // DMA_QUEUES, body, 0, unroll=4)

    @pl.when(i == 0)
    def _():
        start_all(0, 0)

    @pl.when(i + 1 < nsteps)
    def _():
        start_all(i + 1, 1 - slot)

    def wait_row(r, carry):
        row_copy(i, slot, r).wait()
        return carry
    lax.fori_loop(0, tm, wait_row, 0, unroll=8)

    moe = buf_ref[slot].reshape(x1_ref.shape)
    y_ref[...] = _ple_math(x1_ref[...] + moe, p_ref[...], npw_ref, wg_ref, wp_ref, fw_ref)


def _ple(x1, moe, pos, p, npw, wg, wp, fw, tm):
    n, d = x1.shape
    consts = (npw, wg, wp, fw)
    out_shape = jax.ShapeDtypeStruct((n, d), F32)
    if pos is None:
        row = lambda width: pl.BlockSpec((tm, width), lambda i: (i, 0))
        return pl.pallas_call(
            _ple_kernel,
            grid=(n // tm,),
            in_specs=[row(d), row(d), row(p.shape[-1])] + [_const_spec(c.shape) for c in consts],
            out_specs=row(d),
            out_shape=out_shape,
            compiler_params=_cparams("arbitrary"),
        )(x1, moe, p, *consts)
    row = lambda width: pl.BlockSpec((tm, width), lambda i, pos: (i, 0))
    const = lambda shape: pl.BlockSpec(shape, lambda i, pos: (0,) * len(shape))
    return pl.pallas_call(
        functools.partial(_ple_gather_kernel, tm=tm),
        grid_spec=pltpu.PrefetchScalarGridSpec(
            num_scalar_prefetch=1,
            grid=(n // tm,),
            in_specs=[row(d), pl.BlockSpec(memory_space=pl.ANY), row(p.shape[-1])]
                     + [const(c.shape) for c in consts],
            out_specs=row(d),
            scratch_shapes=[pltpu.VMEM((2, tm) + moe.shape[1:], F32), pltpu.SemaphoreType.DMA((2,))]),
        out_shape=out_shape,
        compiler_params=_cparams("arbitrary"),
    )(pos, x1, moe, p, *consts)


def _prepare_weights(norm_mix_w, w_in, conv_w, conv_b, dt_bias, a_log, d_skip, ssd_norm_w,
                     lb_logits, hg_norm_w, w_out, norm_ffn_w, w_router_group, b_router_group,
                     w_router_expert, b_router_expert, w_exp_gate, w_exp_up, w_exp_down,
                     norm_ple_w, w_ple_gate, w_ple_proj, final_norm_w, layer):
    i = layer
    row = lambda t: t.reshape(1, -1).astype(F32)
    lane_pad = lambda t: jnp.pad(t, [(0, 0)] * (t.ndim - 1) + [(0, LANES - t.shape[-1])])
    splits = (SSD_D, CONV_DIM, SSD_HEADS, HG_D, HG_D, HG_D, HG_D)
    offs = [0]
    for s in splits:
        offs.append(offs[-1] + s)
    seg = lambda k: w_in[i][:, offs[k]:offs[k + 1]].astype(BF16)
    w_in_segs = (seg(0), seg(1), seg(3), seg(4), seg(5), seg(6), lane_pad(seg(2)))
    w_router = lane_pad(jnp.concatenate([w_router_expert[i], w_router_group[i]], axis=1).astype(F32))
    r_hi = w_router.astype(BF16)
    r_lo = (w_router - r_hi.astype(F32)).astype(BF16)
    r_bias = jnp.pad(jnp.concatenate([b_router_expert[i].reshape(-1), b_router_group[i]]).astype(F32),
                     [(0, ROUTER_ROWS - N_EXPERTS - N_EGROUPS)]).reshape(ROUTER_ROWS, 1)
    lb = jnp.cumsum(jax.nn.softmax(lb_logits.astype(F32), axis=0), axis=0)[i]
    return dict(
        norm_mix_w=row(norm_mix_w[i]), w_in=w_in_segs, conv_w=conv_w[i].astype(F32),
        conv_b=row(conv_b[i]), dt_bias=lane_pad(row(dt_bias[i])), a_log=lane_pad(row(a_log[i])),
        d_skip=row(jnp.repeat(d_skip[i], SSD_HEADDIM)), ssd_norm_w=row(ssd_norm_w[i]),
        lb=row(lb), hg_norm_w=row(hg_norm_w[i]), w_out=w_out[i].astype(BF16),
        norm_ffn_w=row(norm_ffn_w[i]), r_hi=r_hi, r_lo=r_lo, r_bias=r_bias,
        w_exp_gate=w_exp_gate[i].astype(BF16), w_exp_up=w_exp_up[i].astype(BF16),
        w_exp_down=w_exp_down[i].astype(BF16), norm_ple_w=row(norm_ple_w[i]),
        w_ple_gate=w_ple_gate[i].astype(BF16), w_ple_proj=w_ple_proj[i].astype(BF16),
        final_norm_w=row(final_norm_w))


def _token_tile(n, cap):
    tm = cap
    while tm >= SUBLANES:
        if n % tm == 0:
            return tm
        tm //= 2
    raise ValueError(f"token count {n} must be a multiple of {SUBLANES}")


def _trunk(x, p, ssm_in, conv_in, hg_in, w):
    b, l, d = x.shape
    n = b * l
    tm = _token_tile(n, ROW_TILE)
    x2d = x.reshape(n, d)
    if l == 1 and ssm_in is not None:
        z, xbc, q, f, v, g, dt = _in_proj(x2d, w["norm_mix_w"], w["w_in"], _token_tile(n, IN_PROJ_TILE))
        y, conv_new, ssm_new = _ssd_step(z, xbc, dt, w["conv_w"], w["conv_b"], w["dt_bias"], w["a_log"],
                                         w["d_skip"], w["ssd_norm_w"], conv_in, ssm_in)
        o, hg_new = _hgrn_step(q, f, v, g, w["lb"], w["hg_norm_w"], hg_in)
    else:
        assert ssm_in is None and conv_in is None and hg_in is None, "multi-token groups start empty"
        y, o, conv_new, ssm_new, hg_new = _mixers(x2d, b, l, w)
    router = (w["r_hi"], w["r_lo"], w["r_bias"])
    experts = (w["w_exp_gate"], w["w_exp_up"], w["w_exp_down"])
    sorted_moe = n >= SORTED_MOE_MIN_TOKENS
    x1, h, routed = _out_proj(x2d, y.reshape(n, -1), o.reshape(n, -1), w["w_out"], w["norm_ffn_w"],
                              *router, tm, tiled_rows=sorted_moe)
    if sorted_moe:
        pos, e1_tab, e2_tab = _positions(routed, n, MOE_TILE)
        hs = _dispatch(h, pos, n + N_CLASSES * MOE_TILE, tm)
        moe = _moe_sorted(hs, e1_tab, e2_tab, *router, *experts, MOE_TILE)
    else:
        pos = None
        moe = _moe_dense(h, routed, *experts, tm)
    y_out = _ple(x1, moe, pos, p.reshape(n, -1), w["norm_ple_w"], w["w_ple_gate"], w["w_ple_proj"],
                 w["final_norm_w"], tm)
    return y_out.reshape(b, l, d), ssm_new[None], conv_new[None], hg_new[None]


def kernel(x_prompt, x_sample, state_ssm, state_conv, state_hgrn, p_prompt, p_sample, norm_mix_w, w_in, conv_w, conv_b, dt_bias, a_log, d_skip, ssd_norm_w, lb_logits, hg_norm_w, w_out, norm_ffn_w, w_router_group, b_router_group, w_router_expert, b_router_expert, w_exp_gate, w_exp_up, w_exp_down, norm_ple_w, w_ple_gate, w_ple_proj, final_norm_w):
    assert p_prompt.shape[0] == 1, "the per-layer-embedding kernel also applies the final norm: depth 1 only"
    w = _prepare_weights(norm_mix_w, w_in, conv_w, conv_b, dt_bias, a_log, d_skip, ssd_norm_w,
                         lb_logits, hg_norm_w, w_out, norm_ffn_w, w_router_group, b_router_group,
                         w_router_expert, b_router_expert, w_exp_gate, w_exp_up, w_exp_down,
                         norm_ple_w, w_ple_gate, w_ple_proj, final_norm_w, layer=0)
    y_p, ssm_p, conv_p, hg_p = _trunk(x_prompt, p_prompt[0], None, None, None, w)
    y_s, ssm_s, conv_s, hg_s = _trunk(x_sample, p_sample[0], state_ssm[0], state_conv[0],
                                      state_hgrn[0], w)
    return (y_p, y_s, ssm_p, conv_p, hg_p, ssm_s, conv_s, hg_s)
```

```python
import functools

import jax
import jax.numpy as jnp
from jax import lax
from jax.experimental import pallas as pl
from jax.experimental.pallas import tpu as pltpu

F32 = jnp.float32
BF16 = jnp.bfloat16

EPS = 1e-6
SSD_HEADS = 16
SSD_HEADDIM = 64
SSD_D = SSD_HEADS * SSD_HEADDIM
SSD_GROUPS = 2
D_STATE = 128
CONV_W = 4
CONV_DIM = SSD_D + 2 * SSD_GROUPS * D_STATE
HG_HEADS = 8
HG_DK = 128
HG_DV = 128
HG_D = HG_HEADS * HG_DV
N_EGROUPS = 4
EXPERTS_PER_GROUP = 4
N_EXPERTS = N_EGROUPS * EXPERTS_PER_GROUP

LANES = 128
SUBLANES = 8
VMEM_LIMIT = 48 * 1024 * 1024
DMA_QUEUES = 2
CHUNK = 128
PROJ_ROWS = 256
HG_SAFE_LOG_DECAY = 60.0
ROUTER_ROWS = 32
CLASS_ROW = N_EXPERTS + N_EGROUPS
PAIRS_PER_GROUP = EXPERTS_PER_GROUP * (EXPERTS_PER_GROUP - 1) // 2
N_CLASSES = N_EGROUPS * PAIRS_PER_GROUP
MOE_TILE = 256
IN_PROJ_TILE = 256
ROW_TILE = 512
OUT_PROJ_ROWS = 256
LOG2E = 1.4426950408889634
SORTED_MOE_MIN_TOKENS = 8 * SUBLANES * LANES


def _dot(a, b):
    return jnp.dot(a, b, preferred_element_type=F32)


def _dot_nt(a, b):
    return lax.dot_general(a, b, (((1,), (1,)), ((), ())), preferred_element_type=F32)


def _dot_tn(a, b):
    return lax.dot_general(a, b, (((0,), (0,)), ((), ())), preferred_element_type=F32)


def _rms_unit(x):
    return x * lax.rsqrt(jnp.mean(x * x, axis=-1, keepdims=True) + EPS)


def _sigmoid(x):
    return 1.0 / (1.0 + jnp.exp(-x))


def _softplus(x):
    return jnp.maximum(x, 0.0) + jnp.log(1.0 + jnp.exp(-jnp.abs(x)))


def _silu(x):
    return x * _sigmoid(x)


def _split3(x):
    hi = x.astype(BF16)
    r = x - hi.astype(F32)
    mid = r.astype(BF16)
    lo = (r - mid.astype(F32)).astype(BF16)
    return hi, mid, lo


def _cumsum_rows(x, tri):
    hi, mid, lo = _split3(x)
    return _dot(tri, hi) + _dot(tri, mid) + _dot(tri, lo)


def _tri(n):
    r = lax.broadcasted_iota(jnp.int32, (n, n), 0)
    c = lax.broadcasted_iota(jnp.int32, (n, n), 1)
    return r >= c


def _cparams(*sem):
    return pltpu.CompilerParams(dimension_semantics=sem, vmem_limit_bytes=VMEM_LIMIT)


def _const_spec(shape):
    nd = len(shape)
    return pl.BlockSpec(shape, lambda *_: (0,) * nd, pipeline_mode=pl.Buffered(1))


def _inproj_kernel(x_ref, nw_ref, *refs):
    hb = (_rms_unit(x_ref[...]) * nw_ref[...]).astype(BF16)
    nseg = len(refs) // 2
    for w_ref, out_ref in zip(refs[:nseg], refs[nseg:]):
        n = out_ref.shape[-1]
        for c0 in range(0, n, 512):
            cw = min(512, n - c0)
            out_ref[:, c0:c0 + cw] = _dot(hb, w_ref[:, c0:c0 + cw])


def _in_proj(x, nw, ws, tm):
    n, d = x.shape
    widths = [w.shape[-1] for w in ws]
    return pl.pallas_call(
        _inproj_kernel,
        grid=(n // tm,),
        in_specs=[pl.BlockSpec((tm, d), lambda i: (i, 0)), _const_spec(nw.shape)]
                 + [_const_spec(w.shape) for w in ws],
        out_specs=[pl.BlockSpec((tm, s), lambda i: (i, 0)) for s in widths],
        out_shape=[jax.ShapeDtypeStruct((n, s), F32) for s in widths],
        compiler_params=_cparams("arbitrary"),
    )(x, nw, *ws)


def _expand(x, e_ref):
    hi, mid, lo = _split3(x)
    e = e_ref[...]
    return _dot(hi, e) + _dot(mid, e) + _dot(lo, e)


PROJ_WIDTHS = (SSD_D, CONV_DIM, HG_D, HG_D, HG_D, HG_D, LANES)
OFF_Z, OFF_XBC, OFF_Q, OFF_F, OFF_V, OFF_G, OFF_DT = (sum(PROJ_WIDTHS[:k]) for k in range(len(PROJ_WIDTHS)))
PROJ_COLS = sum(PROJ_WIDTHS)


def _project(x, nw_ref, w_refs, dst_ref):
    hb = (_rms_unit(x) * nw_ref[...]).astype(BF16)
    off = 0
    for w_ref in w_refs:
        n = w_ref.shape[-1]
        for c0 in range(0, n, 512):
            cw = min(512, n - c0)
            dst_ref[:, off + c0:off + c0 + cw] = _dot(hb, w_ref[:, c0:c0 + cw])
        off += n


def _ssd_chunk(p_ref, cw_ref, cb_ref, dtb_ref, alog_ref, dsk_ref, nw_ref, e1_ref, e2_ref,
               y_ref, cout_ref, xc_ref, st_ref):
    rows = p_ref.shape[0]
    hist = SUBLANES
    hpg = SSD_HEADS // SSD_GROUPS
    gw = hpg * SSD_HEADDIM
    gn = D_STATE

    xc_ref[hist:hist + rows, :] = p_ref[:, OFF_XBC:OFF_XBC + CONV_DIM]
    base = hist - (CONV_W - 1)
    acc = cb_ref[...] + cw_ref[0:1, :] * xc_ref[base:base + rows, :]
    for k in range(1, CONV_W):
        acc = acc + cw_ref[k:k + 1, :] * xc_ref[base + k:base + k + rows, :]
    cout_ref[...] = xc_ref[base + rows:base + rows + CONV_W - 1, :]
    xc_ref[0:hist, :] = xc_ref[rows:rows + hist, :]
    xbc = _silu(acc)
    xs = xbc[:, :SSD_D]
    b_bf = [xbc[:, SSD_D + g * gn:SSD_D + (g + 1) * gn].astype(BF16) for g in range(SSD_GROUPS)]
    c_bf = [xbc[:, SSD_D + (SSD_GROUPS + g) * gn:SSD_D + (SSD_GROUPS + g + 1) * gn].astype(BF16)
            for g in range(SSD_GROUPS)]

    dt = _softplus(p_ref[:, OFF_DT:OFF_DT + LANES] + dtb_ref[...])
    a = dt * (-jnp.exp(alog_ref[...]))
    tri = _tri(rows)
    a_cs = _cumsum_rows(a, tri.astype(BF16)) * LOG2E
    a_cs_t = a_cs.T
    dt_full = _expand(dt, e1_ref)
    a_full = _expand(a_cs, e1_ref)
    a_rep = _expand(a_cs, e2_ref)
    a_last = a_full[rows - 1:rows, :]

    xdt = xs * dt_full
    xdt_bf = xdt.astype(BF16)
    xw_bf = (xdt * jnp.exp2(a_last - a_full)).astype(BF16)
    skip = dsk_ref[...] * xs
    lane = lax.broadcasted_iota(jnp.int32, (rows, LANES), 1)
    first_half = lane < SSD_HEADDIM
    zg = _silu(p_ref[:, OFF_Z:OFF_Z + SSD_D])

    for g in range(SSD_GROUPS):
        gs = slice(g * gw, (g + 1) * gw)
        scores = jnp.where(tri, _dot_nt(c_bf[g], b_bf[g]), 0.0)
        st_old = st_ref[g]
        y_off = _dot(c_bf[g], st_old.astype(BF16)) * jnp.exp2(a_full[:, gs])
        st_ref[g] = st_old * jnp.exp2(a_last[:, gs]) + _dot_tn(b_bf[g], xw_bf[:, gs])
        parts = []
        for pair in range(hpg // 2):
            h0 = g * hpg + 2 * pair
            cols = slice(h0 * SSD_HEADDIM, (h0 + 2) * SSD_HEADDIM)
            x_pair = xdt_bf[:, cols]
            acc = None
            for hh, keep in ((h0, first_half), (h0 + 1, jnp.logical_not(first_half))):
                diff = a_rep[:, hh * LANES:hh * LANES + rows] - a_cs_t[hh:hh + 1, :]
                m = (scores * jnp.exp2(jnp.minimum(diff, 0.0))).astype(BF16)
                part = _dot(m, jnp.where(keep, x_pair, jnp.zeros_like(x_pair)))
                acc = part if acc is None else acc + part
            parts.append(acc)
        y = jnp.concatenate(parts, axis=1) + y_off + skip[:, gs]
        y_ref[:, gs] = (_rms_unit(y * zg[:, gs]) * nw_ref[:, gs]).astype(y_ref.dtype)


def _hgrn_chunk(p_ref, lb_ref, nw_ref, o_ref, st_ref, a_ref, kk_ref, gcs_ref):
    rows = p_ref.shape[0]
    dk = HG_DK
    lb = lb_ref[...]
    sig = _sigmoid(p_ref[:, OFF_F:OFF_F + HG_D])
    logf = jnp.log2(lb + (1.0 - lb) * sig)
    kk = (1.0 - lb) * (1.0 - sig)
    tri = _tri(rows)
    gcs = _cumsum_rows(logf, tri.astype(BF16))
    glast = gcs[rows - 1:rows, :]
    qt = p_ref[:, OFF_Q:OFF_Q + HG_D] * jnp.exp2(gcs)
    kk_ref[...] = kk
    gcs_ref[...] = gcs

    kt = kk * jnp.exp2(-gcs)
    khat = kk * jnp.exp2(glast - gcs)
    for h in range(HG_HEADS):
        sl = slice(h * dk, (h + 1) * dk)
        a_ref[h] = _dot_nt(qt[:, sl].astype(BF16), kt[:, sl].astype(BF16))

    @pl.when(jnp.min(glast) < -HG_SAFE_LOG_DECAY * LOG2E)
    def _():
        ri = lax.broadcasted_iota(jnp.int32, (rows, dk), 0)
        ci = lax.broadcasted_iota(jnp.int32, (rows, rows), 1)
        for h in range(HG_HEADS):
            sl = slice(h * dk, (h + 1) * dk)
            q_h = p_ref[:, OFF_Q + h * dk:OFF_Q + (h + 1) * dk]

            def cols(j8, a_h):
                base = pl.multiple_of(j8 * SUBLANES, SUBLANES)
                k_blk = kk_ref[pl.ds(base, SUBLANES), sl]
                g_blk = gcs_ref[pl.ds(base, SUBLANES), sl]
                g_h = gcs_ref[:, sl]
                for r in range(SUBLANES):
                    j = base + r
                    t = q_h * k_blk[r:r + 1] * jnp.exp2(
                        jnp.where(ri >= j, g_h - g_blk[r:r + 1], -jnp.inf))
                    a_h = jnp.where(ci == j, jnp.sum(t, axis=-1, keepdims=True), a_h)
                return a_h

            a_ref[h] = lax.fori_loop(0, rows // SUBLANES, cols, jnp.zeros((rows, rows), F32))

    gate = _silu(p_ref[:, OFF_G:OFF_G + HG_D])
    for h in range(HG_HEADS):
        sl = slice(h * dk, (h + 1) * dk)
        st = st_ref[h]
        v_bf = p_ref[:, OFF_V + h * HG_DV:OFF_V + (h + 1) * HG_DV].astype(BF16)
        a_h = jnp.where(tri, a_ref[h], 0.0).astype(BF16)
        o = _dot(a_h, v_bf) + _dot_nt(qt[:, sl].astype(BF16), st.astype(BF16))
        st_ref[h] = st * jnp.exp2(glast[:, sl]) + _dot_tn(v_bf, khat[:, sl].astype(BF16))
        o_ref[:, sl] = (_rms_unit(o) * nw_ref[:, sl] * gate[:, sl]).astype(o_ref.dtype)


def _mixers_kernel(x_ref, xn_ref, nw_ref, wz_ref, wxbc_ref, wq_ref, wf_ref, wv_ref, wg_ref, wdt_ref,
                   cw_ref, cb_ref, dtb_ref, alog_ref, dsk_ref, snw_ref, e1_ref, e2_ref, lb_ref, hnw_ref,
                   y_ref, o_ref, cout_ref, sout_ref, hout_ref,
                   pa_ref, pb_ref, xc_ref, st_ref, hst_ref, a_ref, kk_ref, gcs_ref, *, chunks):
    s = pl.program_id(0)
    blk = pa_ref.shape[0]
    per_blk = blk // CHUNK
    c0 = (s * 2 * per_blk) % chunks
    w_refs = (wz_ref, wxbc_ref, wq_ref, wf_ref, wv_ref, wg_ref, wdt_ref)
    hpg = SSD_HEADS // SSD_GROUPS

    @pl.when(s == 0)
    def _():
        _project(x_ref[0:blk, :], nw_ref, w_refs, pa_ref)

    @pl.when(c0 == 0)
    def _():
        xc_ref[0:SUBLANES, :] = jnp.zeros((SUBLANES, CONV_DIM), F32)
        st_ref[...] = jnp.zeros(st_ref.shape, F32)
        hst_ref[...] = jnp.zeros(hst_ref.shape, F32)

    for half, (cur_ref, nxt_ref) in enumerate(((pa_ref, pb_ref), (pb_ref, pa_ref))):
        x_next = x_ref[blk:2 * blk, :] if half == 0 else xn_ref[0:blk, :]
        _project(x_next, nw_ref, w_refs, nxt_ref)
        for k in range(per_blk):
            p_ref = cur_ref.at[k * CHUNK:(k + 1) * CHUNK]
            rs = slice((half * per_blk + k) * CHUNK, (half * per_blk + k + 1) * CHUNK)
            _ssd_chunk(p_ref, cw_ref, cb_ref, dtb_ref, alog_ref, dsk_ref, snw_ref, e1_ref, e2_ref,
                       y_ref.at[rs], cout_ref.at[0], xc_ref, st_ref)
            _hgrn_chunk(p_ref, lb_ref, hnw_ref, o_ref.at[rs], hst_ref, a_ref, kk_ref, gcs_ref)

    @pl.when(c0 + 2 * per_blk == chunks)
    def _():
        for g in range(SSD_GROUPS):
            sout_ref[0, g * hpg:(g + 1) * hpg] = st_ref[g].T.reshape(hpg, SSD_HEADDIM, D_STATE)
        for h in range(HG_HEADS):
            hout_ref[0, h] = hst_ref[h].T


def _mixers(x, b, l, w):
    n, d = x.shape
    chunks = l // CHUNK
    step_rows = 2 * PROJ_ROWS
    assert PROJ_ROWS % CHUNK == 0 and l % step_rows == 0
    steps = n // step_rows
    head = jnp.arange(LANES)[:, None]
    e1 = (head == jnp.arange(SSD_D)[None, :] // SSD_HEADDIM).astype(BF16)
    e2 = (head == jnp.arange(SSD_HEADS * LANES)[None, :] // LANES).astype(BF16)
    consts = (w["norm_mix_w"], *w["w_in"], w["conv_w"], w["conv_b"], w["dt_bias"], w["a_log"],
              w["d_skip"], w["ssd_norm_w"], e1, e2, w["lb"], w["hg_norm_w"])
    tok = lambda width: pl.BlockSpec((step_rows, width), lambda s: (s, 0))
    per_seq = lambda shape: pl.BlockSpec((1,) + shape,
                                         lambda s: (s * step_rows // l,) + (0,) * len(shape))
    ssm_shape = (SSD_HEADS, SSD_HEADDIM, D_STATE)
    hg_shape = (HG_HEADS, HG_DK, HG_DV)
    return pl.pallas_call(
        functools.partial(_mixers_kernel, chunks=chunks),
        grid=(steps,),
        in_specs=[tok(d), pl.BlockSpec((step_rows, d), lambda s: (jnp.minimum(s + 1, steps - 1), 0))]
                 + [_const_spec(c.shape) for c in consts],
        out_specs=[tok(SSD_D), tok(HG_D), per_seq((CONV_W - 1, CONV_DIM)), per_seq(ssm_shape),
                   per_seq(hg_shape)],
        out_shape=[jax.ShapeDtypeStruct((n, SSD_D), BF16), jax.ShapeDtypeStruct((n, HG_D), BF16),
                   jax.ShapeDtypeStruct((b, CONV_W - 1, CONV_DIM), F32),
                   jax.ShapeDtypeStruct((b,) + ssm_shape, F32),
                   jax.ShapeDtypeStruct((b,) + hg_shape, F32)],
        scratch_shapes=[pltpu.VMEM((PROJ_ROWS, PROJ_COLS), F32), pltpu.VMEM((PROJ_ROWS, PROJ_COLS), F32),
                        pltpu.VMEM((CHUNK + SUBLANES, CONV_DIM), F32),
                        pltpu.VMEM((SSD_GROUPS, D_STATE, SSD_D // SSD_GROUPS), F32),
                        pltpu.VMEM((HG_HEADS, HG_DV, HG_DK), F32),
                        pltpu.VMEM((HG_HEADS, CHUNK, CHUNK), F32),
                        pltpu.VMEM((CHUNK, HG_D), F32), pltpu.VMEM((CHUNK, HG_D), F32)],
        compiler_params=_cparams("arbitrary"),
    )(x, x, *consts)


def _tokens_to_lanes(x):
    tb, n = x.shape
    return jnp.concatenate([x, jnp.zeros((LANES - tb, n), F32)], axis=0).T


def _step_decay_kernel(dt_ref, dtb_ref, alog_ref, da_ref):
    da_ref[...] = jnp.exp(_softplus(dt_ref[...] + dtb_ref[...]) * (-jnp.exp(alog_ref[...])))


def _ssd_step_kernel(da_ref, z_ref, xbc_ref, dt_ref, cw_ref, cb_ref, dtb_ref, dsk_ref, nw_ref, e1_ref,
                     cs_ref, st_ref, y_ref, cnew_ref, snew_ref, *, tb):
    cd = CONV_DIM
    gn = D_STATE
    hpg = SSD_HEADS // SSD_GROUPS
    gw = hpg * SSD_HEADDIM
    first = pl.program_id(0) * tb
    x_in = xbc_ref[...]
    acc = cb_ref[...] + cw_ref[CONV_W - 1:CONV_W, :] * x_in
    for k in range(CONV_W - 1):
        acc = acc + cw_ref[k:k + 1, :] * cs_ref[:, k * cd:(k + 1) * cd]
    cnew_ref[:, :(CONV_W - 2) * cd] = cs_ref[:, cd:]
    cnew_ref[:, (CONV_W - 2) * cd:] = x_in
    xbc = _silu(acc)
    xs = xbc[:, :SSD_D]
    dt = _softplus(dt_ref[...] + dtb_ref[...])
    xdt_t = _tokens_to_lanes(xs * _expand(dt, e1_ref)).astype(BF16)
    row_tok = lax.broadcasted_iota(jnp.int32, (LANES, tb * gn), 0)
    col_tok = lax.broadcasted_iota(jnp.int32, (LANES, tb * gn), 1) // gn
    lane = lax.broadcasted_iota(jnp.int32, (SSD_HEADDIM, LANES), 1)
    y_groups = []
    for g in range(SSD_GROUPS):
        b_g = xbc[:, SSD_D + g * gn:SSD_D + (g + 1) * gn]
        c_g = xbc[:, SSD_D + (SSD_GROUPS + g) * gn:SSD_D + (SSD_GROUPS + g + 1) * gn]
        b_wide = jnp.concatenate([jnp.tile(b_g, (1, tb)), jnp.zeros((LANES - tb, tb * gn), F32)], axis=0)
        b_diag = jnp.where(row_tok == col_tok, b_wide, 0.0).astype(BF16)
        upd = _dot(xdt_t[g * gw:(g + 1) * gw, :], b_diag)
        y_heads = []
        for hl in range(hpg):
            h = g * hpg + hl
            y_h = jnp.zeros((SSD_HEADDIM, LANES), F32)
            for j in range(tb):
                new = (st_ref[j, h] * da_ref[first + j, h]
                       + upd[hl * SSD_HEADDIM:(hl + 1) * SSD_HEADDIM, j * gn:(j + 1) * gn])
                snew_ref[j, h] = new
                y_h = jnp.where(lane == j, jnp.sum(new * c_g[j:j + 1, :], axis=-1, keepdims=True), y_h)
            y_heads.append(y_h)
        y_groups.append(jnp.concatenate(y_heads, axis=0))
    y = jnp.concatenate(y_groups, axis=0).T[:tb]
    y = (y + dsk_ref[...] * xs) * _silu(z_ref[...])
    for g in range(SSD_GROUPS):
        gs = slice(g * gw, (g + 1) * gw)
        y_ref[:, gs] = (_rms_unit(y[:, gs]) * nw_ref[:, gs]).astype(y_ref.dtype)


def _ssd_step(z, xbc, dt, conv_w, conv_b, dt_bias, a_log, d_skip, norm_w, conv_init, ssm_init):
    b = z.shape[0]
    tb = SUBLANES
    assert b % tb == 0
    da = pl.pallas_call(
        _step_decay_kernel,
        grid=(1,),
        in_specs=[_const_spec(dt.shape), _const_spec(dt_bias.shape), _const_spec(a_log.shape)],
        out_specs=_const_spec(dt.shape),
        out_shape=jax.ShapeDtypeStruct(dt.shape, F32),
    )(dt, dt_bias, a_log)
    head = jnp.arange(LANES)[:, None]
    e1 = (head == jnp.arange(SSD_D)[None, :] // SSD_HEADDIM).astype(BF16)
    params = (conv_w, conv_b, dt_bias, d_skip, norm_w, e1)
    hist = (CONV_W - 1) * CONV_DIM
    row = lambda w: pl.BlockSpec((tb, w), lambda i, da: (i, 0))
    const = lambda shape: pl.BlockSpec(shape, lambda i, da: (0,) * len(shape))
    state_shape = (SSD_HEADS, SSD_HEADDIM, D_STATE)
    st_spec = pl.BlockSpec((tb,) + state_shape, lambda i, da: (i, 0, 0, 0))
    y, conv_new, ssm_new = pl.pallas_call(
        functools.partial(_ssd_step_kernel, tb=tb),
        grid_spec=pltpu.PrefetchScalarGridSpec(
            num_scalar_prefetch=1,
            grid=(b // tb,),
            in_specs=[row(SSD_D), row(CONV_DIM), row(LANES)] + [const(p.shape) for p in params]
                     + [row(hist), st_spec],
            out_specs=[row(SSD_D), row(hist), st_spec]),
        out_shape=[jax.ShapeDtypeStruct((b, SSD_D), BF16), jax.ShapeDtypeStruct((b, hist), F32),
                   jax.ShapeDtypeStruct((b,) + state_shape, F32)],
        compiler_params=_cparams("arbitrary"),
    )(da, z, xbc, dt, *params, conv_init.reshape(b, hist), ssm_init)
    return y, conv_new.reshape(b, CONV_W - 1, CONV_DIM), ssm_new


def _hgrn_step_kernel(q_ref, f_ref, v_ref, g_ref, lb_ref, nw_ref, st_ref, o_ref, snew_ref, *, tb):
    dk, dv = HG_DK, HG_DV
    fr = f_ref[...]
    lb = lb_ref[...]
    sig = _sigmoid(fr)
    f_t = _tokens_to_lanes(lb + (1.0 - lb) * sig)
    k_t = _tokens_to_lanes((1.0 - lb) * (1.0 - sig)).astype(BF16)
    q_bf = q_ref[...].astype(BF16)
    v = v_ref[...]
    gate = _silu(g_ref[...])
    row_tok = lax.broadcasted_iota(jnp.int32, (LANES, tb * dv), 0)
    col_tok = lax.broadcasted_iota(jnp.int32, (LANES, tb * dv), 1) // dv
    sub = lax.broadcasted_iota(jnp.int32, (tb, dv), 0)
    for h in range(HG_HEADS):
        rs = slice(h * dk, (h + 1) * dk)
        vs = slice(h * dv, (h + 1) * dv)
        v_wide = jnp.concatenate([jnp.tile(v[:, vs], (1, tb)), jnp.zeros((LANES - tb, tb * dv), F32)], axis=0)
        v_diag = jnp.where(row_tok == col_tok, v_wide, 0.0).astype(BF16)
        upd = _dot(k_t[rs, :], v_diag)
        o_h = jnp.zeros((tb, dv), F32)
        for j in range(tb):
            fcol = jnp.broadcast_to(f_t[rs, j:j + 1], (dk, dv))
            new = st_ref[j, h] * fcol + upd[:, j * dv:(j + 1) * dv]
            snew_ref[j, h] = new
            o_h = jnp.where(sub == j, _dot(q_bf[:, rs], new.astype(BF16)), o_h)
        o_ref[:, vs] = (_rms_unit(o_h) * nw_ref[:, vs] * gate[:, vs]).astype(o_ref.dtype)


def _hgrn_step(q, f, v, g, lb, norm_w, init):
    b = q.shape[0]
    tb = SUBLANES
    assert b % tb == 0
    row = pl.BlockSpec((tb, HG_D), lambda i: (i, 0))
    state_shape = (HG_HEADS, HG_DK, HG_DV)
    st_spec = pl.BlockSpec((tb,) + state_shape, lambda i: (i, 0, 0, 0))
    return pl.pallas_call(
        functools.partial(_hgrn_step_kernel, tb=tb),
        grid=(b // tb,),
        in_specs=[row, row, row, row, _const_spec(lb.shape), _const_spec(norm_w.shape), st_spec],
        out_specs=[row, st_spec],
        out_shape=[jax.ShapeDtypeStruct((b, HG_D), BF16),
                   jax.ShapeDtypeStruct((b,) + state_shape, F32)],
        compiler_params=_cparams("arbitrary"),
    )(q, f, v, g, lb, norm_w, init)


def _route_t(logits, bias, group=None):
    row = lax.broadcasted_iota(jnp.int32, logits.shape, 0)
    ninf = -jnp.inf
    big = jnp.int32(ROUTER_ROWS)
    is_g = (row >= N_EXPERTS) & (row < N_EXPERTS + N_EGROUPS)
    cmax = lambda t: jnp.max(t, axis=0, keepdims=True)
    csum = lambda t: jnp.sum(t, axis=0, keepdims=True)
    first = lambda m: jnp.min(jnp.where(m, row, big), axis=0, keepdims=True)

    gl = jnp.where(is_g, logits, ninf)
    gp = jnp.exp(gl - cmax(gl))
    gprob = gp / csum(gp)
    biased = logits + bias
    if group is None:
        gb = jnp.where(is_g, biased, ninf)
        gsel = first(gb == cmax(gb))
    else:
        gsel = group + N_EXPERTS
    gw = csum(jnp.where(row == gsel, gprob, 0.0))
    e0 = (gsel - N_EXPERTS) * EXPERTS_PER_GROUP
    in_grp = (row >= e0) & (row < e0 + EXPERTS_PER_GROUP)
    el = jnp.where(in_grp, logits, ninf)
    ep = jnp.exp(el - cmax(el))
    eprob = ep / csum(ep)
    eb = jnp.where(in_grp, biased, ninf)
    i1 = first(eb == cmax(eb))
    eb2 = jnp.where(row == i1, ninf, eb)
    i2 = first(eb2 == cmax(eb2))
    p1 = csum(jnp.where(row == i1, eprob, 0.0))
    p2 = csum(jnp.where(row == i2, eprob, 0.0))
    den = p1 + p2
    a = (jnp.minimum(i1, i2) - e0).astype(F32)
    b = (jnp.maximum(i1, i2) - e0).astype(F32)
    group = jnp.broadcast_to(gsel - N_EXPERTS, (1, logits.shape[1])).astype(F32)
    cls = group * PAIRS_PER_GROUP + a * (7.0 - a) * 0.5 + (b - a - 1.0)
    return (jnp.where(row == i1, gw * (p1 / den), 0.0)
            + jnp.where(row == i2, gw * (p2 / den), 0.0)
            + jnp.where(row == CLASS_ROW, cls, 0.0))


def _router_logits_t(h, rhi_ref, rlo_ref):
    h_hi = h.astype(BF16)
    h_lo = (h - h_hi.astype(F32)).astype(BF16)
    logits = _dot(h_hi, rhi_ref[...]) + _dot(h_hi, rlo_ref[...]) + _dot(h_lo, rhi_ref[...])
    return logits.T[:ROUTER_ROWS]


def _rows_to_lanes(gates_t):
    r, tm = gates_t.shape
    return jnp.concatenate([gates_t, jnp.zeros((LANES - r, tm), F32)], axis=0).T


def _outproj_kernel(x_ref, y_ref, o_ref, w_ref, nw_ref, rhi_ref, rlo_ref, rb_ref,
                    x1_ref, h_ref, r_ref, *, tiled_rows):
    dy = y_ref.shape[-1]
    tm, d = x_ref.shape
    rb = min(tm, OUT_PROJ_ROWS)
    for r0 in range(0, tm, rb):
        rs = slice(r0, r0 + rb)
        mix = _dot(y_ref[rs, :], w_ref[:dy, :]) + _dot(o_ref[rs, :], w_ref[dy:, :])
        x1 = x_ref[rs, :] + mix
        x1_ref[rs, :] = x1
        h = _rms_unit(x1) * nw_ref[...]
        gates_t = _route_t(_router_logits_t(h, rhi_ref, rlo_ref), rb_ref[...])
        if tiled_rows:
            h_ref[rs] = h.reshape(rb, d // LANES, LANES)
            r_ref[:, rs] = jnp.broadcast_to(gates_t[CLASS_ROW:CLASS_ROW + 1, :], (SUBLANES, rb))
        else:
            h_ref[rs, :] = h
            r_ref[rs, :] = _rows_to_lanes(gates_t)


def _out_proj(x, y, o, w, nw, r_hi, r_lo, r_bias, tm, tiled_rows):
    n, d = x.shape
    row = lambda width: pl.BlockSpec((tm, width), lambda i: (i, 0))
    consts = (w, nw, r_hi, r_lo, r_bias)
    if tiled_rows:
        h_spec = pl.BlockSpec((tm, d // LANES, LANES), lambda i: (i, 0, 0))
        h_shape = jax.ShapeDtypeStruct((n, d // LANES, LANES), F32)
        r_spec = pl.BlockSpec((SUBLANES, tm), lambda i: (i, 0))
        r_shape = jax.ShapeDtypeStruct((n // tm * SUBLANES, tm), F32)
    else:
        h_spec, h_shape = row(d), jax.ShapeDtypeStruct((n, d), F32)
        r_spec, r_shape = row(LANES), jax.ShapeDtypeStruct((n, LANES), F32)
    return pl.pallas_call(
        functools.partial(_outproj_kernel, tiled_rows=tiled_rows),
        grid=(n // tm,),
        in_specs=[row(d), row(y.shape[-1]), row(o.shape[-1])] + [_const_spec(c.shape) for c in consts],
        out_specs=[row(d), h_spec, r_spec],
        out_shape=[jax.ShapeDtypeStruct((n, d), F32), h_shape, r_shape],
        compiler_params=_cparams("arbitrary"),
    )(x, y, o, *consts)


def _pos_kernel(cls_ref, pos_ref, tab_ref, *, tile):
    rows, tm = cls_ref.shape
    cls = cls_ref[...]
    ri = lax.broadcasted_iota(jnp.int32, (tm, tm), 0)
    ci = lax.broadcasted_iota(jnp.int32, (tm, tm), 1)
    upper = (ri <= ci).astype(BF16)
    rr = lax.broadcasted_iota(jnp.int32, (rows, rows), 0)
    rc = lax.broadcasted_iota(jnp.int32, (rows, rows), 1)
    earlier = ((rc // SUBLANES < rr // SUBLANES) & (rc % SUBLANES == 0)).astype(BF16)
    tile_lane = lax.broadcasted_iota(jnp.int32, (1, LANES), 1).astype(F32)
    pos = jnp.zeros((rows, tm), F32)
    tiles_before = jnp.zeros((1, 1), F32)
    tile_class = jnp.zeros((1, LANES), F32)
    for c in range(N_CLASSES):
        onehot = jnp.where(cls == float(c), 1.0, 0.0)
        inc = _dot(onehot.astype(BF16), upper)
        rowtot = jnp.broadcast_to(inc[:, tm - 1:tm], (rows, tm))
        rowpre = _cumsum_rows(rowtot, earlier)
        cnt = rowpre[rows - 1:rows, 0:1] + rowtot[rows - 1:rows, 0:1]
        pos = pos + onehot * (tiles_before * tile + rowpre + inc - 1.0)
        tiles_before = tiles_before + jnp.floor((cnt + (tile - 1.0)) * (1.0 / tile))
        tile_class = tile_class + jnp.where(tile_lane >= tiles_before, 1.0, 0.0)
    pos_ref[...] = pos.astype(jnp.int32)
    used = tile_class < N_CLASSES
    group = jnp.floor(tile_class * (1.0 / PAIRS_PER_GROUP))
    pair = tile_class - group * PAIRS_PER_GROUP
    ge3 = jnp.where(pair >= 3.0, 1.0, 0.0)
    ge5 = jnp.where(pair >= 5.0, 1.0, 0.0)
    a = ge3 + ge5
    b = pair + 1.0 - 2.0 * ge3 - ge5
    e1 = jnp.where(used, group * EXPERTS_PER_GROUP + a, float(N_EXPERTS))
    e2 = jnp.where(used, group * EXPERTS_PER_GROUP + b, float(N_EXPERTS))
    sub = lax.broadcasted_iota(jnp.int32, (SUBLANES, LANES), 0)
    tab_ref[...] = jnp.where(sub == 0, e1, jnp.where(sub == 1, e2, 0.0)).astype(jnp.int32)


def _positions(cls, n, tile):
    rows, tm = cls.shape
    assert n // tile + N_CLASSES <= LANES
    pos, tab = pl.pallas_call(
        functools.partial(_pos_kernel, tile=tile),
        grid=(1,),
        in_specs=[_const_spec(cls.shape)],
        out_specs=[_const_spec(cls.shape), _const_spec((SUBLANES, LANES))],
        out_shape=[jax.ShapeDtypeStruct(cls.shape, jnp.int32),
                   jax.ShapeDtypeStruct((SUBLANES, LANES), jnp.int32)],
        compiler_params=_cparams("arbitrary"),
    )(cls)
    return pos.reshape(rows // SUBLANES, SUBLANES, tm)[:, 0, :].reshape(n), tab[0], tab[1]


def _dispatch_kernel(pos_ref, h_ref, hs_in_ref, hs_ref, buf_ref, sem_ref, *, td):
    del hs_in_ref
    i = pl.program_id(0)
    nsteps = pl.num_programs(0)
    slot = i % 2

    def row_copy(step, s, r):
        return pltpu.make_async_copy(buf_ref.at[s, r], hs_ref.at[pos_ref[step * td + r]], sem_ref.at[s])

    def start_all(step, s):
        def body(k, carry):
            for u in range(DMA_QUEUES):
                row_copy(step, s, DMA_QUEUES * k + u).start(priority=u)
            return carry
        lax.fori_loop(0, td // DMA_QUEUES, body, 0, unroll=4)

    def wait_all(step, s):
        def body(r, carry):
            row_copy(step, s, r).wait()
            return carry
        lax.fori_loop(0, td, body, 0, unroll=8)

    @pl.when(i >= 2)
    def _():
        wait_all(i - 2, slot)

    buf_ref[slot] = h_ref[...]
    start_all(i, slot)

    @pl.when(i == nsteps - 1)
    def _():
        @pl.when(i >= 1)
        def _():
            wait_all(i - 1, 1 - slot)
        wait_all(i, slot)


def _dispatch(h3, pos, n_sorted, td):
    n = h3.shape[0]
    tok = h3.shape[1:]
    return pl.pallas_call(
        functools.partial(_dispatch_kernel, td=td),
        grid_spec=pltpu.PrefetchScalarGridSpec(
            num_scalar_prefetch=1,
            grid=(n // td,),
            in_specs=[pl.BlockSpec((td,) + tok, lambda i, pos: (i, 0, 0)),
                      pl.BlockSpec(memory_space=pl.ANY)],
            out_specs=pl.BlockSpec(memory_space=pl.ANY),
            scratch_shapes=[pltpu.VMEM((2, td) + tok, F32), pltpu.SemaphoreType.DMA((2,))]),
        out_shape=jax.ShapeDtypeStruct((n_sorted,) + tok, F32),
        input_output_aliases={2: 0},
        compiler_params=_cparams("arbitrary"),
    )(pos, h3, jnp.zeros((n_sorted,) + tok, F32))


def _expert(h, gates, e, wg, wu, wd):
    lane = lax.broadcasted_iota(jnp.int32, gates.shape, 1)
    gcol = jnp.sum(jnp.where(lane == e, gates, 0.0), axis=-1, keepdims=True)
    act = _silu(_dot(h, wg.astype(BF16))) * _dot(h, wu.astype(BF16)) * gcol
    return _dot(act.astype(BF16), wd.astype(BF16))


def _moe_dense_kernel(h_ref, gates_ref, wg_ref, wu_ref, wd_ref, out_ref):
    e = pl.program_id(1)
    part = _expert(h_ref[...].astype(BF16), gates_ref[...], e, wg_ref[0], wu_ref[0], wd_ref[0])

    @pl.when(e == 0)
    def _():
        out_ref[...] = part

    @pl.when(e > 0)
    def _():
        out_ref[...] += part


def _moe_dense(h, gates, wg, wu, wd, tm):
    n, d = h.shape
    ne, _, ff = wg.shape
    row = lambda width: pl.BlockSpec((tm, width), lambda i, e: (i, 0))
    return pl.pallas_call(
        _moe_dense_kernel,
        grid=(n // tm, ne),
        in_specs=[row(d), row(LANES),
                  pl.BlockSpec((1, d, ff), lambda i, e: (e, 0, 0)),
                  pl.BlockSpec((1, d, ff), lambda i, e: (e, 0, 0)),
                  pl.BlockSpec((1, ff, d), lambda i, e: (e, 0, 0))],
        out_specs=row(d),
        out_shape=jax.ShapeDtypeStruct((n, d), F32),
        compiler_params=_cparams("arbitrary", "arbitrary"),
    )(h, gates, wg, wu, wd)


def _moe_sorted_kernel(e1_ref, e2_ref, hs_ref, rhi_ref, rlo_ref, rb_ref,
                       wg1_ref, wu1_ref, wd1_ref, wg2_ref, wu2_ref, wd2_ref, ys_ref):
    e1 = e1_ref[pl.program_id(0)]
    e2 = e2_ref[pl.program_id(0)]
    tm = hs_ref.shape[0]
    d = wg1_ref.shape[1]

    @pl.when(e1 < N_EXPERTS)
    def _():
        h = hs_ref[...].reshape(tm, d)
        logits_t = _router_logits_t(h, rhi_ref, rlo_ref)
        group = lax.shift_right_logical(e1, jnp.int32(EXPERTS_PER_GROUP.bit_length() - 1))
        gates = _rows_to_lanes(_route_t(logits_t, rb_ref[...], group=group))
        h_bf = h.astype(BF16)
        acc = (_expert(h_bf, gates, e1, wg1_ref[0], wu1_ref[0], wd1_ref[0])
               + _expert(h_bf, gates, e2, wg2_ref[0], wu2_ref[0], wd2_ref[0]))
        ys_ref[...] = acc.reshape(ys_ref.shape)

    @pl.when(e1 >= N_EXPERTS)
    def _():
        ys_ref[...] = jnp.zeros(ys_ref.shape, F32)


def _moe_sorted(hs, e1_tab, e2_tab, r_hi, r_lo, r_bias, wg, wu, wd, tm):
    n_sorted = hs.shape[0]
    ne, d, ff = wg.shape
    first = lambda i, e1, e2: (jnp.minimum(e1[i], ne - 1), 0, 0)
    second = lambda i, e1, e2: (jnp.minimum(e2[i], ne - 1), 0, 0)
    tok = pl.BlockSpec((tm,) + hs.shape[1:], lambda i, e1, e2: (i, 0, 0))
    const = lambda shape: pl.BlockSpec(shape, lambda i, e1, e2: (0,) * len(shape))
    up = lambda sel: pl.BlockSpec((1, d, ff), sel)
    down = lambda sel: pl.BlockSpec((1, ff, d), sel)
    return pl.pallas_call(
        _moe_sorted_kernel,
        grid_spec=pltpu.PrefetchScalarGridSpec(
            num_scalar_prefetch=2,
            grid=(n_sorted // tm,),
            in_specs=[tok, const(r_hi.shape), const(r_lo.shape), const(r_bias.shape),
                      up(first), up(first), down(first), up(second), up(second), down(second)],
            out_specs=tok),
        out_shape=jax.ShapeDtypeStruct(hs.shape, F32),
        compiler_params=_cparams("arbitrary"),
    )(e1_tab, e2_tab, hs, r_hi, r_lo, r_bias, wg, wu, wd, wg, wu, wd)


def _ple_math(x, p, npw_ref, wg_ref, wp_ref, fw_ref):
    hn = (_rms_unit(x) * npw_ref[...]).astype(BF16)
    gate = _sigmoid(_dot(hn, wg_ref[...]))
    x = x + gate * _dot(p.astype(BF16), wp_ref[...])
    return _rms_unit(x) * fw_ref[...]


def _ple_kernel(x1_ref, moe_ref, p_ref, npw_ref, wg_ref, wp_ref, fw_ref, y_ref):
    y_ref[...] = _ple_math(x1_ref[...] + moe_ref[...], p_ref[...], npw_ref, wg_ref, wp_ref, fw_ref)


def _ple_gather_kernel(pos_ref, x1_ref, ys_ref, p_ref, npw_ref, wg_ref, wp_ref, fw_ref, y_ref,
                       buf_ref, sem_ref, *, tm):
    i = pl.program_id(0)
    nsteps = pl.num_programs(0)
    slot = i % 2

    def row_copy(step, s, r):
        return pltpu.make_async_copy(ys_ref.at[pos_ref[step * tm + r]], buf_ref.at[s, r], sem_ref.at[s])

    def start_all(step, s):
        def body(k, carry):
            for u in range(DMA_QUEUES):
                row_copy(step, s, DMA_QUEUES * k + u).start(priority=u)
            return carry
        lax.fori_loop(0, tm // DMA_QUEUES, body, 0, unroll=4)

    @pl.when(i == 0)
    def _():
        start_all(0, 0)

    @pl.when(i + 1 < nsteps)
    def _():
        start_all(i + 1, 1 - slot)

    def wait_row(r, carry):
        row_copy(i, slot, r).wait()
        return carry
    lax.fori_loop(0, tm, wait_row, 0, unroll=8)

    moe = buf_ref[slot].reshape(x1_ref.shape)
    y_ref[...] = _ple_math(x1_ref[...] + moe, p_ref[...], npw_ref, wg_ref, wp_ref, fw_ref)


def _ple(x1, moe, pos, p, npw, wg, wp, fw, tm):
    n, d = x1.shape
    consts = (npw, wg, wp, fw)
    out_shape = jax.ShapeDtypeStruct((n, d), F32)
    if pos is None:
        row = lambda width: pl.BlockSpec((tm, width), lambda i: (i, 0))
        return pl.pallas_call(
            _ple_kernel,
            grid=(n // tm,),
            in_specs=[row(d), row(d), row(p.shape[-1])] + [_const_spec(c.shape) for c in consts],
            out_specs=row(d),
            out_shape=out_shape,
            compiler_params=_cparams("arbitrary"),
        )(x1, moe, p, *consts)
    row = lambda width: pl.BlockSpec((tm, width), lambda i, pos: (i, 0))
    const = lambda shape: pl.BlockSpec(shape, lambda i, pos: (0,) * len(shape))
    return pl.pallas_call(
        functools.partial(_ple_gather_kernel, tm=tm),
        grid_spec=pltpu.PrefetchScalarGridSpec(
            num_scalar_prefetch=1,
            grid=(n // tm,),
            in_specs=[row(d), pl.BlockSpec(memory_space=pl.ANY), row(p.shape[-1])]
                     + [const(c.shape) for c in consts],
            out_specs=row(d),
            scratch_shapes=[pltpu.VMEM((2, tm) + moe.shape[1:], F32), pltpu.SemaphoreType.DMA((2,))]),
        out_shape=out_shape,
        compiler_params=_cparams("arbitrary"),
    )(pos, x1, moe, p, *consts)


def _prepare_weights(norm_mix_w, w_in, conv_w, conv_b, dt_bias, a_log, d_skip, ssd_norm_w,
                     lb_logits, hg_norm_w, w_out, norm_ffn_w, w_router_group, b_router_group,
                     w_router_expert, b_router_expert, w_exp_gate, w_exp_up, w_exp_down,
                     norm_ple_w, w_ple_gate, w_ple_proj, final_norm_w, layer):
    i = layer
    row = lambda t: t.reshape(1, -1).astype(F32)
    lane_pad = lambda t: jnp.pad(t, [(0, 0)] * (t.ndim - 1) + [(0, LANES - t.shape[-1])])
    splits = (SSD_D, CONV_DIM, SSD_HEADS, HG_D, HG_D, HG_D, HG_D)
    offs = [0]
    for s in splits:
        offs.append(offs[-1] + s)
    seg = lambda k: w_in[i][:, offs[k]:offs[k + 1]].astype(BF16)
    w_in_segs = (seg(0), seg(1), seg(3), seg(4), seg(5), seg(6), lane_pad(seg(2)))
    w_router = lane_pad(jnp.concatenate([w_router_expert[i], w_router_group[i]], axis=1).astype(F32))
    r_hi = w_router.astype(BF16)
    r_lo = (w_router - r_hi.astype(F32)).astype(BF16)
    r_bias = jnp.pad(jnp.concatenate([b_router_expert[i].reshape(-1), b_router_group[i]]).astype(F32),
                     [(0, ROUTER_ROWS - N_EXPERTS - N_EGROUPS)]).reshape(ROUTER_ROWS, 1)
    lb = jnp.cumsum(jax.nn.softmax(lb_logits.astype(F32), axis=0), axis=0)[i]
    return dict(
        norm_mix_w=row(norm_mix_w[i]), w_in=w_in_segs, conv_w=conv_w[i].astype(F32),
        conv_b=row(conv_b[i]), dt_bias=lane_pad(row(dt_bias[i])), a_log=lane_pad(row(a_log[i])),
        d_skip=row(jnp.repeat(d_skip[i], SSD_HEADDIM)), ssd_norm_w=row(ssd_norm_w[i]),
        lb=row(lb), hg_norm_w=row(hg_norm_w[i]), w_out=w_out[i].astype(BF16),
        norm_ffn_w=row(norm_ffn_w[i]), r_hi=r_hi, r_lo=r_lo, r_bias=r_bias,
        w_exp_gate=w_exp_gate[i], w_exp_up=w_exp_up[i], w_exp_down=w_exp_down[i],
        norm_ple_w=row(norm_ple_w[i]),
        w_ple_gate=w_ple_gate[i].astype(BF16), w_ple_proj=w_ple_proj[i].astype(BF16),
        final_norm_w=row(final_norm_w))


def _token_tile(n, cap):
    tm = cap
    while tm >= SUBLANES:
        if n % tm == 0:
            return tm
        tm //= 2
    raise ValueError(f"token count {n} must be a multiple of {SUBLANES}")


def _trunk(x, p, ssm_in, conv_in, hg_in, w):
    b, l, d = x.shape
    n = b * l
    tm = _token_tile(n, ROW_TILE)
    x2d = x.reshape(n, d)
    if l == 1 and ssm_in is not None:
        z, xbc, q, f, v, g, dt = _in_proj(x2d, w["norm_mix_w"], w["w_in"], _token_tile(n, IN_PROJ_TILE))
        y, conv_new, ssm_new = _ssd_step(z, xbc, dt, w["conv_w"], w["conv_b"], w["dt_bias"], w["a_log"],
                                         w["d_skip"], w["ssd_norm_w"], conv_in, ssm_in)
        o, hg_new = _hgrn_step(q, f, v, g, w["lb"], w["hg_norm_w"], hg_in)
    else:
        assert ssm_in is None and conv_in is None and hg_in is None, "multi-token groups start empty"
        y, o, conv_new, ssm_new, hg_new = _mixers(x2d, b, l, w)
    router = (w["r_hi"], w["r_lo"], w["r_bias"])
    experts = (w["w_exp_gate"], w["w_exp_up"], w["w_exp_down"])
    sorted_moe = n >= SORTED_MOE_MIN_TOKENS
    x1, h, routed = _out_proj(x2d, y.reshape(n, -1), o.reshape(n, -1), w["w_out"], w["norm_ffn_w"],
                              *router, tm, tiled_rows=sorted_moe)
    if sorted_moe:
        pos, e1_tab, e2_tab = _positions(routed, n, MOE_TILE)
        hs = _dispatch(h, pos, n + N_CLASSES * MOE_TILE, tm)
        moe = _moe_sorted(hs, e1_tab, e2_tab, *router, *experts, MOE_TILE)
    else:
        pos = None
        moe = _moe_dense(h, routed, *experts, tm)
    y_out = _ple(x1, moe, pos, p.reshape(n, -1), w["norm_ple_w"], w["w_ple_gate"], w["w_ple_proj"],
                 w["final_norm_w"], tm)
    return y_out.reshape(b, l, d), ssm_new[None], conv_new[None], hg_new[None]


def kernel(x_prompt, x_sample, state_ssm, state_conv, state_hgrn, p_prompt, p_sample, norm_mix_w, w_in, conv_w, conv_b, dt_bias, a_log, d_skip, ssd_norm_w, lb_logits, hg_norm_w, w_out, norm_ffn_w, w_router_group, b_router_group, w_router_expert, b_router_expert, w_exp_gate, w_exp_up, w_exp_down, norm_ple_w, w_ple_gate, w_ple_proj, final_norm_w):
    assert p_prompt.shape[0] == 1, "the per-layer-embedding kernel also applies the final norm: depth 1 only"
    w = _prepare_weights(norm_mix_w, w_in, conv_w, conv_b, dt_bias, a_log, d_skip, ssd_norm_w,
                         lb_logits, hg_norm_w, w_out, norm_ffn_w, w_router_group, b_router_group,
                         w_router_expert, b_router_expert, w_exp_gate, w_exp_up, w_exp_down,
                         norm_ple_w, w_ple_gate, w_ple_proj, final_norm_w, layer=0)
    y_p, ssm_p, conv_p, hg_p = _trunk(x_prompt, p_prompt[0], None, None, None, w)
    y_s, ssm_s, conv_s, hg_s = _trunk(x_sample, p_sample[0], state_ssm[0], state_conv[0],
                                      state_hgrn[0], w)
    return (y_p, y_s, ssm_p, conv_p, hg_p, ssm_s, conv_s, hg_s)
```

```python
import functools

import jax
import jax.numpy as jnp
from jax import lax
from jax.experimental import pallas as pl
from jax.experimental.pallas import tpu as pltpu

F32 = jnp.float32
BF16 = jnp.bfloat16

EPS = 1e-6
SSD_HEADS = 16
SSD_HEADDIM = 64
SSD_D = SSD_HEADS * SSD_HEADDIM
SSD_GROUPS = 2
D_STATE = 128
CONV_W = 4
CONV_DIM = SSD_D + 2 * SSD_GROUPS * D_STATE
HG_HEADS = 8
HG_DK = 128
HG_DV = 128
HG_D = HG_HEADS * HG_DV
N_EGROUPS = 4
EXPERTS_PER_GROUP = 4
N_EXPERTS = N_EGROUPS * EXPERTS_PER_GROUP

LANES = 128
SUBLANES = 8
VMEM_LIMIT = 48 * 1024 * 1024
DMA_QUEUES = 2
CHUNK = 128
PROJ_ROWS = 256
HG_SAFE_LOG_DECAY = 60.0
ROUTER_ROWS = 32
CLASS_ROW = N_EXPERTS + N_EGROUPS
PAIRS_PER_GROUP = EXPERTS_PER_GROUP * (EXPERTS_PER_GROUP - 1) // 2
N_CLASSES = N_EGROUPS * PAIRS_PER_GROUP
PAIR_ORDER = ((0, 1), (2, 1), (2, 3), (0, 3), (0, 2), (1, 3))
assert len({frozenset(p) for p in PAIR_ORDER}) == PAIRS_PER_GROUP
MOE_TILE = 256
IN_PROJ_TILE = 256
ROW_TILE = 512
OUT_PROJ_ROWS = 256
LOG2E = 1.4426950408889634
SORTED_MOE_MIN_TOKENS = 8 * SUBLANES * LANES


def _dot(a, b):
    return jnp.dot(a, b, preferred_element_type=F32)


def _dot_nt(a, b):
    return lax.dot_general(a, b, (((1,), (1,)), ((), ())), preferred_element_type=F32)


def _dot_tn(a, b):
    return lax.dot_general(a, b, (((0,), (0,)), ((), ())), preferred_element_type=F32)


def _rms_unit(x):
    return x * lax.rsqrt(jnp.mean(x * x, axis=-1, keepdims=True) + EPS)


def _sigmoid(x):
    return 1.0 / (1.0 + jnp.exp(-x))


def _softplus(x):
    return jnp.maximum(x, 0.0) + jnp.log(1.0 + jnp.exp(-jnp.abs(x)))


def _silu(x):
    return x * _sigmoid(x)


def _split3(x):
    hi = x.astype(BF16)
    r = x - hi.astype(F32)
    mid = r.astype(BF16)
    lo = (r - mid.astype(F32)).astype(BF16)
    return hi, mid, lo


def _cumsum_rows(x, tri):
    hi, mid, lo = _split3(x)
    return _dot(tri, hi) + _dot(tri, mid) + _dot(tri, lo)


def _tri(n):
    r = lax.broadcasted_iota(jnp.int32, (n, n), 0)
    c = lax.broadcasted_iota(jnp.int32, (n, n), 1)
    return r >= c


def _cparams(*sem):
    return pltpu.CompilerParams(dimension_semantics=sem, vmem_limit_bytes=VMEM_LIMIT)


def _const_spec(shape):
    nd = len(shape)
    return pl.BlockSpec(shape, lambda *_: (0,) * nd, pipeline_mode=pl.Buffered(1))


def _inproj_kernel(x_ref, nw_ref, *refs):
    hb = (_rms_unit(x_ref[...]) * nw_ref[...]).astype(BF16)
    nseg = len(refs) // 2
    for w_ref, out_ref in zip(refs[:nseg], refs[nseg:]):
        n = out_ref.shape[-1]
        for c0 in range(0, n, 512):
            cw = min(512, n - c0)
            out_ref[:, c0:c0 + cw] = _dot(hb, w_ref[:, c0:c0 + cw])


def _in_proj(x, nw, ws, tm):
    n, d = x.shape
    widths = [w.shape[-1] for w in ws]
    return pl.pallas_call(
        _inproj_kernel,
        grid=(n // tm,),
        in_specs=[pl.BlockSpec((tm, d), lambda i: (i, 0)), _const_spec(nw.shape)]
                 + [_const_spec(w.shape) for w in ws],
        out_specs=[pl.BlockSpec((tm, s), lambda i: (i, 0)) for s in widths],
        out_shape=[jax.ShapeDtypeStruct((n, s), F32) for s in widths],
        compiler_params=_cparams("arbitrary"),
    )(x, nw, *ws)


def _expand(x, e_ref):
    hi, mid, lo = _split3(x)
    e = e_ref[...]
    return _dot(hi, e) + _dot(mid, e) + _dot(lo, e)


PROJ_WIDTHS = (SSD_D, CONV_DIM, HG_D, HG_D, HG_D, HG_D, LANES)
OFF_Z, OFF_XBC, OFF_Q, OFF_F, OFF_V, OFF_G, OFF_DT = (sum(PROJ_WIDTHS[:k]) for k in range(len(PROJ_WIDTHS)))
PROJ_COLS = sum(PROJ_WIDTHS)


def _project(x, nw_ref, w_refs, dst_ref):
    hb = (_rms_unit(x) * nw_ref[...]).astype(BF16)
    off = 0
    for w_ref in w_refs:
        n = w_ref.shape[-1]
        for c0 in range(0, n, 512):
            cw = min(512, n - c0)
            dst_ref[:, off + c0:off + c0 + cw] = _dot(hb, w_ref[:, c0:c0 + cw])
        off += n


def _ssd_chunk(p_ref, cw_ref, cb_ref, dtb_ref, alog_ref, dsk_ref, nw_ref, e1_ref, e2_ref,
               y_ref, cout_ref, xc_ref, st_ref):
    rows = p_ref.shape[0]
    hist = SUBLANES
    hpg = SSD_HEADS // SSD_GROUPS
    gw = hpg * SSD_HEADDIM
    gn = D_STATE

    xc_ref[hist:hist + rows, :] = p_ref[:, OFF_XBC:OFF_XBC + CONV_DIM]
    base = hist - (CONV_W - 1)
    acc = cb_ref[...] + cw_ref[0:1, :] * xc_ref[base:base + rows, :]
    for k in range(1, CONV_W):
        acc = acc + cw_ref[k:k + 1, :] * xc_ref[base + k:base + k + rows, :]
    cout_ref[...] = xc_ref[base + rows:base + rows + CONV_W - 1, :]
    xc_ref[0:hist, :] = xc_ref[rows:rows + hist, :]
    xbc = _silu(acc)
    xs = xbc[:, :SSD_D]
    b_bf = [xbc[:, SSD_D + g * gn:SSD_D + (g + 1) * gn].astype(BF16) for g in range(SSD_GROUPS)]
    c_bf = [xbc[:, SSD_D + (SSD_GROUPS + g) * gn:SSD_D + (SSD_GROUPS + g + 1) * gn].astype(BF16)
            for g in range(SSD_GROUPS)]

    dt = _softplus(p_ref[:, OFF_DT:OFF_DT + LANES] + dtb_ref[...])
    a = dt * (-jnp.exp(alog_ref[...]))
    tri = _tri(rows)
    a_cs = _cumsum_rows(a, tri.astype(BF16)) * LOG2E
    a_cs_t = a_cs.T
    dt_full = _expand(dt, e1_ref)
    a_full = _expand(a_cs, e1_ref)
    a_rep = _expand(a_cs, e2_ref)
    a_last = a_full[rows - 1:rows, :]

    xdt = xs * dt_full
    xdt_bf = xdt.astype(BF16)
    xw_bf = (xdt * jnp.exp2(a_last - a_full)).astype(BF16)
    skip = dsk_ref[...] * xs
    lane = lax.broadcasted_iota(jnp.int32, (rows, LANES), 1)
    first_half = lane < SSD_HEADDIM
    zg = _silu(p_ref[:, OFF_Z:OFF_Z + SSD_D])

    for g in range(SSD_GROUPS):
        gs = slice(g * gw, (g + 1) * gw)
        scores = jnp.where(tri, _dot_nt(c_bf[g], b_bf[g]), 0.0)
        st_old = st_ref[g]
        y_off = _dot(c_bf[g], st_old.astype(BF16)) * jnp.exp2(a_full[:, gs])
        st_ref[g] = st_old * jnp.exp2(a_last[:, gs]) + _dot_tn(b_bf[g], xw_bf[:, gs])
        parts = []
        for pair in range(hpg // 2):
            h0 = g * hpg + 2 * pair
            cols = slice(h0 * SSD_HEADDIM, (h0 + 2) * SSD_HEADDIM)
            x_pair = xdt_bf[:, cols]
            acc = None
            for hh, keep in ((h0, first_half), (h0 + 1, jnp.logical_not(first_half))):
                diff = a_rep[:, hh * LANES:hh * LANES + rows] - a_cs_t[hh:hh + 1, :]
                m = (scores * jnp.exp2(jnp.minimum(diff, 0.0))).astype(BF16)
                part = _dot(m, jnp.where(keep, x_pair, jnp.zeros_like(x_pair)))
                acc = part if acc is None else acc + part
            parts.append(acc)
        y = jnp.concatenate(parts, axis=1) + y_off + skip[:, gs]
        y_ref[:, gs] = (_rms_unit(y * zg[:, gs]) * nw_ref[:, gs]).astype(y_ref.dtype)


def _hgrn_chunk(p_ref, lb_ref, nw_ref, o_ref, st_ref, a_ref, kk_ref, gcs_ref):
    rows = p_ref.shape[0]
    dk = HG_DK
    lb = lb_ref[...]
    sig = _sigmoid(p_ref[:, OFF_F:OFF_F + HG_D])
    logf = jnp.log2(lb + (1.0 - lb) * sig)
    kk = (1.0 - lb) * (1.0 - sig)
    tri = _tri(rows)
    gcs = _cumsum_rows(logf, tri.astype(BF16))
    glast = gcs[rows - 1:rows, :]
    qt = p_ref[:, OFF_Q:OFF_Q + HG_D] * jnp.exp2(gcs)
    kk_ref[...] = kk
    gcs_ref[...] = gcs

    kt = kk * jnp.exp2(-gcs)
    khat = kk * jnp.exp2(glast - gcs)
    for h in range(HG_HEADS):
        sl = slice(h * dk, (h + 1) * dk)
        a_ref[h] = _dot_nt(qt[:, sl].astype(BF16), kt[:, sl].astype(BF16))

    @pl.when(jnp.min(glast) < -HG_SAFE_LOG_DECAY * LOG2E)
    def _():
        ri = lax.broadcasted_iota(jnp.int32, (rows, dk), 0)
        ci = lax.broadcasted_iota(jnp.int32, (rows, rows), 1)
        for h in range(HG_HEADS):
            sl = slice(h * dk, (h + 1) * dk)
            q_h = p_ref[:, OFF_Q + h * dk:OFF_Q + (h + 1) * dk]

            def cols(j8, a_h):
                base = pl.multiple_of(j8 * SUBLANES, SUBLANES)
                k_blk = kk_ref[pl.ds(base, SUBLANES), sl]
                g_blk = gcs_ref[pl.ds(base, SUBLANES), sl]
                g_h = gcs_ref[:, sl]
                for r in range(SUBLANES):
                    j = base + r
                    t = q_h * k_blk[r:r + 1] * jnp.exp2(
                        jnp.where(ri >= j, g_h - g_blk[r:r + 1], -jnp.inf))
                    a_h = jnp.where(ci == j, jnp.sum(t, axis=-1, keepdims=True), a_h)
                return a_h

            a_ref[h] = lax.fori_loop(0, rows // SUBLANES, cols, jnp.zeros((rows, rows), F32))

    gate = _silu(p_ref[:, OFF_G:OFF_G + HG_D])
    for h in range(HG_HEADS):
        sl = slice(h * dk, (h + 1) * dk)
        st = st_ref[h]
        v_bf = p_ref[:, OFF_V + h * HG_DV:OFF_V + (h + 1) * HG_DV].astype(BF16)
        a_h = jnp.where(tri, a_ref[h], 0.0).astype(BF16)
        o = _dot(a_h, v_bf) + _dot_nt(qt[:, sl].astype(BF16), st.astype(BF16))
        st_ref[h] = st * jnp.exp2(glast[:, sl]) + _dot_tn(v_bf, khat[:, sl].astype(BF16))
        o_ref[:, sl] = (_rms_unit(o) * nw_ref[:, sl] * gate[:, sl]).astype(o_ref.dtype)


def _mixers_kernel(x_ref, xn_ref, nw_ref, wz_ref, wxbc_ref, wq_ref, wf_ref, wv_ref, wg_ref, wdt_ref,
                   cw_ref, cb_ref, dtb_ref, alog_ref, dsk_ref, snw_ref, e1_ref, e2_ref, lb_ref, hnw_ref,
                   y_ref, o_ref, cout_ref, sout_ref, hout_ref,
                   pa_ref, pb_ref, xc_ref, st_ref, hst_ref, a_ref, kk_ref, gcs_ref, *, chunks):
    s = pl.program_id(0)
    blk = pa_ref.shape[0]
    per_blk = blk // CHUNK
    c0 = (s * 2 * per_blk) % chunks
    w_refs = (wz_ref, wxbc_ref, wq_ref, wf_ref, wv_ref, wg_ref, wdt_ref)
    hpg = SSD_HEADS // SSD_GROUPS

    @pl.when(s == 0)
    def _():
        _project(x_ref[0:blk, :], nw_ref, w_refs, pa_ref)

    @pl.when(c0 == 0)
    def _():
        xc_ref[0:SUBLANES, :] = jnp.zeros((SUBLANES, CONV_DIM), F32)
        st_ref[...] = jnp.zeros(st_ref.shape, F32)
        hst_ref[...] = jnp.zeros(hst_ref.shape, F32)

    for half, (cur_ref, nxt_ref) in enumerate(((pa_ref, pb_ref), (pb_ref, pa_ref))):
        x_next = x_ref[blk:2 * blk, :] if half == 0 else xn_ref[0:blk, :]
        _project(x_next, nw_ref, w_refs, nxt_ref)
        for k in range(per_blk):
            p_ref = cur_ref.at[k * CHUNK:(k + 1) * CHUNK]
            rs = slice((half * per_blk + k) * CHUNK, (half * per_blk + k + 1) * CHUNK)
            _ssd_chunk(p_ref, cw_ref, cb_ref, dtb_ref, alog_ref, dsk_ref, snw_ref, e1_ref, e2_ref,
                       y_ref.at[rs], cout_ref.at[0], xc_ref, st_ref)
            _hgrn_chunk(p_ref, lb_ref, hnw_ref, o_ref.at[rs], hst_ref, a_ref, kk_ref, gcs_ref)

    @pl.when(c0 + 2 * per_blk == chunks)
    def _():
        for g in range(SSD_GROUPS):
            sout_ref[0, g * hpg:(g + 1) * hpg] = st_ref[g].T.reshape(hpg, SSD_HEADDIM, D_STATE)
        for h in range(HG_HEADS):
            hout_ref[0, h] = hst_ref[h].T


def _mixers(x, b, l, w):
    n, d = x.shape
    chunks = l // CHUNK
    step_rows = 2 * PROJ_ROWS
    assert PROJ_ROWS % CHUNK == 0 and l % step_rows == 0
    steps = n // step_rows
    head = jnp.arange(LANES)[:, None]
    e1 = (head == jnp.arange(SSD_D)[None, :] // SSD_HEADDIM).astype(BF16)
    e2 = (head == jnp.arange(SSD_HEADS * LANES)[None, :] // LANES).astype(BF16)
    consts = (w["norm_mix_w"], *w["w_in"], w["conv_w"], w["conv_b"], w["dt_bias"], w["a_log"],
              w["d_skip"], w["ssd_norm_w"], e1, e2, w["lb"], w["hg_norm_w"])
    tok = lambda width: pl.BlockSpec((step_rows, width), lambda s: (s, 0))
    per_seq = lambda shape: pl.BlockSpec((1,) + shape,
                                         lambda s: (s * step_rows // l,) + (0,) * len(shape))
    ssm_shape = (SSD_HEADS, SSD_HEADDIM, D_STATE)
    hg_shape = (HG_HEADS, HG_DK, HG_DV)
    return pl.pallas_call(
        functools.partial(_mixers_kernel, chunks=chunks),
        grid=(steps,),
        in_specs=[tok(d), pl.BlockSpec((step_rows, d), lambda s: (jnp.minimum(s + 1, steps - 1), 0))]
                 + [_const_spec(c.shape) for c in consts],
        out_specs=[tok(SSD_D), tok(HG_D), per_seq((CONV_W - 1, CONV_DIM)), per_seq(ssm_shape),
                   per_seq(hg_shape)],
        out_shape=[jax.ShapeDtypeStruct((n, SSD_D), BF16), jax.ShapeDtypeStruct((n, HG_D), BF16),
                   jax.ShapeDtypeStruct((b, CONV_W - 1, CONV_DIM), F32),
                   jax.ShapeDtypeStruct((b,) + ssm_shape, F32),
                   jax.ShapeDtypeStruct((b,) + hg_shape, F32)],
        scratch_shapes=[pltpu.VMEM((PROJ_ROWS, PROJ_COLS), F32), pltpu.VMEM((PROJ_ROWS, PROJ_COLS), F32),
                        pltpu.VMEM((CHUNK + SUBLANES, CONV_DIM), F32),
                        pltpu.VMEM((SSD_GROUPS, D_STATE, SSD_D // SSD_GROUPS), F32),
                        pltpu.VMEM((HG_HEADS, HG_DV, HG_DK), F32),
                        pltpu.VMEM((HG_HEADS, CHUNK, CHUNK), F32),
                        pltpu.VMEM((CHUNK, HG_D), F32), pltpu.VMEM((CHUNK, HG_D), F32)],
        compiler_params=_cparams("arbitrary"),
    )(x, x, *consts)


def _tokens_to_lanes(x):
    tb, n = x.shape
    return jnp.concatenate([x, jnp.zeros((LANES - tb, n), F32)], axis=0).T


def _step_decay_kernel(dt_ref, dtb_ref, alog_ref, da_ref):
    da_ref[...] = jnp.exp(_softplus(dt_ref[...] + dtb_ref[...]) * (-jnp.exp(alog_ref[...])))


def _ssd_step_kernel(da_ref, z_ref, xbc_ref, dt_ref, cw_ref, cb_ref, dtb_ref, dsk_ref, nw_ref, e1_ref,
                     cs_ref, st_ref, y_ref, cnew_ref, snew_ref, *, tb):
    cd = CONV_DIM
    gn = D_STATE
    hpg = SSD_HEADS // SSD_GROUPS
    gw = hpg * SSD_HEADDIM
    first = pl.program_id(0) * tb
    x_in = xbc_ref[...]
    acc = cb_ref[...] + cw_ref[CONV_W - 1:CONV_W, :] * x_in
    for k in range(CONV_W - 1):
        acc = acc + cw_ref[k:k + 1, :] * cs_ref[:, k * cd:(k + 1) * cd]
    cnew_ref[:, :(CONV_W - 2) * cd] = cs_ref[:, cd:]
    cnew_ref[:, (CONV_W - 2) * cd:] = x_in
    xbc = _silu(acc)
    xs = xbc[:, :SSD_D]
    dt = _softplus(dt_ref[...] + dtb_ref[...])
    xdt_t = _tokens_to_lanes(xs * _expand(dt, e1_ref)).astype(BF16)
    row_tok = lax.broadcasted_iota(jnp.int32, (LANES, tb * gn), 0)
    col_tok = lax.broadcasted_iota(jnp.int32, (LANES, tb * gn), 1) // gn
    lane = lax.broadcasted_iota(jnp.int32, (SSD_HEADDIM, LANES), 1)
    y_groups = []
    for g in range(SSD_GROUPS):
        b_g = xbc[:, SSD_D + g * gn:SSD_D + (g + 1) * gn]
        c_g = xbc[:, SSD_D + (SSD_GROUPS + g) * gn:SSD_D + (SSD_GROUPS + g + 1) * gn]
        b_wide = jnp.concatenate([jnp.tile(b_g, (1, tb)), jnp.zeros((LANES - tb, tb * gn), F32)], axis=0)
        b_diag = jnp.where(row_tok == col_tok, b_wide, 0.0).astype(BF16)
        upd = _dot(xdt_t[g * gw:(g + 1) * gw, :], b_diag)
        y_heads = []
        for hl in range(hpg):
            h = g * hpg + hl
            y_h = jnp.zeros((SSD_HEADDIM, LANES), F32)
            for j in range(tb):
                new = (st_ref[j, h] * da_ref[first + j, h]
                       + upd[hl * SSD_HEADDIM:(hl + 1) * SSD_HEADDIM, j * gn:(j + 1) * gn])
                snew_ref[j, h] = new
                y_h = jnp.where(lane == j, jnp.sum(new * c_g[j:j + 1, :], axis=-1, keepdims=True), y_h)
            y_heads.append(y_h)
        y_groups.append(jnp.concatenate(y_heads, axis=0))
    y = jnp.concatenate(y_groups, axis=0).T[:tb]
    y = (y + dsk_ref[...] * xs) * _silu(z_ref[...])
    for g in range(SSD_GROUPS):
        gs = slice(g * gw, (g + 1) * gw)
        y_ref[:, gs] = (_rms_unit(y[:, gs]) * nw_ref[:, gs]).astype(y_ref.dtype)


def _ssd_step(z, xbc, dt, conv_w, conv_b, dt_bias, a_log, d_skip, norm_w, conv_init, ssm_init):
    b = z.shape[0]
    tb = SUBLANES
    assert b % tb == 0
    da = pl.pallas_call(
        _step_decay_kernel,
        grid=(1,),
        in_specs=[_const_spec(dt.shape), _const_spec(dt_bias.shape), _const_spec(a_log.shape)],
        out_specs=_const_spec(dt.shape),
        out_shape=jax.ShapeDtypeStruct(dt.shape, F32),
    )(dt, dt_bias, a_log)
    head = jnp.arange(LANES)[:, None]
    e1 = (head == jnp.arange(SSD_D)[None, :] // SSD_HEADDIM).astype(BF16)
    params = (conv_w, conv_b, dt_bias, d_skip, norm_w, e1)
    hist = (CONV_W - 1) * CONV_DIM
    row = lambda w: pl.BlockSpec((tb, w), lambda i, da: (i, 0))
    const = lambda shape: pl.BlockSpec(shape, lambda i, da: (0,) * len(shape))
    state_shape = (SSD_HEADS, SSD_HEADDIM, D_STATE)
    st_spec = pl.BlockSpec((tb,) + state_shape, lambda i, da: (i, 0, 0, 0))
    y, conv_new, ssm_new = pl.pallas_call(
        functools.partial(_ssd_step_kernel, tb=tb),
        grid_spec=pltpu.PrefetchScalarGridSpec(
            num_scalar_prefetch=1,
            grid=(b // tb,),
            in_specs=[row(SSD_D), row(CONV_DIM), row(LANES)] + [const(p.shape) for p in params]
                     + [row(hist), st_spec],
            out_specs=[row(SSD_D), row(hist), st_spec]),
        out_shape=[jax.ShapeDtypeStruct((b, SSD_D), BF16), jax.ShapeDtypeStruct((b, hist), F32),
                   jax.ShapeDtypeStruct((b,) + state_shape, F32)],
        compiler_params=_cparams("arbitrary"),
    )(da, z, xbc, dt, *params, conv_init.reshape(b, hist), ssm_init)
    return y, conv_new.reshape(b, CONV_W - 1, CONV_DIM), ssm_new


def _hgrn_step_kernel(q_ref, f_ref, v_ref, g_ref, lb_ref, nw_ref, st_ref, o_ref, snew_ref, *, tb):
    dk, dv = HG_DK, HG_DV
    fr = f_ref[...]
    lb = lb_ref[...]
    sig = _sigmoid(fr)
    f_t = _tokens_to_lanes(lb + (1.0 - lb) * sig)
    k_t = _tokens_to_lanes((1.0 - lb) * (1.0 - sig)).astype(BF16)
    q_bf = q_ref[...].astype(BF16)
    v = v_ref[...]
    gate = _silu(g_ref[...])
    row_tok = lax.broadcasted_iota(jnp.int32, (LANES, tb * dv), 0)
    col_tok = lax.broadcasted_iota(jnp.int32, (LANES, tb * dv), 1) // dv
    sub = lax.broadcasted_iota(jnp.int32, (tb, dv), 0)
    for h in range(HG_HEADS):
        rs = slice(h * dk, (h + 1) * dk)
        vs = slice(h * dv, (h + 1) * dv)
        v_wide = jnp.concatenate([jnp.tile(v[:, vs], (1, tb)), jnp.zeros((LANES - tb, tb * dv), F32)], axis=0)
        v_diag = jnp.where(row_tok == col_tok, v_wide, 0.0).astype(BF16)
        upd = _dot(k_t[rs, :], v_diag)
        o_h = jnp.zeros((tb, dv), F32)
        for j in range(tb):
            fcol = jnp.broadcast_to(f_t[rs, j:j + 1], (dk, dv))
            new = st_ref[j, h] * fcol + upd[:, j * dv:(j + 1) * dv]
            snew_ref[j, h] = new
            o_h = jnp.where(sub == j, _dot(q_bf[:, rs], new.astype(BF16)), o_h)
        o_ref[:, vs] = (_rms_unit(o_h) * nw_ref[:, vs] * gate[:, vs]).astype(o_ref.dtype)


def _hgrn_step(q, f, v, g, lb, norm_w, init):
    b = q.shape[0]
    tb = SUBLANES
    assert b % tb == 0
    row = pl.BlockSpec((tb, HG_D), lambda i: (i, 0))
    state_shape = (HG_HEADS, HG_DK, HG_DV)
    st_spec = pl.BlockSpec((tb,) + state_shape, lambda i: (i, 0, 0, 0))
    return pl.pallas_call(
        functools.partial(_hgrn_step_kernel, tb=tb),
        grid=(b // tb,),
        in_specs=[row, row, row, row, _const_spec(lb.shape), _const_spec(norm_w.shape), st_spec],
        out_specs=[row, st_spec],
        out_shape=[jax.ShapeDtypeStruct((b, HG_D), BF16),
                   jax.ShapeDtypeStruct((b,) + state_shape, F32)],
        compiler_params=_cparams("arbitrary"),
    )(q, f, v, g, lb, norm_w, init)


def _route_t(logits, bias, group=None):
    row = lax.broadcasted_iota(jnp.int32, logits.shape, 0)
    ninf = -jnp.inf
    big = jnp.int32(ROUTER_ROWS)
    is_g = (row >= N_EXPERTS) & (row < N_EXPERTS + N_EGROUPS)
    cmax = lambda t: jnp.max(t, axis=0, keepdims=True)
    csum = lambda t: jnp.sum(t, axis=0, keepdims=True)
    first = lambda m: jnp.min(jnp.where(m, row, big), axis=0, keepdims=True)

    gl = jnp.where(is_g, logits, ninf)
    gp = jnp.exp(gl - cmax(gl))
    gprob = gp / csum(gp)
    biased = logits + bias
    if group is None:
        gb = jnp.where(is_g, biased, ninf)
        gsel = first(gb == cmax(gb))
    else:
        gsel = group + N_EXPERTS
    gw = csum(jnp.where(row == gsel, gprob, 0.0))
    e0 = (gsel - N_EXPERTS) * EXPERTS_PER_GROUP
    in_grp = (row >= e0) & (row < e0 + EXPERTS_PER_GROUP)
    el = jnp.where(in_grp, logits, ninf)
    ep = jnp.exp(el - cmax(el))
    eprob = ep / csum(ep)
    eb = jnp.where(in_grp, biased, ninf)
    i1 = first(eb == cmax(eb))
    eb2 = jnp.where(row == i1, ninf, eb)
    i2 = first(eb2 == cmax(eb2))
    p1 = csum(jnp.where(row == i1, eprob, 0.0))
    p2 = csum(jnp.where(row == i2, eprob, 0.0))
    den = p1 + p2
    a = jnp.minimum(i1, i2) - e0
    b = jnp.maximum(i1, i2) - e0
    group = jnp.broadcast_to(gsel - N_EXPERTS, (1, logits.shape[1])).astype(F32)
    pair = jnp.zeros_like(group)
    for k, (u, v) in enumerate(PAIR_ORDER):
        pair = pair + jnp.where((a == min(u, v)) & (b == max(u, v)), float(k), 0.0)
    cls = group * PAIRS_PER_GROUP + pair
    return (jnp.where(row == i1, gw * (p1 / den), 0.0)
            + jnp.where(row == i2, gw * (p2 / den), 0.0)
            + jnp.where(row == CLASS_ROW, cls, 0.0))


def _router_logits_t(h, rhi_ref, rlo_ref):
    h_hi = h.astype(BF16)
    h_lo = (h - h_hi.astype(F32)).astype(BF16)
    logits = _dot(h_hi, rhi_ref[...]) + _dot(h_hi, rlo_ref[...]) + _dot(h_lo, rhi_ref[...])
    return logits.T[:ROUTER_ROWS]


def _rows_to_lanes(gates_t):
    r, tm = gates_t.shape
    return jnp.concatenate([gates_t, jnp.zeros((LANES - r, tm), F32)], axis=0).T


def _outproj_kernel(x_ref, y_ref, o_ref, w_ref, nw_ref, rhi_ref, rlo_ref, rb_ref,
                    x1_ref, h_ref, r_ref, *, tiled_rows):
    dy = y_ref.shape[-1]
    tm, d = x_ref.shape
    rb = min(tm, OUT_PROJ_ROWS)
    for r0 in range(0, tm, rb):
        rs = slice(r0, r0 + rb)
        mix = _dot(y_ref[rs, :], w_ref[:dy, :]) + _dot(o_ref[rs, :], w_ref[dy:, :])
        x1 = x_ref[rs, :] + mix
        x1_ref[rs, :] = x1
        h = _rms_unit(x1) * nw_ref[...]
        gates_t = _route_t(_router_logits_t(h, rhi_ref, rlo_ref), rb_ref[...])
        if tiled_rows:
            h_ref[rs] = h.reshape(rb, d // LANES, LANES)
            r_ref[:, rs] = jnp.broadcast_to(gates_t[CLASS_ROW:CLASS_ROW + 1, :], (SUBLANES, rb))
        else:
            h_ref[rs, :] = h
            r_ref[rs, :] = _rows_to_lanes(gates_t)


def _out_proj(x, y, o, w, nw, r_hi, r_lo, r_bias, tm, tiled_rows):
    n, d = x.shape
    row = lambda width: pl.BlockSpec((tm, width), lambda i: (i, 0))
    consts = (w, nw, r_hi, r_lo, r_bias)
    if tiled_rows:
        h_spec = pl.BlockSpec((tm, d // LANES, LANES), lambda i: (i, 0, 0))
        h_shape = jax.ShapeDtypeStruct((n, d // LANES, LANES), F32)
        r_spec = pl.BlockSpec((SUBLANES, tm), lambda i: (i, 0))
        r_shape = jax.ShapeDtypeStruct((n // tm * SUBLANES, tm), F32)
    else:
        h_spec, h_shape = row(d), jax.ShapeDtypeStruct((n, d), F32)
        r_spec, r_shape = row(LANES), jax.ShapeDtypeStruct((n, LANES), F32)
    return pl.pallas_call(
        functools.partial(_outproj_kernel, tiled_rows=tiled_rows),
        grid=(n // tm,),
        in_specs=[row(d), row(y.shape[-1]), row(o.shape[-1])] + [_const_spec(c.shape) for c in consts],
        out_specs=[row(d), h_spec, r_spec],
        out_shape=[jax.ShapeDtypeStruct((n, d), F32), h_shape, r_shape],
        compiler_params=_cparams("arbitrary"),
    )(x, y, o, *consts)


def _pos_kernel(cls_ref, pos_ref, tab_ref, *, tile):
    rows, tm = cls_ref.shape
    cls = cls_ref[...]
    ri = lax.broadcasted_iota(jnp.int32, (tm, tm), 0)
    ci = lax.broadcasted_iota(jnp.int32, (tm, tm), 1)
    upper = (ri <= ci).astype(BF16)
    rr = lax.broadcasted_iota(jnp.int32, (rows, rows), 0)
    rc = lax.broadcasted_iota(jnp.int32, (rows, rows), 1)
    earlier = ((rc // SUBLANES < rr // SUBLANES) & (rc % SUBLANES == 0)).astype(BF16)
    tile_lane = lax.broadcasted_iota(jnp.int32, (1, LANES), 1).astype(F32)
    pos = jnp.zeros((rows, tm), F32)
    tiles_before = jnp.zeros((1, 1), F32)
    tile_class = jnp.zeros((1, LANES), F32)
    pad_start = jnp.zeros((1, LANES), F32)
    pad_len = jnp.zeros((1, LANES), F32)
    for c in range(N_CLASSES):
        onehot = jnp.where(cls == float(c), 1.0, 0.0)
        inc = _dot(onehot.astype(BF16), upper)
        rowtot = jnp.broadcast_to(inc[:, tm - 1:tm], (rows, tm))
        rowpre = _cumsum_rows(rowtot, earlier)
        cnt = rowpre[rows - 1:rows, 0:1] + rowtot[rows - 1:rows, 0:1]
        pos = pos + onehot * (tiles_before * tile + rowpre + inc - 1.0)
        ntiles = jnp.floor((cnt + (tile - 1.0)) * (1.0 / tile))
        pad_start = pad_start + jnp.where(tile_lane == float(c), tiles_before * tile + cnt, 0.0)
        pad_len = pad_len + jnp.where(tile_lane == float(c), ntiles * tile - cnt, 0.0)
        tiles_before = tiles_before + ntiles
        tile_class = tile_class + jnp.where(tile_lane >= tiles_before, 1.0, 0.0)
    pos_ref[...] = pos.astype(jnp.int32)
    pad_start = pad_start + jnp.where(tile_lane == float(N_CLASSES), tiles_before * tile, 0.0)
    used = tile_class < N_CLASSES
    group = jnp.floor(tile_class * (1.0 / PAIRS_PER_GROUP))
    pair = tile_class - group * PAIRS_PER_GROUP
    a = jnp.zeros_like(pair)
    b = jnp.zeros_like(pair)
    for k, (u, v) in enumerate(PAIR_ORDER):
        a = a + jnp.where(pair == float(k), float(u), 0.0)
        b = b + jnp.where(pair == float(k), float(v), 0.0)
    e1 = jnp.where(used, group * EXPERTS_PER_GROUP + a, float(N_EXPERTS))
    e2 = jnp.where(used, group * EXPERTS_PER_GROUP + b, float(N_EXPERTS))
    sub = lax.broadcasted_iota(jnp.int32, (SUBLANES, LANES), 0)
    tab = jnp.where(sub == 0, e1, jnp.where(sub == 1, e2, jnp.where(sub == 2, pad_start, pad_len)))
    tab_ref[...] = tab.astype(jnp.int32)


def _positions(cls, n, tile):
    rows, tm = cls.shape
    assert n // tile + N_CLASSES <= LANES
    pos, tab = pl.pallas_call(
        functools.partial(_pos_kernel, tile=tile),
        grid=(1,),
        in_specs=[_const_spec(cls.shape)],
        out_specs=[_const_spec(cls.shape), _const_spec((SUBLANES, LANES))],
        out_shape=[jax.ShapeDtypeStruct(cls.shape, jnp.int32),
                   jax.ShapeDtypeStruct((SUBLANES, LANES), jnp.int32)],
        compiler_params=_cparams("arbitrary"),
    )(cls)
    return pos.reshape(rows // SUBLANES, SUBLANES, tm)[:, 0, :].reshape(n), tab[0], tab[1], tab[2], tab[3]


def _dispatch_kernel(pos_ref, pad_start_ref, pad_len_ref, h_ref, hs_ref, buf_ref, zero_ref, sem_ref,
                     zsem_ref, *, td):
    i = pl.program_id(0)
    nsteps = pl.num_programs(0)
    slot = i % 2

    tile = zero_ref.shape[0]
    n_sorted = hs_ref.shape[0]

    def pad_copies(act):
        used_end = pad_start_ref[N_CLASSES]
        for j in range(N_CLASSES):
            @pl.when(used_end + j * tile < n_sorted)
            def _(j=j):
                act(pltpu.make_async_copy(zero_ref, hs_ref.at[pl.ds(used_end + j * tile, tile)],
                                          zsem_ref.at[0]))

        def body(c, carry):
            length = pad_len_ref[c]
            off = pad_start_ref[c]
            piece = tile // 2
            while piece >= 1:
                @pl.when((length & piece) != 0)
                def _(off=off, piece=piece):
                    act(pltpu.make_async_copy(zero_ref.at[0:piece], hs_ref.at[pl.ds(off, piece)],
                                              zsem_ref.at[0]))
                off = off + (length & piece)
                piece //= 2
            return carry
        lax.fori_loop(0, N_CLASSES, body, 0)

    @pl.when(i == 0)
    def _():
        zero_ref[...] = jnp.zeros(zero_ref.shape, F32)
        pad_copies(lambda cp: cp.start())
        pad_copies(lambda cp: cp.wait())

    def row_copy(step, s, r):
        return pltpu.make_async_copy(buf_ref.at[s, r], hs_ref.at[pos_ref[step * td + r]], sem_ref.at[s])

    def start_all(step, s):
        def body(k, carry):
            for u in range(DMA_QUEUES):
                row_copy(step, s, DMA_QUEUES * k + u).start(priority=u)
            return carry
        lax.fori_loop(0, td // DMA_QUEUES, body, 0, unroll=4)

    def wait_all(step, s):
        def body(r, carry):
            row_copy(step, s, r).wait()
            return carry
        lax.fori_loop(0, td, body, 0, unroll=8)

    @pl.when(i >= 2)
    def _():
        wait_all(i - 2, slot)

    buf_ref[slot] = h_ref[...]
    start_all(i, slot)

    @pl.when(i == nsteps - 1)
    def _():
        @pl.when(i >= 1)
        def _():
            wait_all(i - 1, 1 - slot)
        wait_all(i, slot)


def _dispatch(h3, pos, pad_start, pad_len, n_sorted, tile, td):
    n = h3.shape[0]
    tok = h3.shape[1:]
    return pl.pallas_call(
        functools.partial(_dispatch_kernel, td=td),
        grid_spec=pltpu.PrefetchScalarGridSpec(
            num_scalar_prefetch=3,
            grid=(n // td,),
            in_specs=[pl.BlockSpec((td,) + tok, lambda i, *_: (i, 0, 0))],
            out_specs=pl.BlockSpec(memory_space=pl.ANY),
            scratch_shapes=[pltpu.VMEM((2, td) + tok, F32), pltpu.VMEM((tile,) + tok, F32),
                            pltpu.SemaphoreType.DMA((2,)), pltpu.SemaphoreType.DMA((1,))]),
        out_shape=jax.ShapeDtypeStruct((n_sorted,) + tok, F32),
        compiler_params=_cparams("arbitrary"),
    )(pos, pad_start, pad_len, h3)


def _expert(h, gates, e, wg, wu, wd):
    lane = lax.broadcasted_iota(jnp.int32, gates.shape, 1)
    gcol = jnp.sum(jnp.where(lane == e, gates, 0.0), axis=-1, keepdims=True)
    act = _silu(_dot(h, wg.astype(BF16))) * _dot(h, wu.astype(BF16)) * gcol
    return _dot(act.astype(BF16), wd.astype(BF16))


def _moe_dense_kernel(h_ref, gates_ref, wg_ref, wu_ref, wd_ref, out_ref):
    e = pl.program_id(1)
    part = _expert(h_ref[...].astype(BF16), gates_ref[...], e, wg_ref[0], wu_ref[0], wd_ref[0])

    @pl.when(e == 0)
    def _():
        out_ref[...] = part

    @pl.when(e > 0)
    def _():
        out_ref[...] += part


def _moe_dense(h, gates, wg, wu, wd, tm):
    n, d = h.shape
    ne, _, ff = wg.shape
    row = lambda width: pl.BlockSpec((tm, width), lambda i, e: (i, 0))
    return pl.pallas_call(
        _moe_dense_kernel,
        grid=(n // tm, ne),
        in_specs=[row(d), row(LANES),
                  pl.BlockSpec((1, d, ff), lambda i, e: (e, 0, 0)),
                  pl.BlockSpec((1, d, ff), lambda i, e: (e, 0, 0)),
                  pl.BlockSpec((1, ff, d), lambda i, e: (e, 0, 0))],
        out_specs=row(d),
        out_shape=jax.ShapeDtypeStruct((n, d), F32),
        compiler_params=_cparams("arbitrary", "arbitrary"),
    )(h, gates, wg, wu, wd)


def _moe_sorted_kernel(e1_ref, e2_ref, hs_ref, rhi_ref, rlo_ref, rb_ref,
                       wg1_ref, wu1_ref, wd1_ref, wg2_ref, wu2_ref, wd2_ref, ys_ref):
    e1 = e1_ref[pl.program_id(0)]
    e2 = e2_ref[pl.program_id(0)]
    tm = hs_ref.shape[0]
    d = wg1_ref.shape[1]

    @pl.when(e1 < N_EXPERTS)
    def _():
        h = hs_ref[...].reshape(tm, d)
        logits_t = _router_logits_t(h, rhi_ref, rlo_ref)
        group = lax.shift_right_logical(e1, jnp.int32(EXPERTS_PER_GROUP.bit_length() - 1))
        gates = _rows_to_lanes(_route_t(logits_t, rb_ref[...], group=group))
        h_bf = h.astype(BF16)
        acc = (_expert(h_bf, gates, e1, wg1_ref[0], wu1_ref[0], wd1_ref[0])
               + _expert(h_bf, gates, e2, wg2_ref[0], wu2_ref[0], wd2_ref[0]))
        ys_ref[...] = acc.reshape(ys_ref.shape)

    @pl.when(e1 >= N_EXPERTS)
    def _():
        ys_ref[...] = jnp.zeros(ys_ref.shape, F32)


def _moe_sorted(hs, e1_tab, e2_tab, r_hi, r_lo, r_bias, wg, wu, wd, tm):
    n_sorted = hs.shape[0]
    ne, d, ff = wg.shape
    first = lambda i, e1, e2: (jnp.minimum(e1[i], ne - 1), 0, 0)
    second = lambda i, e1, e2: (jnp.minimum(e2[i], ne - 1), 0, 0)
    tok = pl.BlockSpec((tm,) + hs.shape[1:], lambda i, e1, e2: (i, 0, 0))
    const = lambda shape: pl.BlockSpec(shape, lambda i, e1, e2: (0,) * len(shape))
    up = lambda sel: pl.BlockSpec((1, d, ff), sel)
    down = lambda sel: pl.BlockSpec((1, ff, d), sel)
    return pl.pallas_call(
        _moe_sorted_kernel,
        grid_spec=pltpu.PrefetchScalarGridSpec(
            num_scalar_prefetch=2,
            grid=(n_sorted // tm,),
            in_specs=[tok, const(r_hi.shape), const(r_lo.shape), const(r_bias.shape),
                      up(first), up(first), down(first), up(second), up(second), down(second)],
            out_specs=tok),
        out_shape=jax.ShapeDtypeStruct(hs.shape, F32),
        compiler_params=_cparams("arbitrary"),
    )(e1_tab, e2_tab, hs, r_hi, r_lo, r_bias, wg, wu, wd, wg, wu, wd)


def _ple_math(x, p, npw_ref, wg_ref, wp_ref, fw_ref):
    hn = (_rms_unit(x) * npw_ref[...]).astype(BF16)
    gate = _sigmoid(_dot(hn, wg_ref[...]))
    x = x + gate * _dot(p.astype(BF16), wp_ref[...])
    return _rms_unit(x) * fw_ref[...]


def _ple_kernel(x1_ref, moe_ref, p_ref, npw_ref, wg_ref, wp_ref, fw_ref, y_ref):
    y_ref[...] = _ple_math(x1_ref[...] + moe_ref[...], p_ref[...], npw_ref, wg_ref, wp_ref, fw_ref)


def _ple_gather_kernel(pos_ref, x1_ref, ys_ref, p_ref, npw_ref, wg_ref, wp_ref, fw_ref, y_ref,
                       buf_ref, sem_ref, *, tm):
    i = pl.program_id(0)
    nsteps = pl.num_programs(0)
    slot = i % 2

    def row_copy(step, s, r):
        return pltpu.make_async_copy(ys_ref.at[pos_ref[step * tm + r]], buf_ref.at[s, r], sem_ref.at[s])

    def start_all(step, s):
        def body(k, carry):
            for u in range(DMA_QUEUES):
                row_copy(step, s, DMA_QUEUES * k + u).start(priority=u)
            return carry
        lax.fori_loop(0, tm // DMA_QUEUES, body, 0, unroll=4)

    @pl.when(i == 0)
    def _():
        start_all(0, 0)

    @pl.when(i + 1 < nsteps)
    def _():
        start_all(i + 1, 1 - slot)

    def wait_row(r, carry):
        row_copy(i, slot, r).wait()
        return carry
    lax.fori_loop(0, tm, wait_row, 0, unroll=8)

    moe = buf_ref[slot].reshape(x1_ref.shape)
    y_ref[...] = _ple_math(x1_ref[...] + moe, p_ref[...], npw_ref, wg_ref, wp_ref, fw_ref)


def _ple(x1, moe, pos, p, npw, wg, wp, fw, tm):
    n, d = x1.shape
    consts = (npw, wg, wp, fw)
    out_shape = jax.ShapeDtypeStruct((n, d), F32)
    if pos is None:
        row = lambda width: pl.BlockSpec((tm, width), lambda i: (i, 0))
        return pl.pallas_call(
            _ple_kernel,
            grid=(n // tm,),
            in_specs=[row(d), row(d), row(p.shape[-1])] + [_const_spec(c.shape) for c in consts],
            out_specs=row(d),
            out_shape=out_shape,
            compiler_params=_cparams("arbitrary"),
        )(x1, moe, p, *consts)
    row = lambda width: pl.BlockSpec((tm, width), lambda i, pos: (i, 0))
    const = lambda shape: pl.BlockSpec(shape, lambda i, pos: (0,) * len(shape))
    return pl.pallas_call(
        functools.partial(_ple_gather_kernel, tm=tm),
        grid_spec=pltpu.PrefetchScalarGridSpec(
            num_scalar_prefetch=1,
            grid=(n // tm,),
            in_specs=[row(d), pl.BlockSpec(memory_space=pl.ANY), row(p.shape[-1])]
                     + [const(c.shape) for c in consts],
            out_specs=row(d),
            scratch_shapes=[pltpu.VMEM((2, tm) + moe.shape[1:], F32), pltpu.SemaphoreType.DMA((2,))]),
        out_shape=out_shape,
        compiler_params=_cparams("arbitrary"),
    )(pos, x1, moe, p, *consts)


def _prepare_weights(norm_mix_w, w_in, conv_w, conv_b, dt_bias, a_log, d_skip, ssd_norm_w,
                     lb_logits, hg_norm_w, w_out, norm_ffn_w, w_router_group, b_router_group,
                     w_router_expert, b_router_expert, w_exp_gate, w_exp_up, w_exp_down,
                     norm_ple_w, w_ple_gate, w_ple_proj, final_norm_w, layer):
    i = layer
    row = lambda t: t.reshape(1, -1).astype(F32)
    lane_pad = lambda t: jnp.pad(t, [(0, 0)] * (t.ndim - 1) + [(0, LANES - t.shape[-1])])
    splits = (SSD_D, CONV_DIM, SSD_HEADS, HG_D, HG_D, HG_D, HG_D)
    offs = [0]
    for s in splits:
        offs.append(offs[-1] + s)
    seg = lambda k: w_in[i][:, offs[k]:offs[k + 1]].astype(BF16)
    w_in_segs = (seg(0), seg(1), seg(3), seg(4), seg(5), seg(6), lane_pad(seg(2)))
    w_router = lane_pad(jnp.concatenate([w_router_expert[i], w_router_group[i]], axis=1).astype(F32))
    r_hi = w_router.astype(BF16)
    r_lo = (w_router - r_hi.astype(F32)).astype(BF16)
    r_bias = jnp.pad(jnp.concatenate([b_router_expert[i].reshape(-1), b_router_group[i]]).astype(F32),
                     [(0, ROUTER_ROWS - N_EXPERTS - N_EGROUPS)]).reshape(ROUTER_ROWS, 1)
    lb = jnp.cumsum(jax.nn.softmax(lb_logits.astype(F32), axis=0), axis=0)[i]
    return dict(
        norm_mix_w=row(norm_mix_w[i]), w_in=w_in_segs, conv_w=conv_w[i].astype(F32),
        conv_b=row(conv_b[i]), dt_bias=lane_pad(row(dt_bias[i])), a_log=lane_pad(row(a_log[i])),
        d_skip=row(jnp.repeat(d_skip[i], SSD_HEADDIM)), ssd_norm_w=row(ssd_norm_w[i]),
        lb=row(lb), hg_norm_w=row(hg_norm_w[i]), w_out=w_out[i].astype(BF16),
        norm_ffn_w=row(norm_ffn_w[i]), r_hi=r_hi, r_lo=r_lo, r_bias=r_bias,
        w_exp_gate=w_exp_gate[i], w_exp_up=w_exp_up[i], w_exp_down=w_exp_down[i],
        norm_ple_w=row(norm_ple_w[i]),
        w_ple_gate=w_ple_gate[i].astype(BF16), w_ple_proj=w_ple_proj[i].astype(BF16),
        final_norm_w=row(final_norm_w))


def _token_tile(n, cap):
    tm = cap
    while tm >= SUBLANES:
        if n % tm == 0:
            return tm
        tm //= 2
    raise ValueError(f"token count {n} must be a multiple of {SUBLANES}")


def _trunk(x, p, ssm_in, conv_in, hg_in, w):
    b, l, d = x.shape
    n = b * l
    tm = _token_tile(n, ROW_TILE)
    x2d = x.reshape(n, d)
    if l == 1 and ssm_in is not None:
        z, xbc, q, f, v, g, dt = _in_proj(x2d, w["norm_mix_w"], w["w_in"], _token_tile(n, IN_PROJ_TILE))
        y, conv_new, ssm_new = _ssd_step(z, xbc, dt, w["conv_w"], w["conv_b"], w["dt_bias"], w["a_log"],
                                         w["d_skip"], w["ssd_norm_w"], conv_in, ssm_in)
        o, hg_new = _hgrn_step(q, f, v, g, w["lb"], w["hg_norm_w"], hg_in)
    else:
        assert ssm_in is None and conv_in is None and hg_in is None, "multi-token groups start empty"
        y, o, conv_new, ssm_new, hg_new = _mixers(x2d, b, l, w)
    router = (w["r_hi"], w["r_lo"], w["r_bias"])
    experts = (w["w_exp_gate"], w["w_exp_up"], w["w_exp_down"])
    sorted_moe = n >= SORTED_MOE_MIN_TOKENS
    x1, h, routed = _out_proj(x2d, y.reshape(n, -1), o.reshape(n, -1), w["w_out"], w["norm_ffn_w"],
                              *router, tm, tiled_rows=sorted_moe)
    if sorted_moe:
        pos, e1_tab, e2_tab, pad_start, pad_len = _positions(routed, n, MOE_TILE)
        hs = _dispatch(h, pos, pad_start, pad_len, n + N_CLASSES * MOE_TILE, MOE_TILE, tm)
        moe = _moe_sorted(hs, e1_tab, e2_tab, *router, *experts, MOE_TILE)
    else:
        pos = None
        moe = _moe_dense(h, routed, *experts, tm)
    y_out = _ple(x1, moe, pos, p.reshape(n, -1), w["norm_ple_w"], w["w_ple_gate"], w["w_ple_proj"],
                 w["final_norm_w"], tm)
    return y_out.reshape(b, l, d), ssm_new[None], conv_new[None], hg_new[None]


def kernel(x_prompt, x_sample, state_ssm, state_conv, state_hgrn, p_prompt, p_sample, norm_mix_w, w_in, conv_w, conv_b, dt_bias, a_log, d_skip, ssd_norm_w, lb_logits, hg_norm_w, w_out, norm_ffn_w, w_router_group, b_router_group, w_router_expert, b_router_expert, w_exp_gate, w_exp_up, w_exp_down, norm_ple_w, w_ple_gate, w_ple_proj, final_norm_w):
    assert p_prompt.shape[0] == 1, "the per-layer-embedding kernel also applies the final norm: depth 1 only"
    w = _prepare_weights(norm_mix_w, w_in, conv_w, conv_b, dt_bias, a_log, d_skip, ssd_norm_w,
                         lb_logits, hg_norm_w, w_out, norm_ffn_w, w_router_group, b_router_group,
                         w_router_expert, b_router_expert, w_exp_gate, w_exp_up, w_exp_down,
                         norm_ple_w, w_ple_gate, w_ple_proj, final_norm_w, layer=0)
    y_p, ssm_p, conv_p, hg_p = _trunk(x_prompt, p_prompt[0], None, None, None, w)
    y_s, ssm_s, conv_s, hg_s = _trunk(x_sample, p_sample[0], state_ssm[0], state_conv[0],
                                      state_hgrn[0], w)
    return (y_p, y_s, ssm_p, conv_p, hg_p, ssm_s, conv_s, hg_s)
```

```python
import functools

import jax
import jax.numpy as jnp
from jax import lax
from jax.experimental import pallas as pl
from jax.experimental.pallas import tpu as pltpu

F32 = jnp.float32
BF16 = jnp.bfloat16

EPS = 1e-6
SSD_HEADS = 16
SSD_HEADDIM = 64
SSD_D = SSD_HEADS * SSD_HEADDIM
SSD_GROUPS = 2
D_STATE = 128
CONV_W = 4
CONV_DIM = SSD_D + 2 * SSD_GROUPS * D_STATE
HG_HEADS = 8
HG_DK = 128
HG_DV = 128
HG_D = HG_HEADS * HG_DV
N_EGROUPS = 4
EXPERTS_PER_GROUP = 4
N_EXPERTS = N_EGROUPS * EXPERTS_PER_GROUP

LANES = 128
SUBLANES = 8
VMEM_LIMIT = 48 * 1024 * 1024
DMA_QUEUES = 2
CHUNK = 128
PROJ_ROWS = 256
HG_SAFE_LOG_DECAY = 60.0
ROUTER_ROWS = 32
CLASS_ROW = N_EXPERTS + N_EGROUPS
PAIRS_PER_GROUP = EXPERTS_PER_GROUP * (EXPERTS_PER_GROUP - 1) // 2
N_CLASSES = N_EGROUPS * PAIRS_PER_GROUP
PAIR_ORDER = ((0, 1), (2, 1), (2, 3), (0, 3), (0, 2), (1, 3))
assert len({frozenset(p) for p in PAIR_ORDER}) == PAIRS_PER_GROUP
MOE_TILE = 256
IN_PROJ_TILE = 256
ROW_TILE = 512
OUT_PROJ_ROWS = 256
LOG2E = 1.4426950408889634
SORTED_MOE_MIN_TOKENS = 8 * SUBLANES * LANES


def _dot(a, b):
    return jnp.dot(a, b, preferred_element_type=F32)


def _dot_nt(a, b):
    return lax.dot_general(a, b, (((1,), (1,)), ((), ())), preferred_element_type=F32)


def _dot_tn(a, b):
    return lax.dot_general(a, b, (((0,), (0,)), ((), ())), preferred_element_type=F32)


def _rms_unit(x):
    return x * lax.rsqrt(jnp.mean(x * x, axis=-1, keepdims=True) + EPS)


def _sigmoid(x):
    return 1.0 / (1.0 + jnp.exp(-x))


def _softplus(x):
    return jnp.maximum(x, 0.0) + jnp.log(1.0 + jnp.exp(-jnp.abs(x)))


def _silu(x):
    return x * _sigmoid(x)


def _split3(x):
    hi = x.astype(BF16)
    r = x - hi.astype(F32)
    mid = r.astype(BF16)
    lo = (r - mid.astype(F32)).astype(BF16)
    return hi, mid, lo


def _cumsum_rows(x, tri):
    hi, mid, lo = _split3(x)
    return _dot(tri, hi) + _dot(tri, mid) + _dot(tri, lo)


def _tri(n):
    r = lax.broadcasted_iota(jnp.int32, (n, n), 0)
    c = lax.broadcasted_iota(jnp.int32, (n, n), 1)
    return r >= c


def _cparams(*sem):
    return pltpu.CompilerParams(dimension_semantics=sem, vmem_limit_bytes=VMEM_LIMIT)


def _const_spec(shape):
    nd = len(shape)
    return pl.BlockSpec(shape, lambda *_: (0,) * nd, pipeline_mode=pl.Buffered(1))


def _inproj_kernel(x_ref, nw_ref, *refs):
    hb = (_rms_unit(x_ref[...]) * nw_ref[...]).astype(BF16)
    nseg = len(refs) // 2
    for w_ref, out_ref in zip(refs[:nseg], refs[nseg:]):
        n = out_ref.shape[-1]
        for c0 in range(0, n, 512):
            cw = min(512, n - c0)
            out_ref[:, c0:c0 + cw] = _dot(hb, w_ref[:, c0:c0 + cw])


def _in_proj(x, nw, ws, tm):
    n, d = x.shape
    widths = [w.shape[-1] for w in ws]
    return pl.pallas_call(
        _inproj_kernel,
        grid=(n // tm,),
        in_specs=[pl.BlockSpec((tm, d), lambda i: (i, 0)), _const_spec(nw.shape)]
                 + [_const_spec(w.shape) for w in ws],
        out_specs=[pl.BlockSpec((tm, s), lambda i: (i, 0)) for s in widths],
        out_shape=[jax.ShapeDtypeStruct((n, s), F32) for s in widths],
        compiler_params=_cparams("arbitrary"),
    )(x, nw, *ws)


def _expand(x, e_ref):
    hi, mid, lo = _split3(x)
    e = e_ref[...]
    return _dot(hi, e) + _dot(mid, e) + _dot(lo, e)


PROJ_WIDTHS = (SSD_D, CONV_DIM, HG_D, HG_D, HG_D, HG_D, LANES)
OFF_Z, OFF_XBC, OFF_Q, OFF_F, OFF_V, OFF_G, OFF_DT = (sum(PROJ_WIDTHS[:k]) for k in range(len(PROJ_WIDTHS)))
PROJ_COLS = sum(PROJ_WIDTHS)


def _project(x, nw_ref, w_refs, dst_ref):
    hb = (_rms_unit(x) * nw_ref[...]).astype(BF16)
    off = 0
    for w_ref in w_refs:
        n = w_ref.shape[-1]
        for c0 in range(0, n, 512):
            cw = min(512, n - c0)
            dst_ref[:, off + c0:off + c0 + cw] = _dot(hb, w_ref[:, c0:c0 + cw])
        off += n


def _ssd_chunk(p_ref, cw_ref, cb_ref, dtb_ref, alog_ref, dsk_ref, nw_ref, e1_ref, e2_ref,
               y_ref, cout_ref, xc_ref, st_ref):
    rows = p_ref.shape[0]
    hist = SUBLANES
    hpg = SSD_HEADS // SSD_GROUPS
    gw = hpg * SSD_HEADDIM
    gn = D_STATE

    xc_ref[hist:hist + rows, :] = p_ref[:, OFF_XBC:OFF_XBC + CONV_DIM]
    base = hist - (CONV_W - 1)
    acc = cb_ref[...] + cw_ref[0:1, :] * xc_ref[base:base + rows, :]
    for k in range(1, CONV_W):
        acc = acc + cw_ref[k:k + 1, :] * xc_ref[base + k:base + k + rows, :]
    cout_ref[...] = xc_ref[base + rows:base + rows + CONV_W - 1, :]
    xc_ref[0:hist, :] = xc_ref[rows:rows + hist, :]
    xbc = _silu(acc)
    xs = xbc[:, :SSD_D]
    b_bf = [xbc[:, SSD_D + g * gn:SSD_D + (g + 1) * gn].astype(BF16) for g in range(SSD_GROUPS)]
    c_bf = [xbc[:, SSD_D + (SSD_GROUPS + g) * gn:SSD_D + (SSD_GROUPS + g + 1) * gn].astype(BF16)
            for g in range(SSD_GROUPS)]

    dt = _softplus(p_ref[:, OFF_DT:OFF_DT + LANES] + dtb_ref[...])
    a = dt * (-jnp.exp(alog_ref[...]))
    tri = _tri(rows)
    a_cs = _cumsum_rows(a, tri.astype(BF16)) * LOG2E
    a_cs_t = a_cs.T
    dt_full = _expand(dt, e1_ref)
    a_full = _expand(a_cs, e1_ref)
    a_rep = _expand(a_cs, e2_ref)
    a_last = a_full[rows - 1:rows, :]

    xdt = xs * dt_full
    xdt_bf = xdt.astype(BF16)
    xw_bf = (xdt * jnp.exp2(a_last - a_full)).astype(BF16)
    skip = dsk_ref[...] * xs
    lane = lax.broadcasted_iota(jnp.int32, (rows, LANES), 1)
    first_half = lane < SSD_HEADDIM
    zg = _silu(p_ref[:, OFF_Z:OFF_Z + SSD_D])

    for g in range(SSD_GROUPS):
        gs = slice(g * gw, (g + 1) * gw)
        scores = jnp.where(tri, _dot_nt(c_bf[g], b_bf[g]), 0.0)
        st_old = st_ref[g]
        y_off = _dot(c_bf[g], st_old.astype(BF16)) * jnp.exp2(a_full[:, gs])
        st_ref[g] = st_old * jnp.exp2(a_last[:, gs]) + _dot_tn(b_bf[g], xw_bf[:, gs])
        parts = []
        for pair in range(hpg // 2):
            h0 = g * hpg + 2 * pair
            cols = slice(h0 * SSD_HEADDIM, (h0 + 2) * SSD_HEADDIM)
            x_pair = xdt_bf[:, cols]
            acc = None
            for hh, keep in ((h0, first_half), (h0 + 1, jnp.logical_not(first_half))):
                diff = a_rep[:, hh * LANES:hh * LANES + rows] - a_cs_t[hh:hh + 1, :]
                m = (scores * jnp.exp2(jnp.minimum(diff, 0.0))).astype(BF16)
                part = _dot(m, jnp.where(keep, x_pair, jnp.zeros_like(x_pair)))
                acc = part if acc is None else acc + part
            parts.append(acc)
        y = jnp.concatenate(parts, axis=1) + y_off + skip[:, gs]
        y_ref[:, gs] = (_rms_unit(y * zg[:, gs]) * nw_ref[:, gs]).astype(y_ref.dtype)


def _hgrn_chunk(p_ref, lb_ref, nw_ref, o_ref, st_ref, a_ref, kk_ref, gcs_ref):
    rows = p_ref.shape[0]
    dk = HG_DK
    lb = lb_ref[...]
    sig = _sigmoid(p_ref[:, OFF_F:OFF_F + HG_D])
    logf = jnp.log2(lb + (1.0 - lb) * sig)
    kk = (1.0 - lb) * (1.0 - sig)
    tri = _tri(rows)
    gcs = _cumsum_rows(logf, tri.astype(BF16))
    glast = gcs[rows - 1:rows, :]
    qt = p_ref[:, OFF_Q:OFF_Q + HG_D] * jnp.exp2(gcs)
    kk_ref[...] = kk
    gcs_ref[...] = gcs

    kt = kk * jnp.exp2(-gcs)
    khat = kk * jnp.exp2(glast - gcs)
    for h in range(HG_HEADS):
        sl = slice(h * dk, (h + 1) * dk)
        a_ref[h] = _dot_nt(qt[:, sl].astype(BF16), kt[:, sl].astype(BF16))

    @pl.when(jnp.min(glast) < -HG_SAFE_LOG_DECAY * LOG2E)
    def _():
        ri = lax.broadcasted_iota(jnp.int32, (rows, dk), 0)
        ci = lax.broadcasted_iota(jnp.int32, (rows, rows), 1)
        for h in range(HG_HEADS):
            sl = slice(h * dk, (h + 1) * dk)
            q_h = p_ref[:, OFF_Q + h * dk:OFF_Q + (h + 1) * dk]

            def cols(j8, a_h):
                base = pl.multiple_of(j8 * SUBLANES, SUBLANES)
                k_blk = kk_ref[pl.ds(base, SUBLANES), sl]
                g_blk = gcs_ref[pl.ds(base, SUBLANES), sl]
                g_h = gcs_ref[:, sl]
                for r in range(SUBLANES):
                    j = base + r
                    t = q_h * k_blk[r:r + 1] * jnp.exp2(
                        jnp.where(ri >= j, g_h - g_blk[r:r + 1], -jnp.inf))
                    a_h = jnp.where(ci == j, jnp.sum(t, axis=-1, keepdims=True), a_h)
                return a_h

            a_ref[h] = lax.fori_loop(0, rows // SUBLANES, cols, jnp.zeros((rows, rows), F32))

    gate = _silu(p_ref[:, OFF_G:OFF_G + HG_D])
    for h in range(HG_HEADS):
        sl = slice(h * dk, (h + 1) * dk)
        st = st_ref[h]
        v_bf = p_ref[:, OFF_V + h * HG_DV:OFF_V + (h + 1) * HG_DV].astype(BF16)
        a_h = jnp.where(tri, a_ref[h], 0.0).astype(BF16)
        o = _dot(a_h, v_bf) + _dot_nt(qt[:, sl].astype(BF16), st.astype(BF16))
        st_ref[h] = st * jnp.exp2(glast[:, sl]) + _dot_tn(v_bf, khat[:, sl].astype(BF16))
        o_ref[:, sl] = (_rms_unit(o) * nw_ref[:, sl] * gate[:, sl]).astype(o_ref.dtype)


def _mixers_kernel(x_ref, xn_ref, nw_ref, wz_ref, wxbc_ref, wq_ref, wf_ref, wv_ref, wg_ref, wdt_ref,
                   cw_ref, cb_ref, dtb_ref, alog_ref, dsk_ref, snw_ref, e1_ref, e2_ref, lb_ref, hnw_ref,
                   y_ref, o_ref, cout_ref, sout_ref, hout_ref,
                   pa_ref, pb_ref, xc_ref, st_ref, hst_ref, a_ref, kk_ref, gcs_ref, *, chunks):
    s = pl.program_id(0)
    blk = pa_ref.shape[0]
    per_blk = blk // CHUNK
    c0 = (s * 2 * per_blk) % chunks
    w_refs = (wz_ref, wxbc_ref, wq_ref, wf_ref, wv_ref, wg_ref, wdt_ref)
    hpg = SSD_HEADS // SSD_GROUPS

    @pl.when(s == 0)
    def _():
        _project(x_ref[0:blk, :], nw_ref, w_refs, pa_ref)

    @pl.when(c0 == 0)
    def _():
        xc_ref[0:SUBLANES, :] = jnp.zeros((SUBLANES, CONV_DIM), F32)
        st_ref[...] = jnp.zeros(st_ref.shape, F32)
        hst_ref[...] = jnp.zeros(hst_ref.shape, F32)

    for half, (cur_ref, nxt_ref) in enumerate(((pa_ref, pb_ref), (pb_ref, pa_ref))):
        x_next = x_ref[blk:2 * blk, :] if half == 0 else xn_ref[0:blk, :]
        _project(x_next, nw_ref, w_refs, nxt_ref)
        for k in range(per_blk):
            p_ref = cur_ref.at[k * CHUNK:(k + 1) * CHUNK]
            rs = slice((half * per_blk + k) * CHUNK, (half * per_blk + k + 1) * CHUNK)
            _ssd_chunk(p_ref, cw_ref, cb_ref, dtb_ref, alog_ref, dsk_ref, snw_ref, e1_ref, e2_ref,
                       y_ref.at[rs], cout_ref.at[0], xc_ref, st_ref)
            _hgrn_chunk(p_ref, lb_ref, hnw_ref, o_ref.at[rs], hst_ref, a_ref, kk_ref, gcs_ref)

    @pl.when(c0 + 2 * per_blk == chunks)
    def _():
        for g in range(SSD_GROUPS):
            sout_ref[0, g * hpg:(g + 1) * hpg] = st_ref[g].T.reshape(hpg, SSD_HEADDIM, D_STATE)
        for h in range(HG_HEADS):
            hout_ref[0, h] = hst_ref[h].T


def _mixers(x, b, l, w):
    n, d = x.shape
    chunks = l // CHUNK
    step_rows = 2 * PROJ_ROWS
    assert PROJ_ROWS % CHUNK == 0 and l % step_rows == 0
    steps = n // step_rows
    head = jnp.arange(LANES)[:, None]
    e1 = (head == jnp.arange(SSD_D)[None, :] // SSD_HEADDIM).astype(BF16)
    e2 = (head == jnp.arange(SSD_HEADS * LANES)[None, :] // LANES).astype(BF16)
    consts = (w["norm_mix_w"], *w["w_in"], w["conv_w"], w["conv_b"], w["dt_bias"], w["a_log"],
              w["d_skip"], w["ssd_norm_w"], e1, e2, w["lb"], w["hg_norm_w"])
    tok = lambda width: pl.BlockSpec((step_rows, width), lambda s: (s, 0))
    per_seq = lambda shape: pl.BlockSpec((1,) + shape,
                                         lambda s: (s * step_rows // l,) + (0,) * len(shape))
    ssm_shape = (SSD_HEADS, SSD_HEADDIM, D_STATE)
    hg_shape = (HG_HEADS, HG_DK, HG_DV)
    return pl.pallas_call(
        functools.partial(_mixers_kernel, chunks=chunks),
        grid=(steps,),
        in_specs=[tok(d), pl.BlockSpec((step_rows, d), lambda s: (jnp.minimum(s + 1, steps - 1), 0))]
                 + [_const_spec(c.shape) for c in consts],
        out_specs=[tok(SSD_D), tok(HG_D), per_seq((CONV_W - 1, CONV_DIM)), per_seq(ssm_shape),
                   per_seq(hg_shape)],
        out_shape=[jax.ShapeDtypeStruct((n, SSD_D), BF16), jax.ShapeDtypeStruct((n, HG_D), BF16),
                   jax.ShapeDtypeStruct((b, CONV_W - 1, CONV_DIM), F32),
                   jax.ShapeDtypeStruct((b,) + ssm_shape, F32),
                   jax.ShapeDtypeStruct((b,) + hg_shape, F32)],
        scratch_shapes=[pltpu.VMEM((PROJ_ROWS, PROJ_COLS), F32), pltpu.VMEM((PROJ_ROWS, PROJ_COLS), F32),
                        pltpu.VMEM((CHUNK + SUBLANES, CONV_DIM), F32),
                        pltpu.VMEM((SSD_GROUPS, D_STATE, SSD_D // SSD_GROUPS), F32),
                        pltpu.VMEM((HG_HEADS, HG_DV, HG_DK), F32),
                        pltpu.VMEM((HG_HEADS, CHUNK, CHUNK), F32),
                        pltpu.VMEM((CHUNK, HG_D), F32), pltpu.VMEM((CHUNK, HG_D), F32)],
        compiler_params=_cparams("arbitrary"),
    )(x, x, *consts)


def _tokens_to_lanes(x):
    tb, n = x.shape
    return jnp.concatenate([x, jnp.zeros((LANES - tb, n), F32)], axis=0).T


def _step_decay_kernel(dt_ref, dtb_ref, alog_ref, da_ref):
    da_ref[...] = jnp.exp(_softplus(dt_ref[...] + dtb_ref[...]) * (-jnp.exp(alog_ref[...])))


def _ssd_step_kernel(da_ref, z_ref, xbc_ref, dt_ref, cw_ref, cb_ref, dtb_ref, dsk_ref, nw_ref, e1_ref,
                     cs_ref, st_ref, y_ref, cnew_ref, snew_ref, *, tb):
    cd = CONV_DIM
    gn = D_STATE
    hpg = SSD_HEADS // SSD_GROUPS
    gw = hpg * SSD_HEADDIM
    first = pl.program_id(0) * tb
    x_in = xbc_ref[...]
    acc = cb_ref[...] + cw_ref[CONV_W - 1:CONV_W, :] * x_in
    for k in range(CONV_W - 1):
        acc = acc + cw_ref[k:k + 1, :] * cs_ref[:, k * cd:(k + 1) * cd]
    cnew_ref[:, :(CONV_W - 2) * cd] = cs_ref[:, cd:]
    cnew_ref[:, (CONV_W - 2) * cd:] = x_in
    xbc = _silu(acc)
    xs = xbc[:, :SSD_D]
    dt = _softplus(dt_ref[...] + dtb_ref[...])
    xdt_t = _tokens_to_lanes(xs * _expand(dt, e1_ref)).astype(BF16)
    row_tok = lax.broadcasted_iota(jnp.int32, (LANES, tb * gn), 0)
    col_tok = lax.broadcasted_iota(jnp.int32, (LANES, tb * gn), 1) // gn
    lane = lax.broadcasted_iota(jnp.int32, (SSD_HEADDIM, LANES), 1)
    y_groups = []
    for g in range(SSD_GROUPS):
        b_g = xbc[:, SSD_D + g * gn:SSD_D + (g + 1) * gn]
        c_g = xbc[:, SSD_D + (SSD_GROUPS + g) * gn:SSD_D + (SSD_GROUPS + g + 1) * gn]
        b_wide = jnp.concatenate([jnp.tile(b_g, (1, tb)), jnp.zeros((LANES - tb, tb * gn), F32)], axis=0)
        b_diag = jnp.where(row_tok == col_tok, b_wide, 0.0).astype(BF16)
        upd = _dot(xdt_t[g * gw:(g + 1) * gw, :], b_diag)
        y_heads = []
        for hl in range(hpg):
            h = g * hpg + hl
            y_h = jnp.zeros((SSD_HEADDIM, LANES), F32)
            for j in range(tb):
                new = (st_ref[j, h] * da_ref[first + j, h]
                       + upd[hl * SSD_HEADDIM:(hl + 1) * SSD_HEADDIM, j * gn:(j + 1) * gn])
                snew_ref[j, h] = new
                y_h = jnp.where(lane == j, jnp.sum(new * c_g[j:j + 1, :], axis=-1, keepdims=True), y_h)
            y_heads.append(y_h)
        y_groups.append(jnp.concatenate(y_heads, axis=0))
    y = jnp.concatenate(y_groups, axis=0).T[:tb]
    y = (y + dsk_ref[...] * xs) * _silu(z_ref[...])
    for g in range(SSD_GROUPS):
        gs = slice(g * gw, (g + 1) * gw)
        y_ref[:, gs] = (_rms_unit(y[:, gs]) * nw_ref[:, gs]).astype(y_ref.dtype)


def _ssd_step(z, xbc, dt, conv_w, conv_b, dt_bias, a_log, d_skip, norm_w, conv_init, ssm_init):
    b = z.shape[0]
    tb = SUBLANES
    assert b % tb == 0
    da = pl.pallas_call(
        _step_decay_kernel,
        grid=(1,),
        in_specs=[_const_spec(dt.shape), _const_spec(dt_bias.shape), _const_spec(a_log.shape)],
        out_specs=_const_spec(dt.shape),
        out_shape=jax.ShapeDtypeStruct(dt.shape, F32),
    )(dt, dt_bias, a_log)
    head = jnp.arange(LANES)[:, None]
    e1 = (head == jnp.arange(SSD_D)[None, :] // SSD_HEADDIM).astype(BF16)
    params = (conv_w, conv_b, dt_bias, d_skip, norm_w, e1)
    hist = (CONV_W - 1) * CONV_DIM
    row = lambda w: pl.BlockSpec((tb, w), lambda i, da: (i, 0))
    const = lambda shape: pl.BlockSpec(shape, lambda i, da: (0,) * len(shape))
    state_shape = (SSD_HEADS, SSD_HEADDIM, D_STATE)
    st_spec = pl.BlockSpec((tb,) + state_shape, lambda i, da: (i, 0, 0, 0))
    y, conv_new, ssm_new = pl.pallas_call(
        functools.partial(_ssd_step_kernel, tb=tb),
        grid_spec=pltpu.PrefetchScalarGridSpec(
            num_scalar_prefetch=1,
            grid=(b // tb,),
            in_specs=[row(SSD_D), row(CONV_DIM), row(LANES)] + [const(p.shape) for p in params]
                     + [row(hist), st_spec],
            out_specs=[row(SSD_D), row(hist), st_spec]),
        out_shape=[jax.ShapeDtypeStruct((b, SSD_D), BF16), jax.ShapeDtypeStruct((b, hist), F32),
                   jax.ShapeDtypeStruct((b,) + state_shape, F32)],
        compiler_params=_cparams("arbitrary"),
    )(da, z, xbc, dt, *params, conv_init.reshape(b, hist), ssm_init)
    return y, conv_new.reshape(b, CONV_W - 1, CONV_DIM), ssm_new


def _hgrn_step_kernel(q_ref, f_ref, v_ref, g_ref, lb_ref, nw_ref, st_ref, o_ref, snew_ref, *, tb):
    dk, dv = HG_DK, HG_DV
    fr = f_ref[...]
    lb = lb_ref[...]
    sig = _sigmoid(fr)
    f_t = _tokens_to_lanes(lb + (1.0 - lb) * sig)
    k_t = _tokens_to_lanes((1.0 - lb) * (1.0 - sig)).astype(BF16)
    q_bf = q_ref[...].astype(BF16)
    v = v_ref[...]
    gate = _silu(g_ref[...])
    row_tok = lax.broadcasted_iota(jnp.int32, (LANES, tb * dv), 0)
    col_tok = lax.broadcasted_iota(jnp.int32, (LANES, tb * dv), 1) // dv
    sub = lax.broadcasted_iota(jnp.int32, (tb, dv), 0)
    for h in range(HG_HEADS):
        rs = slice(h * dk, (h + 1) * dk)
        vs = slice(h * dv, (h + 1) * dv)
        v_wide = jnp.concatenate([jnp.tile(v[:, vs], (1, tb)), jnp.zeros((LANES - tb, tb * dv), F32)], axis=0)
        v_diag = jnp.where(row_tok == col_tok, v_wide, 0.0).astype(BF16)
        upd = _dot(k_t[rs, :], v_diag)
        o_h = jnp.zeros((tb, dv), F32)
        for j in range(tb):
            fcol = jnp.broadcast_to(f_t[rs, j:j + 1], (dk, dv))
            new = st_ref[j, h] * fcol + upd[:, j * dv:(j + 1) * dv]
            snew_ref[j, h] = new
            o_h = jnp.where(sub == j, _dot(q_bf[:, rs], new.astype(BF16)), o_h)
        o_ref[:, vs] = (_rms_unit(o_h) * nw_ref[:, vs] * gate[:, vs]).astype(o_ref.dtype)


def _hgrn_step(q, f, v, g, lb, norm_w, init):
    b = q.shape[0]
    tb = SUBLANES
    assert b % tb == 0
    row = pl.BlockSpec((tb, HG_D), lambda i: (i, 0))
    state_shape = (HG_HEADS, HG_DK, HG_DV)
    st_spec = pl.BlockSpec((tb,) + state_shape, lambda i: (i, 0, 0, 0))
    return pl.pallas_call(
        functools.partial(_hgrn_step_kernel, tb=tb),
        grid=(b // tb,),
        in_specs=[row, row, row, row, _const_spec(lb.shape), _const_spec(norm_w.shape), st_spec],
        out_specs=[row, st_spec],
        out_shape=[jax.ShapeDtypeStruct((b, HG_D), BF16),
                   jax.ShapeDtypeStruct((b,) + state_shape, F32)],
        compiler_params=_cparams("arbitrary"),
    )(q, f, v, g, lb, norm_w, init)


def _route_t(logits, bias, group=None):
    row = lax.broadcasted_iota(jnp.int32, logits.shape, 0)
    ninf = -jnp.inf
    big = jnp.int32(ROUTER_ROWS)
    is_g = (row >= N_EXPERTS) & (row < N_EXPERTS + N_EGROUPS)
    cmax = lambda t: jnp.max(t, axis=0, keepdims=True)
    csum = lambda t: jnp.sum(t, axis=0, keepdims=True)
    first = lambda m: jnp.min(jnp.where(m, row, big), axis=0, keepdims=True)

    gl = jnp.where(is_g, logits, ninf)
    gp = jnp.exp(gl - cmax(gl))
    gprob = gp / csum(gp)
    biased = logits + bias
    if group is None:
        gb = jnp.where(is_g, biased, ninf)
        gsel = first(gb == cmax(gb))
    else:
        gsel = group + N_EXPERTS
    gw = csum(jnp.where(row == gsel, gprob, 0.0))
    e0 = (gsel - N_EXPERTS) * EXPERTS_PER_GROUP
    in_grp = (row >= e0) & (row < e0 + EXPERTS_PER_GROUP)
    el = jnp.where(in_grp, logits, ninf)
    ep = jnp.exp(el - cmax(el))
    eprob = ep / csum(ep)
    eb = jnp.where(in_grp, biased, ninf)
    i1 = first(eb == cmax(eb))
    eb2 = jnp.where(row == i1, ninf, eb)
    i2 = first(eb2 == cmax(eb2))
    p1 = csum(jnp.where(row == i1, eprob, 0.0))
    p2 = csum(jnp.where(row == i2, eprob, 0.0))
    den = p1 + p2
    a = jnp.minimum(i1, i2) - e0
    b = jnp.maximum(i1, i2) - e0
    group = jnp.broadcast_to(gsel - N_EXPERTS, (1, logits.shape[1])).astype(F32)
    pair = jnp.zeros_like(group)
    for k, (u, v) in enumerate(PAIR_ORDER):
        pair = pair + jnp.where((a == min(u, v)) & (b == max(u, v)), float(k), 0.0)
    cls = group * PAIRS_PER_GROUP + pair
    return (jnp.where(row == i1, gw * (p1 / den), 0.0)
            + jnp.where(row == i2, gw * (p2 / den), 0.0)
            + jnp.where(row == CLASS_ROW, cls, 0.0))


def _router_logits_t(h, rhi_ref, rlo_ref):
    h_hi = h.astype(BF16)
    h_lo = (h - h_hi.astype(F32)).astype(BF16)
    logits = _dot(h_hi, rhi_ref[...]) + _dot(h_hi, rlo_ref[...]) + _dot(h_lo, rhi_ref[...])
    return logits.T[:ROUTER_ROWS]


def _rows_to_lanes(gates_t):
    r, tm = gates_t.shape
    return jnp.concatenate([gates_t, jnp.zeros((LANES - r, tm), F32)], axis=0).T


def _outproj_kernel(x_ref, y_ref, o_ref, w_ref, nw_ref, rhi_ref, rlo_ref, rb_ref,
                    x1_ref, h_ref, r_ref, *, tiled_rows):
    dy = y_ref.shape[-1]
    tm, d = x_ref.shape
    rb = min(tm, OUT_PROJ_ROWS)
    for r0 in range(0, tm, rb):
        rs = slice(r0, r0 + rb)
        mix = _dot(y_ref[rs, :], w_ref[:dy, :]) + _dot(o_ref[rs, :], w_ref[dy:, :])
        x1 = x_ref[rs, :] + mix
        x1_ref[rs, :] = x1
        h = _rms_unit(x1) * nw_ref[...]
        gates_t = _route_t(_router_logits_t(h, rhi_ref, rlo_ref), rb_ref[...])
        if tiled_rows:
            h_ref[rs] = h.reshape(rb, d // LANES, LANES)
            r_ref[:, rs] = jnp.broadcast_to(gates_t[CLASS_ROW:CLASS_ROW + 1, :], (SUBLANES, rb))
        else:
            h_ref[rs, :] = h
            r_ref[rs, :] = _rows_to_lanes(gates_t)


def _out_proj(x, y, o, w, nw, r_hi, r_lo, r_bias, tm, tiled_rows):
    n, d = x.shape
    row = lambda width: pl.BlockSpec((tm, width), lambda i: (i, 0))
    consts = (w, nw, r_hi, r_lo, r_bias)
    if tiled_rows:
        h_spec = pl.BlockSpec((tm, d // LANES, LANES), lambda i: (i, 0, 0))
        h_shape = jax.ShapeDtypeStruct((n, d // LANES, LANES), F32)
        r_spec = pl.BlockSpec((SUBLANES, tm), lambda i: (i, 0))
        r_shape = jax.ShapeDtypeStruct((n // tm * SUBLANES, tm), F32)
    else:
        h_spec, h_shape = row(d), jax.ShapeDtypeStruct((n, d), F32)
        r_spec, r_shape = row(LANES), jax.ShapeDtypeStruct((n, LANES), F32)
    return pl.pallas_call(
        functools.partial(_outproj_kernel, tiled_rows=tiled_rows),
        grid=(n // tm,),
        in_specs=[row(d), row(y.shape[-1]), row(o.shape[-1])] + [_const_spec(c.shape) for c in consts],
        out_specs=[row(d), h_spec, r_spec],
        out_shape=[jax.ShapeDtypeStruct((n, d), F32), h_shape, r_shape],
        compiler_params=_cparams("arbitrary"),
    )(x, y, o, *consts)


def _pos_kernel(cls_ref, pos_ref, tab_ref, *, tile):
    rows, tm = cls_ref.shape
    cls = cls_ref[...]
    ri = lax.broadcasted_iota(jnp.int32, (tm, tm), 0)
    ci = lax.broadcasted_iota(jnp.int32, (tm, tm), 1)
    upper = (ri <= ci).astype(BF16)
    rr = lax.broadcasted_iota(jnp.int32, (rows, rows), 0)
    rc = lax.broadcasted_iota(jnp.int32, (rows, rows), 1)
    earlier = ((rc // SUBLANES < rr // SUBLANES) & (rc % SUBLANES == 0)).astype(BF16)
    tile_lane = lax.broadcasted_iota(jnp.int32, (1, LANES), 1).astype(F32)
    pos = jnp.zeros((rows, tm), F32)
    tiles_before = jnp.zeros((1, 1), F32)
    tile_class = jnp.zeros((1, LANES), F32)
    pad_start = jnp.zeros((1, LANES), F32)
    pad_len = jnp.zeros((1, LANES), F32)
    for c in range(N_CLASSES):
        onehot = jnp.where(cls == float(c), 1.0, 0.0)
        inc = _dot(onehot.astype(BF16), upper)
        rowtot = jnp.broadcast_to(inc[:, tm - 1:tm], (rows, tm))
        rowpre = _cumsum_rows(rowtot, earlier)
        cnt = rowpre[rows - 1:rows, 0:1] + rowtot[rows - 1:rows, 0:1]
        pos = pos + onehot * (tiles_before * tile + rowpre + inc - 1.0)
        ntiles = jnp.floor((cnt + (tile - 1.0)) * (1.0 / tile))
        pad_start = pad_start + jnp.where(tile_lane == float(c), tiles_before * tile + cnt, 0.0)
        pad_len = pad_len + jnp.where(tile_lane == float(c), ntiles * tile - cnt, 0.0)
        tiles_before = tiles_before + ntiles
        tile_class = tile_class + jnp.where(tile_lane >= tiles_before, 1.0, 0.0)
    pos_ref[...] = pos.astype(jnp.int32)
    pad_start = pad_start + jnp.where(tile_lane == float(N_CLASSES), tiles_before * tile, 0.0)
    used = tile_class < N_CLASSES
    group = jnp.floor(tile_class * (1.0 / PAIRS_PER_GROUP))
    pair = tile_class - group * PAIRS_PER_GROUP
    a = jnp.zeros_like(pair)
    b = jnp.zeros_like(pair)
    for k, (u, v) in enumerate(PAIR_ORDER):
        a = a + jnp.where(pair == float(k), float(u), 0.0)
        b = b + jnp.where(pair == float(k), float(v), 0.0)
    e1 = jnp.where(used, group * EXPERTS_PER_GROUP + a, float(N_EXPERTS))
    e2 = jnp.where(used, group * EXPERTS_PER_GROUP + b, float(N_EXPERTS))
    sub = lax.broadcasted_iota(jnp.int32, (SUBLANES, LANES), 0)
    tab = jnp.where(sub == 0, e1, jnp.where(sub == 1, e2, jnp.where(sub == 2, pad_start, pad_len)))
    tab_ref[...] = tab.astype(jnp.int32)


def _positions(cls, n, tile):
    rows, tm = cls.shape
    assert n // tile + N_CLASSES <= LANES
    pos, tab = pl.pallas_call(
        functools.partial(_pos_kernel, tile=tile),
        grid=(1,),
        in_specs=[_const_spec(cls.shape)],
        out_specs=[_const_spec(cls.shape), _const_spec((SUBLANES, LANES))],
        out_shape=[jax.ShapeDtypeStruct(cls.shape, jnp.int32),
                   jax.ShapeDtypeStruct((SUBLANES, LANES), jnp.int32)],
        compiler_params=_cparams("arbitrary"),
    )(cls)
    return pos.reshape(rows // SUBLANES, SUBLANES, tm)[:, 0, :].reshape(n), tab[0], tab[1], tab[2], tab[3]


def _dispatch_kernel(pos_ref, pad_start_ref, pad_len_ref, h_ref, hs_ref, buf_ref, zero_ref, sem_ref,
                     zsem_ref, *, td):
    i = pl.program_id(0)
    nsteps = pl.num_programs(0)
    slot = i % 2

    tile = zero_ref.shape[0]
    n_sorted = hs_ref.shape[0]

    def pad_copies(act):
        used_end = pad_start_ref[N_CLASSES]
        for j in range(N_CLASSES):
            @pl.when(used_end + j * tile < n_sorted)
            def _(j=j):
                act(pltpu.make_async_copy(zero_ref, hs_ref.at[pl.ds(used_end + j * tile, tile)],
                                          zsem_ref.at[0]))

        def body(c, carry):
            length = pad_len_ref[c]
            off = pad_start_ref[c]
            piece = tile // 2
            while piece >= 1:
                @pl.when((length & piece) != 0)
                def _(off=off, piece=piece):
                    act(pltpu.make_async_copy(zero_ref.at[0:piece], hs_ref.at[pl.ds(off, piece)],
                                              zsem_ref.at[0]))
                off = off + (length & piece)
                piece //= 2
            return carry
        lax.fori_loop(0, N_CLASSES, body, 0)

    @pl.when(i == 0)
    def _():
        zero_ref[...] = jnp.zeros(zero_ref.shape, F32)
        pad_copies(lambda cp: cp.start())

    def row_copy(step, s, r):
        return pltpu.make_async_copy(buf_ref.at[s, r], hs_ref.at[pos_ref[step * td + r]], sem_ref.at[s])

    def start_all(step, s):
        def body(k, carry):
            for u in range(DMA_QUEUES):
                row_copy(step, s, DMA_QUEUES * k + u).start(priority=u)
            return carry
        lax.fori_loop(0, td // DMA_QUEUES, body, 0, unroll=4)

    def wait_all(step, s):
        def body(r, carry):
            row_copy(step, s, r).wait()
            return carry
        lax.fori_loop(0, td, body, 0, unroll=8)

    @pl.when(i >= 2)
    def _():
        wait_all(i - 2, slot)

    buf_ref[slot] = h_ref[...]
    start_all(i, slot)

    @pl.when(i == nsteps - 1)
    def _():
        @pl.when(i >= 1)
        def _():
            wait_all(i - 1, 1 - slot)
        wait_all(i, slot)
        pad_copies(lambda cp: cp.wait())


def _dispatch(h3, pos, pad_start, pad_len, n_sorted, tile, td):
    n = h3.shape[0]
    tok = h3.shape[1:]
    return pl.pallas_call(
        functools.partial(_dispatch_kernel, td=td),
        grid_spec=pltpu.PrefetchScalarGridSpec(
            num_scalar_prefetch=3,
            grid=(n // td,),
            in_specs=[pl.BlockSpec((td,) + tok, lambda i, *_: (i, 0, 0))],
            out_specs=pl.BlockSpec(memory_space=pl.ANY),
            scratch_shapes=[pltpu.VMEM((2, td) + tok, F32), pltpu.VMEM((tile,) + tok, F32),
                            pltpu.SemaphoreType.DMA((2,)), pltpu.SemaphoreType.DMA((1,))]),
        out_shape=jax.ShapeDtypeStruct((n_sorted,) + tok, F32),
        compiler_params=_cparams("arbitrary"),
    )(pos, pad_start, pad_len, h3)


def _expert(h, gates, e, wg, wu, wd):
    lane = lax.broadcasted_iota(jnp.int32, gates.shape, 1)
    gcol = jnp.sum(jnp.where(lane == e, gates, 0.0), axis=-1, keepdims=True)
    act = _silu(_dot(h, wg.astype(BF16))) * _dot(h, wu.astype(BF16)) * gcol
    return _dot(act.astype(BF16), wd.astype(BF16))


def _moe_dense_kernel(h_ref, gates_ref, wg_ref, wu_ref, wd_ref, out_ref):
    e = pl.program_id(1)
    part = _expert(h_ref[...].astype(BF16), gates_ref[...], e, wg_ref[0], wu_ref[0], wd_ref[0])

    @pl.when(e == 0)
    def _():
        out_ref[...] = part

    @pl.when(e > 0)
    def _():
        out_ref[...] += part


def _moe_dense(h, gates, wg, wu, wd, tm):
    n, d = h.shape
    ne, _, ff = wg.shape
    row = lambda width: pl.BlockSpec((tm, width), lambda i, e: (i, 0))
    return pl.pallas_call(
        _moe_dense_kernel,
        grid=(n // tm, ne),
        in_specs=[row(d), row(LANES),
                  pl.BlockSpec((1, d, ff), lambda i, e: (e, 0, 0)),
                  pl.BlockSpec((1, d, ff), lambda i, e: (e, 0, 0)),
                  pl.BlockSpec((1, ff, d), lambda i, e: (e, 0, 0))],
        out_specs=row(d),
        out_shape=jax.ShapeDtypeStruct((n, d), F32),
        compiler_params=_cparams("arbitrary", "arbitrary"),
    )(h, gates, wg, wu, wd)


def _moe_sorted_kernel(e1_ref, e2_ref, hs_ref, rhi_ref, rlo_ref, rb_ref,
                       wg1_ref, wu1_ref, wd1_ref, wg2_ref, wu2_ref, wd2_ref, ys_ref):
    e1 = e1_ref[pl.program_id(0)]
    e2 = e2_ref[pl.program_id(0)]
    tm = hs_ref.shape[0]
    d = wg1_ref.shape[1]

    @pl.when(e1 < N_EXPERTS)
    def _():
        h = hs_ref[...].reshape(tm, d)
        logits_t = _router_logits_t(h, rhi_ref, rlo_ref)
        group = lax.shift_right_logical(e1, jnp.int32(EXPERTS_PER_GROUP.bit_length() - 1))
        gates = _rows_to_lanes(_route_t(logits_t, rb_ref[...], group=group))
        h_bf = h.astype(BF16)
        acc = (_expert(h_bf, gates, e1, wg1_ref[0], wu1_ref[0], wd1_ref[0])
               + _expert(h_bf, gates, e2, wg2_ref[0], wu2_ref[0], wd2_ref[0]))
        ys_ref[...] = acc.reshape(ys_ref.shape)

    @pl.when(e1 >= N_EXPERTS)
    def _():
        ys_ref[...] = jnp.zeros(ys_ref.shape, F32)


def _moe_sorted(hs, e1_tab, e2_tab, r_hi, r_lo, r_bias, wg, wu, wd, tm):
    n_sorted = hs.shape[0]
    ne, d, ff = wg.shape
    first = lambda i, e1, e2: (jnp.minimum(e1[i], ne - 1), 0, 0)
    second = lambda i, e1, e2: (jnp.minimum(e2[i], ne - 1), 0, 0)
    tok = pl.BlockSpec((tm,) + hs.shape[1:], lambda i, e1, e2: (i, 0, 0))
    const = lambda shape: pl.BlockSpec(shape, lambda i, e1, e2: (0,) * len(shape))
    up = lambda sel: pl.BlockSpec((1, d, ff), sel)
    down = lambda sel: pl.BlockSpec((1, ff, d), sel)
    return pl.pallas_call(
        _moe_sorted_kernel,
        grid_spec=pltpu.PrefetchScalarGridSpec(
            num_scalar_prefetch=2,
            grid=(n_sorted // tm,),
            in_specs=[tok, const(r_hi.shape), const(r_lo.shape), const(r_bias.shape),
                      up(first), up(first), down(first), up(second), up(second), down(second)],
            out_specs=tok),
        out_shape=jax.ShapeDtypeStruct(hs.shape, F32),
        compiler_params=_cparams("arbitrary"),
    )(e1_tab, e2_tab, hs, r_hi, r_lo, r_bias, wg, wu, wd, wg, wu, wd)


def _ple_math(x, p, npw_ref, wg_ref, wp_ref, fw_ref):
    hn = (_rms_unit(x) * npw_ref[...]).astype(BF16)
    gate = _sigmoid(_dot(hn, wg_ref[...]))
    x = x + gate * _dot(p.astype(BF16), wp_ref[...])
    return _rms_unit(x) * fw_ref[...]


def _ple_kernel(x1_ref, moe_ref, p_ref, npw_ref, wg_ref, wp_ref, fw_ref, y_ref):
    y_ref[...] = _ple_math(x1_ref[...] + moe_ref[...], p_ref[...], npw_ref, wg_ref, wp_ref, fw_ref)


def _ple_gather_kernel(pos_ref, x1_ref, ys_ref, p_ref, npw_ref, wg_ref, wp_ref, fw_ref, y_ref,
                       buf_ref, sem_ref, *, tm):
    i = pl.program_id(0)
    nsteps = pl.num_programs(0)
    slot = i % 2

    def row_copy(step, s, r):
        return pltpu.make_async_copy(ys_ref.at[pos_ref[step * tm + r]], buf_ref.at[s, r], sem_ref.at[s])

    def start_all(step, s):
        def body(k, carry):
            for u in range(DMA_QUEUES):
                row_copy(step, s, DMA_QUEUES * k + u).start(priority=u)
            return carry
        lax.fori_loop(0, tm // DMA_QUEUES, body, 0, unroll=4)

    @pl.when(i == 0)
    def _():
        start_all(0, 0)

    @pl.when(i + 1 < nsteps)
    def _():
        start_all(i + 1, 1 - slot)

    def wait_row(r, carry):
        row_copy(i, slot, r).wait()
        return carry
    lax.fori_loop(0, tm, wait_row, 0, unroll=8)

    moe = buf_ref[slot].reshape(x1_ref.shape)
    y_ref[...] = _ple_math(x1_ref[...] + moe, p_ref[...], npw_ref, wg_ref, wp_ref, fw_ref)


def _ple(x1, moe, pos, p, npw, wg, wp, fw, tm):
    n, d = x1.shape
    consts = (npw, wg, wp, fw)
    out_shape = jax.ShapeDtypeStruct((n, d), F32)
    if pos is None:
        row = lambda width: pl.BlockSpec((tm, width), lambda i: (i, 0))
        return pl.pallas_call(
            _ple_kernel,
            grid=(n // tm,),
            in_specs=[row(d), row(d), row(p.shape[-1])] + [_const_spec(c.shape) for c in consts],
            out_specs=row(d),
            out_shape=out_shape,
            compiler_params=_cparams("arbitrary"),
        )(x1, moe, p, *consts)
    row = lambda width: pl.BlockSpec((tm, width), lambda i, pos: (i, 0))
    const = lambda shape: pl.BlockSpec(shape, lambda i, pos: (0,) * len(shape))
    return pl.pallas_call(
        functools.partial(_ple_gather_kernel, tm=tm),
        grid_spec=pltpu.PrefetchScalarGridSpec(
            num_scalar_prefetch=1,
            grid=(n // tm,),
            in_specs=[row(d), pl.BlockSpec(memory_space=pl.ANY), row(p.shape[-1])]
                     + [const(c.shape) for c in consts],
            out_specs=row(d),
            scratch_shapes=[pltpu.VMEM((2, tm) + moe.shape[1:], F32), pltpu.SemaphoreType.DMA((2,))]),
        out_shape=out_shape,
        compiler_params=_cparams("arbitrary"),
    )(pos, x1, moe, p, *consts)


def _prepare_weights(norm_mix_w, w_in, conv_w, conv_b, dt_bias, a_log, d_skip, ssd_norm_w,
                     lb_logits, hg_norm_w, w_out, norm_ffn_w, w_router_group, b_router_group,
                     w_router_expert, b_router_expert, w_exp_gate, w_exp_up, w_exp_down,
                     norm_ple_w, w_ple_gate, w_ple_proj, final_norm_w, layer):
    i = layer
    row = lambda t: t.reshape(1, -1).astype(F32)
    lane_pad = lambda t: jnp.pad(t, [(0, 0)] * (t.ndim - 1) + [(0, LANES - t.shape[-1])])
    splits = (SSD_D, CONV_DIM, SSD_HEADS, HG_D, HG_D, HG_D, HG_D)
    offs = [0]
    for s in splits:
        offs.append(offs[-1] + s)
    seg = lambda k: w_in[i][:, offs[k]:offs[k + 1]].astype(BF16)
    w_in_segs = (seg(0), seg(1), seg(3), seg(4), seg(5), seg(6), lane_pad(seg(2)))
    w_router = lane_pad(jnp.concatenate([w_router_expert[i], w_router_group[i]], axis=1).astype(F32))
    r_hi = w_router.astype(BF16)
    r_lo = (w_router - r_hi.astype(F32)).astype(BF16)
    r_bias = jnp.pad(jnp.concatenate([b_router_expert[i].reshape(-1), b_router_group[i]]).astype(F32),
                     [(0, ROUTER_ROWS - N_EXPERTS - N_EGROUPS)]).reshape(ROUTER_ROWS, 1)
    lb = jnp.cumsum(jax.nn.softmax(lb_logits.astype(F32), axis=0), axis=0)[i]
    return dict(
        norm_mix_w=row(norm_mix_w[i]), w_in=w_in_segs, conv_w=conv_w[i].astype(F32),
        conv_b=row(conv_b[i]), dt_bias=lane_pad(row(dt_bias[i])), a_log=lane_pad(row(a_log[i])),
        d_skip=row(jnp.repeat(d_skip[i], SSD_HEADDIM)), ssd_norm_w=row(ssd_norm_w[i]),
        lb=row(lb), hg_norm_w=row(hg_norm_w[i]), w_out=w_out[i].astype(BF16),
        norm_ffn_w=row(norm_ffn_w[i]), r_hi=r_hi, r_lo=r_lo, r_bias=r_bias,
        w_exp_gate=w_exp_gate[i], w_exp_up=w_exp_up[i], w_exp_down=w_exp_down[i],
        norm_ple_w=row(norm_ple_w[i]),
        w_ple_gate=w_ple_gate[i].astype(BF16), w_ple_proj=w_ple_proj[i].astype(BF16),
        final_norm_w=row(final_norm_w))


def _token_tile(n, cap):
    tm = cap
    while tm >= SUBLANES:
        if n % tm == 0:
            return tm
        tm //= 2
    raise ValueError(f"token count {n} must be a multiple of {SUBLANES}")


def _trunk(x, p, ssm_in, conv_in, hg_in, w):
    b, l, d = x.shape
    n = b * l
    tm = _token_tile(n, ROW_TILE)
    x2d = x.reshape(n, d)
    if l == 1 and ssm_in is not None:
        z, xbc, q, f, v, g, dt = _in_proj(x2d, w["norm_mix_w"], w["w_in"], _token_tile(n, IN_PROJ_TILE))
        y, conv_new, ssm_new = _ssd_step(z, xbc, dt, w["conv_w"], w["conv_b"], w["dt_bias"], w["a_log"],
                                         w["d_skip"], w["ssd_norm_w"], conv_in, ssm_in)
        o, hg_new = _hgrn_step(q, f, v, g, w["lb"], w["hg_norm_w"], hg_in)
    else:
        assert ssm_in is None and conv_in is None and hg_in is None, "multi-token groups start empty"
        y, o, conv_new, ssm_new, hg_new = _mixers(x2d, b, l, w)
    router = (w["r_hi"], w["r_lo"], w["r_bias"])
    experts = (w["w_exp_gate"], w["w_exp_up"], w["w_exp_down"])
    sorted_moe = n >= SORTED_MOE_MIN_TOKENS
    x1, h, routed = _out_proj(x2d, y.reshape(n, -1), o.reshape(n, -1), w["w_out"], w["norm_ffn_w"],
                              *router, tm, tiled_rows=sorted_moe)
    if sorted_moe:
        pos, e1_tab, e2_tab, pad_start, pad_len = _positions(routed, n, MOE_TILE)
        hs = _dispatch(h, pos, pad_start, pad_len, n + N_CLASSES * MOE_TILE, MOE_TILE, tm)
        moe = _moe_sorted(hs, e1_tab, e2_tab, *router, *experts, MOE_TILE)
    else:
        pos = None
        moe = _moe_dense(h, routed, *experts, tm)
    y_out = _ple(x1, moe, pos, p.reshape(n, -1), w["norm_ple_w"], w["w_ple_gate"], w["w_ple_proj"],
                 w["final_norm_w"], tm)
    return y_out.reshape(b, l, d), ssm_new[None], conv_new[None], hg_new[None]


def kernel(x_prompt, x_sample, state_ssm, state_conv, state_hgrn, p_prompt, p_sample, norm_mix_w, w_in, conv_w, conv_b, dt_bias, a_log, d_skip, ssd_norm_w, lb_logits, hg_norm_w, w_out, norm_ffn_w, w_router_group, b_router_group, w_router_expert, b_router_expert, w_exp_gate, w_exp_up, w_exp_down, norm_ple_w, w_ple_gate, w_ple_proj, final_norm_w):
    assert p_prompt.shape[0] == 1, "the per-layer-embedding kernel also applies the final norm: depth 1 only"
    w = _prepare_weights(norm_mix_w, w_in, conv_w, conv_b, dt_bias, a_log, d_skip, ssd_norm_w,
                         lb_logits, hg_norm_w, w_out, norm_ffn_w, w_router_group, b_router_group,
                         w_router_expert, b_router_expert, w_exp_gate, w_exp_up, w_exp_down,
                         norm_ple_w, w_ple_gate, w_ple_proj, final_norm_w, layer=0)
    y_p, ssm_p, conv_p, hg_p = _trunk(x_prompt, p_prompt[0], None, None, None, w)
    y_s, ssm_s, conv_s, hg_s = _trunk(x_sample, p_sample[0], state_ssm[0], state_conv[0],
                                      state_hgrn[0], w)
    return (y_p, y_s, ssm_p, conv_p, hg_p, ssm_s, conv_s, hg_s)
```

```python
import functools

import jax
import jax.numpy as jnp
from jax import lax
from jax.experimental import pallas as pl
from jax.experimental.pallas import tpu as pltpu

F32 = jnp.float32
BF16 = jnp.bfloat16

EPS = 1e-6
SSD_HEADS = 16
SSD_HEADDIM = 64
SSD_D = SSD_HEADS * SSD_HEADDIM
SSD_GROUPS = 2
D_STATE = 128
CONV_W = 4
CONV_DIM = SSD_D + 2 * SSD_GROUPS * D_STATE
HG_HEADS = 8
HG_DK = 128
HG_DV = 128
HG_D = HG_HEADS * HG_DV
N_EGROUPS = 4
EXPERTS_PER_GROUP = 4
N_EXPERTS = N_EGROUPS * EXPERTS_PER_GROUP

LANES = 128
SUBLANES = 8
VMEM_LIMIT = 48 * 1024 * 1024
CHUNK = 128
PROJ_ROWS = 256
HG_SAFE_LOG_DECAY = 60.0
ROUTER_ROWS = 32
CLASS_ROW = N_EXPERTS + N_EGROUPS
PAIRS_PER_GROUP = EXPERTS_PER_GROUP * (EXPERTS_PER_GROUP - 1) // 2
N_CLASSES = N_EGROUPS * PAIRS_PER_GROUP
PAIR_ORDER = ((0, 1), (2, 1), (2, 3), (0, 3), (0, 2), (1, 3))
assert len({frozenset(p) for p in PAIR_ORDER}) == PAIRS_PER_GROUP
MOE_TILE = 256
IN_PROJ_TILE = 256
ROW_TILE = 512
OUT_PROJ_ROWS = 256
LOG2E = 1.4426950408889634
SORTED_MOE_MIN_TOKENS = 8 * SUBLANES * LANES


def _dot(a, b):
    return jnp.dot(a, b, preferred_element_type=F32)


def _dot_nt(a, b):
    return lax.dot_general(a, b, (((1,), (1,)), ((), ())), preferred_element_type=F32)


def _dot_tn(a, b):
    return lax.dot_general(a, b, (((0,), (0,)), ((), ())), preferred_element_type=F32)


def _rms_unit(x):
    return x * lax.rsqrt(jnp.mean(x * x, axis=-1, keepdims=True) + EPS)


def _sigmoid(x):
    return 1.0 / (1.0 + jnp.exp(-x))


def _softplus(x):
    return jnp.maximum(x, 0.0) + jnp.log(1.0 + jnp.exp(-jnp.abs(x)))


def _silu(x):
    return x * _sigmoid(x)


def _split3(x):
    hi = x.astype(BF16)
    r = x - hi.astype(F32)
    mid = r.astype(BF16)
    lo = (r - mid.astype(F32)).astype(BF16)
    return hi, mid, lo


def _cumsum_rows(x, tri):
    hi, mid, lo = _split3(x)
    return _dot(tri, hi) + _dot(tri, mid) + _dot(tri, lo)


def _tri(n):
    r = lax.broadcasted_iota(jnp.int32, (n, n), 0)
    c = lax.broadcasted_iota(jnp.int32, (n, n), 1)
    return r >= c


def _cparams(*sem):
    return pltpu.CompilerParams(dimension_semantics=sem, vmem_limit_bytes=VMEM_LIMIT)


def _const_spec(shape):
    nd = len(shape)
    return pl.BlockSpec(shape, lambda *_: (0,) * nd, pipeline_mode=pl.Buffered(1))


def _inproj_kernel(x_ref, nw_ref, *refs):
    hb = (_rms_unit(x_ref[...]) * nw_ref[...]).astype(BF16)
    nseg = len(refs) // 2
    for w_ref, out_ref in zip(refs[:nseg], refs[nseg:]):
        n = out_ref.shape[-1]
        for c0 in range(0, n, 512):
            cw = min(512, n - c0)
            out_ref[:, c0:c0 + cw] = _dot(hb, w_ref[:, c0:c0 + cw])


def _in_proj(x, nw, ws, tm):
    n, d = x.shape
    widths = [w.shape[-1] for w in ws]
    return pl.pallas_call(
        _inproj_kernel,
        grid=(n // tm,),
        in_specs=[pl.BlockSpec((tm, d), lambda i: (i, 0)), _const_spec(nw.shape)]
                 + [_const_spec(w.shape) for w in ws],
        out_specs=[pl.BlockSpec((tm, s), lambda i: (i, 0)) for s in widths],
        out_shape=[jax.ShapeDtypeStruct((n, s), F32) for s in widths],
        compiler_params=_cparams("arbitrary"),
    )(x, nw, *ws)


def _heads_to_channels(t):
    rows = t.shape[0]
    first_half = lax.broadcasted_iota(jnp.int32, (rows, LANES), 1) < SSD_HEADDIM
    return jnp.concatenate(
        [jnp.where(first_half, jnp.broadcast_to(t[:, h:h + 1], (rows, LANES)),
                   jnp.broadcast_to(t[:, h + 1:h + 2], (rows, LANES)))
         for h in range(0, SSD_HEADS, LANES // SSD_HEADDIM)], axis=1)


PROJ_WIDTHS = (SSD_D, CONV_DIM, HG_D, HG_D, HG_D, HG_D, LANES)
OFF_Z, OFF_XBC, OFF_Q, OFF_F, OFF_V, OFF_G, OFF_DT = (sum(PROJ_WIDTHS[:k]) for k in range(len(PROJ_WIDTHS)))
PROJ_COLS = sum(PROJ_WIDTHS)


def _project(x, nw_ref, w_refs, dst_ref):
    hb = (_rms_unit(x) * nw_ref[...]).astype(BF16)
    off = 0
    for w_ref in w_refs:
        n = w_ref.shape[-1]
        for c0 in range(0, n, 512):
            cw = min(512, n - c0)
            dst_ref[:, off + c0:off + c0 + cw] = _dot(hb, w_ref[:, c0:c0 + cw])
        off += n


def _ssd_chunk(p_ref, cw_ref, cb_ref, dtb_ref, alog_ref, dsk_ref, nw_ref,
               y_ref, cout_ref, xc_ref, st_ref):
    rows = p_ref.shape[0]
    hist = SUBLANES
    hpg = SSD_HEADS // SSD_GROUPS
    gw = hpg * SSD_HEADDIM
    gn = D_STATE

    xc_ref[hist:hist + rows, :] = p_ref[:, OFF_XBC:OFF_XBC + CONV_DIM]
    base = hist - (CONV_W - 1)
    acc = cb_ref[...] + cw_ref[0:1, :] * xc_ref[base:base + rows, :]
    for k in range(1, CONV_W):
        acc = acc + cw_ref[k:k + 1, :] * xc_ref[base + k:base + k + rows, :]
    cout_ref[...] = xc_ref[base + rows:base + rows + CONV_W - 1, :]
    xc_ref[0:hist, :] = xc_ref[rows:rows + hist, :]
    xbc = _silu(acc)
    xs = xbc[:, :SSD_D]
    b_bf = [xbc[:, SSD_D + g * gn:SSD_D + (g + 1) * gn].astype(BF16) for g in range(SSD_GROUPS)]
    c_bf = [xbc[:, SSD_D + (SSD_GROUPS + g) * gn:SSD_D + (SSD_GROUPS + g + 1) * gn].astype(BF16)
            for g in range(SSD_GROUPS)]

    dt = _softplus(p_ref[:, OFF_DT:OFF_DT + LANES] + dtb_ref[...])
    a = dt * (-jnp.exp(alog_ref[...]))
    tri = _tri(rows)
    a_cs = _cumsum_rows(a, tri.astype(BF16)) * LOG2E
    a_cs_t = a_cs.T
    dt_full = _heads_to_channels(dt)
    a_full = _heads_to_channels(a_cs)
    first_half = lax.broadcasted_iota(jnp.int32, (rows, LANES), 1) < SSD_HEADDIM
    a_last = a_full[rows - 1:rows, :]

    xdt = xs * dt_full
    xdt_bf = xdt.astype(BF16)
    xw_bf = (xdt * jnp.exp2(a_last - a_full)).astype(BF16)
    skip = dsk_ref[...] * xs
    zg = _silu(p_ref[:, OFF_Z:OFF_Z + SSD_D])

    for g in range(SSD_GROUPS):
        gs = slice(g * gw, (g + 1) * gw)
        scores = jnp.where(tri, _dot_nt(c_bf[g], b_bf[g]), 0.0)
        st_old = st_ref[g]
        y_off = _dot(c_bf[g], st_old.astype(BF16)) * jnp.exp2(a_full[:, gs])
        st_ref[g] = st_old * jnp.exp2(a_last[:, gs]) + _dot_tn(b_bf[g], xw_bf[:, gs])
        parts = []
        for pair in range(hpg // 2):
            h0 = g * hpg + 2 * pair
            cols = slice(h0 * SSD_HEADDIM, (h0 + 2) * SSD_HEADDIM)
            x_pair = xdt_bf[:, cols]
            acc = None
            for hh, keep in ((h0, first_half), (h0 + 1, jnp.logical_not(first_half))):
                diff = jnp.broadcast_to(a_cs[:, hh:hh + 1], (rows, rows)) - a_cs_t[hh:hh + 1, :]
                m = (scores * jnp.exp2(jnp.minimum(diff, 0.0))).astype(BF16)
                part = _dot(m, jnp.where(keep, x_pair, jnp.zeros_like(x_pair)))
                acc = part if acc is None else acc + part
            parts.append(acc)
        y = jnp.concatenate(parts, axis=1) + y_off + skip[:, gs]
        y_ref[:, gs] = (_rms_unit(y * zg[:, gs]) * nw_ref[:, gs]).astype(y_ref.dtype)


def _hgrn_chunk(p_ref, lb_ref, nw_ref, o_ref, st_ref, a_ref, kk_ref, gcs_ref):
    rows = p_ref.shape[0]
    dk = HG_DK
    lb = lb_ref[...]
    sig = _sigmoid(p_ref[:, OFF_F:OFF_F + HG_D])
    logf = jnp.log2(lb + (1.0 - lb) * sig)
    kk = (1.0 - lb) * (1.0 - sig)
    tri = _tri(rows)
    gcs = _cumsum_rows(logf, tri.astype(BF16))
    glast = gcs[rows - 1:rows, :]
    qt = p_ref[:, OFF_Q:OFF_Q + HG_D] * jnp.exp2(gcs)
    kk_ref[...] = kk
    gcs_ref[...] = gcs

    kt = kk * jnp.exp2(-gcs)
    khat = kk * jnp.exp2(glast - gcs)
    for h in range(HG_HEADS):
        sl = slice(h * dk, (h + 1) * dk)
        a_ref[h] = _dot_nt(qt[:, sl].astype(BF16), kt[:, sl].astype(BF16))

    @pl.when(jnp.min(glast) < -HG_SAFE_LOG_DECAY * LOG2E)
    def _():
        ri = lax.broadcasted_iota(jnp.int32, (rows, dk), 0)
        ci = lax.broadcasted_iota(jnp.int32, (rows, rows), 1)
        for h in range(HG_HEADS):
            sl = slice(h * dk, (h + 1) * dk)
            q_h = p_ref[:, OFF_Q + h * dk:OFF_Q + (h + 1) * dk]

            def cols(j8, a_h):
                base = pl.multiple_of(j8 * SUBLANES, SUBLANES)
                k_blk = kk_ref[pl.ds(base, SUBLANES), sl]
                g_blk = gcs_ref[pl.ds(base, SUBLANES), sl]
                g_h = gcs_ref[:, sl]
                for r in range(SUBLANES):
                    j = base + r
                    t = q_h * k_blk[r:r + 1] * jnp.exp2(
                        jnp.where(ri >= j, g_h - g_blk[r:r + 1], -jnp.inf))
                    a_h = jnp.where(ci == j, jnp.sum(t, axis=-1, keepdims=True), a_h)
                return a_h

            a_ref[h] = lax.fori_loop(0, rows // SUBLANES, cols, jnp.zeros((rows, rows), F32))

    gate = _silu(p_ref[:, OFF_G:OFF_G + HG_D])
    for h in range(HG_HEADS):
        sl = slice(h * dk, (h + 1) * dk)
        st = st_ref[h]
        v_bf = p_ref[:, OFF_V + h * HG_DV:OFF_V + (h + 1) * HG_DV].astype(BF16)
        a_h = jnp.where(tri, a_ref[h], 0.0).astype(BF16)
        o = _dot(a_h, v_bf) + _dot_nt(qt[:, sl].astype(BF16), st.astype(BF16))
        st_ref[h] = st * jnp.exp2(glast[:, sl]) + _dot_tn(v_bf, khat[:, sl].astype(BF16))
        o_ref[:, sl] = (_rms_unit(o) * nw_ref[:, sl] * gate[:, sl]).astype(o_ref.dtype)


def _mixers_kernel(x_ref, xn_ref, nw_ref, wz_ref, wxbc_ref, wq_ref, wf_ref, wv_ref, wg_ref, wdt_ref,
                   cw_ref, cb_ref, dtb_ref, alog_ref, dsk_ref, snw_ref, lb_ref, hnw_ref,
                   y_ref, o_ref, cout_ref, sout_ref, hout_ref,
                   pa_ref, pb_ref, xc_ref, st_ref, hst_ref, a_ref, kk_ref, gcs_ref, *, chunks):
    s = pl.program_id(0)
    blk = pa_ref.shape[0]
    per_blk = blk // CHUNK
    c0 = (s * 2 * per_blk) % chunks
    w_refs = (wz_ref, wxbc_ref, wq_ref, wf_ref, wv_ref, wg_ref, wdt_ref)
    hpg = SSD_HEADS // SSD_GROUPS

    @pl.when(s == 0)
    def _():
        _project(x_ref[0:blk, :], nw_ref, w_refs, pa_ref)

    @pl.when(c0 == 0)
    def _():
        xc_ref[0:SUBLANES, :] = jnp.zeros((SUBLANES, CONV_DIM), F32)
        st_ref[...] = jnp.zeros(st_ref.shape, F32)
        hst_ref[...] = jnp.zeros(hst_ref.shape, F32)

    for half, (cur_ref, nxt_ref) in enumerate(((pa_ref, pb_ref), (pb_ref, pa_ref))):
        x_next = x_ref[blk:2 * blk, :] if half == 0 else xn_ref[0:blk, :]
        _project(x_next, nw_ref, w_refs, nxt_ref)
        for k in range(per_blk):
            p_ref = cur_ref.at[k * CHUNK:(k + 1) * CHUNK]
            rs = slice((half * per_blk + k) * CHUNK, (half * per_blk + k + 1) * CHUNK)
            _ssd_chunk(p_ref, cw_ref, cb_ref, dtb_ref, alog_ref, dsk_ref, snw_ref,
                       y_ref.at[rs], cout_ref.at[0], xc_ref, st_ref)
            _hgrn_chunk(p_ref, lb_ref, hnw_ref, o_ref.at[rs], hst_ref, a_ref, kk_ref, gcs_ref)

    @pl.when(c0 + 2 * per_blk == chunks)
    def _():
        for g in range(SSD_GROUPS):
            sout_ref[0, g * hpg:(g + 1) * hpg] = st_ref[g].T.reshape(hpg, SSD_HEADDIM, D_STATE)
        for h in range(HG_HEADS):
            hout_ref[0, h] = hst_ref[h].T


def _mixers(x, b, l, w):
    n, d = x.shape
    chunks = l // CHUNK
    step_rows = 2 * PROJ_ROWS
    assert PROJ_ROWS % CHUNK == 0 and l % step_rows == 0
    steps = n // step_rows
    consts = (w["norm_mix_w"], *w["w_in"], w["conv_w"], w["conv_b"], w["dt_bias"], w["a_log"],
              w["d_skip"], w["ssd_norm_w"], w["lb"], w["hg_norm_w"])
    tok = lambda width: pl.BlockSpec((step_rows, width), lambda s: (s, 0))
    per_seq = lambda shape: pl.BlockSpec((1,) + shape,
                                         lambda s: (s * step_rows // l,) + (0,) * len(shape))
    ssm_shape = (SSD_HEADS, SSD_HEADDIM, D_STATE)
    hg_shape = (HG_HEADS, HG_DK, HG_DV)
    return pl.pallas_call(
        functools.partial(_mixers_kernel, chunks=chunks),
        grid=(steps,),
        in_specs=[tok(d), pl.BlockSpec((step_rows, d), lambda s: (jnp.minimum(s + 1, steps - 1), 0))]
                 + [_const_spec(c.shape) for c in consts],
        out_specs=[tok(SSD_D), tok(HG_D), per_seq((CONV_W - 1, CONV_DIM)), per_seq(ssm_shape),
                   per_seq(hg_shape)],
        out_shape=[jax.ShapeDtypeStruct((n, SSD_D), BF16), jax.ShapeDtypeStruct((n, HG_D), BF16),
                   jax.ShapeDtypeStruct((b, CONV_W - 1, CONV_DIM), F32),
                   jax.ShapeDtypeStruct((b,) + ssm_shape, F32),
                   jax.ShapeDtypeStruct((b,) + hg_shape, F32)],
        scratch_shapes=[pltpu.VMEM((PROJ_ROWS, PROJ_COLS), F32), pltpu.VMEM((PROJ_ROWS, PROJ_COLS), F32),
                        pltpu.VMEM((CHUNK + SUBLANES, CONV_DIM), F32),
                        pltpu.VMEM((SSD_GROUPS, D_STATE, SSD_D // SSD_GROUPS), F32),
                        pltpu.VMEM((HG_HEADS, HG_DV, HG_DK), F32),
                        pltpu.VMEM((HG_HEADS, CHUNK, CHUNK), F32),
                        pltpu.VMEM((CHUNK, HG_D), F32), pltpu.VMEM((CHUNK, HG_D), F32)],
        compiler_params=_cparams("arbitrary"),
    )(x, x, *consts)


def _tokens_to_lanes(x):
    tb, n = x.shape
    return jnp.concatenate([x, jnp.zeros((LANES - tb, n), F32)], axis=0).T


def _step_decay_kernel(dt_ref, dtb_ref, alog_ref, da_ref):
    da_ref[...] = jnp.exp(_softplus(dt_ref[...] + dtb_ref[...]) * (-jnp.exp(alog_ref[...])))


def _ssd_step_kernel(da_ref, z_ref, xbc_ref, dt_ref, cw_ref, cb_ref, dtb_ref, dsk_ref, nw_ref,
                     cs_ref, st_ref, y_ref, cnew_ref, snew_ref, *, tb):
    cd = CONV_DIM
    gn = D_STATE
    hpg = SSD_HEADS // SSD_GROUPS
    gw = hpg * SSD_HEADDIM
    first = pl.program_id(0) * tb
    x_in = xbc_ref[...]
    acc = cb_ref[...] + cw_ref[CONV_W - 1:CONV_W, :] * x_in
    for k in range(CONV_W - 1):
        acc = acc + cw_ref[k:k + 1, :] * cs_ref[:, k * cd:(k + 1) * cd]
    cnew_ref[:, :(CONV_W - 2) * cd] = cs_ref[:, cd:]
    cnew_ref[:, (CONV_W - 2) * cd:] = x_in
    xbc = _silu(acc)
    xs = xbc[:, :SSD_D]
    dt = _softplus(dt_ref[...] + dtb_ref[...])
    xdt_t = _tokens_to_lanes(xs * _heads_to_channels(dt)).astype(BF16)
    row_tok = lax.broadcasted_iota(jnp.int32, (LANES, tb * gn), 0)
    col_tok = lax.broadcasted_iota(jnp.int32, (LANES, tb * gn), 1) // gn
    lane = lax.broadcasted_iota(jnp.int32, (SSD_HEADDIM, LANES), 1)
    y_groups = []
    for g in range(SSD_GROUPS):
        b_g = xbc[:, SSD_D + g * gn:SSD_D + (g + 1) * gn]
        c_g = xbc[:, SSD_D + (SSD_GROUPS + g) * gn:SSD_D + (SSD_GROUPS + g + 1) * gn]
        b_wide = jnp.concatenate([jnp.tile(b_g, (1, tb)), jnp.zeros((LANES - tb, tb * gn), F32)], axis=0)
        b_diag = jnp.where(row_tok == col_tok, b_wide, 0.0).astype(BF16)
        upd = _dot(xdt_t[g * gw:(g + 1) * gw, :], b_diag)
        y_heads = []
        for hl in range(hpg):
            h = g * hpg + hl
            y_h = jnp.zeros((SSD_HEADDIM, LANES), F32)
            for j in range(tb):
                new = (st_ref[j, h] * da_ref[first + j, h]
                       + upd[hl * SSD_HEADDIM:(hl + 1) * SSD_HEADDIM, j * gn:(j + 1) * gn])
                snew_ref[j, h] = new
                y_h = jnp.where(lane == j, jnp.sum(new * c_g[j:j + 1, :], axis=-1, keepdims=True), y_h)
            y_heads.append(y_h)
        y_groups.append(jnp.concatenate(y_heads, axis=0))
    y = jnp.concatenate(y_groups, axis=0).T[:tb]
    y = (y + dsk_ref[...] * xs) * _silu(z_ref[...])
    for g in range(SSD_GROUPS):
        gs = slice(g * gw, (g + 1) * gw)
        y_ref[:, gs] = (_rms_unit(y[:, gs]) * nw_ref[:, gs]).astype(y_ref.dtype)


def _ssd_step(z, xbc, dt, conv_w, conv_b, dt_bias, a_log, d_skip, norm_w, conv_init, ssm_init):
    b = z.shape[0]
    tb = SUBLANES
    assert b % tb == 0
    da = pl.pallas_call(
        _step_decay_kernel,
        grid=(1,),
        in_specs=[_const_spec(dt.shape), _const_spec(dt_bias.shape), _const_spec(a_log.shape)],
        out_specs=_const_spec(dt.shape),
        out_shape=jax.ShapeDtypeStruct(dt.shape, F32),
    )(dt, dt_bias, a_log)
    params = (conv_w, conv_b, dt_bias, d_skip, norm_w)
    hist = (CONV_W - 1) * CONV_DIM
    row = lambda w: pl.BlockSpec((tb, w), lambda i, da: (i, 0))
    const = lambda shape: pl.BlockSpec(shape, lambda i, da: (0,) * len(shape))
    state_shape = (SSD_HEADS, SSD_HEADDIM, D_STATE)
    st_spec = pl.BlockSpec((tb,) + state_shape, lambda i, da: (i, 0, 0, 0))
    y, conv_new, ssm_new = pl.pallas_call(
        functools.partial(_ssd_step_kernel, tb=tb),
        grid_spec=pltpu.PrefetchScalarGridSpec(
            num_scalar_prefetch=1,
            grid=(b // tb,),
            in_specs=[row(SSD_D), row(CONV_DIM), row(LANES)] + [const(p.shape) for p in params]
                     + [row(hist), st_spec],
            out_specs=[row(SSD_D), row(hist), st_spec]),
        out_shape=[jax.ShapeDtypeStruct((b, SSD_D), BF16), jax.ShapeDtypeStruct((b, hist), F32),
                   jax.ShapeDtypeStruct((b,) + state_shape, F32)],
        compiler_params=_cparams("arbitrary"),
    )(da, z, xbc, dt, *params, conv_init.reshape(b, hist), ssm_init)
    return y, conv_new.reshape(b, CONV_W - 1, CONV_DIM), ssm_new


def _hgrn_step_kernel(q_ref, f_ref, v_ref, g_ref, lb_ref, nw_ref, st_ref, o_ref, snew_ref, *, tb):
    dk, dv = HG_DK, HG_DV
    fr = f_ref[...]
    lb = lb_ref[...]
    sig = _sigmoid(fr)
    f_t = _tokens_to_lanes(lb + (1.0 - lb) * sig)
    k_t = _tokens_to_lanes((1.0 - lb) * (1.0 - sig)).astype(BF16)
    q_bf = q_ref[...].astype(BF16)
    v = v_ref[...]
    gate = _silu(g_ref[...])
    row_tok = lax.broadcasted_iota(jnp.int32, (LANES, tb * dv), 0)
    col_tok = lax.broadcasted_iota(jnp.int32, (LANES, tb * dv), 1) // dv
    sub = lax.broadcasted_iota(jnp.int32, (tb, dv), 0)
    for h in range(HG_HEADS):
        rs = slice(h * dk, (h + 1) * dk)
        vs = slice(h * dv, (h + 1) * dv)
        v_wide = jnp.concatenate([jnp.tile(v[:, vs], (1, tb)), jnp.zeros((LANES - tb, tb * dv), F32)], axis=0)
        v_diag = jnp.where(row_tok == col_tok, v_wide, 0.0).astype(BF16)
        upd = _dot(k_t[rs, :], v_diag)
        o_h = jnp.zeros((tb, dv), F32)
        for j in range(tb):
            fcol = jnp.broadcast_to(f_t[rs, j:j + 1], (dk, dv))
            new = st_ref[j, h] * fcol + upd[:, j * dv:(j + 1) * dv]
            snew_ref[j, h] = new
            o_h = jnp.where(sub == j, _dot(q_bf[:, rs], new.astype(BF16)), o_h)
        o_ref[:, vs] = (_rms_unit(o_h) * nw_ref[:, vs] * gate[:, vs]).astype(o_ref.dtype)


def _hgrn_step(q, f, v, g, lb, norm_w, init):
    b = q.shape[0]
    tb = SUBLANES
    assert b % tb == 0
    row = pl.BlockSpec((tb, HG_D), lambda i: (i, 0))
    state_shape = (HG_HEADS, HG_DK, HG_DV)
    st_spec = pl.BlockSpec((tb,) + state_shape, lambda i: (i, 0, 0, 0))
    return pl.pallas_call(
        functools.partial(_hgrn_step_kernel, tb=tb),
        grid=(b // tb,),
        in_specs=[row, row, row, row, _const_spec(lb.shape), _const_spec(norm_w.shape), st_spec],
        out_specs=[row, st_spec],
        out_shape=[jax.ShapeDtypeStruct((b, HG_D), BF16),
                   jax.ShapeDtypeStruct((b,) + state_shape, F32)],
        compiler_params=_cparams("arbitrary"),
    )(q, f, v, g, lb, norm_w, init)


def _route_t(logits, bias, group=None):
    row = lax.broadcasted_iota(jnp.int32, logits.shape, 0)
    ninf = -jnp.inf
    big = jnp.int32(ROUTER_ROWS)
    is_g = (row >= N_EXPERTS) & (row < N_EXPERTS + N_EGROUPS)
    cmax = lambda t: jnp.max(t, axis=0, keepdims=True)
    csum = lambda t: jnp.sum(t, axis=0, keepdims=True)
    first = lambda m: jnp.min(jnp.where(m, row, big), axis=0, keepdims=True)

    gl = jnp.where(is_g, logits, ninf)
    gp = jnp.exp(gl - cmax(gl))
    gprob = gp / csum(gp)
    biased = logits + bias
    if group is None:
        gb = jnp.where(is_g, biased, ninf)
        gsel = first(gb == cmax(gb))
    else:
        gsel = group + N_EXPERTS
    gw = csum(jnp.where(row == gsel, gprob, 0.0))
    e0 = (gsel - N_EXPERTS) * EXPERTS_PER_GROUP
    in_grp = (row >= e0) & (row < e0 + EXPERTS_PER_GROUP)
    el = jnp.where(in_grp, logits, ninf)
    ep = jnp.exp(el - cmax(el))
    eprob = ep / csum(ep)
    eb = jnp.where(in_grp, biased, ninf)
    i1 = first(eb == cmax(eb))
    eb2 = jnp.where(row == i1, ninf, eb)
    i2 = first(eb2 == cmax(eb2))
    p1 = csum(jnp.where(row == i1, eprob, 0.0))
    p2 = csum(jnp.where(row == i2, eprob, 0.0))
    den = p1 + p2
    a = jnp.minimum(i1, i2) - e0
    b = jnp.maximum(i1, i2) - e0
    group = jnp.broadcast_to(gsel - N_EXPERTS, (1, logits.shape[1])).astype(F32)
    pair = jnp.zeros_like(group)
    for k, (u, v) in enumerate(PAIR_ORDER):
        pair = pair + jnp.where((a == min(u, v)) & (b == max(u, v)), float(k), 0.0)
    cls = group * PAIRS_PER_GROUP + pair
    return (jnp.where(row == i1, gw * (p1 / den), 0.0)
            + jnp.where(row == i2, gw * (p2 / den), 0.0)
            + jnp.where(row == CLASS_ROW, cls, 0.0))


def _router_logits_t(h, rhi_ref, rlo_ref):
    h_hi = h.astype(BF16)
    h_lo = (h - h_hi.astype(F32)).astype(BF16)
    logits = _dot(h_hi, rhi_ref[...]) + _dot(h_hi, rlo_ref[...]) + _dot(h_lo, rhi_ref[...])
    return logits.T[:ROUTER_ROWS]


def _rows_to_lanes(gates_t):
    r, tm = gates_t.shape
    return jnp.concatenate([gates_t, jnp.zeros((LANES - r, tm), F32)], axis=0).T


def _outproj_kernel(x_ref, y_ref, o_ref, w_ref, nw_ref, rhi_ref, rlo_ref, rb_ref,
                    x1_ref, h_ref, r_ref, *, tiled_rows):
    dy = y_ref.shape[-1]
    tm, d = x_ref.shape
    rb = min(tm, OUT_PROJ_ROWS)
    for r0 in range(0, tm, rb):
        rs = slice(r0, r0 + rb)
        mix = _dot(y_ref[rs, :], w_ref[:dy, :]) + _dot(o_ref[rs, :], w_ref[dy:, :])
        x1 = x_ref[rs, :] + mix
        x1_ref[rs, :] = x1
        h = _rms_unit(x1) * nw_ref[...]
        gates_t = _route_t(_router_logits_t(h, rhi_ref, rlo_ref), rb_ref[...])
        if tiled_rows:
            h_ref[rs] = h.reshape(rb, d // LANES, LANES)
            r_ref[:, rs] = jnp.broadcast_to(gates_t[CLASS_ROW:CLASS_ROW + 1, :], (SUBLANES, rb))
        else:
            h_ref[rs, :] = h
            r_ref[rs, :] = _rows_to_lanes(gates_t)


def _out_proj(x, y, o, w, nw, r_hi, r_lo, r_bias, tm, tiled_rows):
    n, d = x.shape
    row = lambda width: pl.BlockSpec((tm, width), lambda i: (i, 0))
    consts = (w, nw, r_hi, r_lo, r_bias)
    if tiled_rows:
        h_spec = pl.BlockSpec((tm, d // LANES, LANES), lambda i: (i, 0, 0))
        h_shape = jax.ShapeDtypeStruct((n, d // LANES, LANES), F32)
        r_spec = pl.BlockSpec((SUBLANES, tm), lambda i: (i, 0))
        r_shape = jax.ShapeDtypeStruct((n // tm * SUBLANES, tm), F32)
    else:
        h_spec, h_shape = row(d), jax.ShapeDtypeStruct((n, d), F32)
        r_spec, r_shape = row(LANES), jax.ShapeDtypeStruct((n, LANES), F32)
    return pl.pallas_call(
        functools.partial(_outproj_kernel, tiled_rows=tiled_rows),
        grid=(n // tm,),
        in_specs=[row(d), row(y.shape[-1]), row(o.shape[-1])] + [_const_spec(c.shape) for c in consts],
        out_specs=[row(d), h_spec, r_spec],
        out_shape=[jax.ShapeDtypeStruct((n, d), F32), h_shape, r_shape],
        compiler_params=_cparams("arbitrary"),
    )(x, y, o, *consts)


def _pos_kernel(cls_ref, pos_ref, tab_ref, *, tile):
    rows, tm = cls_ref.shape
    cls = cls_ref[...]
    ri = lax.broadcasted_iota(jnp.int32, (tm, tm), 0)
    ci = lax.broadcasted_iota(jnp.int32, (tm, tm), 1)
    upper = (ri <= ci).astype(BF16)
    rr = lax.broadcasted_iota(jnp.int32, (rows, rows), 0)
    rc = lax.broadcasted_iota(jnp.int32, (rows, rows), 1)
    earlier = ((rc // SUBLANES < rr // SUBLANES) & (rc % SUBLANES == 0)).astype(BF16)
    tile_lane = lax.broadcasted_iota(jnp.int32, (1, LANES), 1).astype(F32)
    pos = jnp.zeros((rows, tm), F32)
    tiles_before = jnp.zeros((1, 1), F32)
    tile_class = jnp.zeros((1, LANES), F32)
    pad_start = jnp.zeros((1, LANES), F32)
    pad_len = jnp.zeros((1, LANES), F32)
    for c in range(N_CLASSES):
        onehot = jnp.where(cls == float(c), 1.0, 0.0)
        inc = _dot(onehot.astype(BF16), upper)
        rowtot = jnp.broadcast_to(inc[:, tm - 1:tm], (rows, tm))
        rowpre = _cumsum_rows(rowtot, earlier)
        cnt = rowpre[rows - 1:rows, 0:1] + rowtot[rows - 1:rows, 0:1]
        pos = pos + onehot * (tiles_before * tile + rowpre + inc - 1.0)
        ntiles = jnp.floor((cnt + (tile - 1.0)) * (1.0 / tile))
        pad_start = pad_start + jnp.where(tile_lane == float(c), tiles_before * tile + cnt, 0.0)
        pad_len = pad_len + jnp.where(tile_lane == float(c), ntiles * tile - cnt, 0.0)
        tiles_before = tiles_before + ntiles
        tile_class = tile_class + jnp.where(tile_lane >= tiles_before, 1.0, 0.0)
    pos_ref[...] = pos.astype(jnp.int32)
    pad_start = pad_start + jnp.where(tile_lane == float(N_CLASSES), tiles_before * tile, 0.0)
    used = tile_class < N_CLASSES
    group = jnp.floor(tile_class * (1.0 / PAIRS_PER_GROUP))
    pair = tile_class - group * PAIRS_PER_GROUP
    a = jnp.zeros_like(pair)
    b = jnp.zeros_like(pair)
    for k, (u, v) in enumerate(PAIR_ORDER):
        a = a + jnp.where(pair == float(k), float(u), 0.0)
        b = b + jnp.where(pair == float(k), float(v), 0.0)
    e1 = jnp.where(used, group * EXPERTS_PER_GROUP + a, float(N_EXPERTS))
    e2 = jnp.where(used, group * EXPERTS_PER_GROUP + b, float(N_EXPERTS))
    sub = lax.broadcasted_iota(jnp.int32, (SUBLANES, LANES), 0)
    tab = jnp.where(sub == 0, e1, jnp.where(sub == 1, e2, jnp.where(sub == 2, pad_start, pad_len)))
    tab_ref[...] = tab.astype(jnp.int32)


def _positions(cls, n, tile):
    rows, tm = cls.shape
    assert n // tile + N_CLASSES <= LANES
    pos, tab = pl.pallas_call(
        functools.partial(_pos_kernel, tile=tile),
        grid=(1,),
        in_specs=[_const_spec(cls.shape)],
        out_specs=[_const_spec(cls.shape), _const_spec((SUBLANES, LANES))],
        out_shape=[jax.ShapeDtypeStruct(cls.shape, jnp.int32),
                   jax.ShapeDtypeStruct((SUBLANES, LANES), jnp.int32)],
        compiler_params=_cparams("arbitrary"),
    )(cls)
    return pos.reshape(rows // SUBLANES, SUBLANES, tm)[:, 0, :].reshape(n), tab[0], tab[1], tab[2], tab[3]


def _dispatch_kernel(pos_ref, pad_start_ref, pad_len_ref, h_ref, hs_ref, buf_ref, zero_ref, sem_ref,
                     zsem_ref, *, td):
    i = pl.program_id(0)
    nsteps = pl.num_programs(0)
    slot = i % 2

    tile = zero_ref.shape[0]
    n_sorted = hs_ref.shape[0]

    def pad_copies(act):
        used_end = pad_start_ref[N_CLASSES]
        for j in range(N_CLASSES):
            @pl.when(used_end + j * tile < n_sorted)
            def _(j=j):
                act(pltpu.make_async_copy(zero_ref, hs_ref.at[pl.ds(used_end + j * tile, tile)],
                                          zsem_ref.at[0]))

        def body(c, carry):
            length = pad_len_ref[c]
            off = pad_start_ref[c]
            piece = tile // 2
            while piece >= 1:
                @pl.when((length & piece) != 0)
                def _(off=off, piece=piece):
                    act(pltpu.make_async_copy(zero_ref.at[0:piece], hs_ref.at[pl.ds(off, piece)],
                                              zsem_ref.at[0]))
                off = off + (length & piece)
                piece //= 2
            return carry
        lax.fori_loop(0, N_CLASSES, body, 0)

    @pl.when(i == 0)
    def _():
        zero_ref[...] = jnp.zeros(zero_ref.shape, F32)
        pad_copies(lambda cp: cp.start())

    def row_copy(step, s, r):
        return pltpu.make_async_copy(buf_ref.at[s, r], hs_ref.at[pos_ref[step * td + r]], sem_ref.at[s])

    def start_all(step, s):
        def body(r, carry):
            row_copy(step, s, r).start()
            return carry
        lax.fori_loop(0, td, body, 0, unroll=8)

    def wait_all(step, s):
        def body(r, carry):
            row_copy(step, s, r).wait()
            return carry
        lax.fori_loop(0, td, body, 0, unroll=8)

    @pl.when(i >= 2)
    def _():
        wait_all(i - 2, slot)

    buf_ref[slot] = h_ref[...]
    start_all(i, slot)

    @pl.when(i == nsteps - 1)
    def _():
        @pl.when(i >= 1)
        def _():
            wait_all(i - 1, 1 - slot)
        wait_all(i, slot)
        pad_copies(lambda cp: cp.wait())


def _dispatch(h3, pos, pad_start, pad_len, n_sorted, tile, td):
    n = h3.shape[0]
    tok = h3.shape[1:]
    return pl.pallas_call(
        functools.partial(_dispatch_kernel, td=td),
        grid_spec=pltpu.PrefetchScalarGridSpec(
            num_scalar_prefetch=3,
            grid=(n // td,),
            in_specs=[pl.BlockSpec((td,) + tok, lambda i, *_: (i, 0, 0))],
            out_specs=pl.BlockSpec(memory_space=pl.ANY),
            scratch_shapes=[pltpu.VMEM((2, td) + tok, F32), pltpu.VMEM((tile,) + tok, F32),
                            pltpu.SemaphoreType.DMA((2,)), pltpu.SemaphoreType.DMA((1,))]),
        out_shape=jax.ShapeDtypeStruct((n_sorted,) + tok, F32),
        compiler_params=_cparams("arbitrary"),
    )(pos, pad_start, pad_len, h3)


def _expert(h, gates, e, wg, wu, wd):
    lane = lax.broadcasted_iota(jnp.int32, gates.shape, 1)
    gcol = jnp.sum(jnp.where(lane == e, gates, 0.0), axis=-1, keepdims=True)
    act = _silu(_dot(h, wg.astype(BF16))) * _dot(h, wu.astype(BF16)) * gcol
    return _dot(act.astype(BF16), wd.astype(BF16))


def _moe_dense_kernel(h_ref, gates_ref, wg_ref, wu_ref, wd_ref, out_ref):
    e = pl.program_id(1)
    part = _expert(h_ref[...].astype(BF16), gates_ref[...], e, wg_ref[0], wu_ref[0], wd_ref[0])

    @pl.when(e == 0)
    def _():
        out_ref[...] = part

    @pl.when(e > 0)
    def _():
        out_ref[...] += part


def _moe_dense(h, gates, wg, wu, wd, tm):
    n, d = h.shape
    ne, _, ff = wg.shape
    row = lambda width: pl.BlockSpec((tm, width), lambda i, e: (i, 0))
    return pl.pallas_call(
        _moe_dense_kernel,
        grid=(n // tm, ne),
        in_specs=[row(d), row(LANES),
                  pl.BlockSpec((1, d, ff), lambda i, e: (e, 0, 0)),
                  pl.BlockSpec((1, d, ff), lambda i, e: (e, 0, 0)),
                  pl.BlockSpec((1, ff, d), lambda i, e: (e, 0, 0))],
        out_specs=row(d),
        out_shape=jax.ShapeDtypeStruct((n, d), F32),
        compiler_params=_cparams("arbitrary", "arbitrary"),
    )(h, gates, wg, wu, wd)


def _moe_sorted_kernel(e1_ref, e2_ref, hs_ref, rhi_ref, rlo_ref, rb_ref,
                       wg1_ref, wu1_ref, wd1_ref, wg2_ref, wu2_ref, wd2_ref, ys_ref):
    e1 = e1_ref[pl.program_id(0)]
    e2 = e2_ref[pl.program_id(0)]
    tm = hs_ref.shape[0]
    d = wg1_ref.shape[1]

    @pl.when(e1 < N_EXPERTS)
    def _():
        h = hs_ref[...].reshape(tm, d)
        logits_t = _router_logits_t(h, rhi_ref, rlo_ref)
        group = lax.shift_right_logical(e1, jnp.int32(EXPERTS_PER_GROUP.bit_length() - 1))
        gates = _rows_to_lanes(_route_t(logits_t, rb_ref[...], group=group))
        h_bf = h.astype(BF16)
        acc = (_expert(h_bf, gates, e1, wg1_ref[0], wu1_ref[0], wd1_ref[0])
               + _expert(h_bf, gates, e2, wg2_ref[0], wu2_ref[0], wd2_ref[0]))
        ys_ref[...] = acc.reshape(ys_ref.shape)

    @pl.when(e1 >= N_EXPERTS)
    def _():
        ys_ref[...] = jnp.zeros(ys_ref.shape, F32)


def _moe_sorted(hs, e1_tab, e2_tab, r_hi, r_lo, r_bias, wg, wu, wd, tm):
    n_sorted = hs.shape[0]
    ne, d, ff = wg.shape
    first = lambda i, e1, e2: (jnp.minimum(e1[i], ne - 1), 0, 0)
    second = lambda i, e1, e2: (jnp.minimum(e2[i], ne - 1), 0, 0)
    tok = pl.BlockSpec((tm,) + hs.shape[1:], lambda i, e1, e2: (i, 0, 0))
    const = lambda shape: pl.BlockSpec(shape, lambda i, e1, e2: (0,) * len(shape))
    up = lambda sel: pl.BlockSpec((1, d, ff), sel)
    down = lambda sel: pl.BlockSpec((1, ff, d), sel)
    return pl.pallas_call(
        _moe_sorted_kernel,
        grid_spec=pltpu.PrefetchScalarGridSpec(
            num_scalar_prefetch=2,
            grid=(n_sorted // tm,),
            in_specs=[tok, const(r_hi.shape), const(r_lo.shape), const(r_bias.shape),
                      up(first), up(first), down(first), up(second), up(second), down(second)],
            out_specs=tok),
        out_shape=jax.ShapeDtypeStruct(hs.shape, F32),
        compiler_params=_cparams("arbitrary"),
    )(e1_tab, e2_tab, hs, r_hi, r_lo, r_bias, wg, wu, wd, wg, wu, wd)


def _ple_math(x, p, npw_ref, wg_ref, wp_ref, fw_ref):
    hn = (_rms_unit(x) * npw_ref[...]).astype(BF16)
    gate = _sigmoid(_dot(hn, wg_ref[...]))
    x = x + gate * _dot(p.astype(BF16), wp_ref[...])
    return _rms_unit(x) * fw_ref[...]


def _ple_kernel(x1_ref, moe_ref, p_ref, npw_ref, wg_ref, wp_ref, fw_ref, y_ref):
    y_ref[...] = _ple_math(x1_ref[...] + moe_ref[...], p_ref[...], npw_ref, wg_ref, wp_ref, fw_ref)


def _ple_gather_kernel(pos_ref, x1_ref, ys_ref, p_ref, npw_ref, wg_ref, wp_ref, fw_ref, y_ref,
                       buf_ref, sem_ref, *, tm):
    i = pl.program_id(0)
    nsteps = pl.num_programs(0)
    slot = i % 2

    def row_copy(step, s, r):
        return pltpu.make_async_copy(ys_ref.at[pos_ref[step * tm + r]], buf_ref.at[s, r], sem_ref.at[s])

    def start_all(step, s):
        def body(r, carry):
            row_copy(step, s, r).start()
            return carry
        lax.fori_loop(0, tm, body, 0, unroll=8)

    @pl.when(i == 0)
    def _():
        start_all(0, 0)

    @pl.when(i + 1 < nsteps)
    def _():
        start_all(i + 1, 1 - slot)

    def wait_row(r, carry):
        row_copy(i, slot, r).wait()
        return carry
    lax.fori_loop(0, tm, wait_row, 0, unroll=8)

    moe = buf_ref[slot].reshape(x1_ref.shape)
    y_ref[...] = _ple_math(x1_ref[...] + moe, p_ref[...], npw_ref, wg_ref, wp_ref, fw_ref)


def _ple(x1, moe, pos, p, npw, wg, wp, fw, tm):
    n, d = x1.shape
    consts = (npw, wg, wp, fw)
    out_shape = jax.ShapeDtypeStruct((n, d), F32)
    if pos is None:
        row = lambda width: pl.BlockSpec((tm, width), lambda i: (i, 0))
        return pl.pallas_call(
            _ple_kernel,
            grid=(n // tm,),
            in_specs=[row(d), row(d), row(p.shape[-1])] + [_const_spec(c.shape) for c in consts],
            out_specs=row(d),
            out_shape=out_shape,
            compiler_params=_cparams("arbitrary"),
        )(x1, moe, p, *consts)
    row = lambda width: pl.BlockSpec((tm, width), lambda i, pos: (i, 0))
    const = lambda shape: pl.BlockSpec(shape, lambda i, pos: (0,) * len(shape))
    return pl.pallas_call(
        functools.partial(_ple_gather_kernel, tm=tm),
        grid_spec=pltpu.PrefetchScalarGridSpec(
            num_scalar_prefetch=1,
            grid=(n // tm,),
            in_specs=[row(d), pl.BlockSpec(memory_space=pl.ANY), row(p.shape[-1])]
                     + [const(c.shape) for c in consts],
            out_specs=row(d),
            scratch_shapes=[pltpu.VMEM((2, tm) + moe.shape[1:], F32), pltpu.SemaphoreType.DMA((2,))]),
        out_shape=out_shape,
        compiler_params=_cparams("arbitrary"),
    )(pos, x1, moe, p, *consts)


def _prepare_weights(norm_mix_w, w_in, conv_w, conv_b, dt_bias, a_log, d_skip, ssd_norm_w,
                     lb_logits, hg_norm_w, w_out, norm_ffn_w, w_router_group, b_router_group,
                     w_router_expert, b_router_expert, w_exp_gate, w_exp_up, w_exp_down,
                     norm_ple_w, w_ple_gate, w_ple_proj, final_norm_w, layer):
    i = layer
    row = lambda t: t.reshape(1, -1).astype(F32)
    lane_pad = lambda t: jnp.pad(t, [(0, 0)] * (t.ndim - 1) + [(0, LANES - t.shape[-1])])
    splits = (SSD_D, CONV_DIM, SSD_HEADS, HG_D, HG_D, HG_D, HG_D)
    offs = [0]
    for s in splits:
        offs.append(offs[-1] + s)
    seg = lambda k: w_in[i][:, offs[k]:offs[k + 1]].astype(BF16)
    w_in_segs = (seg(0), seg(1), seg(3), seg(4), seg(5), seg(6), lane_pad(seg(2)))
    w_router = lane_pad(jnp.concatenate([w_router_expert[i], w_router_group[i]], axis=1).astype(F32))
    r_hi = w_router.astype(BF16)
    r_lo = (w_router - r_hi.astype(F32)).astype(BF16)
    r_bias = jnp.pad(jnp.concatenate([b_router_expert[i].reshape(-1), b_router_group[i]]).astype(F32),
                     [(0, ROUTER_ROWS - N_EXPERTS - N_EGROUPS)]).reshape(ROUTER_ROWS, 1)
    lb = jnp.cumsum(jax.nn.softmax(lb_logits.astype(F32), axis=0), axis=0)[i]
    return dict(
        norm_mix_w=row(norm_mix_w[i]), w_in=w_in_segs, conv_w=conv_w[i].astype(F32),
        conv_b=row(conv_b[i]), dt_bias=lane_pad(row(dt_bias[i])), a_log=lane_pad(row(a_log[i])),
        d_skip=row(jnp.repeat(d_skip[i], SSD_HEADDIM)), ssd_norm_w=row(ssd_norm_w[i]),
        lb=row(lb), hg_norm_w=row(hg_norm_w[i]), w_out=w_out[i].astype(BF16),
        norm_ffn_w=row(norm_ffn_w[i]), r_hi=r_hi, r_lo=r_lo, r_bias=r_bias,
        w_exp_gate=w_exp_gate[i], w_exp_up=w_exp_up[i], w_exp_down=w_exp_down[i],
        norm_ple_w=row(norm_ple_w[i]),
        w_ple_gate=w_ple_gate[i].astype(BF16), w_ple_proj=w_ple_proj[i].astype(BF16),
        final_norm_w=row(final_norm_w))


def _token_tile(n, cap):
    tm = cap
    while tm >= SUBLANES:
        if n % tm == 0:
            return tm
        tm //= 2
    raise ValueError(f"token count {n} must be a multiple of {SUBLANES}")


def _trunk(x, p, ssm_in, conv_in, hg_in, w):
    b, l, d = x.shape
    n = b * l
    tm = _token_tile(n, ROW_TILE)
    x2d = x.reshape(n, d)
    if l == 1 and ssm_in is not None:
        z, xbc, q, f, v, g, dt = _in_proj(x2d, w["norm_mix_w"], w["w_in"], _token_tile(n, IN_PROJ_TILE))
        y, conv_new, ssm_new = _ssd_step(z, xbc, dt, w["conv_w"], w["conv_b"], w["dt_bias"], w["a_log"],
                                         w["d_skip"], w["ssd_norm_w"], conv_in, ssm_in)
        o, hg_new = _hgrn_step(q, f, v, g, w["lb"], w["hg_norm_w"], hg_in)
    else:
        assert ssm_in is None and conv_in is None and hg_in is None, "multi-token groups start empty"
        y, o, conv_new, ssm_new, hg_new = _mixers(x2d, b, l, w)
    router = (w["r_hi"], w["r_lo"], w["r_bias"])
    experts = (w["w_exp_gate"], w["w_exp_up"], w["w_exp_down"])
    sorted_moe = n >= SORTED_MOE_MIN_TOKENS
    x1, h, routed = _out_proj(x2d, y.reshape(n, -1), o.reshape(n, -1), w["w_out"], w["norm_ffn_w"],
                              *router, tm, tiled_rows=sorted_moe)
    if sorted_moe:
        pos, e1_tab, e2_tab, pad_start, pad_len = _positions(routed, n, MOE_TILE)
        hs = _dispatch(h, pos, pad_start, pad_len, n + N_CLASSES * MOE_TILE, MOE_TILE, tm)
        moe = _moe_sorted(hs, e1_tab, e2_tab, *router, *experts, MOE_TILE)
    else:
        pos = None
        moe = _moe_dense(h, routed, *experts, tm)
    y_out = _ple(x1, moe, pos, p.reshape(n, -1), w["norm_ple_w"], w["w_ple_gate"], w["w_ple_proj"],
                 w["final_norm_w"], tm)
    return y_out.reshape(b, l, d), ssm_new[None], conv_new[None], hg_new[None]


def kernel(x_prompt, x_sample, state_ssm, state_conv, state_hgrn, p_prompt, p_sample, norm_mix_w, w_in, conv_w, conv_b, dt_bias, a_log, d_skip, ssd_norm_w, lb_logits, hg_norm_w, w_out, norm_ffn_w, w_router_group, b_router_group, w_router_expert, b_router_expert, w_exp_gate, w_exp_up, w_exp_down, norm_ple_w, w_ple_gate, w_ple_proj, final_norm_w):
    assert p_prompt.shape[0] == 1, "the per-layer-embedding kernel also applies the final norm: depth 1 only"
    w = _prepare_weights(norm_mix_w, w_in, conv_w, conv_b, dt_bias, a_log, d_skip, ssd_norm_w,
                         lb_logits, hg_norm_w, w_out, norm_ffn_w, w_router_group, b_router_group,
                         w_router_expert, b_router_expert, w_exp_gate, w_exp_up, w_exp_down,
                         norm_ple_w, w_ple_gate, w_ple_proj, final_norm_w, layer=0)
    y_p, ssm_p, conv_p, hg_p = _trunk(x_prompt, p_prompt[0], None, None, None, w)
    y_s, ssm_s, conv_s, hg_s = _trunk(x_sample, p_sample[0], state_ssm[0], state_conv[0],
                                      state_hgrn[0], w)
    return (y_p, y_s, ssm_p, conv_p, hg_p, ssm_s, conv_s, hg_s)
```

```python
import functools

import jax
import jax.numpy as jnp
from jax import lax
from jax.experimental import pallas as pl
from jax.experimental.pallas import tpu as pltpu

F32 = jnp.float32
BF16 = jnp.bfloat16

EPS = 1e-6
SSD_HEADS = 16
SSD_HEADDIM = 64
SSD_D = SSD_HEADS * SSD_HEADDIM
SSD_GROUPS = 2
D_STATE = 128
CONV_W = 4
CONV_DIM = SSD_D + 2 * SSD_GROUPS * D_STATE
HG_HEADS = 8
HG_DK = 128
HG_DV = 128
HG_D = HG_HEADS * HG_DV
N_EGROUPS = 4
EXPERTS_PER_GROUP = 4
N_EXPERTS = N_EGROUPS * EXPERTS_PER_GROUP

LANES = 128
SUBLANES = 8
VMEM_LIMIT = 48 * 1024 * 1024
CHUNK = 128
PROJ_ROWS = 256
HG_SAFE_LOG_DECAY = 60.0
ROUTER_ROWS = 32
CLASS_ROW = N_EXPERTS + N_EGROUPS
PAIRS_PER_GROUP = EXPERTS_PER_GROUP * (EXPERTS_PER_GROUP - 1) // 2
N_CLASSES = N_EGROUPS * PAIRS_PER_GROUP
PAIR_ORDER = ((0, 1), (2, 1), (2, 3), (0, 3), (0, 2), (1, 3))
assert len({frozenset(p) for p in PAIR_ORDER}) == PAIRS_PER_GROUP
MOE_TILE = 256
IN_PROJ_TILE = 256
ROW_TILE = 512
OUT_PROJ_ROWS = 256
LOG2E = 1.4426950408889634
SORTED_MOE_MIN_TOKENS = 8 * SUBLANES * LANES


def _dot(a, b):
    return jnp.dot(a, b, preferred_element_type=F32)


def _dot_nt(a, b):
    return lax.dot_general(a, b, (((1,), (1,)), ((), ())), preferred_element_type=F32)


def _dot_tn(a, b):
    return lax.dot_general(a, b, (((0,), (0,)), ((), ())), preferred_element_type=F32)


def _rms_unit(x):
    return x * lax.rsqrt(jnp.mean(x * x, axis=-1, keepdims=True) + EPS)


def _sigmoid(x):
    return 1.0 / (1.0 + jnp.exp(-x))


def _softplus(x):
    return jnp.maximum(x, 0.0) + jnp.log(1.0 + jnp.exp(-jnp.abs(x)))


def _silu(x):
    return x * _sigmoid(x)


def _split3(x):
    hi = x.astype(BF16)
    r = x - hi.astype(F32)
    mid = r.astype(BF16)
    lo = (r - mid.astype(F32)).astype(BF16)
    return hi, mid, lo


def _cumsum_rows(x, tri):
    hi, mid, lo = _split3(x)
    return _dot(tri, hi) + _dot(tri, mid) + _dot(tri, lo)


def _tri(n):
    r = lax.broadcasted_iota(jnp.int32, (n, n), 0)
    c = lax.broadcasted_iota(jnp.int32, (n, n), 1)
    return r >= c


def _cparams(*sem):
    return pltpu.CompilerParams(dimension_semantics=sem, vmem_limit_bytes=VMEM_LIMIT)


def _const_spec(shape):
    nd = len(shape)
    return pl.BlockSpec(shape, lambda *_: (0,) * nd, pipeline_mode=pl.Buffered(1))


def _inproj_kernel(x_ref, nw_ref, *refs):
    hb = (_rms_unit(x_ref[...]) * nw_ref[...]).astype(BF16)
    nseg = len(refs) // 2
    for w_ref, out_ref in zip(refs[:nseg], refs[nseg:]):
        n = out_ref.shape[-1]
        for c0 in range(0, n, 512):
            cw = min(512, n - c0)
            out_ref[:, c0:c0 + cw] = _dot(hb, w_ref[:, c0:c0 + cw])


def _in_proj(x, nw, ws, tm):
    n, d = x.shape
    widths = [w.shape[-1] for w in ws]
    return pl.pallas_call(
        _inproj_kernel,
        grid=(n // tm,),
        in_specs=[pl.BlockSpec((tm, d), lambda i: (i, 0)), _const_spec(nw.shape)]
                 + [_const_spec(w.shape) for w in ws],
        out_specs=[pl.BlockSpec((tm, s), lambda i: (i, 0)) for s in widths],
        out_shape=[jax.ShapeDtypeStruct((n, s), F32) for s in widths],
        compiler_params=_cparams("arbitrary"),
    )(x, nw, *ws)


def _heads_to_channels(t):
    rows = t.shape[0]
    first_half = lax.broadcasted_iota(jnp.int32, (rows, LANES), 1) < SSD_HEADDIM
    return jnp.concatenate(
        [jnp.where(first_half, jnp.broadcast_to(t[:, h:h + 1], (rows, LANES)),
                   jnp.broadcast_to(t[:, h + 1:h + 2], (rows, LANES)))
         for h in range(0, SSD_HEADS, LANES // SSD_HEADDIM)], axis=1)


PROJ_WIDTHS = (SSD_D, CONV_DIM, HG_D, HG_D, HG_D, HG_D, LANES)
OFF_Z, OFF_XBC, OFF_Q, OFF_F, OFF_V, OFF_G, OFF_DT = (sum(PROJ_WIDTHS[:k]) for k in range(len(PROJ_WIDTHS)))
PROJ_COLS = sum(PROJ_WIDTHS)


def _project(x, nw_ref, w_refs, dst_ref):
    hb = (_rms_unit(x) * nw_ref[...]).astype(BF16)
    off = 0
    for w_ref in w_refs:
        n = w_ref.shape[-1]
        for c0 in range(0, n, 512):
            cw = min(512, n - c0)
            dst_ref[:, off + c0:off + c0 + cw] = _dot(hb, w_ref[:, c0:c0 + cw])
        off += n


def _ssd_chunk(p_ref, cw_ref, cb_ref, dtb_ref, alog_ref, dsk_ref, nw_ref,
               y_ref, cout_ref, xc_ref, st_ref):
    rows = p_ref.shape[0]
    hist = SUBLANES
    hpg = SSD_HEADS // SSD_GROUPS
    gw = hpg * SSD_HEADDIM
    gn = D_STATE

    xc_ref[hist:hist + rows, :] = p_ref[:, OFF_XBC:OFF_XBC + CONV_DIM]
    base = hist - (CONV_W - 1)
    acc = cb_ref[...] + cw_ref[0:1, :] * xc_ref[base:base + rows, :]
    for k in range(1, CONV_W):
        acc = acc + cw_ref[k:k + 1, :] * xc_ref[base + k:base + k + rows, :]
    cout_ref[...] = xc_ref[base + rows:base + rows + CONV_W - 1, :]
    xc_ref[0:hist, :] = xc_ref[rows:rows + hist, :]
    xbc = _silu(acc)
    xs = xbc[:, :SSD_D]
    b_bf = [xbc[:, SSD_D + g * gn:SSD_D + (g + 1) * gn].astype(BF16) for g in range(SSD_GROUPS)]
    c_bf = [xbc[:, SSD_D + (SSD_GROUPS + g) * gn:SSD_D + (SSD_GROUPS + g + 1) * gn].astype(BF16)
            for g in range(SSD_GROUPS)]

    dt = _softplus(p_ref[:, OFF_DT:OFF_DT + LANES] + dtb_ref[...])
    a = dt * (-jnp.exp(alog_ref[...]))
    tri = _tri(rows)
    a_cs = _cumsum_rows(a, tri.astype(BF16)) * LOG2E
    a_cs_t = a_cs.T
    dt_full = _heads_to_channels(dt)
    a_full = _heads_to_channels(a_cs)
    first_half = lax.broadcasted_iota(jnp.int32, (rows, LANES), 1) < SSD_HEADDIM
    a_last = a_full[rows - 1:rows, :]

    xdt = xs * dt_full
    xdt_bf = xdt.astype(BF16)
    xw_bf = (xdt * jnp.exp2(a_last - a_full)).astype(BF16)
    skip = dsk_ref[...] * xs
    zg = _silu(p_ref[:, OFF_Z:OFF_Z + SSD_D])

    for g in range(SSD_GROUPS):
        gs = slice(g * gw, (g + 1) * gw)
        scores = jnp.where(tri, _dot_nt(c_bf[g], b_bf[g]), 0.0)
        st_old = st_ref[g]
        y_off = _dot(c_bf[g], st_old.astype(BF16)) * jnp.exp2(a_full[:, gs])
        st_ref[g] = st_old * jnp.exp2(a_last[:, gs]) + _dot_tn(b_bf[g], xw_bf[:, gs])
        parts = []
        for pair in range(hpg // 2):
            h0 = g * hpg + 2 * pair
            cols = slice(h0 * SSD_HEADDIM, (h0 + 2) * SSD_HEADDIM)
            x_pair = xdt_bf[:, cols]
            acc = None
            for hh, keep in ((h0, first_half), (h0 + 1, jnp.logical_not(first_half))):
                diff = jnp.broadcast_to(a_cs[:, hh:hh + 1], (rows, rows)) - a_cs_t[hh:hh + 1, :]
                m = (scores * jnp.exp2(jnp.minimum(diff, 0.0))).astype(BF16)
                part = _dot(m, jnp.where(keep, x_pair, jnp.zeros_like(x_pair)))
                acc = part if acc is None else acc + part
            parts.append(acc)
        y = jnp.concatenate(parts, axis=1) + y_off + skip[:, gs]
        y_ref[:, gs] = (_rms_unit(y * zg[:, gs]) * nw_ref[:, gs]).astype(y_ref.dtype)


def _hgrn_chunk(p_ref, lb_ref, nw_ref, o_ref, st_ref, a_ref, kk_ref, gcs_ref):
    rows = p_ref.shape[0]
    dk = HG_DK
    lb = lb_ref[...]
    sig = _sigmoid(p_ref[:, OFF_F:OFF_F + HG_D])
    logf = jnp.log2(lb + (1.0 - lb) * sig)
    kk = (1.0 - lb) * (1.0 - sig)
    tri = _tri(rows)
    gcs = _cumsum_rows(logf, tri.astype(BF16))
    glast = gcs[rows - 1:rows, :]
    qt = p_ref[:, OFF_Q:OFF_Q + HG_D] * jnp.exp2(gcs)
    kk_ref[...] = kk
    gcs_ref[...] = gcs

    kt = kk * jnp.exp2(-gcs)
    khat = kk * jnp.exp2(glast - gcs)
    for h in range(HG_HEADS):
        sl = slice(h * dk, (h + 1) * dk)
        a_ref[h] = _dot_nt(qt[:, sl].astype(BF16), kt[:, sl].astype(BF16))

    @pl.when(jnp.min(glast) < -HG_SAFE_LOG_DECAY * LOG2E)
    def _():
        ri = lax.broadcasted_iota(jnp.int32, (rows, dk), 0)
        ci = lax.broadcasted_iota(jnp.int32, (rows, rows), 1)
        for h in range(HG_HEADS):
            sl = slice(h * dk, (h + 1) * dk)
            q_h = p_ref[:, OFF_Q + h * dk:OFF_Q + (h + 1) * dk]

            def cols(j8, a_h):
                base = pl.multiple_of(j8 * SUBLANES, SUBLANES)
                k_blk = kk_ref[pl.ds(base, SUBLANES), sl]
                g_blk = gcs_ref[pl.ds(base, SUBLANES), sl]
                g_h = gcs_ref[:, sl]
                for r in range(SUBLANES):
                    j = base + r
                    t = q_h * k_blk[r:r + 1] * jnp.exp2(
                        jnp.where(ri >= j, g_h - g_blk[r:r + 1], -jnp.inf))
                    a_h = jnp.where(ci == j, jnp.sum(t, axis=-1, keepdims=True), a_h)
                return a_h

            a_ref[h] = lax.fori_loop(0, rows // SUBLANES, cols, jnp.zeros((rows, rows), F32))

    gate = _silu(p_ref[:, OFF_G:OFF_G + HG_D])
    for h in range(HG_HEADS):
        sl = slice(h * dk, (h + 1) * dk)
        st = st_ref[h]
        v_bf = p_ref[:, OFF_V + h * HG_DV:OFF_V + (h + 1) * HG_DV].astype(BF16)
        a_h = jnp.where(tri, a_ref[h], 0.0).astype(BF16)
        o = _dot(a_h, v_bf) + _dot_nt(qt[:, sl].astype(BF16), st.astype(BF16))
        st_ref[h] = st * jnp.exp2(glast[:, sl]) + _dot_tn(v_bf, khat[:, sl].astype(BF16))
        o_ref[:, sl] = (_rms_unit(o) * nw_ref[:, sl] * gate[:, sl]).astype(o_ref.dtype)


def _mixers_kernel(x_ref, xn_ref, nw_ref, wz_ref, wxbc_ref, wq_ref, wf_ref, wv_ref, wg_ref, wdt_ref,
                   cw_ref, cb_ref, dtb_ref, alog_ref, dsk_ref, snw_ref, lb_ref, hnw_ref,
                   y_ref, o_ref, cout_ref, sout_ref, hout_ref,
                   pa_ref, pb_ref, xc_ref, st_ref, hst_ref, a_ref, kk_ref, gcs_ref, *, chunks):
    s = pl.program_id(0)
    blk = pa_ref.shape[0]
    per_blk = blk // CHUNK
    c0 = (s * 2 * per_blk) % chunks
    w_refs = (wz_ref, wxbc_ref, wq_ref, wf_ref, wv_ref, wg_ref, wdt_ref)
    hpg = SSD_HEADS // SSD_GROUPS

    @pl.when(s == 0)
    def _():
        _project(x_ref[0:blk, :], nw_ref, w_refs, pa_ref)

    @pl.when(c0 == 0)
    def _():
        xc_ref[0:SUBLANES, :] = jnp.zeros((SUBLANES, CONV_DIM), F32)
        st_ref[...] = jnp.zeros(st_ref.shape, F32)
        hst_ref[...] = jnp.zeros(hst_ref.shape, F32)

    for half, (cur_ref, nxt_ref) in enumerate(((pa_ref, pb_ref), (pb_ref, pa_ref))):
        x_next = x_ref[blk:2 * blk, :] if half == 0 else xn_ref[0:blk, :]
        _project(x_next, nw_ref, w_refs, nxt_ref)
        for k in range(per_blk):
            p_ref = cur_ref.at[k * CHUNK:(k + 1) * CHUNK]
            rs = slice((half * per_blk + k) * CHUNK, (half * per_blk + k + 1) * CHUNK)
            _ssd_chunk(p_ref, cw_ref, cb_ref, dtb_ref, alog_ref, dsk_ref, snw_ref,
                       y_ref.at[rs], cout_ref.at[0], xc_ref, st_ref)
            _hgrn_chunk(p_ref, lb_ref, hnw_ref, o_ref.at[rs], hst_ref, a_ref, kk_ref, gcs_ref)

    @pl.when(c0 + 2 * per_blk == chunks)
    def _():
        for g in range(SSD_GROUPS):
            sout_ref[0, g * hpg:(g + 1) * hpg] = st_ref[g].T.reshape(hpg, SSD_HEADDIM, D_STATE)
        for h in range(HG_HEADS):
            hout_ref[0, h] = hst_ref[h].T


def _mixers(x, b, l, w):
    n, d = x.shape
    chunks = l // CHUNK
    step_rows = 2 * PROJ_ROWS
    assert PROJ_ROWS % CHUNK == 0 and l % step_rows == 0
    steps = n // step_rows
    consts = (w["norm_mix_w"], *w["w_in"], w["conv_w"], w["conv_b"], w["dt_bias"], w["a_log"],
              w["d_skip"], w["ssd_norm_w"], w["lb"], w["hg_norm_w"])
    tok = lambda width: pl.BlockSpec((step_rows, width), lambda s: (s, 0))
    per_seq = lambda shape: pl.BlockSpec((1,) + shape,
                                         lambda s: (s * step_rows // l,) + (0,) * len(shape))
    ssm_shape = (SSD_HEADS, SSD_HEADDIM, D_STATE)
    hg_shape = (HG_HEADS, HG_DK, HG_DV)
    return pl.pallas_call(
        functools.partial(_mixers_kernel, chunks=chunks),
        grid=(steps,),
        in_specs=[tok(d), pl.BlockSpec((step_rows, d), lambda s: (jnp.minimum(s + 1, steps - 1), 0))]
                 + [_const_spec(c.shape) for c in consts],
        out_specs=[tok(SSD_D), tok(HG_D), per_seq((CONV_W - 1, CONV_DIM)), per_seq(ssm_shape),
                   per_seq(hg_shape)],
        out_shape=[jax.ShapeDtypeStruct((n, SSD_D), BF16), jax.ShapeDtypeStruct((n, HG_D), BF16),
                   jax.ShapeDtypeStruct((b, CONV_W - 1, CONV_DIM), F32),
                   jax.ShapeDtypeStruct((b,) + ssm_shape, F32),
                   jax.ShapeDtypeStruct((b,) + hg_shape, F32)],
        scratch_shapes=[pltpu.VMEM((PROJ_ROWS, PROJ_COLS), F32), pltpu.VMEM((PROJ_ROWS, PROJ_COLS), F32),
                        pltpu.VMEM((CHUNK + SUBLANES, CONV_DIM), F32),
                        pltpu.VMEM((SSD_GROUPS, D_STATE, SSD_D // SSD_GROUPS), F32),
                        pltpu.VMEM((HG_HEADS, HG_DV, HG_DK), F32),
                        pltpu.VMEM((HG_HEADS, CHUNK, CHUNK), F32),
                        pltpu.VMEM((CHUNK, HG_D), F32), pltpu.VMEM((CHUNK, HG_D), F32)],
        compiler_params=_cparams("arbitrary"),
    )(x, x, *consts)


def _tokens_to_lanes(x):
    tb, n = x.shape
    return jnp.concatenate([x, jnp.zeros((LANES - tb, n), F32)], axis=0).T


def _step_decay_kernel(dt_ref, dtb_ref, alog_ref, da_ref):
    da_ref[...] = jnp.exp(_softplus(dt_ref[...] + dtb_ref[...]) * (-jnp.exp(alog_ref[...])))


def _ssd_step_kernel(da_ref, z_ref, xbc_ref, dt_ref, cw_ref, cb_ref, dtb_ref, dsk_ref, nw_ref,
                     cs_ref, st_ref, y_ref, cnew_ref, snew_ref, *, tb):
    cd = CONV_DIM
    gn = D_STATE
    hpg = SSD_HEADS // SSD_GROUPS
    gw = hpg * SSD_HEADDIM
    first = pl.program_id(0) * tb
    x_in = xbc_ref[...]
    acc = cb_ref[...] + cw_ref[CONV_W - 1:CONV_W, :] * x_in
    for k in range(CONV_W - 1):
        acc = acc + cw_ref[k:k + 1, :] * cs_ref[:, k * cd:(k + 1) * cd]
    cnew_ref[:, :(CONV_W - 2) * cd] = cs_ref[:, cd:]
    cnew_ref[:, (CONV_W - 2) * cd:] = x_in
    xbc = _silu(acc)
    xs = xbc[:, :SSD_D]
    dt = _softplus(dt_ref[...] + dtb_ref[...])
    xdt_t = _tokens_to_lanes(xs * _heads_to_channels(dt)).astype(BF16)
    row_tok = lax.broadcasted_iota(jnp.int32, (LANES, tb * gn), 0)
    col_tok = lax.broadcasted_iota(jnp.int32, (LANES, tb * gn), 1) // gn
    lane = lax.broadcasted_iota(jnp.int32, (SSD_HEADDIM, LANES), 1)
    y_groups = []
    for g in range(SSD_GROUPS):
        b_g = xbc[:, SSD_D + g * gn:SSD_D + (g + 1) * gn]
        c_g = xbc[:, SSD_D + (SSD_GROUPS + g) * gn:SSD_D + (SSD_GROUPS + g + 1) * gn]
        b_wide = jnp.concatenate([jnp.tile(b_g, (1, tb)), jnp.zeros((LANES - tb, tb * gn), F32)], axis=0)
        b_diag = jnp.where(row_tok == col_tok, b_wide, 0.0).astype(BF16)
        upd = _dot(xdt_t[g * gw:(g + 1) * gw, :], b_diag)
        y_heads = []
        for hl in range(hpg):
            h = g * hpg + hl
            y_h = jnp.zeros((SSD_HEADDIM, LANES), F32)
            for j in range(tb):
                new = (st_ref[j, h] * da_ref[first + j, h]
                       + upd[hl * SSD_HEADDIM:(hl + 1) * SSD_HEADDIM, j * gn:(j + 1) * gn])
                snew_ref[j, h] = new
                y_h = jnp.where(lane == j, jnp.sum(new * c_g[j:j + 1, :], axis=-1, keepdims=True), y_h)
            y_heads.append(y_h)
        y_groups.append(jnp.concatenate(y_heads, axis=0))
    y = jnp.concatenate(y_groups, axis=0).T[:tb]
    y = (y + dsk_ref[...] * xs) * _silu(z_ref[...])
    for g in range(SSD_GROUPS):
        gs = slice(g * gw, (g + 1) * gw)
        y_ref[:, gs] = (_rms_unit(y[:, gs]) * nw_ref[:, gs]).astype(y_ref.dtype)


def _ssd_step(z, xbc, dt, conv_w, conv_b, dt_bias, a_log, d_skip, norm_w, conv_init, ssm_init):
    b = z.shape[0]
    tb = SUBLANES
    assert b % tb == 0
    da = pl.pallas_call(
        _step_decay_kernel,
        grid=(1,),
        in_specs=[_const_spec(dt.shape), _const_spec(dt_bias.shape), _const_spec(a_log.shape)],
        out_specs=_const_spec(dt.shape),
        out_shape=jax.ShapeDtypeStruct(dt.shape, F32),
    )(dt, dt_bias, a_log)
    params = (conv_w, conv_b, dt_bias, d_skip, norm_w)
    hist = (CONV_W - 1) * CONV_DIM
    row = lambda w: pl.BlockSpec((tb, w), lambda i, da: (i, 0))
    const = lambda shape: pl.BlockSpec(shape, lambda i, da: (0,) * len(shape))
    state_shape = (SSD_HEADS, SSD_HEADDIM, D_STATE)
    st_spec = pl.BlockSpec((tb,) + state_shape, lambda i, da: (i, 0, 0, 0))
    y, conv_new, ssm_new = pl.pallas_call(
        functools.partial(_ssd_step_kernel, tb=tb),
        grid_spec=pltpu.PrefetchScalarGridSpec(
            num_scalar_prefetch=1,
            grid=(b // tb,),
            in_specs=[row(SSD_D), row(CONV_DIM), row(LANES)] + [const(p.shape) for p in params]
                     + [row(hist), st_spec],
            out_specs=[row(SSD_D), row(hist), st_spec]),
        out_shape=[jax.ShapeDtypeStruct((b, SSD_D), BF16), jax.ShapeDtypeStruct((b, hist), F32),
                   jax.ShapeDtypeStruct((b,) + state_shape, F32)],
        compiler_params=_cparams("arbitrary"),
    )(da, z, xbc, dt, *params, conv_init.reshape(b, hist), ssm_init)
    return y, conv_new.reshape(b, CONV_W - 1, CONV_DIM), ssm_new


def _hgrn_step_kernel(q_ref, f_ref, v_ref, g_ref, lb_ref, nw_ref, st_ref, o_ref, snew_ref, *, tb):
    dk, dv = HG_DK, HG_DV
    fr = f_ref[...]
    lb = lb_ref[...]
    sig = _sigmoid(fr)
    f_t = _tokens_to_lanes(lb + (1.0 - lb) * sig)
    k_t = _tokens_to_lanes((1.0 - lb) * (1.0 - sig)).astype(BF16)
    q_bf = q_ref[...].astype(BF16)
    v = v_ref[...]
    gate = _silu(g_ref[...])
    row_tok = lax.broadcasted_iota(jnp.int32, (LANES, tb * dv), 0)
    col_tok = lax.broadcasted_iota(jnp.int32, (LANES, tb * dv), 1) // dv
    sub = lax.broadcasted_iota(jnp.int32, (tb, dv), 0)
    for h in range(HG_HEADS):
        rs = slice(h * dk, (h + 1) * dk)
        vs = slice(h * dv, (h + 1) * dv)
        v_wide = jnp.concatenate([jnp.tile(v[:, vs], (1, tb)), jnp.zeros((LANES - tb, tb * dv), F32)], axis=0)
        v_diag = jnp.where(row_tok == col_tok, v_wide, 0.0).astype(BF16)
        upd = _dot(k_t[rs, :], v_diag)
        o_h = jnp.zeros((tb, dv), F32)
        for j in range(tb):
            fcol = jnp.broadcast_to(f_t[rs, j:j + 1], (dk, dv))
            new = st_ref[j, h] * fcol + upd[:, j * dv:(j + 1) * dv]
            snew_ref[j, h] = new
            o_h = jnp.where(sub == j, _dot(q_bf[:, rs], new.astype(BF16)), o_h)
        o_ref[:, vs] = (_rms_unit(o_h) * nw_ref[:, vs] * gate[:, vs]).astype(o_ref.dtype)


def _hgrn_step(q, f, v, g, lb, norm_w, init):
    b = q.shape[0]
    tb = SUBLANES
    assert b % tb == 0
    row = pl.BlockSpec((tb, HG_D), lambda i: (i, 0))
    state_shape = (HG_HEADS, HG_DK, HG_DV)
    st_spec = pl.BlockSpec((tb,) + state_shape, lambda i: (i, 0, 0, 0))
    return pl.pallas_call(
        functools.partial(_hgrn_step_kernel, tb=tb),
        grid=(b // tb,),
        in_specs=[row, row, row, row, _const_spec(lb.shape), _const_spec(norm_w.shape), st_spec],
        out_specs=[row, st_spec],
        out_shape=[jax.ShapeDtypeStruct((b, HG_D), BF16),
                   jax.ShapeDtypeStruct((b,) + state_shape, F32)],
        compiler_params=_cparams("arbitrary"),
    )(q, f, v, g, lb, norm_w, init)


def _route_t(logits, bias, group=None):
    row = lax.broadcasted_iota(jnp.int32, logits.shape, 0)
    ninf = -jnp.inf
    big = jnp.int32(ROUTER_ROWS)
    is_g = (row >= N_EXPERTS) & (row < N_EXPERTS + N_EGROUPS)
    cmax = lambda t: jnp.max(t, axis=0, keepdims=True)
    csum = lambda t: jnp.sum(t, axis=0, keepdims=True)
    first = lambda m: jnp.min(jnp.where(m, row, big), axis=0, keepdims=True)

    gl = jnp.where(is_g, logits, ninf)
    gp = jnp.exp(gl - cmax(gl))
    gprob = gp / csum(gp)
    biased = logits + bias
    if group is None:
        gb = jnp.where(is_g, biased, ninf)
        gsel = first(gb == cmax(gb))
    else:
        gsel = group + N_EXPERTS
    gw = csum(jnp.where(row == gsel, gprob, 0.0))
    e0 = (gsel - N_EXPERTS) * EXPERTS_PER_GROUP
    in_grp = (row >= e0) & (row < e0 + EXPERTS_PER_GROUP)
    el = jnp.where(in_grp, logits, ninf)
    ep = jnp.exp(el - cmax(el))
    eprob = ep / csum(ep)
    eb = jnp.where(in_grp, biased, ninf)
    i1 = first(eb == cmax(eb))
    eb2 = jnp.where(row == i1, ninf, eb)
    i2 = first(eb2 == cmax(eb2))
    p1 = csum(jnp.where(row == i1, eprob, 0.0))
    p2 = csum(jnp.where(row == i2, eprob, 0.0))
    den = p1 + p2
    a = jnp.minimum(i1, i2) - e0
    b = jnp.maximum(i1, i2) - e0
    group = jnp.broadcast_to(gsel - N_EXPERTS, (1, logits.shape[1])).astype(F32)
    pair = jnp.zeros_like(group)
    for k, (u, v) in enumerate(PAIR_ORDER):
        pair = pair + jnp.where((a == min(u, v)) & (b == max(u, v)), float(k), 0.0)
    cls = group * PAIRS_PER_GROUP + pair
    return (jnp.where(row == i1, gw * (p1 / den), 0.0)
            + jnp.where(row == i2, gw * (p2 / den), 0.0)
            + jnp.where(row == CLASS_ROW, cls, 0.0))


def _router_logits_t(h, rhi_ref, rlo_ref):
    h_hi = h.astype(BF16)
    h_lo = (h - h_hi.astype(F32)).astype(BF16)
    logits = _dot(h_hi, rhi_ref[...]) + _dot(h_hi, rlo_ref[...]) + _dot(h_lo, rhi_ref[...])
    return logits.T[:ROUTER_ROWS]


def _rows_to_lanes(gates_t):
    r, tm = gates_t.shape
    return jnp.concatenate([gates_t, jnp.zeros((LANES - r, tm), F32)], axis=0).T


def _outproj_kernel(x_ref, y_ref, o_ref, w_ref, nw_ref, rhi_ref, rlo_ref, rb_ref,
                    x1_ref, h_ref, r_ref, *, tiled_rows):
    dy = y_ref.shape[-1]
    tm, d = x_ref.shape
    rb = min(tm, OUT_PROJ_ROWS)
    for r0 in range(0, tm, rb):
        rs = slice(r0, r0 + rb)
        mix = _dot(y_ref[rs, :], w_ref[:dy, :]) + _dot(o_ref[rs, :], w_ref[dy:, :])
        x1 = x_ref[rs, :] + mix
        x1_ref[rs, :] = x1
        h = _rms_unit(x1) * nw_ref[...]
        gates_t = _route_t(_router_logits_t(h, rhi_ref, rlo_ref), rb_ref[...])
        if tiled_rows:
            h_ref[rs] = h.reshape(rb, d // LANES, LANES)
            r_ref[:, rs] = jnp.broadcast_to(gates_t[CLASS_ROW:CLASS_ROW + 1, :], (SUBLANES, rb))
        else:
            h_ref[rs, :] = h
            r_ref[rs, :] = _rows_to_lanes(gates_t)


def _out_proj(x, y, o, w, nw, r_hi, r_lo, r_bias, tm, tiled_rows):
    n, d = x.shape
    row = lambda width: pl.BlockSpec((tm, width), lambda i: (i, 0))
    consts = (w, nw, r_hi, r_lo, r_bias)
    if tiled_rows:
        h_spec = pl.BlockSpec((tm, d // LANES, LANES), lambda i: (i, 0, 0))
        h_shape = jax.ShapeDtypeStruct((n, d // LANES, LANES), F32)
        r_spec = pl.BlockSpec((SUBLANES, tm), lambda i: (i, 0))
        r_shape = jax.ShapeDtypeStruct((n // tm * SUBLANES, tm), F32)
    else:
        h_spec, h_shape = row(d), jax.ShapeDtypeStruct((n, d), F32)
        r_spec, r_shape = row(LANES), jax.ShapeDtypeStruct((n, LANES), F32)
    return pl.pallas_call(
        functools.partial(_outproj_kernel, tiled_rows=tiled_rows),
        grid=(n // tm,),
        in_specs=[row(d), row(y.shape[-1]), row(o.shape[-1])] + [_const_spec(c.shape) for c in consts],
        out_specs=[row(d), h_spec, r_spec],
        out_shape=[jax.ShapeDtypeStruct((n, d), F32), h_shape, r_shape],
        compiler_params=_cparams("arbitrary"),
    )(x, y, o, *consts)


def _pos_kernel(cls_ref, pos_ref, tab_ref, *, tile):
    tm = cls_ref.shape[1]
    rows = cls_ref.shape[0] // SUBLANES
    cls = cls_ref[...].reshape(rows, SUBLANES, tm)[:, 0, :]
    ri = lax.broadcasted_iota(jnp.int32, (tm, tm), 0)
    ci = lax.broadcasted_iota(jnp.int32, (tm, tm), 1)
    upper = (ri <= ci).astype(BF16)
    rr = lax.broadcasted_iota(jnp.int32, (rows, rows), 0)
    rc = lax.broadcasted_iota(jnp.int32, (rows, rows), 1)
    earlier = (rc < rr).astype(BF16)
    tile_lane = lax.broadcasted_iota(jnp.int32, (1, LANES), 1).astype(F32)
    pos = jnp.zeros((rows, tm), F32)
    tiles_before = jnp.zeros((1, 1), F32)
    tile_class = jnp.zeros((1, LANES), F32)
    pad_start = jnp.zeros((1, LANES), F32)
    pad_len = jnp.zeros((1, LANES), F32)
    for c in range(N_CLASSES):
        onehot = jnp.where(cls == float(c), 1.0, 0.0)
        inc = _dot(onehot.astype(BF16), upper)
        rowtot = jnp.broadcast_to(inc[:, tm - 1:tm], (rows, tm))
        rowpre = _cumsum_rows(rowtot, earlier)
        cnt = rowpre[rows - 1:rows, 0:1] + rowtot[rows - 1:rows, 0:1]
        pos = pos + onehot * (tiles_before * tile + rowpre + inc - 1.0)
        ntiles = jnp.floor((cnt + (tile - 1.0)) * (1.0 / tile))
        pad_start = pad_start + jnp.where(tile_lane == float(c), tiles_before * tile + cnt, 0.0)
        pad_len = pad_len + jnp.where(tile_lane == float(c), ntiles * tile - cnt, 0.0)
        tiles_before = tiles_before + ntiles
        tile_class = tile_class + jnp.where(tile_lane >= tiles_before, 1.0, 0.0)
    pos_ref[...] = pos.astype(jnp.int32)
    pad_start = pad_start + jnp.where(tile_lane == float(N_CLASSES), tiles_before * tile, 0.0)
    used = tile_class < N_CLASSES
    group = jnp.floor(tile_class * (1.0 / PAIRS_PER_GROUP))
    pair = tile_class - group * PAIRS_PER_GROUP
    a = jnp.zeros_like(pair)
    b = jnp.zeros_like(pair)
    for k, (u, v) in enumerate(PAIR_ORDER):
        a = a + jnp.where(pair == float(k), float(u), 0.0)
        b = b + jnp.where(pair == float(k), float(v), 0.0)
    e1 = jnp.where(used, group * EXPERTS_PER_GROUP + a, float(N_EXPERTS))
    e2 = jnp.where(used, group * EXPERTS_PER_GROUP + b, float(N_EXPERTS))
    sub = lax.broadcasted_iota(jnp.int32, (SUBLANES, LANES), 0)
    tab = jnp.where(sub == 0, e1, jnp.where(sub == 1, e2, jnp.where(sub == 2, pad_start, pad_len)))
    tab_ref[...] = tab.astype(jnp.int32)


def _positions(cls, n, tile):
    rows, tm = cls.shape
    assert n // tile + N_CLASSES <= LANES
    pos, tab = pl.pallas_call(
        functools.partial(_pos_kernel, tile=tile),
        grid=(1,),
        in_specs=[_const_spec(cls.shape)],
        out_specs=[_const_spec((rows // SUBLANES, tm)), _const_spec((SUBLANES, LANES))],
        out_shape=[jax.ShapeDtypeStruct((rows // SUBLANES, tm), jnp.int32),
                   jax.ShapeDtypeStruct((SUBLANES, LANES), jnp.int32)],
        compiler_params=_cparams("arbitrary"),
    )(cls)
    return pos.reshape(n), tab[0], tab[1], tab[2], tab[3]


def _dispatch_kernel(pos_ref, pad_start_ref, pad_len_ref, h_ref, hs_ref, buf_ref, zero_ref, sem_ref,
                     zsem_ref, *, td):
    i = pl.program_id(0)
    nsteps = pl.num_programs(0)
    slot = i % 2

    tile = zero_ref.shape[0]
    n_sorted = hs_ref.shape[0]

    def pad_copies(act):
        used_end = pad_start_ref[N_CLASSES]
        for j in range(N_CLASSES):
            @pl.when(used_end + j * tile < n_sorted)
            def _(j=j):
                act(pltpu.make_async_copy(zero_ref, hs_ref.at[pl.ds(used_end + j * tile, tile)],
                                          zsem_ref.at[0]))

        def body(c, carry):
            length = pad_len_ref[c]
            off = pad_start_ref[c]
            piece = tile // 2
            while piece >= 1:
                @pl.when((length & piece) != 0)
                def _(off=off, piece=piece):
                    act(pltpu.make_async_copy(zero_ref.at[0:piece], hs_ref.at[pl.ds(off, piece)],
                                              zsem_ref.at[0]))
                off = off + (length & piece)
                piece //= 2
            return carry
        lax.fori_loop(0, N_CLASSES, body, 0)

    @pl.when(i == 0)
    def _():
        zero_ref[...] = jnp.zeros(zero_ref.shape, F32)
        pad_copies(lambda cp: cp.start())

    def row_copy(step, s, r):
        return pltpu.make_async_copy(buf_ref.at[s, r], hs_ref.at[pos_ref[step * td + r]], sem_ref.at[s])

    def start_all(step, s):
        def body(r, carry):
            row_copy(step, s, r).start()
            return carry
        lax.fori_loop(0, td, body, 0, unroll=8)

    def wait_all(step, s):
        def body(r, carry):
            row_copy(step, s, r).wait()
            return carry
        lax.fori_loop(0, td, body, 0, unroll=8)

    @pl.when(i >= 2)
    def _():
        wait_all(i - 2, slot)

    buf_ref[slot] = h_ref[...]
    start_all(i, slot)

    @pl.when(i == nsteps - 1)
    def _():
        @pl.when(i >= 1)
        def _():
            wait_all(i - 1, 1 - slot)
        wait_all(i, slot)
        pad_copies(lambda cp: cp.wait())


def _dispatch(h3, pos, pad_start, pad_len, n_sorted, tile, td):
    n = h3.shape[0]
    tok = h3.shape[1:]
    return pl.pallas_call(
        functools.partial(_dispatch_kernel, td=td),
        grid_spec=pltpu.PrefetchScalarGridSpec(
            num_scalar_prefetch=3,
            grid=(n // td,),
            in_specs=[pl.BlockSpec((td,) + tok, lambda i, *_: (i, 0, 0))],
            out_specs=pl.BlockSpec(memory_space=pl.ANY),
            scratch_shapes=[pltpu.VMEM((2, td) + tok, F32), pltpu.VMEM((tile,) + tok, F32),
                            pltpu.SemaphoreType.DMA((2,)), pltpu.SemaphoreType.DMA((1,))]),
        out_shape=jax.ShapeDtypeStruct((n_sorted,) + tok, F32),
        compiler_params=_cparams("arbitrary"),
    )(pos, pad_start, pad_len, h3)


def _expert(h, gates, e, wg, wu, wd):
    lane = lax.broadcasted_iota(jnp.int32, gates.shape, 1)
    gcol = jnp.sum(jnp.where(lane == e, gates, 0.0), axis=-1, keepdims=True)
    act = _silu(_dot(h, wg.astype(BF16))) * _dot(h, wu.astype(BF16)) * gcol
    return _dot(act.astype(BF16), wd.astype(BF16))


def _moe_dense_kernel(h_ref, gates_ref, wg_ref, wu_ref, wd_ref, out_ref):
    e = pl.program_id(1)
    part = _expert(h_ref[...].astype(BF16), gates_ref[...], e, wg_ref[0], wu_ref[0], wd_ref[0])

    @pl.when(e == 0)
    def _():
        out_ref[...] = part

    @pl.when(e > 0)
    def _():
        out_ref[...] += part


def _moe_dense(h, gates, wg, wu, wd, tm):
    n, d = h.shape
    ne, _, ff = wg.shape
    row = lambda width: pl.BlockSpec((tm, width), lambda i, e: (i, 0))
    return pl.pallas_call(
        _moe_dense_kernel,
        grid=(n // tm, ne),
        in_specs=[row(d), row(LANES),
                  pl.BlockSpec((1, d, ff), lambda i, e: (e, 0, 0)),
                  pl.BlockSpec((1, d, ff), lambda i, e: (e, 0, 0)),
                  pl.BlockSpec((1, ff, d), lambda i, e: (e, 0, 0))],
        out_specs=row(d),
        out_shape=jax.ShapeDtypeStruct((n, d), F32),
        compiler_params=_cparams("arbitrary", "arbitrary"),
    )(h, gates, wg, wu, wd)


def _moe_sorted_kernel(e1_ref, e2_ref, hs_ref, rhi_ref, rlo_ref, rb_ref,
                       wg1_ref, wu1_ref, wd1_ref, wg2_ref, wu2_ref, wd2_ref, ys_ref):
    e1 = e1_ref[pl.program_id(0)]
    e2 = e2_ref[pl.program_id(0)]
    tm = hs_ref.shape[0]
    d = wg1_ref.shape[1]

    @pl.when(e1 < N_EXPERTS)
    def _():
        h = hs_ref[...].reshape(tm, d)
        logits_t = _router_logits_t(h, rhi_ref, rlo_ref)
        group = lax.shift_right_logical(e1, jnp.int32(EXPERTS_PER_GROUP.bit_length() - 1))
        gates = _rows_to_lanes(_route_t(logits_t, rb_ref[...], group=group))
        h_bf = h.astype(BF16)
        acc = (_expert(h_bf, gates, e1, wg1_ref[0], wu1_ref[0], wd1_ref[0])
               + _expert(h_bf, gates, e2, wg2_ref[0], wu2_ref[0], wd2_ref[0]))
        ys_ref[...] = acc.reshape(ys_ref.shape)

    @pl.when(e1 >= N_EXPERTS)
    def _():
        ys_ref[...] = jnp.zeros(ys_ref.shape, F32)


def _moe_sorted(hs, e1_tab, e2_tab, r_hi, r_lo, r_bias, wg, wu, wd, tm):
    n_sorted = hs.shape[0]
    ne, d, ff = wg.shape
    first = lambda i, e1, e2: (jnp.minimum(e1[i], ne - 1), 0, 0)
    second = lambda i, e1, e2: (jnp.minimum(e2[i], ne - 1), 0, 0)
    tok = pl.BlockSpec((tm,) + hs.shape[1:], lambda i, e1, e2: (i, 0, 0))
    const = lambda shape: pl.BlockSpec(shape, lambda i, e1, e2: (0,) * len(shape))
    up = lambda sel: pl.BlockSpec((1, d, ff), sel)
    down = lambda sel: pl.BlockSpec((1, ff, d), sel)
    return pl.pallas_call(
        _moe_sorted_kernel,
        grid_spec=pltpu.PrefetchScalarGridSpec(
            num_scalar_prefetch=2,
            grid=(n_sorted // tm,),
            in_specs=[tok, const(r_hi.shape), const(r_lo.shape), const(r_bias.shape),
                      up(first), up(first), down(first), up(second), up(second), down(second)],
            out_specs=tok),
        out_shape=jax.ShapeDtypeStruct(hs.shape, F32),
        compiler_params=_cparams("arbitrary"),
    )(e1_tab, e2_tab, hs, r_hi, r_lo, r_bias, wg, wu, wd, wg, wu, wd)


def _ple_math(x, p, npw_ref, wg_ref, wp_ref, fw_ref):
    hn = (_rms_unit(x) * npw_ref[...]).astype(BF16)
    gate = _sigmoid(_dot(hn, wg_ref[...]))
    x = x + gate * _dot(p.astype(BF16), wp_ref[...])
    return _rms_unit(x) * fw_ref[...]


def _ple_kernel(x1_ref, moe_ref, p_ref, npw_ref, wg_ref, wp_ref, fw_ref, y_ref):
    y_ref[...] = _ple_math(x1_ref[...] + moe_ref[...], p_ref[...], npw_ref, wg_ref, wp_ref, fw_ref)


def _ple_gather_kernel(pos_ref, x1_ref, ys_ref, p_ref, npw_ref, wg_ref, wp_ref, fw_ref, y_ref,
                       buf_ref, sem_ref, *, tm):
    i = pl.program_id(0)
    nsteps = pl.num_programs(0)
    slot = i % 2

    def row_copy(step, s, r):
        return pltpu.make_async_copy(ys_ref.at[pos_ref[step * tm + r]], buf_ref.at[s, r], sem_ref.at[s])

    def start_all(step, s):
        def body(r, carry):
            row_copy(step, s, r).start()
            return carry
        lax.fori_loop(0, tm, body, 0, unroll=8)

    @pl.when(i == 0)
    def _():
        start_all(0, 0)

    @pl.when(i + 1 < nsteps)
    def _():
        start_all(i + 1, 1 - slot)

    def wait_row(r, carry):
        row_copy(i, slot, r).wait()
        return carry
    lax.fori_loop(0, tm, wait_row, 0, unroll=8)

    moe = buf_ref[slot].reshape(x1_ref.shape)
    y_ref[...] = _ple_math(x1_ref[...] + moe, p_ref[...], npw_ref, wg_ref, wp_ref, fw_ref)


def _ple(x1, moe, pos, p, npw, wg, wp, fw, tm):
    n, d = x1.shape
    consts = (npw, wg, wp, fw)
    out_shape = jax.ShapeDtypeStruct((n, d), F32)
    if pos is None:
        row = lambda width: pl.BlockSpec((tm, width), lambda i: (i, 0))
        return pl.pallas_call(
            _ple_kernel,
            grid=(n // tm,),
            in_specs=[row(d), row(d), row(p.shape[-1])] + [_const_spec(c.shape) for c in consts],
            out_specs=row(d),
            out_shape=out_shape,
            compiler_params=_cparams("arbitrary"),
        )(x1, moe, p, *consts)
    row = lambda width: pl.BlockSpec((tm, width), lambda i, pos: (i, 0))
    const = lambda shape: pl.BlockSpec(shape, lambda i, pos: (0,) * len(shape))
    return pl.pallas_call(
        functools.partial(_ple_gather_kernel, tm=tm),
        grid_spec=pltpu.PrefetchScalarGridSpec(
            num_scalar_prefetch=1,
            grid=(n // tm,),
            in_specs=[row(d), pl.BlockSpec(memory_space=pl.ANY), row(p.shape[-1])]
                     + [const(c.shape) for c in consts],
            out_specs=row(d),
            scratch_shapes=[pltpu.VMEM((2, tm) + moe.shape[1:], F32), pltpu.SemaphoreType.DMA((2,))]),
        out_shape=out_shape,
        compiler_params=_cparams("arbitrary"),
    )(pos, x1, moe, p, *consts)


def _prepare_weights(norm_mix_w, w_in, conv_w, conv_b, dt_bias, a_log, d_skip, ssd_norm_w,
                     lb_logits, hg_norm_w, w_out, norm_ffn_w, w_router_group, b_router_group,
                     w_router_expert, b_router_expert, w_exp_gate, w_exp_up, w_exp_down,
                     norm_ple_w, w_ple_gate, w_ple_proj, final_norm_w, layer):
    i = layer
    row = lambda t: t.reshape(1, -1).astype(F32)
    lane_pad = lambda t: jnp.pad(t, [(0, 0)] * (t.ndim - 1) + [(0, LANES - t.shape[-1])])
    splits = (SSD_D, CONV_DIM, SSD_HEADS, HG_D, HG_D, HG_D, HG_D)
    offs = [0]
    for s in splits:
        offs.append(offs[-1] + s)
    seg = lambda k: w_in[i][:, offs[k]:offs[k + 1]].astype(BF16)
    w_in_segs = (seg(0), seg(1), seg(3), seg(4), seg(5), seg(6), lane_pad(seg(2)))
    w_router = lane_pad(jnp.concatenate([w_router_expert[i], w_router_group[i]], axis=1).astype(F32))
    r_hi = w_router.astype(BF16)
    r_lo = (w_router - r_hi.astype(F32)).astype(BF16)
    r_bias = jnp.pad(jnp.concatenate([b_router_expert[i].reshape(-1), b_router_group[i]]).astype(F32),
                     [(0, ROUTER_ROWS - N_EXPERTS - N_EGROUPS)]).reshape(ROUTER_ROWS, 1)
    lb = jnp.cumsum(jax.nn.softmax(lb_logits.astype(F32), axis=0), axis=0)[i]
    return dict(
        norm_mix_w=row(norm_mix_w[i]), w_in=w_in_segs, conv_w=conv_w[i].astype(F32),
        conv_b=row(conv_b[i]), dt_bias=lane_pad(row(dt_bias[i])), a_log=lane_pad(row(a_log[i])),
        d_skip=row(jnp.repeat(d_skip[i], SSD_HEADDIM)), ssd_norm_w=row(ssd_norm_w[i]),
        lb=row(lb), hg_norm_w=row(hg_norm_w[i]), w_out=w_out[i].astype(BF16),
        norm_ffn_w=row(norm_ffn_w[i]), r_hi=r_hi, r_lo=r_lo, r_bias=r_bias,
        w_exp_gate=w_exp_gate[i], w_exp_up=w_exp_up[i], w_exp_down=w_exp_down[i],
        norm_ple_w=row(norm_ple_w[i]),
        w_ple_gate=w_ple_gate[i].astype(BF16), w_ple_proj=w_ple_proj[i].astype(BF16),
        final_norm_w=row(final_norm_w))


def _token_tile(n, cap):
    tm = cap
    while tm >= SUBLANES:
        if n % tm == 0:
            return tm
        tm //= 2
    raise ValueError(f"token count {n} must be a multiple of {SUBLANES}")


def _trunk(x, p, ssm_in, conv_in, hg_in, w):
    b, l, d = x.shape
    n = b * l
    tm = _token_tile(n, ROW_TILE)
    x2d = x.reshape(n, d)
    if l == 1 and ssm_in is not None:
        z, xbc, q, f, v, g, dt = _in_proj(x2d, w["norm_mix_w"], w["w_in"], _token_tile(n, IN_PROJ_TILE))
        y, conv_new, ssm_new = _ssd_step(z, xbc, dt, w["conv_w"], w["conv_b"], w["dt_bias"], w["a_log"],
                                         w["d_skip"], w["ssd_norm_w"], conv_in, ssm_in)
        o, hg_new = _hgrn_step(q, f, v, g, w["lb"], w["hg_norm_w"], hg_in)
    else:
        assert ssm_in is None and conv_in is None and hg_in is None, "multi-token groups start empty"
        y, o, conv_new, ssm_new, hg_new = _mixers(x2d, b, l, w)
    router = (w["r_hi"], w["r_lo"], w["r_bias"])
    experts = (w["w_exp_gate"], w["w_exp_up"], w["w_exp_down"])
    sorted_moe = n >= SORTED_MOE_MIN_TOKENS
    x1, h, routed = _out_proj(x2d, y.reshape(n, -1), o.reshape(n, -1), w["w_out"], w["norm_ffn_w"],
                              *router, tm, tiled_rows=sorted_moe)
    if sorted_moe:
        pos, e1_tab, e2_tab, pad_start, pad_len = _positions(routed, n, MOE_TILE)
        hs = _dispatch(h, pos, pad_start, pad_len, n + N_CLASSES * MOE_TILE, MOE_TILE, tm)
        moe = _moe_sorted(hs, e1_tab, e2_tab, *router, *experts, MOE_TILE)
    else:
        pos = None
        moe = _moe_dense(h, routed, *experts, tm)
    y_out = _ple(x1, moe, pos, p.reshape(n, -1), w["norm_ple_w"], w["w_ple_gate"], w["w_ple_proj"],
                 w["final_norm_w"], tm)
    return y_out.reshape(b, l, d), ssm_new[None], conv_new[None], hg_new[None]


def kernel(x_prompt, x_sample, state_ssm, state_conv, state_hgrn, p_prompt, p_sample, norm_mix_w, w_in, conv_w, conv_b, dt_bias, a_log, d_skip, ssd_norm_w, lb_logits, hg_norm_w, w_out, norm_ffn_w, w_router_group, b_router_group, w_router_expert, b_router_expert, w_exp_gate, w_exp_up, w_exp_down, norm_ple_w, w_ple_gate, w_ple_proj, final_norm_w):
    assert p_prompt.shape[0] == 1, "the per-layer-embedding kernel also applies the final norm: depth 1 only"
    w = _prepare_weights(norm_mix_w, w_in, conv_w, conv_b, dt_bias, a_log, d_skip, ssd_norm_w,
                         lb_logits, hg_norm_w, w_out, norm_ffn_w, w_router_group, b_router_group,
                         w_router_expert, b_router_expert, w_exp_gate, w_exp_up, w_exp_down,
                         norm_ple_w, w_ple_gate, w_ple_proj, final_norm_w, layer=0)
    y_p, ssm_p, conv_p, hg_p = _trunk(x_prompt, p_prompt[0], None, None, None, w)
    y_s, ssm_s, conv_s, hg_s = _trunk(x_sample, p_sample[0], state_ssm[0], state_conv[0],
                                      state_hgrn[0], w)
    return (y_p, y_s, ssm_p, conv_p, hg_p, ssm_s, conv_s, hg_s)
```

```python
import functools

import jax
import jax.numpy as jnp
from jax import lax
from jax.experimental import pallas as pl
from jax.experimental.pallas import tpu as pltpu

F32 = jnp.float32
BF16 = jnp.bfloat16

EPS = 1e-6
SSD_HEADS = 16
SSD_HEADDIM = 64
SSD_D = SSD_HEADS * SSD_HEADDIM
SSD_GROUPS = 2
D_STATE = 128
CONV_W = 4
CONV_DIM = SSD_D + 2 * SSD_GROUPS * D_STATE
HG_HEADS = 8
HG_DK = 128
HG_DV = 128
HG_D = HG_HEADS * HG_DV
N_EGROUPS = 4
EXPERTS_PER_GROUP = 4
N_EXPERTS = N_EGROUPS * EXPERTS_PER_GROUP

LANES = 128
SUBLANES = 8
VMEM_LIMIT = 48 * 1024 * 1024
CHUNK = 128
PROJ_ROWS = 256
HG_SAFE_LOG_DECAY = 60.0
ROUTER_ROWS = 32
CLASS_ROW = N_EXPERTS + N_EGROUPS
PAIRS_PER_GROUP = EXPERTS_PER_GROUP * (EXPERTS_PER_GROUP - 1) // 2
N_CLASSES = N_EGROUPS * PAIRS_PER_GROUP
PAIR_ORDER = ((0, 1), (2, 1), (2, 3), (0, 3), (0, 2), (1, 3))
assert len({frozenset(p) for p in PAIR_ORDER}) == PAIRS_PER_GROUP
MOE_TILE = 256
IN_PROJ_TILE = 256
ROW_TILE = 512
OUT_PROJ_ROWS = 256
LOG2E = 1.4426950408889634
SORTED_MOE_MIN_TOKENS = 8 * SUBLANES * LANES


def _dot(a, b):
    return jnp.dot(a, b, preferred_element_type=F32)


def _dot_nt(a, b):
    return lax.dot_general(a, b, (((1,), (1,)), ((), ())), preferred_element_type=F32)


def _dot_tn(a, b):
    return lax.dot_general(a, b, (((0,), (0,)), ((), ())), preferred_element_type=F32)


def _rms_unit(x):
    return x * lax.rsqrt(jnp.mean(x * x, axis=-1, keepdims=True) + EPS)


def _sigmoid(x):
    return 1.0 / (1.0 + jnp.exp(-x))


def _softplus(x):
    return jnp.maximum(x, 0.0) + jnp.log(1.0 + jnp.exp(-jnp.abs(x)))


def _silu(x):
    return x * _sigmoid(x)


def _split3(x):
    hi = x.astype(BF16)
    r = x - hi.astype(F32)
    mid = r.astype(BF16)
    lo = (r - mid.astype(F32)).astype(BF16)
    return hi, mid, lo


def _cumsum_rows(x, tri):
    hi, mid, lo = _split3(x)
    return _dot(tri, hi) + _dot(tri, mid) + _dot(tri, lo)


def _tri(n):
    r = lax.broadcasted_iota(jnp.int32, (n, n), 0)
    c = lax.broadcasted_iota(jnp.int32, (n, n), 1)
    return r >= c


def _cparams(*sem):
    return pltpu.CompilerParams(dimension_semantics=sem, vmem_limit_bytes=VMEM_LIMIT)


def _const_spec(shape):
    nd = len(shape)
    return pl.BlockSpec(shape, lambda *_: (0,) * nd, pipeline_mode=pl.Buffered(1))


def _inproj_kernel(x_ref, nw_ref, *refs):
    hb = (_rms_unit(x_ref[...]) * nw_ref[...]).astype(BF16)
    nseg = len(refs) // 2
    for w_ref, out_ref in zip(refs[:nseg], refs[nseg:]):
        n = out_ref.shape[-1]
        for c0 in range(0, n, 512):
            cw = min(512, n - c0)
            out_ref[:, c0:c0 + cw] = _dot(hb, w_ref[:, c0:c0 + cw])


def _in_proj(x, nw, ws, tm):
    n, d = x.shape
    widths = [w.shape[-1] for w in ws]
    return pl.pallas_call(
        _inproj_kernel,
        grid=(n // tm,),
        in_specs=[pl.BlockSpec((tm, d), lambda i: (i, 0)), _const_spec(nw.shape)]
                 + [_const_spec(w.shape) for w in ws],
        out_specs=[pl.BlockSpec((tm, s), lambda i: (i, 0)) for s in widths],
        out_shape=[jax.ShapeDtypeStruct((n, s), F32) for s in widths],
        compiler_params=_cparams("arbitrary"),
    )(x, nw, *ws)


def _heads_to_channels(t):
    rows = t.shape[0]
    first_half = lax.broadcasted_iota(jnp.int32, (rows, LANES), 1) < SSD_HEADDIM
    return jnp.concatenate(
        [jnp.where(first_half, jnp.broadcast_to(t[:, h:h + 1], (rows, LANES)),
                   jnp.broadcast_to(t[:, h + 1:h + 2], (rows, LANES)))
         for h in range(0, SSD_HEADS, LANES // SSD_HEADDIM)], axis=1)


PROJ_WIDTHS = (SSD_D, CONV_DIM, HG_D, HG_D, HG_D, HG_D, LANES)
OFF_Z, OFF_XBC, OFF_Q, OFF_F, OFF_V, OFF_G, OFF_DT = (sum(PROJ_WIDTHS[:k]) for k in range(len(PROJ_WIDTHS)))
PROJ_COLS = sum(PROJ_WIDTHS)


def _project(x, nw_ref, w_refs, dst_ref):
    hb = (_rms_unit(x) * nw_ref[...]).astype(BF16)
    off = 0
    for w_ref in w_refs:
        n = w_ref.shape[-1]
        for c0 in range(0, n, 512):
            cw = min(512, n - c0)
            dst_ref[:, off + c0:off + c0 + cw] = _dot(hb, w_ref[:, c0:c0 + cw])
        off += n


def _ssd_chunk(p_ref, cw_ref, cb_ref, dtb_ref, alog_ref, dsk_ref, nw_ref,
               y_ref, cout_ref, xc_ref, st_ref):
    rows = p_ref.shape[0]
    hist = SUBLANES
    hpg = SSD_HEADS // SSD_GROUPS
    gw = hpg * SSD_HEADDIM
    gn = D_STATE

    xc_ref[hist:hist + rows, :] = p_ref[:, OFF_XBC:OFF_XBC + CONV_DIM]
    base = hist - (CONV_W - 1)
    acc = cb_ref[...] + cw_ref[0:1, :] * xc_ref[base:base + rows, :]
    for k in range(1, CONV_W):
        acc = acc + cw_ref[k:k + 1, :] * xc_ref[base + k:base + k + rows, :]
    cout_ref[...] = xc_ref[base + rows:base + rows + CONV_W - 1, :]
    xc_ref[0:hist, :] = xc_ref[rows:rows + hist, :]
    xbc = _silu(acc)
    xs = xbc[:, :SSD_D]
    b_bf = [xbc[:, SSD_D + g * gn:SSD_D + (g + 1) * gn].astype(BF16) for g in range(SSD_GROUPS)]
    c_bf = [xbc[:, SSD_D + (SSD_GROUPS + g) * gn:SSD_D + (SSD_GROUPS + g + 1) * gn].astype(BF16)
            for g in range(SSD_GROUPS)]

    dt = _softplus(p_ref[:, OFF_DT:OFF_DT + LANES] + dtb_ref[...])
    a = dt * (-jnp.exp(alog_ref[...]))
    tri = _tri(rows)
    a_cs = _cumsum_rows(a, tri.astype(BF16)) * LOG2E
    a_cs_t = a_cs.T
    dt_full = _heads_to_channels(dt)
    a_full = _heads_to_channels(a_cs)
    first_half = lax.broadcasted_iota(jnp.int32, (rows, LANES), 1) < SSD_HEADDIM
    a_last = a_full[rows - 1:rows, :]

    xdt = xs * dt_full
    xdt_bf = xdt.astype(BF16)
    xw_bf = (xdt * jnp.exp2(a_last - a_full)).astype(BF16)
    skip = dsk_ref[...] * xs
    zg = _silu(p_ref[:, OFF_Z:OFF_Z + SSD_D])

    for g in range(SSD_GROUPS):
        gs = slice(g * gw, (g + 1) * gw)
        scores = jnp.where(tri, _dot_nt(c_bf[g], b_bf[g]), 0.0)
        st_old = st_ref[g]
        y_off = _dot(c_bf[g], st_old.astype(BF16)) * jnp.exp2(a_full[:, gs])
        st_ref[g] = st_old * jnp.exp2(a_last[:, gs]) + _dot_tn(b_bf[g], xw_bf[:, gs])
        parts = []
        for pair in range(hpg // 2):
            h0 = g * hpg + 2 * pair
            cols = slice(h0 * SSD_HEADDIM, (h0 + 2) * SSD_HEADDIM)
            x_pair = xdt_bf[:, cols]
            acc = None
            for hh, keep in ((h0, first_half), (h0 + 1, jnp.logical_not(first_half))):
                diff = jnp.broadcast_to(a_cs[:, hh:hh + 1], (rows, rows)) - a_cs_t[hh:hh + 1, :]
                m = (scores * jnp.exp2(jnp.minimum(diff, 0.0))).astype(BF16)
                part = _dot(m, jnp.where(keep, x_pair, jnp.zeros_like(x_pair)))
                acc = part if acc is None else acc + part
            parts.append(acc)
        y = jnp.concatenate(parts, axis=1) + y_off + skip[:, gs]
        y_ref[:, gs] = (_rms_unit(y * zg[:, gs]) * nw_ref[:, gs]).astype(y_ref.dtype)


def _hgrn_chunk(p_ref, lb_ref, nw_ref, o_ref, st_ref, a_ref, kk_ref, gcs_ref):
    rows = p_ref.shape[0]
    dk = HG_DK
    lb = lb_ref[...]
    sig = _sigmoid(p_ref[:, OFF_F:OFF_F + HG_D])
    logf = jnp.log2(lb + (1.0 - lb) * sig)
    kk = (1.0 - lb) * (1.0 - sig)
    tri = _tri(rows)
    gcs = _cumsum_rows(logf, tri.astype(BF16))
    glast = gcs[rows - 1:rows, :]
    qt = p_ref[:, OFF_Q:OFF_Q + HG_D] * jnp.exp2(gcs)
    kk_ref[...] = kk
    gcs_ref[...] = gcs

    kt = kk * jnp.exp2(-gcs)
    khat = kk * jnp.exp2(glast - gcs)
    for h in range(HG_HEADS):
        sl = slice(h * dk, (h + 1) * dk)
        a_ref[h] = _dot_nt(qt[:, sl].astype(BF16), kt[:, sl].astype(BF16))

    @pl.when(jnp.min(glast) < -HG_SAFE_LOG_DECAY * LOG2E)
    def _():
        ri = lax.broadcasted_iota(jnp.int32, (rows, dk), 0)
        ci = lax.broadcasted_iota(jnp.int32, (rows, rows), 1)
        for h in range(HG_HEADS):
            sl = slice(h * dk, (h + 1) * dk)
            q_h = p_ref[:, OFF_Q + h * dk:OFF_Q + (h + 1) * dk]

            def cols(j8, a_h):
                base = pl.multiple_of(j8 * SUBLANES, SUBLANES)
                k_blk = kk_ref[pl.ds(base, SUBLANES), sl]
                g_blk = gcs_ref[pl.ds(base, SUBLANES), sl]
                g_h = gcs_ref[:, sl]
                for r in range(SUBLANES):
                    j = base + r
                    t = q_h * k_blk[r:r + 1] * jnp.exp2(
                        jnp.where(ri >= j, g_h - g_blk[r:r + 1], -jnp.inf))
                    a_h = jnp.where(ci == j, jnp.sum(t, axis=-1, keepdims=True), a_h)
                return a_h

            a_ref[h] = lax.fori_loop(0, rows // SUBLANES, cols, jnp.zeros((rows, rows), F32))

    gate = _silu(p_ref[:, OFF_G:OFF_G + HG_D])
    for h in range(HG_HEADS):
        sl = slice(h * dk, (h + 1) * dk)
        st = st_ref[h]
        v_bf = p_ref[:, OFF_V + h * HG_DV:OFF_V + (h + 1) * HG_DV].astype(BF16)
        a_h = jnp.where(tri, a_ref[h], 0.0).astype(BF16)
        o = _dot(a_h, v_bf) + _dot_nt(qt[:, sl].astype(BF16), st.astype(BF16))
        st_ref[h] = st * jnp.exp2(glast[:, sl]) + _dot_tn(v_bf, khat[:, sl].astype(BF16))
        o_ref[:, sl] = (_rms_unit(o) * nw_ref[:, sl] * gate[:, sl]).astype(o_ref.dtype)


def _mixers_kernel(x_ref, xn_ref, nw_ref, wz_ref, wxbc_ref, wq_ref, wf_ref, wv_ref, wg_ref, wdt_ref,
                   cw_ref, cb_ref, dtb_ref, alog_ref, dsk_ref, snw_ref, lb_ref, hnw_ref,
                   y_ref, o_ref, cout_ref, sout_ref, hout_ref,
                   pa_ref, pb_ref, xc_ref, st_ref, hst_ref, a_ref, kk_ref, gcs_ref, *, chunks):
    s = pl.program_id(0)
    blk = pa_ref.shape[0]
    per_blk = blk // CHUNK
    c0 = (s * 2 * per_blk) % chunks
    w_refs = (wz_ref, wxbc_ref, wq_ref, wf_ref, wv_ref, wg_ref, wdt_ref)
    hpg = SSD_HEADS // SSD_GROUPS

    @pl.when(s == 0)
    def _():
        _project(x_ref[0:blk, :], nw_ref, w_refs, pa_ref)

    @pl.when(c0 == 0)
    def _():
        xc_ref[0:SUBLANES, :] = jnp.zeros((SUBLANES, CONV_DIM), F32)
        st_ref[...] = jnp.zeros(st_ref.shape, F32)
        hst_ref[...] = jnp.zeros(hst_ref.shape, F32)

    for half, (cur_ref, nxt_ref) in enumerate(((pa_ref, pb_ref), (pb_ref, pa_ref))):
        x_next = x_ref[blk:2 * blk, :] if half == 0 else xn_ref[0:blk, :]
        _project(x_next, nw_ref, w_refs, nxt_ref)
        for k in range(per_blk):
            p_ref = cur_ref.at[k * CHUNK:(k + 1) * CHUNK]
            rs = slice((half * per_blk + k) * CHUNK, (half * per_blk + k + 1) * CHUNK)
            _ssd_chunk(p_ref, cw_ref, cb_ref, dtb_ref, alog_ref, dsk_ref, snw_ref,
                       y_ref.at[rs], cout_ref.at[0], xc_ref, st_ref)
            _hgrn_chunk(p_ref, lb_ref, hnw_ref, o_ref.at[rs], hst_ref, a_ref, kk_ref, gcs_ref)

    @pl.when(c0 + 2 * per_blk == chunks)
    def _():
        for g in range(SSD_GROUPS):
            sout_ref[0, g * hpg:(g + 1) * hpg] = st_ref[g].T.reshape(hpg, SSD_HEADDIM, D_STATE)
        for h in range(HG_HEADS):
            hout_ref[0, h] = hst_ref[h].T


def _mixers(x, b, l, w):
    n, d = x.shape
    chunks = l // CHUNK
    step_rows = 2 * PROJ_ROWS
    assert PROJ_ROWS % CHUNK == 0 and l % step_rows == 0
    steps = n // step_rows
    consts = (w["norm_mix_w"], *w["w_in"], w["conv_w"], w["conv_b"], w["dt_bias"], w["a_log"],
              w["d_skip"], w["ssd_norm_w"], w["lb"], w["hg_norm_w"])
    tok = lambda width: pl.BlockSpec((step_rows, width), lambda s: (s, 0))
    per_seq = lambda shape: pl.BlockSpec((1,) + shape,
                                         lambda s: (s * step_rows // l,) + (0,) * len(shape))
    ssm_shape = (SSD_HEADS, SSD_HEADDIM, D_STATE)
    hg_shape = (HG_HEADS, HG_DK, HG_DV)
    return pl.pallas_call(
        functools.partial(_mixers_kernel, chunks=chunks),
        grid=(steps,),
        in_specs=[tok(d), pl.BlockSpec((step_rows, d), lambda s: (jnp.minimum(s + 1, steps - 1), 0))]
                 + [_const_spec(c.shape) for c in consts],
        out_specs=[tok(SSD_D), tok(HG_D), per_seq((CONV_W - 1, CONV_DIM)), per_seq(ssm_shape),
                   per_seq(hg_shape)],
        out_shape=[jax.ShapeDtypeStruct((n, SSD_D), BF16), jax.ShapeDtypeStruct((n, HG_D), BF16),
                   jax.ShapeDtypeStruct((b, CONV_W - 1, CONV_DIM), F32),
                   jax.ShapeDtypeStruct((b,) + ssm_shape, F32),
                   jax.ShapeDtypeStruct((b,) + hg_shape, F32)],
        scratch_shapes=[pltpu.VMEM((PROJ_ROWS, PROJ_COLS), F32), pltpu.VMEM((PROJ_ROWS, PROJ_COLS), F32),
                        pltpu.VMEM((CHUNK + SUBLANES, CONV_DIM), F32),
                        pltpu.VMEM((SSD_GROUPS, D_STATE, SSD_D // SSD_GROUPS), F32),
                        pltpu.VMEM((HG_HEADS, HG_DV, HG_DK), F32),
                        pltpu.VMEM((HG_HEADS, CHUNK, CHUNK), F32),
                        pltpu.VMEM((CHUNK, HG_D), F32), pltpu.VMEM((CHUNK, HG_D), F32)],
        compiler_params=_cparams("arbitrary"),
    )(x, x, *consts)


def _tokens_to_lanes(x):
    tb, n = x.shape
    return jnp.concatenate([x, jnp.zeros((LANES - tb, n), F32)], axis=0).T


def _step_decay_kernel(dt_ref, dtb_ref, alog_ref, da_ref):
    da_ref[...] = jnp.exp(_softplus(dt_ref[...] + dtb_ref[...]) * (-jnp.exp(alog_ref[...])))


def _ssd_step_kernel(da_ref, z_ref, xbc_ref, dt_ref, cw_ref, cb_ref, dtb_ref, dsk_ref, nw_ref,
                     cs_ref, st_ref, y_ref, cnew_ref, snew_ref, *, tb):
    cd = CONV_DIM
    gn = D_STATE
    hpg = SSD_HEADS // SSD_GROUPS
    gw = hpg * SSD_HEADDIM
    first = pl.program_id(0) * tb
    x_in = xbc_ref[...]
    acc = cb_ref[...] + cw_ref[CONV_W - 1:CONV_W, :] * x_in
    for k in range(CONV_W - 1):
        acc = acc + cw_ref[k:k + 1, :] * cs_ref[:, k * cd:(k + 1) * cd]
    cnew_ref[:, :(CONV_W - 2) * cd] = cs_ref[:, cd:]
    cnew_ref[:, (CONV_W - 2) * cd:] = x_in
    xbc = _silu(acc)
    xs = xbc[:, :SSD_D]
    dt = _softplus(dt_ref[...] + dtb_ref[...])
    xdt_t = _tokens_to_lanes(xs * _heads_to_channels(dt)).astype(BF16)
    row_tok = lax.broadcasted_iota(jnp.int32, (LANES, tb * gn), 0)
    col_tok = lax.broadcasted_iota(jnp.int32, (LANES, tb * gn), 1) // gn
    lane = lax.broadcasted_iota(jnp.int32, (SSD_HEADDIM, LANES), 1)
    y_groups = []
    for g in range(SSD_GROUPS):
        b_g = xbc[:, SSD_D + g * gn:SSD_D + (g + 1) * gn]
        c_g = xbc[:, SSD_D + (SSD_GROUPS + g) * gn:SSD_D + (SSD_GROUPS + g + 1) * gn]
        b_wide = jnp.concatenate([jnp.tile(b_g, (1, tb)), jnp.zeros((LANES - tb, tb * gn), F32)], axis=0)
        b_diag = jnp.where(row_tok == col_tok, b_wide, 0.0).astype(BF16)
        upd = _dot(xdt_t[g * gw:(g + 1) * gw, :], b_diag)
        y_heads = []
        for hl in range(hpg):
            h = g * hpg + hl
            y_h = jnp.zeros((SSD_HEADDIM, LANES), F32)
            for j in range(tb):
                new = (st_ref[j, h] * da_ref[first + j, h]
                       + upd[hl * SSD_HEADDIM:(hl + 1) * SSD_HEADDIM, j * gn:(j + 1) * gn])
                snew_ref[j, h] = new
                y_h = jnp.where(lane == j, jnp.sum(new * c_g[j:j + 1, :], axis=-1, keepdims=True), y_h)
            y_heads.append(y_h)
        y_groups.append(jnp.concatenate(y_heads, axis=0))
    y = jnp.concatenate(y_groups, axis=0).T[:tb]
    y = (y + dsk_ref[...] * xs) * _silu(z_ref[...])
    for g in range(SSD_GROUPS):
        gs = slice(g * gw, (g + 1) * gw)
        y_ref[:, gs] = (_rms_unit(y[:, gs]) * nw_ref[:, gs]).astype(y_ref.dtype)


def _ssd_step(z, xbc, dt, conv_w, conv_b, dt_bias, a_log, d_skip, norm_w, conv_init, ssm_init):
    b = z.shape[0]
    tb = SUBLANES
    assert b % tb == 0
    da = pl.pallas_call(
        _step_decay_kernel,
        grid=(1,),
        in_specs=[_const_spec(dt.shape), _const_spec(dt_bias.shape), _const_spec(a_log.shape)],
        out_specs=_const_spec(dt.shape),
        out_shape=jax.ShapeDtypeStruct(dt.shape, F32),
    )(dt, dt_bias, a_log)
    params = (conv_w, conv_b, dt_bias, d_skip, norm_w)
    hist = (CONV_W - 1) * CONV_DIM
    row = lambda w: pl.BlockSpec((tb, w), lambda i, da: (i, 0))
    const = lambda shape: pl.BlockSpec(shape, lambda i, da: (0,) * len(shape))
    state_shape = (SSD_HEADS, SSD_HEADDIM, D_STATE)
    st_spec = pl.BlockSpec((tb,) + state_shape, lambda i, da: (i, 0, 0, 0))
    y, conv_new, ssm_new = pl.pallas_call(
        functools.partial(_ssd_step_kernel, tb=tb),
        grid_spec=pltpu.PrefetchScalarGridSpec(
            num_scalar_prefetch=1,
            grid=(b // tb,),
            in_specs=[row(SSD_D), row(CONV_DIM), row(LANES)] + [const(p.shape) for p in params]
                     + [row(hist), st_spec],
            out_specs=[row(SSD_D), row(hist), st_spec]),
        out_shape=[jax.ShapeDtypeStruct((b, SSD_D), BF16), jax.ShapeDtypeStruct((b, hist), F32),
                   jax.ShapeDtypeStruct((b,) + state_shape, F32)],
        compiler_params=_cparams("arbitrary"),
    )(da, z, xbc, dt, *params, conv_init.reshape(b, hist), ssm_init)
    return y, conv_new.reshape(b, CONV_W - 1, CONV_DIM), ssm_new


def _hgrn_step_kernel(q_ref, f_ref, v_ref, g_ref, lb_ref, nw_ref, st_ref, o_ref, snew_ref, *, tb):
    dk, dv = HG_DK, HG_DV
    fr = f_ref[...]
    lb = lb_ref[...]
    sig = _sigmoid(fr)
    f_t = _tokens_to_lanes(lb + (1.0 - lb) * sig)
    k_t = _tokens_to_lanes((1.0 - lb) * (1.0 - sig)).astype(BF16)
    q_bf = q_ref[...].astype(BF16)
    v = v_ref[...]
    gate = _silu(g_ref[...])
    row_tok = lax.broadcasted_iota(jnp.int32, (LANES, tb * dv), 0)
    col_tok = lax.broadcasted_iota(jnp.int32, (LANES, tb * dv), 1) // dv
    sub = lax.broadcasted_iota(jnp.int32, (tb, dv), 0)
    for h in range(HG_HEADS):
        rs = slice(h * dk, (h + 1) * dk)
        vs = slice(h * dv, (h + 1) * dv)
        v_wide = jnp.concatenate([jnp.tile(v[:, vs], (1, tb)), jnp.zeros((LANES - tb, tb * dv), F32)], axis=0)
        v_diag = jnp.where(row_tok == col_tok, v_wide, 0.0).astype(BF16)
        upd = _dot(k_t[rs, :], v_diag)
        o_h = jnp.zeros((tb, dv), F32)
        for j in range(tb):
            fcol = jnp.broadcast_to(f_t[rs, j:j + 1], (dk, dv))
            new = st_ref[j, h] * fcol + upd[:, j * dv:(j + 1) * dv]
            snew_ref[j, h] = new
            o_h = jnp.where(sub == j, _dot(q_bf[:, rs], new.astype(BF16)), o_h)
        o_ref[:, vs] = (_rms_unit(o_h) * nw_ref[:, vs] * gate[:, vs]).astype(o_ref.dtype)


def _hgrn_step(q, f, v, g, lb, norm_w, init):
    b = q.shape[0]
    tb = SUBLANES
    assert b % tb == 0
    row = pl.BlockSpec((tb, HG_D), lambda i: (i, 0))
    state_shape = (HG_HEADS, HG_DK, HG_DV)
    st_spec = pl.BlockSpec((tb,) + state_shape, lambda i: (i, 0, 0, 0))
    return pl.pallas_call(
        functools.partial(_hgrn_step_kernel, tb=tb),
        grid=(b // tb,),
        in_specs=[row, row, row, row, _const_spec(lb.shape), _const_spec(norm_w.shape), st_spec],
        out_specs=[row, st_spec],
        out_shape=[jax.ShapeDtypeStruct((b, HG_D), BF16),
                   jax.ShapeDtypeStruct((b,) + state_shape, F32)],
        compiler_params=_cparams("arbitrary"),
    )(q, f, v, g, lb, norm_w, init)


def _route_t(logits, bias, group=None):
    row = lax.broadcasted_iota(jnp.int32, logits.shape, 0)
    ninf = -jnp.inf
    big = jnp.int32(ROUTER_ROWS)
    is_g = (row >= N_EXPERTS) & (row < N_EXPERTS + N_EGROUPS)
    cmax = lambda t: jnp.max(t, axis=0, keepdims=True)
    csum = lambda t: jnp.sum(t, axis=0, keepdims=True)
    first = lambda m: jnp.min(jnp.where(m, row, big), axis=0, keepdims=True)

    gl = jnp.where(is_g, logits, ninf)
    gp = jnp.exp(gl - cmax(gl))
    gprob = gp / csum(gp)
    biased = logits + bias
    if group is None:
        gb = jnp.where(is_g, biased, ninf)
        gsel = first(gb == cmax(gb))
    else:
        gsel = group + N_EXPERTS
    gw = csum(jnp.where(row == gsel, gprob, 0.0))
    e0 = (gsel - N_EXPERTS) * EXPERTS_PER_GROUP
    in_grp = (row >= e0) & (row < e0 + EXPERTS_PER_GROUP)
    el = jnp.where(in_grp, logits, ninf)
    ep = jnp.exp(el - cmax(el))
    eprob = ep / csum(ep)
    eb = jnp.where(in_grp, biased, ninf)
    i1 = first(eb == cmax(eb))
    eb2 = jnp.where(row == i1, ninf, eb)
    i2 = first(eb2 == cmax(eb2))
    p1 = csum(jnp.where(row == i1, eprob, 0.0))
    p2 = csum(jnp.where(row == i2, eprob, 0.0))
    den = p1 + p2
    a = jnp.minimum(i1, i2) - e0
    b = jnp.maximum(i1, i2) - e0
    group = jnp.broadcast_to(gsel - N_EXPERTS, (1, logits.shape[1])).astype(F32)
    pair = jnp.zeros_like(group)
    for k, (u, v) in enumerate(PAIR_ORDER):
        pair = pair + jnp.where((a == min(u, v)) & (b == max(u, v)), float(k), 0.0)
    cls = group * PAIRS_PER_GROUP + pair
    return (jnp.where(row == i1, gw * (p1 / den), 0.0)
            + jnp.where(row == i2, gw * (p2 / den), 0.0)
            + jnp.where(row == CLASS_ROW, cls, 0.0))


def _router_logits_t(h, rhi_ref, rlo_ref):
    h_hi = h.astype(BF16)
    h_lo = (h - h_hi.astype(F32)).astype(BF16)
    logits = _dot(h_hi, rhi_ref[...]) + _dot(h_hi, rlo_ref[...]) + _dot(h_lo, rhi_ref[...])
    return logits.T[:ROUTER_ROWS]


def _rows_to_lanes(gates_t):
    r, tm = gates_t.shape
    return jnp.concatenate([gates_t, jnp.zeros((LANES - r, tm), F32)], axis=0).T


def _outproj_kernel(x_ref, y_ref, o_ref, w_ref, nw_ref, rhi_ref, rlo_ref, rb_ref,
                    x1_ref, h_ref, r_ref, *, tiled_rows):
    dy = y_ref.shape[-1]
    tm, d = x_ref.shape
    rb = min(tm, OUT_PROJ_ROWS)
    for r0 in range(0, tm, rb):
        rs = slice(r0, r0 + rb)
        mix = _dot(y_ref[rs, :], w_ref[:dy, :]) + _dot(o_ref[rs, :], w_ref[dy:, :])
        x1 = x_ref[rs, :] + mix
        x1_ref[rs, :] = x1
        h = _rms_unit(x1) * nw_ref[...]
        gates_t = _route_t(_router_logits_t(h, rhi_ref, rlo_ref), rb_ref[...])
        if tiled_rows:
            h_ref[rs] = h.reshape(rb, d // LANES, LANES)
            r_ref[:, rs] = jnp.broadcast_to(gates_t[CLASS_ROW:CLASS_ROW + 1, :], (SUBLANES, rb))
        else:
            h_ref[rs, :] = h
            r_ref[rs, :] = _rows_to_lanes(gates_t)


def _out_proj(x, y, o, w, nw, r_hi, r_lo, r_bias, tm, tiled_rows):
    n, d = x.shape
    row = lambda width: pl.BlockSpec((tm, width), lambda i: (i, 0))
    consts = (w, nw, r_hi, r_lo, r_bias)
    if tiled_rows:
        h_spec = pl.BlockSpec((tm, d // LANES, LANES), lambda i: (i, 0, 0))
        h_shape = jax.ShapeDtypeStruct((n, d // LANES, LANES), F32)
        r_spec = pl.BlockSpec((SUBLANES, tm), lambda i: (i, 0))
        r_shape = jax.ShapeDtypeStruct((n // tm * SUBLANES, tm), F32)
    else:
        h_spec, h_shape = row(d), jax.ShapeDtypeStruct((n, d), F32)
        r_spec, r_shape = row(LANES), jax.ShapeDtypeStruct((n, LANES), F32)
    return pl.pallas_call(
        functools.partial(_outproj_kernel, tiled_rows=tiled_rows),
        grid=(n // tm,),
        in_specs=[row(d), row(y.shape[-1]), row(o.shape[-1])] + [_const_spec(c.shape) for c in consts],
        out_specs=[row(d), h_spec, r_spec],
        out_shape=[jax.ShapeDtypeStruct((n, d), F32), h_shape, r_shape],
        compiler_params=_cparams("arbitrary"),
    )(x, y, o, *consts)


def _pos_kernel(cls_ref, pos_ref, tab_ref, *, tile):
    rows, tm = cls_ref.shape
    cls = cls_ref[...]
    ri = lax.broadcasted_iota(jnp.int32, (tm, tm), 0)
    ci = lax.broadcasted_iota(jnp.int32, (tm, tm), 1)
    upper = (ri <= ci).astype(BF16)
    rr = lax.broadcasted_iota(jnp.int32, (rows, rows), 0)
    rc = lax.broadcasted_iota(jnp.int32, (rows, rows), 1)
    earlier = ((rc // SUBLANES < rr // SUBLANES) & (rc % SUBLANES == 0)).astype(BF16)
    tile_lane = lax.broadcasted_iota(jnp.int32, (1, LANES), 1).astype(F32)
    pos = jnp.zeros((rows, tm), F32)
    tiles_before = jnp.zeros((1, 1), F32)
    tile_class = jnp.zeros((1, LANES), F32)
    pad_start = jnp.zeros((1, LANES), F32)
    pad_len = jnp.zeros((1, LANES), F32)
    for c in range(N_CLASSES):
        onehot = jnp.where(cls == float(c), 1.0, 0.0)
        inc = _dot(onehot.astype(BF16), upper)
        rowtot = jnp.broadcast_to(inc[:, tm - 1:tm], (rows, tm))
        rowpre = _cumsum_rows(rowtot, earlier)
        cnt = rowpre[rows - 1:rows, 0:1] + rowtot[rows - 1:rows, 0:1]
        pos = pos + onehot * (tiles_before * tile + rowpre + inc - 1.0)
        ntiles = jnp.floor((cnt + (tile - 1.0)) * (1.0 / tile))
        pad_start = pad_start + jnp.where(tile_lane == float(c), tiles_before * tile + cnt, 0.0)
        pad_len = pad_len + jnp.where(tile_lane == float(c), ntiles * tile - cnt, 0.0)
        tiles_before = tiles_before + ntiles
        tile_class = tile_class + jnp.where(tile_lane >= tiles_before, 1.0, 0.0)
    pos_ref[...] = pos.astype(jnp.int32)
    pad_start = pad_start + jnp.where(tile_lane == float(N_CLASSES), tiles_before * tile, 0.0)
    used = tile_class < N_CLASSES
    group = jnp.floor(tile_class * (1.0 / PAIRS_PER_GROUP))
    pair = tile_class - group * PAIRS_PER_GROUP
    a = jnp.zeros_like(pair)
    b = jnp.zeros_like(pair)
    for k, (u, v) in enumerate(PAIR_ORDER):
        a = a + jnp.where(pair == float(k), float(u), 0.0)
        b = b + jnp.where(pair == float(k), float(v), 0.0)
    e1 = jnp.where(used, group * EXPERTS_PER_GROUP + a, float(N_EXPERTS))
    e2 = jnp.where(used, group * EXPERTS_PER_GROUP + b, float(N_EXPERTS))
    sub = lax.broadcasted_iota(jnp.int32, (SUBLANES, LANES), 0)
    tab = jnp.where(sub == 0, e1, jnp.where(sub == 1, e2, jnp.where(sub == 2, pad_start, pad_len)))
    tab_ref[...] = tab.astype(jnp.int32)


def _positions(cls, n, tile):
    rows, tm = cls.shape
    assert n // tile + N_CLASSES <= LANES
    pos, tab = pl.pallas_call(
        functools.partial(_pos_kernel, tile=tile),
        grid=(1,),
        in_specs=[_const_spec(cls.shape)],
        out_specs=[_const_spec(cls.shape), _const_spec((SUBLANES, LANES))],
        out_shape=[jax.ShapeDtypeStruct(cls.shape, jnp.int32),
                   jax.ShapeDtypeStruct((SUBLANES, LANES), jnp.int32)],
        compiler_params=_cparams("arbitrary"),
    )(cls)
    return pos.reshape(rows // SUBLANES, SUBLANES, tm)[:, 0, :].reshape(n), tab[0], tab[1], tab[2], tab[3]


def _dispatch_kernel(pos_ref, pad_start_ref, pad_len_ref, h_ref, hs_ref, buf_ref, zero_ref, sem_ref,
                     zsem_ref, *, td):
    i = pl.program_id(0)
    nsteps = pl.num_programs(0)
    slot = i % 2

    tile = zero_ref.shape[0]
    n_sorted = hs_ref.shape[0]

    def pad_copies(act):
        used_end = pad_start_ref[N_CLASSES]
        for j in range(N_CLASSES):
            @pl.when(used_end + j * tile < n_sorted)
            def _(j=j):
                act(pltpu.make_async_copy(zero_ref, hs_ref.at[pl.ds(used_end + j * tile, tile)],
                                          zsem_ref.at[0]))

        def body(c, carry):
            length = pad_len_ref[c]
            off = pad_start_ref[c]
            piece = tile // 2
            while piece >= 1:
                @pl.when((length & piece) != 0)
                def _(off=off, piece=piece):
                    act(pltpu.make_async_copy(zero_ref.at[0:piece], hs_ref.at[pl.ds(off, piece)],
                                              zsem_ref.at[0]))
                off = off + (length & piece)
                piece //= 2
            return carry
        lax.fori_loop(0, N_CLASSES, body, 0)

    @pl.when(i == 0)
    def _():
        zero_ref[...] = jnp.zeros(zero_ref.shape, F32)
        pad_copies(lambda cp: cp.start())

    def row_copy(step, s, r):
        return pltpu.make_async_copy(buf_ref.at[s, r], hs_ref.at[pos_ref[step * td + r]], sem_ref.at[s])

    def start_all(step, s):
        def body(r, carry):
            row_copy(step, s, r).start()
            return carry
        lax.fori_loop(0, td, body, 0, unroll=8)

    def wait_all(step, s):
        def body(r, carry):
            row_copy(step, s, r).wait()
            return carry
        lax.fori_loop(0, td, body, 0, unroll=8)

    @pl.when(i >= 2)
    def _():
        wait_all(i - 2, slot)

    buf_ref[slot] = h_ref[...]
    start_all(i, slot)

    @pl.when(i == nsteps - 1)
    def _():
        @pl.when(i >= 1)
        def _():
            wait_all(i - 1, 1 - slot)
        wait_all(i, slot)
        pad_copies(lambda cp: cp.wait())


def _dispatch(h3, pos, pad_start, pad_len, n_sorted, tile, td):
    n = h3.shape[0]
    tok = h3.shape[1:]
    return pl.pallas_call(
        functools.partial(_dispatch_kernel, td=td),
        grid_spec=pltpu.PrefetchScalarGridSpec(
            num_scalar_prefetch=3,
            grid=(n // td,),
            in_specs=[pl.BlockSpec((td,) + tok, lambda i, *_: (i, 0, 0))],
            out_specs=pl.BlockSpec(memory_space=pl.ANY),
            scratch_shapes=[pltpu.VMEM((2, td) + tok, F32), pltpu.VMEM((tile,) + tok, F32),
                            pltpu.SemaphoreType.DMA((2,)), pltpu.SemaphoreType.DMA((1,))]),
        out_shape=jax.ShapeDtypeStruct((n_sorted,) + tok, F32),
        compiler_params=_cparams("arbitrary"),
    )(pos, pad_start, pad_len, h3)


def _expert(h, gates, e, wg, wu, wd):
    lane = lax.broadcasted_iota(jnp.int32, gates.shape, 1)
    gcol = jnp.sum(jnp.where(lane == e, gates, 0.0), axis=-1, keepdims=True)
    act = _silu(_dot(h, wg.astype(BF16))) * _dot(h, wu.astype(BF16)) * gcol
    return _dot(act.astype(BF16), wd.astype(BF16))


def _moe_dense_kernel(h_ref, gates_ref, wg_ref, wu_ref, wd_ref, out_ref):
    e = pl.program_id(1)
    part = _expert(h_ref[...].astype(BF16), gates_ref[...], e, wg_ref[0], wu_ref[0], wd_ref[0])

    @pl.when(e == 0)
    def _():
        out_ref[...] = part

    @pl.when(e > 0)
    def _():
        out_ref[...] += part


def _moe_dense(h, gates, wg, wu, wd, tm):
    n, d = h.shape
    ne, _, ff = wg.shape
    row = lambda width: pl.BlockSpec((tm, width), lambda i, e: (i, 0))
    return pl.pallas_call(
        _moe_dense_kernel,
        grid=(n // tm, ne),
        in_specs=[row(d), row(LANES),
                  pl.BlockSpec((1, d, ff), lambda i, e: (e, 0, 0)),
                  pl.BlockSpec((1, d, ff), lambda i, e: (e, 0, 0)),
                  pl.BlockSpec((1, ff, d), lambda i, e: (e, 0, 0))],
        out_specs=row(d),
        out_shape=jax.ShapeDtypeStruct((n, d), F32),
        compiler_params=_cparams("arbitrary", "arbitrary"),
    )(h, gates, wg, wu, wd)


def _moe_sorted_kernel(e1_ref, e2_ref, hs_ref, rhi_ref, rlo_ref, rb_ref, wg_hbm, wu_hbm, wd_hbm, ys_ref,
                       wg_buf, wu_buf, wd_buf, sem_ref, cur_ref):
    i = pl.program_id(0)
    nsteps = pl.num_programs(0)
    e1 = e1_ref[i]
    e2 = e2_ref[i]
    tm = hs_ref.shape[0]
    d = wg_buf.shape[2]
    valid = e1 < N_EXPERTS

    def weight_copies(slot, a, b):
        sem = sem_ref.at[slot]
        return [pltpu.make_async_copy(src.at[e], dst.at[slot, role], sem)
                for role, e in ((0, a), (1, b))
                for src, dst in ((wg_hbm, wg_buf), (wu_hbm, wu_buf), (wd_hbm, wd_buf))]

    prev = jnp.maximum(i - 1, 0)
    first = (i == 0) | (e1_ref[prev] != e1) | (e2_ref[prev] != e2)

    @pl.when(valid & first)
    def _():
        @pl.when(i == 0)
        def _():
            cur_ref[0] = 0
            for cp in weight_copies(0, e1, e2):
                cp.start()

        @pl.when(i > 0)
        def _():
            cur_ref[0] = 1 - cur_ref[0]

        cur = cur_ref[0]
        for cp in weight_copies(cur, e1, e2):
            cp.wait()
        last = nsteps - 1
        nxt = lax.while_loop(
            lambda j: (j < nsteps) & (e1_ref[jnp.minimum(j, last)] == e1) & (e2_ref[jnp.minimum(j, last)] == e2),
            lambda j: j + 1, i + 1)
        nj = jnp.minimum(nxt, last)

        @pl.when((nxt < nsteps) & (e1_ref[nj] < N_EXPERTS))
        def _():
            for cp in weight_copies(1 - cur, e1_ref[nj], e2_ref[nj]):
                cp.start()

    @pl.when(valid)
    def _():
        cur = cur_ref[0]
        h = hs_ref[...].reshape(tm, d)
        logits_t = _router_logits_t(h, rhi_ref, rlo_ref)
        group = lax.shift_right_logical(e1, jnp.int32(EXPERTS_PER_GROUP.bit_length() - 1))
        gates = _rows_to_lanes(_route_t(logits_t, rb_ref[...], group=group))
        h_bf = h.astype(BF16)
        acc = (_expert(h_bf, gates, e1, wg_buf[cur, 0], wu_buf[cur, 0], wd_buf[cur, 0])
               + _expert(h_bf, gates, e2, wg_buf[cur, 1], wu_buf[cur, 1], wd_buf[cur, 1]))
        ys_ref[...] = acc.reshape(ys_ref.shape)

    @pl.when(jnp.logical_not(valid))
    def _():
        ys_ref[...] = jnp.zeros(ys_ref.shape, F32)


def _moe_sorted(hs, e1_tab, e2_tab, r_hi, r_lo, r_bias, wg, wu, wd, tm):
    n_sorted = hs.shape[0]
    _, d, ff = wg.shape
    tok = pl.BlockSpec((tm,) + hs.shape[1:], lambda i, e1, e2: (i, 0, 0))
    const = lambda shape: pl.BlockSpec(shape, lambda i, e1, e2: (0,) * len(shape))
    anywhere = pl.BlockSpec(memory_space=pl.ANY)
    return pl.pallas_call(
        _moe_sorted_kernel,
        grid_spec=pltpu.PrefetchScalarGridSpec(
            num_scalar_prefetch=2,
            grid=(n_sorted // tm,),
            in_specs=[tok, const(r_hi.shape), const(r_lo.shape), const(r_bias.shape),
                      anywhere, anywhere, anywhere],
            out_specs=tok,
            scratch_shapes=[pltpu.VMEM((2, 2, d, ff), wg.dtype), pltpu.VMEM((2, 2, d, ff), wu.dtype),
                            pltpu.VMEM((2, 2, ff, d), wd.dtype), pltpu.SemaphoreType.DMA((2,)),
                            pltpu.SMEM((1,), jnp.int32)]),
        out_shape=jax.ShapeDtypeStruct(hs.shape, F32),
        compiler_params=_cparams("arbitrary"),
    )(e1_tab, e2_tab, hs, r_hi, r_lo, r_bias, wg, wu, wd)


def _ple_math(x, p, npw_ref, wg_ref, wp_ref, fw_ref):
    hn = (_rms_unit(x) * npw_ref[...]).astype(BF16)
    gate = _sigmoid(_dot(hn, wg_ref[...]))
    x = x + gate * _dot(p.astype(BF16), wp_ref[...])
    return _rms_unit(x) * fw_ref[...]


def _ple_kernel(x1_ref, moe_ref, p_ref, npw_ref, wg_ref, wp_ref, fw_ref, y_ref):
    y_ref[...] = _ple_math(x1_ref[...] + moe_ref[...], p_ref[...], npw_ref, wg_ref, wp_ref, fw_ref)


def _ple_gather_kernel(pos_ref, x1_ref, ys_ref, p_ref, npw_ref, wg_ref, wp_ref, fw_ref, y_ref,
                       buf_ref, sem_ref, *, tm):
    i = pl.program_id(0)
    nsteps = pl.num_programs(0)
    slot = i % 2

    def row_copy(step, s, r):
        return pltpu.make_async_copy(ys_ref.at[pos_ref[step * tm + r]], buf_ref.at[s, r], sem_ref.at[s])

    def start_all(step, s):
        def body(r, carry):
            row_copy(step, s, r).start()
            return carry
        lax.fori_loop(0, tm, body, 0, unroll=8)

    @pl.when(i == 0)
    def _():
        start_all(0, 0)

    @pl.when(i + 1 < nsteps)
    def _():
        start_all(i + 1, 1 - slot)

    def wait_row(r, carry):
        row_copy(i, slot, r).wait()
        return carry
    lax.fori_loop(0, tm, wait_row, 0, unroll=8)

    moe = buf_ref[slot].reshape(x1_ref.shape)
    y_ref[...] = _ple_math(x1_ref[...] + moe, p_ref[...], npw_ref, wg_ref, wp_ref, fw_ref)


def _ple(x1, moe, pos, p, npw, wg, wp, fw, tm):
    n, d = x1.shape
    consts = (npw, wg, wp, fw)
    out_shape = jax.ShapeDtypeStruct((n, d), F32)
    if pos is None:
        row = lambda width: pl.BlockSpec((tm, width), lambda i: (i, 0))
        return pl.pallas_call(
            _ple_kernel,
            grid=(n // tm,),
            in_specs=[row(d), row(d), row(p.shape[-1])] + [_const_spec(c.shape) for c in consts],
            out_specs=row(d),
            out_shape=out_shape,
            compiler_params=_cparams("arbitrary"),
        )(x1, moe, p, *consts)
    row = lambda width: pl.BlockSpec((tm, width), lambda i, pos: (i, 0))
    const = lambda shape: pl.BlockSpec(shape, lambda i, pos: (0,) * len(shape))
    return pl.pallas_call(
        functools.partial(_ple_gather_kernel, tm=tm),
        grid_spec=pltpu.PrefetchScalarGridSpec(
            num_scalar_prefetch=1,
            grid=(n // tm,),
            in_specs=[row(d), pl.BlockSpec(memory_space=pl.ANY), row(p.shape[-1])]
                     + [const(c.shape) for c in consts],
            out_specs=row(d),
            scratch_shapes=[pltpu.VMEM((2, tm) + moe.shape[1:], F32), pltpu.SemaphoreType.DMA((2,))]),
        out_shape=out_shape,
        compiler_params=_cparams("arbitrary"),
    )(pos, x1, moe, p, *consts)


def _prepare_weights(norm_mix_w, w_in, conv_w, conv_b, dt_bias, a_log, d_skip, ssd_norm_w,
                     lb_logits, hg_norm_w, w_out, norm_ffn_w, w_router_group, b_router_group,
                     w_router_expert, b_router_expert, w_exp_gate, w_exp_up, w_exp_down,
                     norm_ple_w, w_ple_gate, w_ple_proj, final_norm_w, layer):
    i = layer
    row = lambda t: t.reshape(1, -1).astype(F32)
    lane_pad = lambda t: jnp.pad(t, [(0, 0)] * (t.ndim - 1) + [(0, LANES - t.shape[-1])])
    splits = (SSD_D, CONV_DIM, SSD_HEADS, HG_D, HG_D, HG_D, HG_D)
    offs = [0]
    for s in splits:
        offs.append(offs[-1] + s)
    seg = lambda k: w_in[i][:, offs[k]:offs[k + 1]].astype(BF16)
    w_in_segs = (seg(0), seg(1), seg(3), seg(4), seg(5), seg(6), lane_pad(seg(2)))
    w_router = lane_pad(jnp.concatenate([w_router_expert[i], w_router_group[i]], axis=1).astype(F32))
    r_hi = w_router.astype(BF16)
    r_lo = (w_router - r_hi.astype(F32)).astype(BF16)
    r_bias = jnp.pad(jnp.concatenate([b_router_expert[i].reshape(-1), b_router_group[i]]).astype(F32),
                     [(0, ROUTER_ROWS - N_EXPERTS - N_EGROUPS)]).reshape(ROUTER_ROWS, 1)
    lb = jnp.cumsum(jax.nn.softmax(lb_logits.astype(F32), axis=0), axis=0)[i]
    return dict(
        norm_mix_w=row(norm_mix_w[i]), w_in=w_in_segs, conv_w=conv_w[i].astype(F32),
        conv_b=row(conv_b[i]), dt_bias=lane_pad(row(dt_bias[i])), a_log=lane_pad(row(a_log[i])),
        d_skip=row(jnp.repeat(d_skip[i], SSD_HEADDIM)), ssd_norm_w=row(ssd_norm_w[i]),
        lb=row(lb), hg_norm_w=row(hg_norm_w[i]), w_out=w_out[i].astype(BF16),
        norm_ffn_w=row(norm_ffn_w[i]), r_hi=r_hi, r_lo=r_lo, r_bias=r_bias,
        w_exp_gate=w_exp_gate[i], w_exp_up=w_exp_up[i], w_exp_down=w_exp_down[i],
        norm_ple_w=row(norm_ple_w[i]),
        w_ple_gate=w_ple_gate[i].astype(BF16), w_ple_proj=w_ple_proj[i].astype(BF16),
        final_norm_w=row(final_norm_w))


def _token_tile(n, cap):
    tm = cap
    while tm >= SUBLANES:
        if n % tm == 0:
            return tm
        tm //= 2
    raise ValueError(f"token count {n} must be a multiple of {SUBLANES}")


def _trunk(x, p, ssm_in, conv_in, hg_in, w):
    b, l, d = x.shape
    n = b * l
    tm = _token_tile(n, ROW_TILE)
    x2d = x.reshape(n, d)
    if l == 1 and ssm_in is not None:
        z, xbc, q, f, v, g, dt = _in_proj(x2d, w["norm_mix_w"], w["w_in"], _token_tile(n, IN_PROJ_TILE))
        y, conv_new, ssm_new = _ssd_step(z, xbc, dt, w["conv_w"], w["conv_b"], w["dt_bias"], w["a_log"],
                                         w["d_skip"], w["ssd_norm_w"], conv_in, ssm_in)
        o, hg_new = _hgrn_step(q, f, v, g, w["lb"], w["hg_norm_w"], hg_in)
    else:
        assert ssm_in is None and conv_in is None and hg_in is None, "multi-token groups start empty"
        y, o, conv_new, ssm_new, hg_new = _mixers(x2d, b, l, w)
    router = (w["r_hi"], w["r_lo"], w["r_bias"])
    experts = (w["w_exp_gate"], w["w_exp_up"], w["w_exp_down"])
    sorted_moe = n >= SORTED_MOE_MIN_TOKENS
    x1, h, routed = _out_proj(x2d, y.reshape(n, -1), o.reshape(n, -1), w["w_out"], w["norm_ffn_w"],
                              *router, tm, tiled_rows=sorted_moe)
    if sorted_moe:
        pos, e1_tab, e2_tab, pad_start, pad_len = _positions(routed, n, MOE_TILE)
        hs = _dispatch(h, pos, pad_start, pad_len, n + N_CLASSES * MOE_TILE, MOE_TILE, tm)
        moe = _moe_sorted(hs, e1_tab, e2_tab, *router, *experts, MOE_TILE)
    else:
        pos = None
        moe = _moe_dense(h, routed, *experts, tm)
    y_out = _ple(x1, moe, pos, p.reshape(n, -1), w["norm_ple_w"], w["w_ple_gate"], w["w_ple_proj"],
                 w["final_norm_w"], tm)
    return y_out.reshape(b, l, d), ssm_new[None], conv_new[None], hg_new[None]


def kernel(x_prompt, x_sample, state_ssm, state_conv, state_hgrn, p_prompt, p_sample, norm_mix_w, w_in, conv_w, conv_b, dt_bias, a_log, d_skip, ssd_norm_w, lb_logits, hg_norm_w, w_out, norm_ffn_w, w_router_group, b_router_group, w_router_expert, b_router_expert, w_exp_gate, w_exp_up, w_exp_down, norm_ple_w, w_ple_gate, w_ple_proj, final_norm_w):
    assert p_prompt.shape[0] == 1, "the per-layer-embedding kernel also applies the final norm: depth 1 only"
    w = _prepare_weights(norm_mix_w, w_in, conv_w, conv_b, dt_bias, a_log, d_skip, ssd_norm_w,
                         lb_logits, hg_norm_w, w_out, norm_ffn_w, w_router_group, b_router_group,
                         w_router_expert, b_router_expert, w_exp_gate, w_exp_up, w_exp_down,
                         norm_ple_w, w_ple_gate, w_ple_proj, final_norm_w, layer=0)
    y_p, ssm_p, conv_p, hg_p = _trunk(x_prompt, p_prompt[0], None, None, None, w)
    y_s, ssm_s, conv_s, hg_s = _trunk(x_sample, p_sample[0], state_ssm[0], state_conv[0],
                                      state_hgrn[0], w)
    return (y_p, y_s, ssm_p, conv_p, hg_p, ssm_s, conv_s, hg_s)
```

```python
import functools

import jax
import jax.numpy as jnp
from jax import lax
from jax.experimental import pallas as pl
from jax.experimental.pallas import tpu as pltpu

F32 = jnp.float32
BF16 = jnp.bfloat16

EPS = 1e-6
SSD_HEADS = 16
SSD_HEADDIM = 64
SSD_D = SSD_HEADS * SSD_HEADDIM
SSD_GROUPS = 2
D_STATE = 128
CONV_W = 4
CONV_DIM = SSD_D + 2 * SSD_GROUPS * D_STATE
HG_HEADS = 8
HG_DK = 128
HG_DV = 128
HG_D = HG_HEADS * HG_DV
N_EGROUPS = 4
EXPERTS_PER_GROUP = 4
N_EXPERTS = N_EGROUPS * EXPERTS_PER_GROUP

LANES = 128
SUBLANES = 8
VMEM_LIMIT = 48 * 1024 * 1024
CHUNK = 128
PROJ_ROWS = 256
HG_SAFE_LOG_DECAY = 60.0
ROUTER_ROWS = 32
CLASS_ROW = N_EXPERTS + N_EGROUPS
PAIRS_PER_GROUP = EXPERTS_PER_GROUP * (EXPERTS_PER_GROUP - 1) // 2
N_CLASSES = N_EGROUPS * PAIRS_PER_GROUP
PAIR_ORDER = ((0, 1), (2, 1), (2, 3), (0, 3), (0, 2), (1, 3))
assert len({frozenset(p) for p in PAIR_ORDER}) == PAIRS_PER_GROUP
MOE_TILE = 256
IN_PROJ_TILE = 256
ROW_TILE = 512
OUT_PROJ_ROWS = 256
LOG2E = 1.4426950408889634
SORTED_MOE_MIN_TOKENS = 8 * SUBLANES * LANES


def _dot(a, b):
    return jnp.dot(a, b, preferred_element_type=F32)


def _dot_nt(a, b):
    return lax.dot_general(a, b, (((1,), (1,)), ((), ())), preferred_element_type=F32)


def _dot_tn(a, b):
    return lax.dot_general(a, b, (((0,), (0,)), ((), ())), preferred_element_type=F32)


def _rms_unit(x):
    return x * lax.rsqrt(jnp.mean(x * x, axis=-1, keepdims=True) + EPS)


def _sigmoid(x):
    return 1.0 / (1.0 + jnp.exp(-x))


def _softplus(x):
    return jnp.maximum(x, 0.0) + jnp.log(1.0 + jnp.exp(-jnp.abs(x)))


def _silu(x):
    return x * _sigmoid(x)


def _split3(x):
    hi = x.astype(BF16)
    r = x - hi.astype(F32)
    mid = r.astype(BF16)
    lo = (r - mid.astype(F32)).astype(BF16)
    return hi, mid, lo


def _cumsum_rows(x, tri):
    hi, mid, lo = _split3(x)
    return _dot(tri, hi) + _dot(tri, mid) + _dot(tri, lo)


def _tri(n):
    r = lax.broadcasted_iota(jnp.int32, (n, n), 0)
    c = lax.broadcasted_iota(jnp.int32, (n, n), 1)
    return r >= c


def _cparams(*sem):
    return pltpu.CompilerParams(dimension_semantics=sem, vmem_limit_bytes=VMEM_LIMIT)


def _const_spec(shape):
    nd = len(shape)
    return pl.BlockSpec(shape, lambda *_: (0,) * nd, pipeline_mode=pl.Buffered(1))


def _inproj_kernel(x_ref, nw_ref, *refs):
    hb = (_rms_unit(x_ref[...]) * nw_ref[...]).astype(BF16)
    nseg = len(refs) // 2
    for w_ref, out_ref in zip(refs[:nseg], refs[nseg:]):
        n = out_ref.shape[-1]
        for c0 in range(0, n, 512):
            cw = min(512, n - c0)
            out_ref[:, c0:c0 + cw] = _dot(hb, w_ref[:, c0:c0 + cw])


def _in_proj(x, nw, ws, tm):
    n, d = x.shape
    widths = [w.shape[-1] for w in ws]
    return pl.pallas_call(
        _inproj_kernel,
        grid=(n // tm,),
        in_specs=[pl.BlockSpec((tm, d), lambda i: (i, 0)), _const_spec(nw.shape)]
                 + [_const_spec(w.shape) for w in ws],
        out_specs=[pl.BlockSpec((tm, s), lambda i: (i, 0)) for s in widths],
        out_shape=[jax.ShapeDtypeStruct((n, s), F32) for s in widths],
        compiler_params=_cparams("arbitrary"),
    )(x, nw, *ws)


def _heads_to_channels(t):
    rows = t.shape[0]
    first_half = lax.broadcasted_iota(jnp.int32, (rows, LANES), 1) < SSD_HEADDIM
    return jnp.concatenate(
        [jnp.where(first_half, jnp.broadcast_to(t[:, h:h + 1], (rows, LANES)),
                   jnp.broadcast_to(t[:, h + 1:h + 2], (rows, LANES)))
         for h in range(0, SSD_HEADS, LANES // SSD_HEADDIM)], axis=1)


PROJ_WIDTHS = (SSD_D, CONV_DIM, HG_D, HG_D, HG_D, HG_D, LANES)
OFF_Z, OFF_XBC, OFF_Q, OFF_F, OFF_V, OFF_G, OFF_DT = (sum(PROJ_WIDTHS[:k]) for k in range(len(PROJ_WIDTHS)))
PROJ_COLS = sum(PROJ_WIDTHS)


def _project(x, nw_ref, w_refs, dst_ref):
    hb = (_rms_unit(x) * nw_ref[...]).astype(BF16)
    off = 0
    for w_ref in w_refs:
        n = w_ref.shape[-1]
        for c0 in range(0, n, 512):
            cw = min(512, n - c0)
            dst_ref[:, off + c0:off + c0 + cw] = _dot(hb, w_ref[:, c0:c0 + cw])
        off += n


def _ssd_chunk(p_ref, cw_ref, cb_ref, dtb_ref, alog_ref, dsk_ref, nw_ref,
               y_ref, cout_ref, xc_ref, st_ref):
    rows = p_ref.shape[0]
    hist = SUBLANES
    hpg = SSD_HEADS // SSD_GROUPS
    gw = hpg * SSD_HEADDIM
    gn = D_STATE

    xc_ref[hist:hist + rows, :] = p_ref[:, OFF_XBC:OFF_XBC + CONV_DIM]
    base = hist - (CONV_W - 1)
    acc = cb_ref[...] + cw_ref[0:1, :] * xc_ref[base:base + rows, :]
    for k in range(1, CONV_W):
        acc = acc + cw_ref[k:k + 1, :] * xc_ref[base + k:base + k + rows, :]
    cout_ref[...] = xc_ref[base + rows:base + rows + CONV_W - 1, :]
    xc_ref[0:hist, :] = xc_ref[rows:rows + hist, :]
    xbc = _silu(acc)
    xs = xbc[:, :SSD_D]
    b_bf = [xbc[:, SSD_D + g * gn:SSD_D + (g + 1) * gn].astype(BF16) for g in range(SSD_GROUPS)]
    c_bf = [xbc[:, SSD_D + (SSD_GROUPS + g) * gn:SSD_D + (SSD_GROUPS + g + 1) * gn].astype(BF16)
            for g in range(SSD_GROUPS)]

    dt = _softplus(p_ref[:, OFF_DT:OFF_DT + LANES] + dtb_ref[...])
    a = dt * (-jnp.exp(alog_ref[...]))
    tri = _tri(rows)
    a_cs = _cumsum_rows(a, tri.astype(BF16)) * LOG2E
    a_cs_t = a_cs.T
    dt_full = _heads_to_channels(dt)
    a_full = _heads_to_channels(a_cs)
    first_half = lax.broadcasted_iota(jnp.int32, (rows, LANES), 1) < SSD_HEADDIM
    a_last = a_full[rows - 1:rows, :]

    xdt = xs * dt_full
    xdt_bf = xdt.astype(BF16)
    xw_bf = (xdt * jnp.exp2(a_last - a_full)).astype(BF16)
    skip = dsk_ref[...] * xs
    zg = _silu(p_ref[:, OFF_Z:OFF_Z + SSD_D])

    for g in range(SSD_GROUPS):
        gs = slice(g * gw, (g + 1) * gw)
        scores = jnp.where(tri, _dot_nt(c_bf[g], b_bf[g]), 0.0)
        st_old = st_ref[g]
        y_off = _dot(c_bf[g], st_old.astype(BF16)) * jnp.exp2(a_full[:, gs])
        st_ref[g] = st_old * jnp.exp2(a_last[:, gs]) + _dot_tn(b_bf[g], xw_bf[:, gs])
        parts = []
        for pair in range(hpg // 2):
            h0 = g * hpg + 2 * pair
            cols = slice(h0 * SSD_HEADDIM, (h0 + 2) * SSD_HEADDIM)
            x_pair = xdt_bf[:, cols]
            acc = None
            for hh, keep in ((h0, first_half), (h0 + 1, jnp.logical_not(first_half))):
                diff = jnp.broadcast_to(a_cs[:, hh:hh + 1], (rows, rows)) - a_cs_t[hh:hh + 1, :]
                m = (scores * jnp.exp2(jnp.minimum(diff, 0.0))).astype(BF16)
                part = _dot(m, jnp.where(keep, x_pair, jnp.zeros_like(x_pair)))
                acc = part if acc is None else acc + part
            parts.append(acc)
        y = jnp.concatenate(parts, axis=1) + y_off + skip[:, gs]
        y_ref[:, gs] = (_rms_unit(y * zg[:, gs]) * nw_ref[:, gs]).astype(y_ref.dtype)


def _hgrn_chunk(p_ref, lb_ref, nw_ref, o_ref, st_ref, a_ref, kk_ref, gcs_ref):
    rows = p_ref.shape[0]
    dk = HG_DK
    lb = lb_ref[...]
    sig = _sigmoid(p_ref[:, OFF_F:OFF_F + HG_D])
    logf = jnp.log2(lb + (1.0 - lb) * sig)
    kk = (1.0 - lb) * (1.0 - sig)
    tri = _tri(rows)
    gcs = _cumsum_rows(logf, tri.astype(BF16))
    glast = gcs[rows - 1:rows, :]
    qt = p_ref[:, OFF_Q:OFF_Q + HG_D] * jnp.exp2(gcs)
    kk_ref[...] = kk
    gcs_ref[...] = gcs

    kt = kk * jnp.exp2(-gcs)
    khat = kk * jnp.exp2(glast - gcs)
    for h in range(HG_HEADS):
        sl = slice(h * dk, (h + 1) * dk)
        a_ref[h] = _dot_nt(qt[:, sl].astype(BF16), kt[:, sl].astype(BF16))

    @pl.when(jnp.min(glast) < -HG_SAFE_LOG_DECAY * LOG2E)
    def _():
        ri = lax.broadcasted_iota(jnp.int32, (rows, dk), 0)
        ci = lax.broadcasted_iota(jnp.int32, (rows, rows), 1)
        for h in range(HG_HEADS):
            sl = slice(h * dk, (h + 1) * dk)
            q_h = p_ref[:, OFF_Q + h * dk:OFF_Q + (h + 1) * dk]

            def cols(j8, a_h):
                base = pl.multiple_of(j8 * SUBLANES, SUBLANES)
                k_blk = kk_ref[pl.ds(base, SUBLANES), sl]
                g_blk = gcs_ref[pl.ds(base, SUBLANES), sl]
                g_h = gcs_ref[:, sl]
                for r in range(SUBLANES):
                    j = base + r
                    t = q_h * k_blk[r:r + 1] * jnp.exp2(
                        jnp.where(ri >= j, g_h - g_blk[r:r + 1], -jnp.inf))
                    a_h = jnp.where(ci == j, jnp.sum(t, axis=-1, keepdims=True), a_h)
                return a_h

            a_ref[h] = lax.fori_loop(0, rows // SUBLANES, cols, jnp.zeros((rows, rows), F32))

    gate = _silu(p_ref[:, OFF_G:OFF_G + HG_D])
    for h in range(HG_HEADS):
        sl = slice(h * dk, (h + 1) * dk)
        st = st_ref[h]
        v_bf = p_ref[:, OFF_V + h * HG_DV:OFF_V + (h + 1) * HG_DV].astype(BF16)
        a_h = jnp.where(tri, a_ref[h], 0.0).astype(BF16)
        o = _dot(a_h, v_bf) + _dot_nt(qt[:, sl].astype(BF16), st.astype(BF16))
        st_ref[h] = st * jnp.exp2(glast[:, sl]) + _dot_tn(v_bf, khat[:, sl].astype(BF16))
        o_ref[:, sl] = (_rms_unit(o) * nw_ref[:, sl] * gate[:, sl]).astype(o_ref.dtype)


def _mixers_kernel(x_ref, xn_ref, nw_ref, wz_ref, wxbc_ref, wq_ref, wf_ref, wv_ref, wg_ref, wdt_ref,
                   cw_ref, cb_ref, dtb_ref, alog_ref, dsk_ref, snw_ref, lb_ref, hnw_ref,
                   y_ref, o_ref, cout_ref, sout_ref, hout_ref,
                   pa_ref, pb_ref, xc_ref, st_ref, hst_ref, a_ref, kk_ref, gcs_ref, *, chunks):
    s = pl.program_id(0)
    blk = pa_ref.shape[0]
    per_blk = blk // CHUNK
    c0 = (s * 2 * per_blk) % chunks
    w_refs = (wz_ref, wxbc_ref, wq_ref, wf_ref, wv_ref, wg_ref, wdt_ref)
    hpg = SSD_HEADS // SSD_GROUPS

    @pl.when(s == 0)
    def _():
        _project(x_ref[0:blk, :], nw_ref, w_refs, pa_ref)

    @pl.when(c0 == 0)
    def _():
        xc_ref[0:SUBLANES, :] = jnp.zeros((SUBLANES, CONV_DIM), F32)
        st_ref[...] = jnp.zeros(st_ref.shape, F32)
        hst_ref[...] = jnp.zeros(hst_ref.shape, F32)

    for half, (cur_ref, nxt_ref) in enumerate(((pa_ref, pb_ref), (pb_ref, pa_ref))):
        x_next = x_ref[blk:2 * blk, :] if half == 0 else xn_ref[0:blk, :]
        _project(x_next, nw_ref, w_refs, nxt_ref)
        for k in range(per_blk):
            p_ref = cur_ref.at[k * CHUNK:(k + 1) * CHUNK]
            rs = slice((half * per_blk + k) * CHUNK, (half * per_blk + k + 1) * CHUNK)
            _ssd_chunk(p_ref, cw_ref, cb_ref, dtb_ref, alog_ref, dsk_ref, snw_ref,
                       y_ref.at[rs], cout_ref.at[0], xc_ref, st_ref)
            _hgrn_chunk(p_ref, lb_ref, hnw_ref, o_ref.at[rs], hst_ref, a_ref, kk_ref, gcs_ref)

    @pl.when(c0 + 2 * per_blk == chunks)
    def _():
        for g in range(SSD_GROUPS):
            sout_ref[0, g * hpg:(g + 1) * hpg] = st_ref[g].T.reshape(hpg, SSD_HEADDIM, D_STATE)
        for h in range(HG_HEADS):
            hout_ref[0, h] = hst_ref[h].T


def _mixers(x, b, l, w):
    n, d = x.shape
    chunks = l // CHUNK
    step_rows = 2 * PROJ_ROWS
    assert PROJ_ROWS % CHUNK == 0 and l % step_rows == 0
    steps = n // step_rows
    consts = (w["norm_mix_w"], *w["w_in"], w["conv_w"], w["conv_b"], w["dt_bias"], w["a_log"],
              w["d_skip"], w["ssd_norm_w"], w["lb"], w["hg_norm_w"])
    tok = lambda width: pl.BlockSpec((step_rows, width), lambda s: (s, 0))
    per_seq = lambda shape: pl.BlockSpec((1,) + shape,
                                         lambda s: (s * step_rows // l,) + (0,) * len(shape))
    ssm_shape = (SSD_HEADS, SSD_HEADDIM, D_STATE)
    hg_shape = (HG_HEADS, HG_DK, HG_DV)
    return pl.pallas_call(
        functools.partial(_mixers_kernel, chunks=chunks),
        grid=(steps,),
        in_specs=[tok(d), pl.BlockSpec((step_rows, d), lambda s: (jnp.minimum(s + 1, steps - 1), 0))]
                 + [_const_spec(c.shape) for c in consts],
        out_specs=[tok(SSD_D), tok(HG_D), per_seq((CONV_W - 1, CONV_DIM)), per_seq(ssm_shape),
                   per_seq(hg_shape)],
        out_shape=[jax.ShapeDtypeStruct((n, SSD_D), BF16), jax.ShapeDtypeStruct((n, HG_D), BF16),
                   jax.ShapeDtypeStruct((b, CONV_W - 1, CONV_DIM), F32),
                   jax.ShapeDtypeStruct((b,) + ssm_shape, F32),
                   jax.ShapeDtypeStruct((b,) + hg_shape, F32)],
        scratch_shapes=[pltpu.VMEM((PROJ_ROWS, PROJ_COLS), F32), pltpu.VMEM((PROJ_ROWS, PROJ_COLS), F32),
                        pltpu.VMEM((CHUNK + SUBLANES, CONV_DIM), F32),
                        pltpu.VMEM((SSD_GROUPS, D_STATE, SSD_D // SSD_GROUPS), F32),
                        pltpu.VMEM((HG_HEADS, HG_DV, HG_DK), F32),
                        pltpu.VMEM((HG_HEADS, CHUNK, CHUNK), F32),
                        pltpu.VMEM((CHUNK, HG_D), F32), pltpu.VMEM((CHUNK, HG_D), F32)],
        compiler_params=_cparams("arbitrary"),
    )(x, x, *consts)


def _tokens_to_lanes(x):
    tb, n = x.shape
    return jnp.concatenate([x, jnp.zeros((LANES - tb, n), F32)], axis=0).T


def _step_decay_kernel(dt_ref, dtb_ref, alog_ref, da_ref):
    da_ref[...] = jnp.exp(_softplus(dt_ref[...] + dtb_ref[...]) * (-jnp.exp(alog_ref[...])))


def _ssd_step_kernel(da_ref, z_ref, xbc_ref, dt_ref, cw_ref, cb_ref, dtb_ref, dsk_ref, nw_ref,
                     cs_ref, st_ref, y_ref, cnew_ref, snew_ref, *, tb):
    cd = CONV_DIM
    gn = D_STATE
    hpg = SSD_HEADS // SSD_GROUPS
    gw = hpg * SSD_HEADDIM
    first = pl.program_id(0) * tb
    x_in = xbc_ref[...]
    acc = cb_ref[...] + cw_ref[CONV_W - 1:CONV_W, :] * x_in
    for k in range(CONV_W - 1):
        acc = acc + cw_ref[k:k + 1, :] * cs_ref[:, k * cd:(k + 1) * cd]
    cnew_ref[:, :(CONV_W - 2) * cd] = cs_ref[:, cd:]
    cnew_ref[:, (CONV_W - 2) * cd:] = x_in
    xbc = _silu(acc)
    xs = xbc[:, :SSD_D]
    dt = _softplus(dt_ref[...] + dtb_ref[...])
    xdt_t = _tokens_to_lanes(xs * _heads_to_channels(dt)).astype(BF16)
    row_tok = lax.broadcasted_iota(jnp.int32, (LANES, tb * gn), 0)
    col_tok = lax.broadcasted_iota(jnp.int32, (LANES, tb * gn), 1) // gn
    lane = lax.broadcasted_iota(jnp.int32, (SSD_HEADDIM, LANES), 1)
    y_groups = []
    for g in range(SSD_GROUPS):
        b_g = xbc[:, SSD_D + g * gn:SSD_D + (g + 1) * gn]
        c_g = xbc[:, SSD_D + (SSD_GROUPS + g) * gn:SSD_D + (SSD_GROUPS + g + 1) * gn]
        b_wide = jnp.concatenate([jnp.tile(b_g, (1, tb)), jnp.zeros((LANES - tb, tb * gn), F32)], axis=0)
        b_diag = jnp.where(row_tok == col_tok, b_wide, 0.0).astype(BF16)
        upd = _dot(xdt_t[g * gw:(g + 1) * gw, :], b_diag)
        y_heads = []
        for hl in range(hpg):
            h = g * hpg + hl
            y_h = jnp.zeros((SSD_HEADDIM, LANES), F32)
            for j in range(tb):
                new = (st_ref[j, h] * da_ref[first + j, h]
                       + upd[hl * SSD_HEADDIM:(hl + 1) * SSD_HEADDIM, j * gn:(j + 1) * gn])
                snew_ref[j, h] = new
                y_h = jnp.where(lane == j, jnp.sum(new * c_g[j:j + 1, :], axis=-1, keepdims=True), y_h)
            y_heads.append(y_h)
        y_groups.append(jnp.concatenate(y_heads, axis=0))
    y = jnp.concatenate(y_groups, axis=0).T[:tb]
    y = (y + dsk_ref[...] * xs) * _silu(z_ref[...])
    for g in range(SSD_GROUPS):
        gs = slice(g * gw, (g + 1) * gw)
        y_ref[:, gs] = (_rms_unit(y[:, gs]) * nw_ref[:, gs]).astype(y_ref.dtype)


def _ssd_step(z, xbc, dt, conv_w, conv_b, dt_bias, a_log, d_skip, norm_w, conv_init, ssm_init):
    b = z.shape[0]
    tb = SUBLANES
    assert b % tb == 0
    da = pl.pallas_call(
        _step_decay_kernel,
        grid=(1,),
        in_specs=[_const_spec(dt.shape), _const_spec(dt_bias.shape), _const_spec(a_log.shape)],
        out_specs=_const_spec(dt.shape),
        out_shape=jax.ShapeDtypeStruct(dt.shape, F32),
    )(dt, dt_bias, a_log)
    params = (conv_w, conv_b, dt_bias, d_skip, norm_w)
    hist = (CONV_W - 1) * CONV_DIM
    row = lambda w: pl.BlockSpec((tb, w), lambda i, da: (i, 0))
    const = lambda shape: pl.BlockSpec(shape, lambda i, da: (0,) * len(shape))
    state_shape = (SSD_HEADS, SSD_HEADDIM, D_STATE)
    st_spec = pl.BlockSpec((tb,) + state_shape, lambda i, da: (i, 0, 0, 0))
    y, conv_new, ssm_new = pl.pallas_call(
        functools.partial(_ssd_step_kernel, tb=tb),
        grid_spec=pltpu.PrefetchScalarGridSpec(
            num_scalar_prefetch=1,
            grid=(b // tb,),
            in_specs=[row(SSD_D), row(CONV_DIM), row(LANES)] + [const(p.shape) for p in params]
                     + [row(hist), st_spec],
            out_specs=[row(SSD_D), row(hist), st_spec]),
        out_shape=[jax.ShapeDtypeStruct((b, SSD_D), BF16), jax.ShapeDtypeStruct((b, hist), F32),
                   jax.ShapeDtypeStruct((b,) + state_shape, F32)],
        compiler_params=_cparams("arbitrary"),
    )(da, z, xbc, dt, *params, conv_init.reshape(b, hist), ssm_init)
    return y, conv_new.reshape(b, CONV_W - 1, CONV_DIM), ssm_new


def _hgrn_step_kernel(q_ref, f_ref, v_ref, g_ref, lb_ref, nw_ref, st_ref, o_ref, snew_ref, *, tb):
    dk, dv = HG_DK, HG_DV
    fr = f_ref[...]
    lb = lb_ref[...]
    sig = _sigmoid(fr)
    f_t = _tokens_to_lanes(lb + (1.0 - lb) * sig)
    k_t = _tokens_to_lanes((1.0 - lb) * (1.0 - sig)).astype(BF16)
    q_bf = q_ref[...].astype(BF16)
    v = v_ref[...]
    gate = _silu(g_ref[...])
    row_tok = lax.broadcasted_iota(jnp.int32, (LANES, tb * dv), 0)
    col_tok = lax.broadcasted_iota(jnp.int32, (LANES, tb * dv), 1) // dv
    sub = lax.broadcasted_iota(jnp.int32, (tb, dv), 0)
    for h in range(HG_HEADS):
        rs = slice(h * dk, (h + 1) * dk)
        vs = slice(h * dv, (h + 1) * dv)
        v_wide = jnp.concatenate([jnp.tile(v[:, vs], (1, tb)), jnp.zeros((LANES - tb, tb * dv), F32)], axis=0)
        v_diag = jnp.where(row_tok == col_tok, v_wide, 0.0).astype(BF16)
        upd = _dot(k_t[rs, :], v_diag)
        o_h = jnp.zeros((tb, dv), F32)
        for j in range(tb):
            fcol = jnp.broadcast_to(f_t[rs, j:j + 1], (dk, dv))
            new = st_ref[j, h] * fcol + upd[:, j * dv:(j + 1) * dv]
            snew_ref[j, h] = new
            o_h = jnp.where(sub == j, _dot(q_bf[:, rs], new.astype(BF16)), o_h)
        o_ref[:, vs] = (_rms_unit(o_h) * nw_ref[:, vs] * gate[:, vs]).astype(o_ref.dtype)


def _hgrn_step(q, f, v, g, lb, norm_w, init):
    b = q.shape[0]
    tb = SUBLANES
    assert b % tb == 0
    row = pl.BlockSpec((tb, HG_D), lambda i: (i, 0))
    state_shape = (HG_HEADS, HG_DK, HG_DV)
    st_spec = pl.BlockSpec((tb,) + state_shape, lambda i: (i, 0, 0, 0))
    return pl.pallas_call(
        functools.partial(_hgrn_step_kernel, tb=tb),
        grid=(b // tb,),
        in_specs=[row, row, row, row, _const_spec(lb.shape), _const_spec(norm_w.shape), st_spec],
        out_specs=[row, st_spec],
        out_shape=[jax.ShapeDtypeStruct((b, HG_D), BF16),
                   jax.ShapeDtypeStruct((b,) + state_shape, F32)],
        compiler_params=_cparams("arbitrary"),
    )(q, f, v, g, lb, norm_w, init)


def _route_t(logits, bias, group=None):
    row = lax.broadcasted_iota(jnp.int32, logits.shape, 0)
    ninf = -jnp.inf
    big = jnp.int32(ROUTER_ROWS)
    is_g = (row >= N_EXPERTS) & (row < N_EXPERTS + N_EGROUPS)
    cmax = lambda t: jnp.max(t, axis=0, keepdims=True)
    csum = lambda t: jnp.sum(t, axis=0, keepdims=True)
    first = lambda m: jnp.min(jnp.where(m, row, big), axis=0, keepdims=True)

    gl = jnp.where(is_g, logits, ninf)
    gp = jnp.exp(gl - cmax(gl))
    gprob = gp / csum(gp)
    biased = logits + bias
    if group is None:
        gb = jnp.where(is_g, biased, ninf)
        gsel = first(gb == cmax(gb))
    else:
        gsel = group + N_EXPERTS
    gw = csum(jnp.where(row == gsel, gprob, 0.0))
    e0 = (gsel - N_EXPERTS) * EXPERTS_PER_GROUP
    in_grp = (row >= e0) & (row < e0 + EXPERTS_PER_GROUP)
    el = jnp.where(in_grp, logits, ninf)
    ep = jnp.exp(el - cmax(el))
    eprob = ep / csum(ep)
    eb = jnp.where(in_grp, biased, ninf)
    i1 = first(eb == cmax(eb))
    eb2 = jnp.where(row == i1, ninf, eb)
    i2 = first(eb2 == cmax(eb2))
    p1 = csum(jnp.where(row == i1, eprob, 0.0))
    p2 = csum(jnp.where(row == i2, eprob, 0.0))
    den = p1 + p2
    a = jnp.minimum(i1, i2) - e0
    b = jnp.maximum(i1, i2) - e0
    group = jnp.broadcast_to(gsel - N_EXPERTS, (1, logits.shape[1])).astype(F32)
    pair = jnp.zeros_like(group)
    for k, (u, v) in enumerate(PAIR_ORDER):
        pair = pair + jnp.where((a == min(u, v)) & (b == max(u, v)), float(k), 0.0)
    cls = group * PAIRS_PER_GROUP + pair
    return (jnp.where(row == i1, gw * (p1 / den), 0.0)
            + jnp.where(row == i2, gw * (p2 / den), 0.0)
            + jnp.where(row == CLASS_ROW, cls, 0.0))


def _router_logits_t(h, rhi_ref, rlo_ref):
    h_hi = h.astype(BF16)
    h_lo = (h - h_hi.astype(F32)).astype(BF16)
    logits = _dot(h_hi, rhi_ref[...]) + _dot(h_hi, rlo_ref[...]) + _dot(h_lo, rhi_ref[...])
    return logits.T[:ROUTER_ROWS]


def _rows_to_lanes(gates_t):
    r, tm = gates_t.shape
    return jnp.concatenate([gates_t, jnp.zeros((LANES - r, tm), F32)], axis=0).T


def _outproj_kernel(x_ref, y_ref, o_ref, w_ref, nw_ref, rhi_ref, rlo_ref, rb_ref,
                    x1_ref, h_ref, r_ref, *, tiled_rows):
    dy = y_ref.shape[-1]
    tm, d = x_ref.shape
    rb = min(tm, OUT_PROJ_ROWS)
    for r0 in range(0, tm, rb):
        rs = slice(r0, r0 + rb)
        mix = _dot(y_ref[rs, :], w_ref[:dy, :]) + _dot(o_ref[rs, :], w_ref[dy:, :])
        x1 = x_ref[rs, :] + mix
        x1_ref[rs, :] = x1
        h = _rms_unit(x1) * nw_ref[...]
        gates_t = _route_t(_router_logits_t(h, rhi_ref, rlo_ref), rb_ref[...])
        if tiled_rows:
            h_ref[rs] = h.reshape(rb, d // LANES, LANES)
            r_ref[:, rs] = jnp.broadcast_to(gates_t[CLASS_ROW:CLASS_ROW + 1, :], (SUBLANES, rb))
        else:
            h_ref[rs, :] = h
            r_ref[rs, :] = _rows_to_lanes(gates_t)


def _out_proj(x, y, o, w, nw, r_hi, r_lo, r_bias, tm, tiled_rows):
    n, d = x.shape
    row = lambda width: pl.BlockSpec((tm, width), lambda i: (i, 0))
    consts = (w, nw, r_hi, r_lo, r_bias)
    if tiled_rows:
        h_spec = pl.BlockSpec((tm, d // LANES, LANES), lambda i: (i, 0, 0))
        h_shape = jax.ShapeDtypeStruct((n, d // LANES, LANES), F32)
        r_spec = pl.BlockSpec((SUBLANES, tm), lambda i: (i, 0))
        r_shape = jax.ShapeDtypeStruct((n // tm * SUBLANES, tm), F32)
    else:
        h_spec, h_shape = row(d), jax.ShapeDtypeStruct((n, d), F32)
        r_spec, r_shape = row(LANES), jax.ShapeDtypeStruct((n, LANES), F32)
    return pl.pallas_call(
        functools.partial(_outproj_kernel, tiled_rows=tiled_rows),
        grid=(n // tm,),
        in_specs=[row(d), row(y.shape[-1]), row(o.shape[-1])] + [_const_spec(c.shape) for c in consts],
        out_specs=[row(d), h_spec, r_spec],
        out_shape=[jax.ShapeDtypeStruct((n, d), F32), h_shape, r_shape],
        compiler_params=_cparams("arbitrary"),
    )(x, y, o, *consts)


def _pos_kernel(cls_ref, pos_ref, tab_ref, *, tile):
    rows, tm = cls_ref.shape
    cls = cls_ref[...]
    ri = lax.broadcasted_iota(jnp.int32, (tm, tm), 0)
    ci = lax.broadcasted_iota(jnp.int32, (tm, tm), 1)
    upper = (ri <= ci).astype(BF16)
    rr = lax.broadcasted_iota(jnp.int32, (rows, rows), 0)
    rc = lax.broadcasted_iota(jnp.int32, (rows, rows), 1)
    earlier = ((rc // SUBLANES < rr // SUBLANES) & (rc % SUBLANES == 0)).astype(BF16)
    tile_lane = lax.broadcasted_iota(jnp.int32, (1, LANES), 1).astype(F32)
    pos = jnp.zeros((rows, tm), F32)
    tiles_before = jnp.zeros((1, 1), F32)
    tile_class = jnp.zeros((1, LANES), F32)
    pad_start = jnp.zeros((1, LANES), F32)
    pad_len = jnp.zeros((1, LANES), F32)
    for c in range(N_CLASSES):
        onehot = jnp.where(cls == float(c), 1.0, 0.0)
        inc = _dot(onehot.astype(BF16), upper)
        rowtot = jnp.broadcast_to(inc[:, tm - 1:tm], (rows, tm))
        rowpre = _cumsum_rows(rowtot, earlier)
        cnt = rowpre[rows - 1:rows, 0:1] + rowtot[rows - 1:rows, 0:1]
        pos = pos + onehot * (tiles_before * tile + rowpre + inc - 1.0)
        ntiles = jnp.floor((cnt + (tile - 1.0)) * (1.0 / tile))
        pad_start = pad_start + jnp.where(tile_lane == float(c), tiles_before * tile + cnt, 0.0)
        pad_len = pad_len + jnp.where(tile_lane == float(c), ntiles * tile - cnt, 0.0)
        tiles_before = tiles_before + ntiles
        tile_class = tile_class + jnp.where(tile_lane >= tiles_before, 1.0, 0.0)
    pos_ref[...] = pos.astype(jnp.int32)
    pad_start = pad_start + jnp.where(tile_lane == float(N_CLASSES), tiles_before * tile, 0.0)
    used = tile_class < N_CLASSES
    group = jnp.floor(tile_class * (1.0 / PAIRS_PER_GROUP))
    pair = tile_class - group * PAIRS_PER_GROUP
    a = jnp.zeros_like(pair)
    b = jnp.zeros_like(pair)
    for k, (u, v) in enumerate(PAIR_ORDER):
        a = a + jnp.where(pair == float(k), float(u), 0.0)
        b = b + jnp.where(pair == float(k), float(v), 0.0)
    e1 = jnp.where(used, group * EXPERTS_PER_GROUP + a, float(N_EXPERTS))
    e2 = jnp.where(used, group * EXPERTS_PER_GROUP + b, float(N_EXPERTS))
    sub = lax.broadcasted_iota(jnp.int32, (SUBLANES, LANES), 0)
    tab = jnp.where(sub == 0, e1, jnp.where(sub == 1, e2, jnp.where(sub == 2, pad_start, pad_len)))
    tab_ref[...] = tab.astype(jnp.int32)


def _positions(cls, n, tile):
    rows, tm = cls.shape
    assert n // tile + N_CLASSES <= LANES
    pos, tab = pl.pallas_call(
        functools.partial(_pos_kernel, tile=tile),
        grid=(1,),
        in_specs=[_const_spec(cls.shape)],
        out_specs=[_const_spec(cls.shape), _const_spec((SUBLANES, LANES))],
        out_shape=[jax.ShapeDtypeStruct(cls.shape, jnp.int32),
                   jax.ShapeDtypeStruct((SUBLANES, LANES), jnp.int32)],
        compiler_params=_cparams("arbitrary"),
    )(cls)
    return pos.reshape(rows // SUBLANES, SUBLANES, tm)[:, 0, :].reshape(n), tab[0], tab[1], tab[2], tab[3]


def _dispatch_kernel(pos_ref, pad_start_ref, pad_len_ref, h_ref, hs_ref, buf_ref, zero_ref, sem_ref,
                     zsem_ref, *, td):
    i = pl.program_id(0)
    nsteps = pl.num_programs(0)
    slot = i % 2

    tile = zero_ref.shape[0]
    n_sorted = hs_ref.shape[0]

    def pad_copies(act):
        used_end = pad_start_ref[N_CLASSES]
        for j in range(N_CLASSES):
            @pl.when(used_end + j * tile < n_sorted)
            def _(j=j):
                act(pltpu.make_async_copy(zero_ref, hs_ref.at[pl.ds(used_end + j * tile, tile)],
                                          zsem_ref.at[0]))

        def body(c, carry):
            length = pad_len_ref[c]
            off = pad_start_ref[c]
            piece = tile // 2
            while piece >= 1:
                @pl.when((length & piece) != 0)
                def _(off=off, piece=piece):
                    act(pltpu.make_async_copy(zero_ref.at[0:piece], hs_ref.at[pl.ds(off, piece)],
                                              zsem_ref.at[0]))
                off = off + (length & piece)
                piece //= 2
            return carry
        lax.fori_loop(0, N_CLASSES, body, 0)

    @pl.when(i == 0)
    def _():
        zero_ref[...] = jnp.zeros(zero_ref.shape, F32)
        pad_copies(lambda cp: cp.start())

    def row_copy(step, s, r):
        return pltpu.make_async_copy(buf_ref.at[s, r], hs_ref.at[pos_ref[step * td + r]], sem_ref.at[s])

    def start_all(step, s):
        def body(r, carry):
            row_copy(step, s, r).start()
            return carry
        lax.fori_loop(0, td, body, 0, unroll=8)

    def wait_all(step, s):
        def body(r, carry):
            row_copy(step, s, r).wait()
            return carry
        lax.fori_loop(0, td, body, 0, unroll=8)

    @pl.when(i >= 2)
    def _():
        wait_all(i - 2, slot)

    buf_ref[slot] = h_ref[...]
    start_all(i, slot)

    @pl.when(i == nsteps - 1)
    def _():
        @pl.when(i >= 1)
        def _():
            wait_all(i - 1, 1 - slot)
        wait_all(i, slot)
        pad_copies(lambda cp: cp.wait())


def _dispatch(h3, pos, pad_start, pad_len, n_sorted, tile, td):
    n = h3.shape[0]
    tok = h3.shape[1:]
    return pl.pallas_call(
        functools.partial(_dispatch_kernel, td=td),
        grid_spec=pltpu.PrefetchScalarGridSpec(
            num_scalar_prefetch=3,
            grid=(n // td,),
            in_specs=[pl.BlockSpec((td,) + tok, lambda i, *_: (i, 0, 0))],
            out_specs=pl.BlockSpec(memory_space=pl.ANY),
            scratch_shapes=[pltpu.VMEM((2, td) + tok, F32), pltpu.VMEM((tile,) + tok, F32),
                            pltpu.SemaphoreType.DMA((2,)), pltpu.SemaphoreType.DMA((1,))]),
        out_shape=jax.ShapeDtypeStruct((n_sorted,) + tok, F32),
        compiler_params=_cparams("arbitrary"),
    )(pos, pad_start, pad_len, h3)


def _expert(h, gates, e, wg, wu, wd):
    lane = lax.broadcasted_iota(jnp.int32, gates.shape, 1)
    gcol = jnp.sum(jnp.where(lane == e, gates, 0.0), axis=-1, keepdims=True)
    act = _silu(_dot(h, wg.astype(BF16))) * _dot(h, wu.astype(BF16)) * gcol
    return _dot(act.astype(BF16), wd.astype(BF16))


def _moe_dense_kernel(h_ref, gates_ref, wg_ref, wu_ref, wd_ref, out_ref):
    e = pl.program_id(1)
    part = _expert(h_ref[...].astype(BF16), gates_ref[...], e, wg_ref[0], wu_ref[0], wd_ref[0])

    @pl.when(e == 0)
    def _():
        out_ref[...] = part

    @pl.when(e > 0)
    def _():
        out_ref[...] += part


def _moe_dense(h, gates, wg, wu, wd, tm):
    n, d = h.shape
    ne, _, ff = wg.shape
    row = lambda width: pl.BlockSpec((tm, width), lambda i, e: (i, 0))
    return pl.pallas_call(
        _moe_dense_kernel,
        grid=(n // tm, ne),
        in_specs=[row(d), row(LANES),
                  pl.BlockSpec((1, d, ff), lambda i, e: (e, 0, 0)),
                  pl.BlockSpec((1, d, ff), lambda i, e: (e, 0, 0)),
                  pl.BlockSpec((1, ff, d), lambda i, e: (e, 0, 0))],
        out_specs=row(d),
        out_shape=jax.ShapeDtypeStruct((n, d), F32),
        compiler_params=_cparams("arbitrary", "arbitrary"),
    )(h, gates, wg, wu, wd)


def _moe_sorted_kernel(e1_ref, e2_ref, hs_ref, rhi_ref, rlo_ref, rb_ref, wg_hbm, wu_hbm, wd_hbm, ys_ref,
                       wg_buf, wu_buf, wd_buf, sem_ref, cur_ref):
    i = pl.program_id(0)
    nsteps = pl.num_programs(0)
    e1 = e1_ref[i]
    e2 = e2_ref[i]
    tm = hs_ref.shape[0]
    d = wg_buf.shape[2]
    valid = e1 < N_EXPERTS

    def role_copies(role, slot, e):
        sem = sem_ref.at[role, slot]
        return [pltpu.make_async_copy(src.at[e], dst.at[slot, role], sem)
                for src, dst in ((wg_hbm, wg_buf), (wu_hbm, wu_buf), (wd_hbm, wd_buf))]

    prev = jnp.maximum(i - 1, 0)
    mine = (e1, e2)
    before = (e1_ref[prev], e2_ref[prev])
    first = (i == 0) | (before[0] != e1) | (before[1] != e2)

    @pl.when(valid & first)
    def _():
        last = nsteps - 1
        nxt = lax.while_loop(
            lambda j: (j < nsteps) & (e1_ref[jnp.minimum(j, last)] == e1) & (e2_ref[jnp.minimum(j, last)] == e2),
            lambda j: j + 1, i + 1)
        nj = jnp.minimum(nxt, last)
        has_next = (nxt < nsteps) & (e1_ref[nj] < N_EXPERTS)
        after = (e1_ref[nj], e2_ref[nj])
        for role in range(2):
            @pl.when(i == 0)
            def _(role=role):
                cur_ref[role] = 0
                for cp in role_copies(role, 0, mine[role]):
                    cp.start()
                for cp in role_copies(role, 0, mine[role]):
                    cp.wait()

            @pl.when((i > 0) & (before[role] != mine[role]))
            def _(role=role):
                cur_ref[role] = 1 - cur_ref[role]
                for cp in role_copies(role, cur_ref[role], mine[role]):
                    cp.wait()

            @pl.when(has_next & (after[role] != mine[role]))
            def _(role=role):
                for cp in role_copies(role, 1 - cur_ref[role], after[role]):
                    cp.start()

    @pl.when(valid)
    def _():
        ca = cur_ref[0]
        cb = cur_ref[1]
        h = hs_ref[...].reshape(tm, d)
        logits_t = _router_logits_t(h, rhi_ref, rlo_ref)
        group = lax.shift_right_logical(e1, jnp.int32(EXPERTS_PER_GROUP.bit_length() - 1))
        gates = _rows_to_lanes(_route_t(logits_t, rb_ref[...], group=group))
        h_bf = h.astype(BF16)
        acc = (_expert(h_bf, gates, e1, wg_buf[ca, 0], wu_buf[ca, 0], wd_buf[ca, 0])
               + _expert(h_bf, gates, e2, wg_buf[cb, 1], wu_buf[cb, 1], wd_buf[cb, 1]))
        ys_ref[...] = acc.reshape(ys_ref.shape)

    @pl.when(jnp.logical_not(valid))
    def _():
        ys_ref[...] = jnp.zeros(ys_ref.shape, F32)


def _moe_sorted(hs, e1_tab, e2_tab, r_hi, r_lo, r_bias, wg, wu, wd, tm):
    n_sorted = hs.shape[0]
    _, d, ff = wg.shape
    tok = pl.BlockSpec((tm,) + hs.shape[1:], lambda i, e1, e2: (i, 0, 0))
    const = lambda shape: pl.BlockSpec(shape, lambda i, e1, e2: (0,) * len(shape))
    anywhere = pl.BlockSpec(memory_space=pl.ANY)
    return pl.pallas_call(
        _moe_sorted_kernel,
        grid_spec=pltpu.PrefetchScalarGridSpec(
            num_scalar_prefetch=2,
            grid=(n_sorted // tm,),
            in_specs=[tok, const(r_hi.shape), const(r_lo.shape), const(r_bias.shape),
                      anywhere, anywhere, anywhere],
            out_specs=tok,
            scratch_shapes=[pltpu.VMEM((2, 2, d, ff), wg.dtype), pltpu.VMEM((2, 2, d, ff), wu.dtype),
                            pltpu.VMEM((2, 2, ff, d), wd.dtype), pltpu.SemaphoreType.DMA((2, 2)),
                            pltpu.SMEM((2,), jnp.int32)]),
        out_shape=jax.ShapeDtypeStruct(hs.shape, F32),
        compiler_params=_cparams("arbitrary"),
    )(e1_tab, e2_tab, hs, r_hi, r_lo, r_bias, wg, wu, wd)


def _ple_math(x, p, npw_ref, wg_ref, wp_ref, fw_ref):
    hn = (_rms_unit(x) * npw_ref[...]).astype(BF16)
    gate = _sigmoid(_dot(hn, wg_ref[...]))
    x = x + gate * _dot(p.astype(BF16), wp_ref[...])
    return _rms_unit(x) * fw_ref[...]


def _ple_kernel(x1_ref, moe_ref, p_ref, npw_ref, wg_ref, wp_ref, fw_ref, y_ref):
    y_ref[...] = _ple_math(x1_ref[...] + moe_ref[...], p_ref[...], npw_ref, wg_ref, wp_ref, fw_ref)


def _ple_gather_kernel(pos_ref, x1_ref, ys_ref, p_ref, npw_ref, wg_ref, wp_ref, fw_ref, y_ref,
                       buf_ref, sem_ref, *, tm):
    i = pl.program_id(0)
    nsteps = pl.num_programs(0)
    slot = i % 2

    def row_copy(step, s, r):
        return pltpu.make_async_copy(ys_ref.at[pos_ref[step * tm + r]], buf_ref.at[s, r], sem_ref.at[s])

    def start_all(step, s):
        def body(r, carry):
            row_copy(step, s, r).start()
            return carry
        lax.fori_loop(0, tm, body, 0, unroll=8)

    @pl.when(i == 0)
    def _():
        start_all(0, 0)

    @pl.when(i + 1 < nsteps)
    def _():
        start_all(i + 1, 1 - slot)

    def wait_row(r, carry):
        row_copy(i, slot, r).wait()
        return carry
    lax.fori_loop(0, tm, wait_row, 0, unroll=8)

    moe = buf_ref[slot].reshape(x1_ref.shape)
    y_ref[...] = _ple_math(x1_ref[...] + moe, p_ref[...], npw_ref, wg_ref, wp_ref, fw_ref)


def _ple(x1, moe, pos, p, npw, wg, wp, fw, tm):
    n, d = x1.shape
    consts = (npw, wg, wp, fw)
    out_shape = jax.ShapeDtypeStruct((n, d), F32)
    if pos is None:
        row = lambda width: pl.BlockSpec((tm, width), lambda i: (i, 0))
        return pl.pallas_call(
            _ple_kernel,
            grid=(n // tm,),
            in_specs=[row(d), row(d), row(p.shape[-1])] + [_const_spec(c.shape) for c in consts],
            out_specs=row(d),
            out_shape=out_shape,
            compiler_params=_cparams("arbitrary"),
        )(x1, moe, p, *consts)
    row = lambda width: pl.BlockSpec((tm, width), lambda i, pos: (i, 0))
    const = lambda shape: pl.BlockSpec(shape, lambda i, pos: (0,) * len(shape))
    return pl.pallas_call(
        functools.partial(_ple_gather_kernel, tm=tm),
        grid_spec=pltpu.PrefetchScalarGridSpec(
            num_scalar_prefetch=1,
            grid=(n // tm,),
            in_specs=[row(d), pl.BlockSpec(memory_space=pl.ANY), row(p.shape[-1])]
                     + [const(c.shape) for c in consts],
            out_specs=row(d),
            scratch_shapes=[pltpu.VMEM((2, tm) + moe.shape[1:], F32), pltpu.SemaphoreType.DMA((2,))]),
        out_shape=out_shape,
        compiler_params=_cparams("arbitrary"),
    )(pos, x1, moe, p, *consts)


def _prepare_weights(norm_mix_w, w_in, conv_w, conv_b, dt_bias, a_log, d_skip, ssd_norm_w,
                     lb_logits, hg_norm_w, w_out, norm_ffn_w, w_router_group, b_router_group,
                     w_router_expert, b_router_expert, w_exp_gate, w_exp_up, w_exp_down,
                     norm_ple_w, w_ple_gate, w_ple_proj, final_norm_w, layer):
    i = layer
    row = lambda t: t.reshape(1, -1).astype(F32)
    lane_pad = lambda t: jnp.pad(t, [(0, 0)] * (t.ndim - 1) + [(0, LANES - t.shape[-1])])
    splits = (SSD_D, CONV_DIM, SSD_HEADS, HG_D, HG_D, HG_D, HG_D)
    offs = [0]
    for s in splits:
        offs.append(offs[-1] + s)
    seg = lambda k: w_in[i][:, offs[k]:offs[k + 1]].astype(BF16)
    w_in_segs = (seg(0), seg(1), seg(3), seg(4), seg(5), seg(6), lane_pad(seg(2)))
    w_router = lane_pad(jnp.concatenate([w_router_expert[i], w_router_group[i]], axis=1).astype(F32))
    r_hi = w_router.astype(BF16)
    r_lo = (w_router - r_hi.astype(F32)).astype(BF16)
    r_bias = jnp.pad(jnp.concatenate([b_router_expert[i].reshape(-1), b_router_group[i]]).astype(F32),
                     [(0, ROUTER_ROWS - N_EXPERTS - N_EGROUPS)]).reshape(ROUTER_ROWS, 1)
    lb = jnp.cumsum(jax.nn.softmax(lb_logits.astype(F32), axis=0), axis=0)[i]
    return dict(
        norm_mix_w=row(norm_mix_w[i]), w_in=w_in_segs, conv_w=conv_w[i].astype(F32),
        conv_b=row(conv_b[i]), dt_bias=lane_pad(row(dt_bias[i])), a_log=lane_pad(row(a_log[i])),
        d_skip=row(jnp.repeat(d_skip[i], SSD_HEADDIM)), ssd_norm_w=row(ssd_norm_w[i]),
        lb=row(lb), hg_norm_w=row(hg_norm_w[i]), w_out=w_out[i].astype(BF16),
        norm_ffn_w=row(norm_ffn_w[i]), r_hi=r_hi, r_lo=r_lo, r_bias=r_bias,
        w_exp_gate=w_exp_gate[i], w_exp_up=w_exp_up[i], w_exp_down=w_exp_down[i],
        norm_ple_w=row(norm_ple_w[i]),
        w_ple_gate=w_ple_gate[i].astype(BF16), w_ple_proj=w_ple_proj[i].astype(BF16),
        final_norm_w=row(final_norm_w))


def _token_tile(n, cap):
    tm = cap
    while tm >= SUBLANES:
        if n % tm == 0:
            return tm
        tm //= 2
    raise ValueError(f"token count {n} must be a multiple of {SUBLANES}")


def _trunk(x, p, ssm_in, conv_in, hg_in, w):
    b, l, d = x.shape
    n = b * l
    tm = _token_tile(n, ROW_TILE)
    x2d = x.reshape(n, d)
    if l == 1 and ssm_in is not None:
        z, xbc, q, f, v, g, dt = _in_proj(x2d, w["norm_mix_w"], w["w_in"], _token_tile(n, IN_PROJ_TILE))
        y, conv_new, ssm_new = _ssd_step(z, xbc, dt, w["conv_w"], w["conv_b"], w["dt_bias"], w["a_log"],
                                         w["d_skip"], w["ssd_norm_w"], conv_in, ssm_in)
        o, hg_new = _hgrn_step(q, f, v, g, w["lb"], w["hg_norm_w"], hg_in)
    else:
        assert ssm_in is None and conv_in is None and hg_in is None, "multi-token groups start empty"
        y, o, conv_new, ssm_new, hg_new = _mixers(x2d, b, l, w)
    router = (w["r_hi"], w["r_lo"], w["r_bias"])
    experts = (w["w_exp_gate"], w["w_exp_up"], w["w_exp_down"])
    sorted_moe = n >= SORTED_MOE_MIN_TOKENS
    x1, h, routed = _out_proj(x2d, y.reshape(n, -1), o.reshape(n, -1), w["w_out"], w["norm_ffn_w"],
                              *router, tm, tiled_rows=sorted_moe)
    if sorted_moe:
        pos, e1_tab, e2_tab, pad_start, pad_len = _positions(routed, n, MOE_TILE)
        hs = _dispatch(h, pos, pad_start, pad_len, n + N_CLASSES * MOE_TILE, MOE_TILE, tm)
        moe = _moe_sorted(hs, e1_tab, e2_tab, *router, *experts, MOE_TILE)
    else:
        pos = None
        moe = _moe_dense(h, routed, *experts, tm)
    y_out = _ple(x1, moe, pos, p.reshape(n, -1), w["norm_ple_w"], w["w_ple_gate"], w["w_ple_proj"],
                 w["final_norm_w"], tm)
    return y_out.reshape(b, l, d), ssm_new[None], conv_new[None], hg_new[None]


def kernel(x_prompt, x_sample, state_ssm, state_conv, state_hgrn, p_prompt, p_sample, norm_mix_w, w_in, conv_w, conv_b, dt_bias, a_log, d_skip, ssd_norm_w, lb_logits, hg_norm_w, w_out, norm_ffn_w, w_router_group, b_router_group, w_router_expert, b_router_expert, w_exp_gate, w_exp_up, w_exp_down, norm_ple_w, w_ple_gate, w_ple_proj, final_norm_w):
    assert p_prompt.shape[0] == 1, "the per-layer-embedding kernel also applies the final norm: depth 1 only"
    w = _prepare_weights(norm_mix_w, w_in, conv_w, conv_b, dt_bias, a_log, d_skip, ssd_norm_w,
                         lb_logits, hg_norm_w, w_out, norm_ffn_w, w_router_group, b_router_group,
                         w_router_expert, b_router_expert, w_exp_gate, w_exp_up, w_exp_down,
                         norm_ple_w, w_ple_gate, w_ple_proj, final_norm_w, layer=0)
    y_p, ssm_p, conv_p, hg_p = _trunk(x_prompt, p_prompt[0], None, None, None, w)
    y_s, ssm_s, conv_s, hg_s = _trunk(x_sample, p_sample[0], state_ssm[0], state_conv[0],
                                      state_hgrn[0], w)
    return (y_p, y_s, ssm_p, conv_p, hg_p, ssm_s, conv_s, hg_s)
```
